```python
import jax, jax.numpy as jnp
from jax import lax
import numpy as np

D_MODEL = 1024
BATCH = 1
SEQ = 16384
DEPTH = 1
DEC_BATCH = 16
DEC_SEQ = 64
PAST_LEN = 2048

CHUNK = 64
MIX_WIDTH = D_MODEL
H_RET = 8
DK_RET = (MIX_WIDTH // 2) // H_RET
DV_RET = (MIX_WIDTH // 2) // H_RET
H_HG = 8
DK_HG = (MIX_WIDTH // 2) // H_HG
DV_HG = (MIX_WIDTH // 2) // H_HG
IN_COLS = 2 * H_RET * DK_RET + 2 * H_RET * DV_RET + 2 * H_HG * DK_HG + 2 * H_HG * DV_HG
ROPE_BASE = 10000.0
N_GROUPS = 4
EXPERTS_PER_GROUP = 8
N_EXPERTS = N_GROUPS * EXPERTS_PER_GROUP
TOP_K_IN_GROUP = 2
D_EXPERT = 256
RMS_EPS = 1e-6

kernel_name = 'hybrid_retention_hgrn2_hmoe_step'


def rms_norm(x, g):
    xf = x.astype(jnp.float32)
    y = xf * lax.rsqrt(jnp.mean(xf * xf, axis=-1, keepdims=True) + RMS_EPS)
    return (y * g.astype(jnp.float32)).astype(x.dtype)


def head_layer_norm(x):
    mu = jnp.mean(x, axis=-1, keepdims=True)
    var = jnp.mean(jnp.square(x - mu), axis=-1, keepdims=True)
    return (x - mu) * lax.rsqrt(var + RMS_EPS)


def head_rms_norm(x, g):
    return x * lax.rsqrt(jnp.mean(x * x, axis=-1, keepdims=True) + RMS_EPS) * g.astype(jnp.float32)


def rotary(x, pos):
    half = x.shape[-1] // 2
    inv_freq = ROPE_BASE ** (-jnp.arange(half, dtype=jnp.float32) / half)
    ang = pos.astype(jnp.float32)[:, None] * inv_freq[None, :]
    cos = jnp.cos(ang)[None, :, None, :]
    sin = jnp.sin(ang)[None, :, None, :]
    xf = x.astype(jnp.float32)
    x1, x2 = xf[..., :half], xf[..., half:]
    return jnp.concatenate([x1 * cos - x2 * sin, x2 * cos + x1 * sin], axis=-1)


def retention_log_gamma():
    return jnp.log1p(-(2.0 ** (-5.0 - jnp.arange(H_RET, dtype=jnp.float32))))


def retention_chunk(q, k, v, R0):
    L = q.shape[1]
    lg = retention_log_gamma()
    t = jnp.arange(L, dtype=jnp.float32)
    decay = jnp.exp(jnp.abs(t[:, None] - t[None, :])[None] * lg[:, None, None])
    scores = jnp.einsum('bthk,bshk->bhts', q, k) * decay[None]
    intra = jnp.einsum('bhts,bshv->bthv', scores, v)
    xi = jnp.exp((t[:, None] + 1.0) * lg[None, :])
    inter = jnp.einsum('bthk,bhkv->bthv', q, R0) * xi[None, :, :, None]
    zeta = jnp.exp((L - 1.0 - t)[:, None] * lg[None, :])
    R = jnp.exp(L * lg)[None, :, None, None] * R0 + jnp.einsum('bshk,bshv->bhkv', k * zeta[None, :, :, None], v)
    return intra + inter, R


def hgrn2_chunk(q, k, logf, v, S0):
    L = q.shape[1]
    b = jnp.cumsum(logf, axis=1)
    causal = jnp.tril(jnp.ones((L, L), dtype=bool))
    diff = b[:, :, None] - b[:, None, :]
    decay = jnp.exp(jnp.where(causal[None, :, :, None, None], diff, -jnp.inf))
    scores = jnp.einsum('bthk,btshk,bshk->bhts', q, decay, k)
    intra = jnp.einsum('bhts,bshv->bthv', scores, v)
    inter = jnp.einsum('bthk,bhkv->bthv', q * jnp.exp(b), S0)
    b_last = b[:, -1]
    S = jnp.exp(b_last)[..., None] * S0 + jnp.einsum('bshk,bshv->bhkv', k * jnp.exp(b_last[:, None] - b), v)
    return intra + inter, S


def token_mixers(h, ret_state0, hg_state0, pos0, chunk_len, w_in, lb, hg_norm_g):
    B, S, _ = h.shape
    f32 = jnp.float32
    proj = (h @ w_in).astype(f32)
    sizes = (H_RET * DK_RET, H_RET * DK_RET, H_RET * DV_RET, H_RET * DV_RET,
             H_HG * DK_HG, H_HG * DK_HG, H_HG * DV_HG, H_HG * DV_HG)
    offs = [sum(sizes[:i]) for i in range(1, len(sizes))]
    q_r, k_r, v_r, g_r, q_g, f_g, i_g, g_g = jnp.split(proj, offs, axis=-1)
    pos = pos0 + jnp.arange(S)
    q_r = rotary(q_r.reshape(B, S, H_RET, DK_RET), pos) * (DK_RET ** -0.5)
    k_r = rotary(k_r.reshape(B, S, H_RET, DK_RET), pos)
    v_r = v_r.reshape(B, S, H_RET, DV_RET)
    f = lb[None, None, :] + (1.0 - lb[None, None, :]) * jax.nn.sigmoid(f_g)
    logf = jnp.log(f).reshape(B, S, H_HG, DK_HG)
    k_g = (1.0 - f).reshape(B, S, H_HG, DK_HG)
    q_g = jax.nn.silu(q_g).reshape(B, S, H_HG, DK_HG)
    v_g = i_g.reshape(B, S, H_HG, DV_HG)

    nc = S // chunk_len

    def to_chunks(a):
        return jnp.moveaxis(a.reshape(B, nc, chunk_len, *a.shape[2:]), 1, 0)

    def from_chunks(a):
        return jnp.moveaxis(a, 0, 1).reshape(B, S, *a.shape[3:])

    def step(carry, xs):
        R, Sg = carry
        qr, kr, vr, qg, kg, lf, vg = xs
        o_r, R = retention_chunk(qr, kr, vr, R)
        o_g, Sg = hgrn2_chunk(qg, kg, lf, vg, Sg)
        return (R, Sg), (o_r, o_g)

    xs = tuple(to_chunks(a) for a in (q_r, k_r, v_r, q_g, k_g, logf, v_g))
    (R, Sg), (o_r, o_g) = lax.scan(step, (ret_state0.astype(f32), hg_state0.astype(f32)), xs)
    o_r = head_layer_norm(from_chunks(o_r)) * jax.nn.silu(g_r).reshape(B, S, H_RET, DV_RET)
    o_g = head_rms_norm(from_chunks(o_g), hg_norm_g) * jax.nn.silu(g_g).reshape(B, S, H_HG, DV_HG)
    out = jnp.concatenate([o_r.reshape(B, S, -1), o_g.reshape(B, S, -1)], axis=-1).astype(h.dtype)
    return out, R, Sg


def hier_moe(h, w_rg, b_rg, w_re, b_re, w_gate, w_up, w_down):
    B, S, D = h.shape
    t = h.reshape(-1, D)
    glog = (t @ w_rg + b_rg).astype(jnp.float32)
    gprob = jax.nn.softmax(glog, axis=-1)
    g_idx = jnp.argmax(glog, axis=-1)
    p_g = jnp.take_along_axis(gprob, g_idx[:, None], axis=1)[:, 0]
    elog = (t @ w_re + b_re).astype(jnp.float32).reshape(-1, N_GROUPS, EXPERTS_PER_GROUP)
    elog_sel = jnp.take_along_axis(elog, g_idx[:, None, None], axis=1)[:, 0]
    top_vals, top_idx = lax.top_k(elog_sel, TOP_K_IN_GROUP)
    p_e = jax.nn.softmax(top_vals, axis=-1)
    eid = g_idx[:, None] * EXPERTS_PER_GROUP + top_idx
    gate = jnp.sum(jax.nn.one_hot(eid, N_EXPERTS, dtype=jnp.float32) * (p_g[:, None] * p_e)[..., None], axis=1)
    y = jnp.zeros(t.shape, jnp.float32)
    for e in range(N_EXPERTS):
        he = jax.nn.silu(t @ w_gate[e]) * (t @ w_up[e])
        y = y + gate[:, e:e + 1] * (he @ w_down[e]).astype(jnp.float32)
    return y.reshape(B, S, D).astype(h.dtype)


def trunk_layer(x, ret0, hg0, pos0, chunk_len, lb, norm_mix_g, w_in, hgrn_norm_g, w_out,
                norm_ffn_g, w_rg, b_rg, w_re, b_re, w_gate, w_up, w_down):
    h = rms_norm(x, norm_mix_g)
    mix, R, Sg = token_mixers(h, ret0, hg0, pos0, chunk_len, w_in, lb, hgrn_norm_g)
    x = x + mix @ w_out
    x = x + hier_moe(rms_norm(x, norm_ffn_g), w_rg, b_rg, w_re, b_re, w_gate, w_up, w_down)
    return x, R, Sg


def setup_inputs(seed: int = 0) -> dict:
    key = jax.random.key(seed)
    ks = jax.random.split(key, 18)
    f32 = jnp.float32

    def nrm(k, shape, scale):
        return jax.random.normal(k, shape, f32) * scale

    return {
        'x_prompt': nrm(ks[0], (BATCH, SEQ, D_MODEL), 1.0),
        'x_sample': nrm(ks[1], (DEC_BATCH, DEC_SEQ, D_MODEL), 1.0),
        'state_ret': nrm(ks[2], (DEPTH, DEC_BATCH, H_RET, DK_RET, DV_RET), 0.5),
        'state_hgrn': nrm(ks[3], (DEPTH, DEC_BATCH, H_HG, DK_HG, DV_HG), 0.5),
        'norm_mix_g': 1.0 + nrm(ks[4], (DEPTH, D_MODEL), 0.02),
        'w_in': nrm(ks[5], (DEPTH, D_MODEL, IN_COLS), D_MODEL ** -0.5),
        'hgrn_lb_logits': nrm(ks[6], (DEPTH + 1, H_HG * DK_HG), 0.5),
        'hgrn_norm_g': 1.0 + nrm(ks[7], (DEPTH, DV_HG), 0.02),
        'w_out': nrm(ks[8], (DEPTH, MIX_WIDTH, D_MODEL), MIX_WIDTH ** -0.5),
        'norm_ffn_g': 1.0 + nrm(ks[9], (DEPTH, D_MODEL), 0.02),
        'w_router_group': nrm(ks[10], (DEPTH, D_MODEL, N_GROUPS), D_MODEL ** -0.5),
        'b_router_group': nrm(ks[11], (DEPTH, N_GROUPS), 0.01),
        'w_router_expert': nrm(ks[12], (DEPTH, D_MODEL, N_GROUPS * EXPERTS_PER_GROUP), D_MODEL ** -0.5),
        'b_router_expert': nrm(ks[13], (DEPTH, N_GROUPS * EXPERTS_PER_GROUP), 0.01),
        'w_exp_gate': nrm(ks[14], (DEPTH, N_EXPERTS, D_MODEL, D_EXPERT), D_MODEL ** -0.5),
        'w_exp_up': nrm(ks[15], (DEPTH, N_EXPERTS, D_MODEL, D_EXPERT), D_MODEL ** -0.5),
        'w_exp_down': nrm(ks[16], (DEPTH, N_EXPERTS, D_EXPERT, D_MODEL), D_EXPERT ** -0.5),
        'norm_final_g': 1.0 + nrm(ks[17], (D_MODEL,), 0.02),
    }


def reference(x_prompt, x_sample, state_ret, state_hgrn, norm_mix_g, w_in, hgrn_lb_logits, hgrn_norm_g,
              w_out, norm_ffn_g, w_router_group, b_router_group, w_router_expert, b_router_expert,
              w_exp_gate, w_exp_up, w_exp_down, norm_final_g):
    f32 = jnp.float32
    lb_all = jnp.cumsum(jax.nn.softmax(hgrn_lb_logits.astype(f32), axis=0), axis=0)
    bp = x_prompt.shape[0]
    dec_len = x_sample.shape[1]
    xp, xs = x_prompt, x_sample
    ret_p, hg_p, ret_s, hg_s = [], [], [], []
    for l in range(DEPTH):
        params = (lb_all[l], norm_mix_g[l], w_in[l], hgrn_norm_g[l], w_out[l], norm_ffn_g[l],
                  w_router_group[l], b_router_group[l], w_router_expert[l], b_router_expert[l],
                  w_exp_gate[l], w_exp_up[l], w_exp_down[l])
        r0 = jnp.zeros((bp, H_RET, DK_RET, DV_RET), f32)
        s0 = jnp.zeros((bp, H_HG, DK_HG, DV_HG), f32)
        xp, rp, sp = trunk_layer(xp, r0, s0, 0, CHUNK, *params)
        xs, rs, ss = trunk_layer(xs, state_ret[l], state_hgrn[l], PAST_LEN, dec_len, *params)
        ret_p.append(rp)
        hg_p.append(sp)
        ret_s.append(rs)
        hg_s.append(ss)
    y_prompt = rms_norm(xp, norm_final_g)
    y_sample = rms_norm(xs, norm_final_g)
    return (y_prompt, y_sample, jnp.stack(ret_p), jnp.stack(hg_p), jnp.stack(ret_s), jnp.stack(hg_s))
```

```python
import functools

import numpy as np
import jax
import jax.numpy as jnp
from jax import lax
from jax.experimental import pallas as pl
from jax.experimental.pallas import tpu as pltpu

F32 = jnp.float32
BF16 = jnp.bfloat16

D_MODEL = 1024
N_HEADS = 8
D_HEAD = 64
GROUP_W = N_HEADS * D_HEAD
N_PAIRS = N_HEADS // 2
PAIR_W = 2 * D_HEAD
CHUNK = 64
SUB = 16
N_SUB = CHUNK // SUB
PAST_LEN = 2048
ROPE_BASE = 10000.0
RMS_EPS = 1e-6
N_GROUPS = 4
EXPERTS_PER_GROUP = 8
N_EXPERTS = N_GROUPS * EXPERTS_PER_GROUP
D_EXPERT = 256
ROUTER_W = 128
V7X_VMEM_LIMIT = 60 * 1024 * 1024

C_QR, C_KR, C_VR, C_GR, C_QG, C_FG, C_VG, C_GG = (i * GROUP_W for i in range(8))


def _dot(a, b):
    return jnp.dot(a, b, preferred_element_type=F32)


def _dot_nt(a, b):
    return lax.dot_general(a, b, (((1,), (1,)), ((), ())), preferred_element_type=F32)


def _dot_tn(a, b):
    return lax.dot_general(a, b, (((0,), (0,)), ((), ())), preferred_element_type=F32)


def _split(x, n):
    parts = []
    for _ in range(n):
        p = x.astype(BF16)
        parts.append(p)
        x = x - p.astype(F32)
    return parts


def _group_sum(x, ones_ref):
    halves = []
    for c in range(2):
        xs = x[:, c * 256:(c + 1) * 256]
        hi, lo = _split(xs, 2)
        halves.append(_dot(hi, ones_ref[...]) + _dot(lo, ones_ref[...]))
    return jnp.concatenate(halves, axis=1)


def _silu(x):
    return x * jax.nn.sigmoid(x)


def _mixer_kernel(cfg, x_ref, gmix_ref, win_ref, invf_ref, sgn_ref, lbl_ref, hgn_ref, wout_ref, gffn_ref,
                  wr_ref, br_ref, tri_ref, dret_ref, xi_ref, zeta_ref, gls_ref, ones_ref, cmask_ref,
                  ret0_ref, hg0_ref,
                  x1_ref, h2_ref, gate_ref, retout_ref, hgout_ref,
                  proj, kbuf, obuf, qw, kw, qx, kx, pbuf, rst, sst):
    TB, LS, NSEG, carry = cfg
    NCH = LS // CHUNK
    i = pl.program_id(0)
    lane128 = lax.broadcasted_iota(jnp.int32, (1, PAIR_W), 1)
    head_a = lane128 < D_HEAD
    bd_mask = (lax.broadcasted_iota(jnp.int32, (PAIR_W, PAIR_W), 0) < D_HEAD) == head_a

    if carry:
        @pl.when(i == 0)
        def _():
            rst[...] = ret0_ref[...]
            sst[...] = hg0_ref[...]
    else:
        rst[...] = ret0_ref[...]
        sst[...] = hg0_ref[...]

    @pl.when(i == 0)
    def _():
        qw[...] = jnp.zeros_like(qw)
        kw[...] = jnp.zeros_like(kw)
        if NCH > 1:
            qx[...] = jnp.zeros_like(qx)
            kx[...] = jnp.zeros_like(kx)

    x = x_ref[...]
    h = x * lax.rsqrt(jnp.mean(x * x, axis=-1, keepdims=True) + RMS_EPS) * gmix_ref[...]
    proj[...] = _dot(h.astype(BF16), win_ref[...])

    row = lax.broadcasted_iota(jnp.int32, (TB, 1), 0)
    pos = (row & (LS - 1)) + (i * TB if carry else PAST_LEN)
    ang = pos.astype(F32) * invf_ref[...]
    cos = jnp.cos(ang)
    sin = jnp.sin(ang) * sgn_ref[...]
    first_half = (lane128 & (D_HEAD - 1)) < (D_HEAD // 2)
    for blk in range(2 * N_PAIRS):
        cols = slice(blk * PAIR_W, (blk + 1) * PAIR_W)
        xx = proj[:, cols]
        partner = jnp.where(first_half, pltpu.roll(xx, PAIR_W - D_HEAD // 2, 1), pltpu.roll(xx, D_HEAD // 2, 1))
        r = xx * cos + partner * sin
        if blk < N_PAIRS:
            r = r * (D_HEAD ** -0.5)
        proj[:, cols] = r

    la = lbl_ref[0:1, :]
    lb_ = lbl_ref[1:2, :]
    lmax = jnp.maximum(la, lb_)
    ea = jnp.exp(la - lmax)
    lbv = ea / (ea + jnp.exp(lb_ - lmax))
    qg = proj[:, C_QG:C_QG + GROUP_W]
    proj[:, C_QG:C_QG + GROUP_W] = _silu(qg)
    f = lbv + (1.0 - lbv) * jax.nn.sigmoid(proj[:, C_FG:C_FG + GROUP_W])
    kbuf[...] = 1.0 - f
    logf = jnp.log(f)
    for sg in range(NSEG):
        rows = slice(sg * LS, (sg + 1) * LS)
        parts = _split(logf[rows], 3)
        proj[rows, C_FG:C_FG + GROUP_W] = sum(_dot(tri_ref[...], p) for p in parts)

    riota = lax.broadcasted_iota(jnp.int32, (SUB, 1), 0)

    def diag_chunk(c, _):
        r0 = pl.multiple_of(c * CHUNK, CHUNK)
        for sb in range(N_SUB):
            rs = r0 + sb * SUB
            q16 = proj[pl.ds(rs, SUB), C_QG:C_QG + GROUP_W]
            b16 = proj[pl.ds(rs, SUB), C_FG:C_FG + GROUP_W]
            for s in range(SUB):
                ks = kbuf[pl.ds(rs + s, 1), :]
                bs = proj[pl.ds(rs + s, 1), C_FG:C_FG + GROUP_W]
                p = q16 * ks * jnp.exp(jnp.minimum(b16 - bs, 0.0))
                p = jnp.where(riota >= s, p, 0.0)
                pbuf[(sb * SUB + s) * SUB:(sb * SUB + s + 1) * SUB, :] = p.astype(BF16)
        sc = [_dot(pbuf[:, hh * 256:(hh + 1) * 256], ones_ref[...]) for hh in range(2)]
        for sb in range(N_SUB):
            rs = r0 + sb * SUB
            acc = [jnp.zeros((SUB, 256), F32) for _ in range(2)]
            for s in range(SUB):
                vs = proj[pl.ds(rs + s, 1), C_VG:C_VG + GROUP_W]
                lo = (sb * SUB + s) * SUB
                for hh in range(2):
                    acc[hh] = acc[hh] + sc[hh][lo:lo + SUB, :] * vs[:, hh * 256:(hh + 1) * 256]
            obuf[pl.ds(rs, SUB), GROUP_W:2 * GROUP_W] = jnp.concatenate(acc, axis=1)
        return 0

    lax.fori_loop(0, TB // CHUNK, diag_chunk, 0)

    for sg in range(NSEG):
        r0 = sg * LS
        rows = slice(r0, r0 + LS)
        st = 0 if carry else sg
        qg_s = proj[rows, C_QG:C_QG + GROUP_W]
        kg_s = kbuf[rows, :]
        b_s = proj[rows, C_FG:C_FG + GROUP_W]

        for c in range(NCH):
            for j in range(N_SUB - 1):
                e = c * CHUNK + j * SUB + SUB - 1
                bref = b_s[e:e + 1, :]
                q_rows = slice(e + 1, (c + 1) * CHUNK)
                k_rows = slice(e + 1 - SUB, e + 1)
                qt = (qg_s[q_rows] * jnp.exp(b_s[q_rows] - bref)).astype(BF16)
                kt = (kg_s[k_rows] * jnp.exp(bref - b_s[k_rows])).astype(BF16)
                for p in range(N_PAIRS):
                    dst = slice((p * (N_SUB - 1) + j) * PAIR_W, (p * (N_SUB - 1) + j + 1) * PAIR_W)
                    src = slice(p * PAIR_W, (p + 1) * PAIR_W)
                    qw[r0 + q_rows.start:r0 + q_rows.stop, dst] = qt[:, src]
                    kw[r0 + k_rows.start:r0 + k_rows.stop, dst] = kt[:, src]
        for c in range(NCH - 1):
            e = c * CHUNK + CHUNK - 1
            bref = b_s[e:e + 1, :]
            q_rows = slice(e + 1, LS)
            k_rows = slice(c * CHUNK, e + 1)
            qt = (qg_s[q_rows] * jnp.exp(b_s[q_rows] - bref)).astype(BF16)
            kt = (kg_s[k_rows] * jnp.exp(bref - b_s[k_rows])).astype(BF16)
            for p in range(N_PAIRS):
                dst = slice((p * (NCH - 1) + c) * PAIR_W, (p * (NCH - 1) + c + 1) * PAIR_W)
                src = slice(p * PAIR_W, (p + 1) * PAIR_W)
                qx[r0 + q_rows.start:r0 + q_rows.stop, dst] = qt[:, src]
                kx[r0 + k_rows.start:r0 + k_rows.stop, dst] = kt[:, src]

        b_last = b_s[LS - 1:LS, :]
        q_in = (qg_s * jnp.exp(b_s)).astype(BF16)
        k_out = (kg_s * jnp.exp(b_last - b_s)).astype(BF16)
        s_decay = jnp.exp(b_last)

        for p in range(N_PAIRS):
            pc = slice(p * PAIR_W, (p + 1) * PAIR_W)

            qb = proj[rows, C_QR + p * PAIR_W:C_QR + (p + 1) * PAIR_W].astype(BF16)
            kr = proj[rows, C_KR + p * PAIR_W:C_KR + (p + 1) * PAIR_W]
            kb = kr.astype(BF16)
            vb = proj[rows, C_VR + p * PAIR_W:C_VR + (p + 1) * PAIR_W].astype(BF16)
            zero = jnp.zeros_like(qb)
            s_a = _dot_nt(jnp.where(head_a, qb, zero), kb) * dret_ref[2 * p]
            s_b = _dot_nt(jnp.where(head_a, zero, qb), kb) * dret_ref[2 * p + 1]
            o_r = jnp.where(head_a, _dot(s_a.astype(BF16), vb), _dot(s_b.astype(BF16), vb))
            r_old = rst[st, p]
            o_r = o_r + _dot(qb, r_old.astype(BF16)) * xi_ref[:, pc]
            u = _dot_tn((kr * zeta_ref[:, pc]).astype(BF16), vb)
            r_new = r_old * gls_ref[:, pc] + jnp.where(bd_mask, u, 0.0)
            obuf[rows, pc] = o_r

            vg = proj[rows, C_VG + p * PAIR_W:C_VG + (p + 1) * PAIR_W].astype(BF16)
            wc = slice(p * (N_SUB - 1) * PAIR_W, (p + 1) * (N_SUB - 1) * PAIR_W)
            qw_p = qw[rows, wc]
            kw_p = kw[rows, wc]
            lane_w = lax.broadcasted_iota(jnp.int32, (1, qw_p.shape[1]), 1)
            head_a_w = (lane_w & (PAIR_W - 1)) < D_HEAD
            zw = jnp.zeros_like(qw_p)
            g_a = _dot_nt(jnp.where(head_a_w, qw_p, zw), kw_p)
            g_b = _dot_nt(jnp.where(head_a_w, zw, qw_p), kw_p)
            if NCH > 1:
                g_a = g_a * cmask_ref[...]
                g_b = g_b * cmask_ref[...]
                xc = slice(p * (NCH - 1) * PAIR_W, (p + 1) * (NCH - 1) * PAIR_W)
                qx_p = qx[rows, xc]
                kx_p = kx[rows, xc]
                lane_x = lax.broadcasted_iota(jnp.int32, (1, qx_p.shape[1]), 1)
                head_a_x = (lane_x & (PAIR_W - 1)) < D_HEAD
                zx = jnp.zeros_like(qx_p)
                g_a = g_a + _dot_nt(jnp.where(head_a_x, qx_p, zx), kx_p)
                g_b = g_b + _dot_nt(jnp.where(head_a_x, zx, qx_p), kx_p)
            o_g = jnp.where(head_a, _dot(g_a.astype(BF16), vg), _dot(g_b.astype(BF16), vg))
            s_old = sst[st, p]
            o_g = o_g + _dot_nt(q_in[:, pc], s_old.astype(BF16))
            ut = _dot_tn(vg, k_out[:, pc])
            s_new = s_old * s_decay[:, pc] + jnp.where(bd_mask, ut, 0.0)
            gc = slice(GROUP_W + p * PAIR_W, GROUP_W + (p + 1) * PAIR_W)
            obuf[rows, gc] = obuf[rows, gc] + o_g

            if carry:
                rst[st, p] = r_new
                sst[st, p] = s_new
            retout_ref[st, p] = r_new
            hgout_ref[st, p] = s_new

    o_r = obuf[:, 0:GROUP_W]
    mu = _group_sum(o_r, ones_ref) * (1.0 / D_HEAD)
    dlt = o_r - mu
    var = _group_sum(dlt * dlt, ones_ref) * (1.0 / D_HEAD)
    y_r = dlt * lax.rsqrt(var + RMS_EPS) * _silu(proj[:, C_GR:C_GR + GROUP_W])
    o_g = obuf[:, GROUP_W:2 * GROUP_W]
    ms = _group_sum(o_g * o_g, ones_ref) * (1.0 / D_HEAD)
    y_g = o_g * lax.rsqrt(ms + RMS_EPS) * hgn_ref[...] * _silu(proj[:, C_GG:C_GG + GROUP_W])
    mix = jnp.concatenate([y_r, y_g], axis=1).astype(BF16)
    x1 = x + _dot(mix, wout_ref[...])
    x1_ref[...] = x1

    h2 = x1 * lax.rsqrt(jnp.mean(x1 * x1, axis=-1, keepdims=True) + RMS_EPS) * gffn_ref[...]
    h2_ref[...] = h2.astype(BF16)
    h_hi, h_lo = _split(h2, 2)
    w_hi, w_lo = _split(wr_ref[...], 2)
    logits = _dot(h_hi, w_hi) + _dot(h_hi, w_lo) + _dot(h_lo, w_hi) + br_ref[...]
    lane = lax.broadcasted_iota(jnp.int32, (1, ROUTER_W), 1)
    neg = -jnp.inf
    gl = jnp.where(lane < N_GROUPS, logits, neg)
    gmax = jnp.max(gl, axis=-1, keepdims=True)
    g_idx = jnp.min(jnp.where(gl == gmax, lane, ROUTER_W), axis=-1, keepdims=True)
    p_g = 1.0 / jnp.sum(jnp.exp(gl - gmax), axis=-1, keepdims=True)
    e_lo = N_GROUPS + EXPERTS_PER_GROUP * g_idx
    el = jnp.where((lane >= e_lo) & (lane < e_lo + EXPERTS_PER_GROUP), logits, neg)
    v1 = jnp.max(el, axis=-1, keepdims=True)
    i1 = jnp.min(jnp.where(el == v1, lane, ROUTER_W), axis=-1, keepdims=True)
    el2 = jnp.where(lane == i1, neg, el)
    v2 = jnp.max(el2, axis=-1, keepdims=True)
    i2 = jnp.min(jnp.where(el2 == v2, lane, ROUTER_W), axis=-1, keepdims=True)
    t = jnp.exp(v2 - v1)
    p1 = 1.0 / (1.0 + t)
    p2 = t * p1
    gate_ref[...] = jnp.where(lane == i1, p_g * p1, 0.0) + jnp.where(lane == i2, p_g * p2, 0.0)


def _moe_kernel(h2_ref, gate_ref, x1_ref, wg_ref, wu_ref, wd_ref, gfin_ref, y_ref, acc):
    e = pl.program_id(1)

    @pl.when(e == 0)
    def _():
        acc[...] = jnp.zeros_like(acc)

    hb = h2_ref[...]
    a = _dot(hb, wg_ref[0].astype(BF16))
    b = _dot(hb, wu_ref[0].astype(BF16))
    he = (_silu(a) * b).astype(BF16)
    lane = lax.broadcasted_iota(jnp.int32, (1, ROUTER_W), 1)
    gcol = jnp.sum(jnp.where(lane == e + N_GROUPS, gate_ref[...], 0.0), axis=-1, keepdims=True)
    acc[...] += gcol * _dot(he, wd_ref[0].astype(BF16))

    @pl.when(e == pl.num_programs(1) - 1)
    def _():
        xo = x1_ref[...] + acc[...]
        y_ref[...] = xo * lax.rsqrt(jnp.mean(xo * xo, axis=-1, keepdims=True) + RMS_EPS) * gfin_ref[...]


def _to_block_diag(s):
    n = s.shape[0]
    s = s.reshape(n, N_PAIRS, 2, D_HEAD, D_HEAD)
    z = jnp.zeros_like(s[:, :, 0])
    top = jnp.concatenate([s[:, :, 0], z], axis=-1)
    bot = jnp.concatenate([z, s[:, :, 1]], axis=-1)
    return jnp.concatenate([top, bot], axis=-2)


def _from_block_diag(b):
    n = b.shape[0]
    return jnp.stack([b[:, :, :D_HEAD, :D_HEAD], b[:, :, D_HEAD:, D_HEAD:]], axis=2).reshape(
        n, N_HEADS, D_HEAD, D_HEAD)


def _const_spec(shape):
    nd = len(shape)
    return pl.BlockSpec(shape, lambda i: (0,) * nd)


def _mixer_call(x2d, ret0, hg0, params, *, seg_len, n_seg, carry):
    (gmix, win, lbl, hgn, wout, gffn, wr, br) = params
    T = x2d.shape[0]
    TB = seg_len * n_seg
    LS = seg_len
    NCH = LS // CHUNK
    n_states = ret0.shape[0]
    grid = T // TB

    lg = jnp.log1p(-(2.0 ** (-5.0 - jnp.arange(N_HEADS, dtype=F32))))
    tt = jnp.arange(LS, dtype=F32)
    ch = np.arange(LS) // CHUNK
    dret = jnp.exp(jnp.abs(tt[:, None] - tt[None, :])[None] * lg[:, None, None])
    dret = jnp.where(jnp.asarray(ch[None, :] <= ch[:, None])[None], dret, 0.0)
    lg_lane = jnp.repeat(lg, D_HEAD)[None, :]
    xi = jnp.exp((tt[:, None] + 1.0) * lg_lane)
    zeta = jnp.exp((LS - 1.0 - tt)[:, None] * lg_lane)
    gls = jnp.exp(LS * lg_lane)
    half = D_HEAD // 2
    inv_freq = ROPE_BASE ** (-jnp.arange(half, dtype=F32) / half)
    invf = jnp.tile(inv_freq, PAIR_W // half)[None, :]
    sgn = jnp.asarray(np.where((np.arange(PAIR_W) % D_HEAD) < half, -1.0, 1.0), F32)[None, :]
    tri = jnp.asarray(np.tril(np.ones((LS, LS), np.float32)), BF16)
    lane_head = np.arange(256) // D_HEAD
    ones_bd = jnp.asarray((lane_head[:, None] == lane_head[None, :]).astype(np.float32), BF16)
    cmask = jnp.asarray((ch[:, None] == ch[None, :]).astype(np.float32))

    if carry:
        state_spec = pl.BlockSpec((1, N_PAIRS, PAIR_W, PAIR_W), lambda i: (0, 0, 0, 0))
    else:
        state_spec = pl.BlockSpec((n_seg, N_PAIRS, PAIR_W, PAIR_W), lambda i: (i, 0, 0, 0))
    row_spec = lambda w: pl.BlockSpec((TB, w), lambda i: (i, 0))
    nw = (N_SUB - 1) * N_PAIRS * PAIR_W
    nx = max(NCH - 1, 1) * N_PAIRS * PAIR_W

    kern = functools.partial(_mixer_kernel, (TB, LS, n_seg, carry))
    return pl.pallas_call(
        kern,
        grid=(grid,),
        in_specs=[
            row_spec(D_MODEL),
            _const_spec((1, D_MODEL)), _const_spec(win.shape), _const_spec((1, PAIR_W)), _const_spec((1, PAIR_W)),
            _const_spec(lbl.shape), _const_spec((1, GROUP_W)), _const_spec(wout.shape), _const_spec((1, D_MODEL)),
            _const_spec(wr.shape), _const_spec((1, ROUTER_W)),
            _const_spec((LS, LS)), _const_spec((N_HEADS, LS, LS)), _const_spec((LS, GROUP_W)),
            _const_spec((LS, GROUP_W)), _const_spec((1, GROUP_W)), _const_spec((256, 256)), _const_spec((LS, LS)),
            state_spec, state_spec,
        ],
        out_specs=[row_spec(D_MODEL), row_spec(D_MODEL), row_spec(ROUTER_W), state_spec, state_spec],
        out_shape=[
            jax.ShapeDtypeStruct((T, D_MODEL), F32),
            jax.ShapeDtypeStruct((T, D_MODEL), BF16),
            jax.ShapeDtypeStruct((T, ROUTER_W), F32),
            jax.ShapeDtypeStruct((n_states, N_PAIRS, PAIR_W, PAIR_W), F32),
            jax.ShapeDtypeStruct((n_states, N_PAIRS, PAIR_W, PAIR_W), F32),
        ],
        scratch_shapes=[
            pltpu.VMEM((TB, 8 * GROUP_W), F32),
            pltpu.VMEM((TB, GROUP_W), F32),
            pltpu.VMEM((TB, 2 * GROUP_W), F32),
            pltpu.VMEM((TB, nw), BF16), pltpu.VMEM((TB, nw), BF16),
            pltpu.VMEM((TB, nx), BF16), pltpu.VMEM((TB, nx), BF16),
            pltpu.VMEM((CHUNK * SUB, GROUP_W), BF16),
            pltpu.VMEM((1 if carry else n_seg, N_PAIRS, PAIR_W, PAIR_W), F32),
            pltpu.VMEM((1 if carry else n_seg, N_PAIRS, PAIR_W, PAIR_W), F32),
        ],
        compiler_params=pltpu.CompilerParams(
            dimension_semantics=("arbitrary",), vmem_limit_bytes=V7X_VMEM_LIMIT),
    )(x2d, gmix, win, invf, sgn, lbl, hgn, wout, gffn, wr, br, tri, dret, xi, zeta, gls, ones_bd, cmask,
      ret0, hg0)


def _moe_call(h2, gate, x1, wg, wu, wd, gfin, *, tm):
    T = h2.shape[0]
    return pl.pallas_call(
        _moe_kernel,
        grid=(T // tm, N_EXPERTS),
        in_specs=[
            pl.BlockSpec((tm, D_MODEL), lambda i, e: (i, 0)),
            pl.BlockSpec((tm, ROUTER_W), lambda i, e: (i, 0)),
            pl.BlockSpec((tm, D_MODEL), lambda i, e: (i, 0)),
            pl.BlockSpec((1, D_MODEL, D_EXPERT), lambda i, e: (e, 0, 0)),
            pl.BlockSpec((1, D_MODEL, D_EXPERT), lambda i, e: (e, 0, 0)),
            pl.BlockSpec((1, D_EXPERT, D_MODEL), lambda i, e: (e, 0, 0)),
            pl.BlockSpec((1, D_MODEL), lambda i, e: (0, 0)),
        ],
        out_specs=pl.BlockSpec((tm, D_MODEL), lambda i, e: (i, 0)),
        out_shape=jax.ShapeDtypeStruct((T, D_MODEL), F32),
        scratch_shapes=[pltpu.VMEM((tm, D_MODEL), F32)],
        compiler_params=pltpu.CompilerParams(
            dimension_semantics=("arbitrary", "arbitrary"), vmem_limit_bytes=V7X_VMEM_LIMIT),
    )(h2, gate, x1, wg, wu, wd, gfin)


def kernel(x_prompt, x_sample, state_ret, state_hgrn, norm_mix_g, w_in, hgrn_lb_logits, hgrn_norm_g, w_out,
           norm_ffn_g, w_router_group, b_router_group, w_router_expert, b_router_expert, w_exp_gate, w_exp_up,
           w_exp_down, norm_final_g):
    depth = w_in.shape[0]
    assert depth == 1 and hgrn_lb_logits.shape[0] == 2, "single-layer configuration only"
    bp, seq, d = x_prompt.shape
    db, dec_len, _ = x_sample.shape
    assert bp == 1 and d == D_MODEL and dec_len == CHUNK and seq % 256 == 0 and db % 4 == 0

    wr = jnp.zeros((D_MODEL, ROUTER_W), F32)
    wr = wr.at[:, :N_GROUPS].set(w_router_group[0]).at[:, N_GROUPS:N_GROUPS + N_EXPERTS].set(w_router_expert[0])
    br = jnp.zeros((1, ROUTER_W), F32)
    br = br.at[0, :N_GROUPS].set(b_router_group[0]).at[0, N_GROUPS:N_GROUPS + N_EXPERTS].set(b_router_expert[0])
    params = (norm_mix_g[0][None, :], w_in[0].astype(BF16), hgrn_lb_logits,
              jnp.tile(hgrn_norm_g[0], N_HEADS)[None, :], w_out[0].astype(BF16), norm_ffn_g[0][None, :], wr, br)

    zeros_state = jnp.zeros((1, N_PAIRS, PAIR_W, PAIR_W), F32)
    x1_p, h2_p, gate_p, ret_p, hg_p = _mixer_call(
        x_prompt.reshape(seq, d), zeros_state, zeros_state, params, seg_len=256, n_seg=1, carry=True)
    ret0_s = _to_block_diag(state_ret[0])
    hg0_s = _to_block_diag(jnp.swapaxes(state_hgrn[0], -1, -2))
    x1_s, h2_s, gate_s, ret_s, hg_s = _mixer_call(
        x_sample.reshape(db * dec_len, d), ret0_s, hg0_s, params, seg_len=CHUNK, n_seg=4, carry=False)

    gfin = norm_final_g[None, :]
    y_p = _moe_call(h2_p, gate_p, x1_p, w_exp_gate[0], w_exp_up[0], w_exp_down[0], gfin, tm=min(1024, seq))
    y_s = _moe_call(h2_s, gate_s, x1_s, w_exp_gate[0], w_exp_up[0], w_exp_down[0], gfin,
                    tm=min(1024, db * dec_len))

    def states(r, s):
        return (_from_block_diag(r)[None], jnp.swapaxes(_from_block_diag(s), -1, -2)[None])

    rp, sp = states(ret_p, hg_p)
    rs, ss = states(ret_s, hg_s)
    return (y_p.reshape(bp, seq, d), y_s.reshape(db, dec_len, d), rp, sp, rs, ss)
```

```python
import functools

import numpy as np
import jax
import jax.numpy as jnp
from jax import lax
from jax.experimental import pallas as pl
from jax.experimental.pallas import tpu as pltpu

F32 = jnp.float32
BF16 = jnp.bfloat16

D_MODEL = 1024
N_HEADS = 8
D_HEAD = 64
GROUP_W = N_HEADS * D_HEAD
N_PAIRS = N_HEADS // 2
PAIR_W = 2 * D_HEAD
CHUNK = 64
SUB = 16
N_SUB = CHUNK // SUB
PAST_LEN = 2048
ROPE_BASE = 10000.0
RMS_EPS = 1e-6
N_GROUPS = 4
EXPERTS_PER_GROUP = 8
N_EXPERTS = N_GROUPS * EXPERTS_PER_GROUP
D_EXPERT = 256
ROUTER_W = 128
V7X_VMEM_LIMIT = 60 * 1024 * 1024
TB = 256
SLAB = 16
SLABS_PER_BLOCK = TB // SLAB + N_GROUPS
SORT_ROWS = SLABS_PER_BLOCK * SLAB
SORT_W = 384
TILE_SLABS = 32
TM = TILE_SLABS * SLAB

C_QR, C_KR, C_VR, C_GR, C_QG, C_FG, C_VG, C_GG = (i * GROUP_W for i in range(8))


def _dot(a, b):
    return jnp.dot(a, b, preferred_element_type=F32)


def _dot_nt(a, b):
    return lax.dot_general(a, b, (((1,), (1,)), ((), ())), preferred_element_type=F32)


def _dot_tn(a, b):
    return lax.dot_general(a, b, (((0,), (0,)), ((), ())), preferred_element_type=F32)


def _split(x, n):
    parts = []
    for _ in range(n):
        p = x.astype(BF16)
        parts.append(p)
        x = x - p.astype(F32)
    return parts


def _group_sum(x, ones_ref):
    halves = []
    for c in range(2):
        xs = x[:, c * 256:(c + 1) * 256]
        hi, lo = _split(xs, 2)
        halves.append(_dot(hi, ones_ref[...]) + _dot(lo, ones_ref[...]))
    return jnp.concatenate(halves, axis=1)


def _silu(x):
    return x * jax.nn.sigmoid(x)


N_MIXER_INPUTS = 21


def _mixer_kernel(cfg, *refs):
    n_own, n_fill, n_alias = cfg[-3:]
    i = pl.program_id(0)

    @pl.when(i < n_own)
    def _():
        _mixer_body(cfg, *refs)

    if n_fill:
        @pl.when(i >= n_own)
        def _():
            for out_ref in refs[N_MIXER_INPUTS + n_alias:N_MIXER_INPUTS + n_alias + 5]:
                out_ref[...] = jnp.zeros_like(out_ref)


def _mixer_body(cfg, x_ref, gmix_ref, win_ref, invf_ref, sgn_ref, lbl_ref, hgn_ref, wout_ref, gffn_ref,
                wr_ref, br_ref, tri_ref, dret_ref, xi_ref, zeta_ref, gls_ref, ones_ref, cmask_ref, stri_ref,
                ret0_ref, hg0_ref, *rest):
    LS, NSEG, carry, _, _, n_alias = cfg
    (x1_ref, h2s_ref, gs_ref, rinfo_ref, info_ref, retout_ref, hgout_ref,
     proj, kbuf, obuf, qw, kw, qx, kx, pbuf, rst, sst) = rest[n_alias:]
    NCH = LS // CHUNK
    i = pl.program_id(0)
    lane128 = lax.broadcasted_iota(jnp.int32, (1, PAIR_W), 1)
    head_a = lane128 < D_HEAD
    bd_mask = (lax.broadcasted_iota(jnp.int32, (PAIR_W, PAIR_W), 0) < D_HEAD) == head_a

    if carry:
        @pl.when(i == 0)
        def _():
            rst[...] = ret0_ref[...]
            sst[...] = hg0_ref[...]
    else:
        rst[...] = ret0_ref[...]
        sst[...] = hg0_ref[...]

    @pl.when(i == 0)
    def _():
        qw[...] = jnp.zeros_like(qw)
        kw[...] = jnp.zeros_like(kw)
        if NCH > 1:
            qx[...] = jnp.zeros_like(qx)
            kx[...] = jnp.zeros_like(kx)

    x = x_ref[...]
    h = x * lax.rsqrt(jnp.mean(x * x, axis=-1, keepdims=True) + RMS_EPS) * gmix_ref[...]
    proj[...] = _dot(h.astype(BF16), win_ref[...])

    row = lax.broadcasted_iota(jnp.int32, (TB, 1), 0)
    pos = (row & (LS - 1)) + (i * TB if carry else PAST_LEN)
    ang = pos.astype(F32) * invf_ref[...]
    cos = jnp.cos(ang)
    sin = jnp.sin(ang) * sgn_ref[...]
    first_half = (lane128 & (D_HEAD - 1)) < (D_HEAD // 2)
    for blk in range(2 * N_PAIRS):
        cols = slice(blk * PAIR_W, (blk + 1) * PAIR_W)
        xx = proj[:, cols]
        partner = jnp.where(first_half, pltpu.roll(xx, PAIR_W - D_HEAD // 2, 1), pltpu.roll(xx, D_HEAD // 2, 1))
        r = xx * cos + partner * sin
        if blk < N_PAIRS:
            r = r * (D_HEAD ** -0.5)
        proj[:, cols] = r

    la = lbl_ref[0:1, :]
    lb_ = lbl_ref[1:2, :]
    lmax = jnp.maximum(la, lb_)
    ea = jnp.exp(la - lmax)
    lbv = ea / (ea + jnp.exp(lb_ - lmax))
    qg = proj[:, C_QG:C_QG + GROUP_W]
    proj[:, C_QG:C_QG + GROUP_W] = _silu(qg)
    f = lbv + (1.0 - lbv) * jax.nn.sigmoid(proj[:, C_FG:C_FG + GROUP_W])
    kbuf[...] = 1.0 - f
    logf = jnp.log(f)
    for sg in range(NSEG):
        rows = slice(sg * LS, (sg + 1) * LS)
        parts = _split(logf[rows], 3)
        proj[rows, C_FG:C_FG + GROUP_W] = sum(_dot(tri_ref[...], p) for p in parts)

    riota = lax.broadcasted_iota(jnp.int32, (SUB, 1), 0)

    def diag_chunk(c, _):
        r0 = pl.multiple_of(c * CHUNK, CHUNK)
        for sb in range(N_SUB):
            rs = r0 + sb * SUB
            q16 = proj[pl.ds(rs, SUB), C_QG:C_QG + GROUP_W]
            b16 = proj[pl.ds(rs, SUB), C_FG:C_FG + GROUP_W]
            for s in range(SUB):
                ks = kbuf[pl.ds(rs + s, 1), :]
                bs = proj[pl.ds(rs + s, 1), C_FG:C_FG + GROUP_W]
                p = q16 * ks * jnp.exp(jnp.minimum(b16 - bs, 0.0))
                p = jnp.where(riota >= s, p, 0.0)
                pbuf[(sb * SUB + s) * SUB:(sb * SUB + s + 1) * SUB, :] = p.astype(BF16)
        sc = [_dot(pbuf[:, hh * 256:(hh + 1) * 256], ones_ref[...]) for hh in range(2)]
        for sb in range(N_SUB):
            rs = r0 + sb * SUB
            acc = [jnp.zeros((SUB, 256), F32) for _ in range(2)]
            for s in range(SUB):
                vs = proj[pl.ds(rs + s, 1), C_VG:C_VG + GROUP_W]
                lo = (sb * SUB + s) * SUB
                for hh in range(2):
                    acc[hh] = acc[hh] + sc[hh][lo:lo + SUB, :] * vs[:, hh * 256:(hh + 1) * 256]
            obuf[pl.ds(rs, SUB), GROUP_W:2 * GROUP_W] = jnp.concatenate(acc, axis=1)
        return 0

    lax.fori_loop(0, TB // CHUNK, diag_chunk, 0)

    for sg in range(NSEG):
        r0 = sg * LS
        rows = slice(r0, r0 + LS)
        st = 0 if carry else sg
        qg_s = proj[rows, C_QG:C_QG + GROUP_W]
        kg_s = kbuf[rows, :]
        b_s = proj[rows, C_FG:C_FG + GROUP_W]

        for c in range(NCH):
            for j in range(N_SUB - 1):
                e = c * CHUNK + j * SUB + SUB - 1
                bref = b_s[e:e + 1, :]
                q_rows = slice(e + 1, (c + 1) * CHUNK)
                k_rows = slice(e + 1 - SUB, e + 1)
                qt = (qg_s[q_rows] * jnp.exp(b_s[q_rows] - bref)).astype(BF16)
                kt = (kg_s[k_rows] * jnp.exp(bref - b_s[k_rows])).astype(BF16)
                for p in range(N_PAIRS):
                    dst = slice((p * (N_SUB - 1) + j) * PAIR_W, (p * (N_SUB - 1) + j + 1) * PAIR_W)
                    src = slice(p * PAIR_W, (p + 1) * PAIR_W)
                    qw[r0 + q_rows.start:r0 + q_rows.stop, dst] = qt[:, src]
                    kw[r0 + k_rows.start:r0 + k_rows.stop, dst] = kt[:, src]
        for c in range(NCH - 1):
            e = c * CHUNK + CHUNK - 1
            bref = b_s[e:e + 1, :]
            q_rows = slice(e + 1, LS)
            k_rows = slice(c * CHUNK, e + 1)
            qt = (qg_s[q_rows] * jnp.exp(b_s[q_rows] - bref)).astype(BF16)
            kt = (kg_s[k_rows] * jnp.exp(bref - b_s[k_rows])).astype(BF16)
            for p in range(N_PAIRS):
                dst = slice((p * (NCH - 1) + c) * PAIR_W, (p * (NCH - 1) + c + 1) * PAIR_W)
                src = slice(p * PAIR_W, (p + 1) * PAIR_W)
                qx[r0 + q_rows.start:r0 + q_rows.stop, dst] = qt[:, src]
                kx[r0 + k_rows.start:r0 + k_rows.stop, dst] = kt[:, src]

        b_last = b_s[LS - 1:LS, :]
        q_in = (qg_s * jnp.exp(b_s)).astype(BF16)
        k_out = (kg_s * jnp.exp(b_last - b_s)).astype(BF16)
        s_decay = jnp.exp(b_last)

        for p in range(N_PAIRS):
            pc = slice(p * PAIR_W, (p + 1) * PAIR_W)

            qb = proj[rows, C_QR + p * PAIR_W:C_QR + (p + 1) * PAIR_W].astype(BF16)
            kr = proj[rows, C_KR + p * PAIR_W:C_KR + (p + 1) * PAIR_W]
            kb = kr.astype(BF16)
            vb = proj[rows, C_VR + p * PAIR_W:C_VR + (p + 1) * PAIR_W].astype(BF16)
            zero = jnp.zeros_like(qb)
            s_a = _dot_nt(jnp.where(head_a, qb, zero), kb) * dret_ref[2 * p]
            s_b = _dot_nt(jnp.where(head_a, zero, qb), kb) * dret_ref[2 * p + 1]
            o_r = jnp.where(head_a, _dot(s_a.astype(BF16), vb), _dot(s_b.astype(BF16), vb))
            r_old = rst[st, p]
            o_r = o_r + _dot(qb, r_old.astype(BF16)) * xi_ref[:, pc]
            u = _dot_tn((kr * zeta_ref[:, pc]).astype(BF16), vb)
            r_new = r_old * gls_ref[:, pc] + jnp.where(bd_mask, u, 0.0)
            obuf[rows, pc] = o_r

            vg = proj[rows, C_VG + p * PAIR_W:C_VG + (p + 1) * PAIR_W].astype(BF16)
            wc = slice(p * (N_SUB - 1) * PAIR_W, (p + 1) * (N_SUB - 1) * PAIR_W)
            qw_p = qw[rows, wc]
            kw_p = kw[rows, wc]
            lane_w = lax.broadcasted_iota(jnp.int32, (1, qw_p.shape[1]), 1)
            head_a_w = (lane_w & (PAIR_W - 1)) < D_HEAD
            zw = jnp.zeros_like(qw_p)
            g_a = _dot_nt(jnp.where(head_a_w, qw_p, zw), kw_p)
            g_b = _dot_nt(jnp.where(head_a_w, zw, qw_p), kw_p)
            if NCH > 1:
                g_a = g_a * cmask_ref[...]
                g_b = g_b * cmask_ref[...]
                xc = slice(p * (NCH - 1) * PAIR_W, (p + 1) * (NCH - 1) * PAIR_W)
                qx_p = qx[rows, xc]
                kx_p = kx[rows, xc]
                lane_x = lax.broadcasted_iota(jnp.int32, (1, qx_p.shape[1]), 1)
                head_a_x = (lane_x & (PAIR_W - 1)) < D_HEAD
                zx = jnp.zeros_like(qx_p)
                g_a = g_a + _dot_nt(jnp.where(head_a_x, qx_p, zx), kx_p)
                g_b = g_b + _dot_nt(jnp.where(head_a_x, zx, qx_p), kx_p)
            o_g = jnp.where(head_a, _dot(g_a.astype(BF16), vg), _dot(g_b.astype(BF16), vg))
            s_old = sst[st, p]
            o_g = o_g + _dot_nt(q_in[:, pc], s_old.astype(BF16))
            ut = _dot_tn(vg, k_out[:, pc])
            s_new = s_old * s_decay[:, pc] + jnp.where(bd_mask, ut, 0.0)
            gc = slice(GROUP_W + p * PAIR_W, GROUP_W + (p + 1) * PAIR_W)
            obuf[rows, gc] = obuf[rows, gc] + o_g

            if carry:
                rst[st, p] = r_new
                sst[st, p] = s_new
            retout_ref[st, p] = r_new
            hgout_ref[st, p] = s_new

    o_r = obuf[:, 0:GROUP_W]
    mu = _group_sum(o_r, ones_ref) * (1.0 / D_HEAD)
    dlt = o_r - mu
    var = _group_sum(dlt * dlt, ones_ref) * (1.0 / D_HEAD)
    y_r = dlt * lax.rsqrt(var + RMS_EPS) * _silu(proj[:, C_GR:C_GR + GROUP_W])
    o_g = obuf[:, GROUP_W:2 * GROUP_W]
    ms = _group_sum(o_g * o_g, ones_ref) * (1.0 / D_HEAD)
    y_g = o_g * lax.rsqrt(ms + RMS_EPS) * hgn_ref[...] * _silu(proj[:, C_GG:C_GG + GROUP_W])
    mix = jnp.concatenate([y_r, y_g], axis=1).astype(BF16)
    x1 = x + _dot(mix, wout_ref[...])
    x1_ref[...] = x1

    h2 = x1 * lax.rsqrt(jnp.mean(x1 * x1, axis=-1, keepdims=True) + RMS_EPS) * gffn_ref[...]
    h_hi, h_lo = _split(h2, 2)
    w_hi, w_lo = _split(wr_ref[...], 2)
    logits = _dot(h_hi, w_hi) + _dot(h_hi, w_lo) + _dot(h_lo, w_hi) + br_ref[...]
    lane = lax.broadcasted_iota(jnp.int32, (1, ROUTER_W), 1)
    neg = -jnp.inf
    gl = jnp.where(lane < N_GROUPS, logits, neg)
    gmax = jnp.max(gl, axis=-1, keepdims=True)
    g_idx = jnp.min(jnp.where(gl == gmax, lane, ROUTER_W), axis=-1, keepdims=True)
    prob_g = 1.0 / jnp.sum(jnp.exp(gl - gmax), axis=-1, keepdims=True)
    e_lo = N_GROUPS + EXPERTS_PER_GROUP * g_idx
    el = jnp.where((lane >= e_lo) & (lane < e_lo + EXPERTS_PER_GROUP), logits, neg)
    v1 = jnp.max(el, axis=-1, keepdims=True)
    i1 = jnp.min(jnp.where(el == v1, lane, ROUTER_W), axis=-1, keepdims=True)
    el2 = jnp.where(lane == i1, neg, el)
    v2 = jnp.max(el2, axis=-1, keepdims=True)
    i2 = jnp.min(jnp.where(el2 == v2, lane, ROUTER_W), axis=-1, keepdims=True)
    t = jnp.exp(v2 - v1)
    p1 = 1.0 / (1.0 + t)
    p2 = t * p1
    gate = jnp.where(lane == i1, prob_g * p1, 0.0) + jnp.where(lane == i2, prob_g * p2, 0.0)

    onehot = jnp.where(lane == g_idx, 1.0, 0.0)
    before = _dot(stri_ref[...], onehot.astype(BF16))
    count = jnp.sum(onehot, axis=0, keepdims=True)
    padded = jnp.floor((count + (SLAB - 1.0)) * (1.0 / SLAB)) * SLAB
    padded8 = jnp.broadcast_to(padded, (8, ROUTER_W))
    seg_start = sum(jnp.where(lane >= k, pltpu.roll(padded8, k, 1), 0.0) for k in range(1, N_GROUPS))[0:1]
    rank = jnp.sum(onehot * (seg_start + before), axis=-1, keepdims=True)
    col = lax.broadcasted_iota(jnp.int32, (1, SORT_W), 1).astype(F32)
    place = jnp.where(col == rank, 1.0, 0.0).astype(BF16)
    h2s_ref[...] = _dot_tn(place, h2.astype(BF16))[:SORT_ROWS].astype(BF16)
    gs_ref[...] = sum(_dot_tn(place, part) for part in _split(gate, 3))[:SORT_ROWS]
    rinfo_ref[...] = jnp.where(lane == 0, rank, 0.0)
    slab_lo = (lane * SLAB).astype(F32)
    slab_group = jnp.full((1, ROUTER_W), -1.0, F32)
    for g in range(N_GROUPS):
        s_g = jnp.sum(jnp.where(lane == g, seg_start, 0.0), axis=-1, keepdims=True)
        n_g = jnp.sum(jnp.where(lane == g, padded, 0.0), axis=-1, keepdims=True)
        slab_group = jnp.where((slab_lo >= s_g) & (slab_lo < s_g + n_g), float(g), slab_group)
    info_ref[0] = jnp.broadcast_to(slab_group.astype(jnp.int32), (8, ROUTER_W))


def _slab_gather(table_ref, first, n_slabs, srcs, bufs, sems, slot, *, wait):
    for j in range(n_slabs):
        n = table_ref[first + j]
        row = pl.multiple_of(jnp.maximum(n, 0) * SLAB, SLAB)
        dst_rows = pl.ds(j * SLAB, SLAB)

        @pl.when(n >= 0)
        def _():
            for src, buf, sem in zip(srcs, bufs, sems):
                cp = pltpu.make_async_copy(src.at[pl.ds(row, SLAB)], buf.at[slot, dst_rows], sem.at[slot])
                if wait:
                    cp.wait()
                else:
                    cp.start()

        if not wait:
            @pl.when(n < 0)
            def _():
                for buf in bufs:
                    buf[slot, dst_rows] = jnp.zeros((SLAB,) + buf.shape[2:], buf.dtype)


def _moe_kernel(src_ref, tg_ref, nt_ref, h2s_hbm, gs_hbm, wgu_ref, wd_ref, ys_ref, hbuf, gbuf, sem_h, sem_g):
    i = pl.program_id(0)
    n_tiles = nt_ref[0]
    slot = lax.rem(i, 2)
    gather = functools.partial(_slab_gather, src_ref, srcs=(h2s_hbm, gs_hbm), bufs=(hbuf, gbuf),
                               sems=(sem_h, sem_g), n_slabs=TILE_SLABS)

    @pl.when(i == 0)
    def _():
        gather(first=0, slot=0, wait=False)

    @pl.when(i < n_tiles)
    def _():
        gather(first=i * TILE_SLABS, slot=slot, wait=True)

        @pl.when(i + 1 < n_tiles)
        def _():
            gather(first=(i + 1) * TILE_SLABS, slot=1 - slot, wait=False)

        hb = hbuf[slot]
        gates = gbuf[slot]
        lane = lax.broadcasted_iota(jnp.int32, (1, ROUTER_W), 1)
        first_lane = N_GROUPS + tg_ref[i] * EXPERTS_PER_GROUP
        acc = jnp.zeros((TM, D_MODEL), F32)
        for e in range(EXPERTS_PER_GROUP):
            ab = _dot(hb, wgu_ref[e])
            he = (_silu(ab[:, :D_EXPERT]) * ab[:, D_EXPERT:]).astype(BF16)
            gcol = jnp.sum(jnp.where(lane == first_lane + e, gates, 0.0), axis=-1, keepdims=True)
            acc = acc + gcol * _dot(he, wd_ref[e])
        ys_ref[...] = acc

    @pl.when(i >= n_tiles)
    def _():
        ys_ref[...] = jnp.zeros_like(ys_ref)


def _final_kernel(blocks_p, dest_ref, x1_ref, rinfo_ref, ys_hbm, gfin_ref, yp_ref, ysmp_ref, ybuf, sem):
    i = pl.program_id(0)
    slot = lax.rem(i, 2)
    gather = functools.partial(_slab_gather, dest_ref, srcs=(ys_hbm,), bufs=(ybuf,), sems=(sem,),
                               n_slabs=SLABS_PER_BLOCK)

    @pl.when(i == 0)
    def _():
        ybuf[:, SORT_ROWS:SORT_W, :] = jnp.zeros((2, SORT_W - SORT_ROWS, D_MODEL), F32)
        gather(first=0, slot=0, wait=False)

    gather(first=i * SLABS_PER_BLOCK, slot=slot, wait=True)

    @pl.when(i + 1 < pl.num_programs(0))
    def _():
        gather(first=(i + 1) * SLABS_PER_BLOCK, slot=1 - slot, wait=False)

    rank = rinfo_ref[:, 0:1]
    col = lax.broadcasted_iota(jnp.int32, (1, SORT_W), 1).astype(F32)
    place = jnp.where(col == rank, 1.0, 0.0).astype(BF16)
    moe = sum(_dot(place, part) for part in _split(ybuf[slot], 2))
    xo = x1_ref[...] + moe
    y = xo * lax.rsqrt(jnp.mean(xo * xo, axis=-1, keepdims=True) + RMS_EPS) * gfin_ref[...]

    @pl.when(i < blocks_p)
    def _():
        yp_ref[...] = y

    @pl.when(i >= blocks_p)
    def _():
        ysmp_ref[...] = y


def _to_block_diag(s):
    n = s.shape[0]
    s = s.reshape(n, N_PAIRS, 2, D_HEAD, D_HEAD)
    z = jnp.zeros_like(s[:, :, 0])
    top = jnp.concatenate([s[:, :, 0], z], axis=-1)
    bot = jnp.concatenate([z, s[:, :, 1]], axis=-1)
    return jnp.concatenate([top, bot], axis=-2)


def _from_block_diag(b):
    n = b.shape[0]
    return jnp.stack([b[:, :, :D_HEAD, :D_HEAD], b[:, :, D_HEAD:, D_HEAD:]], axis=2).reshape(
        n, N_HEADS, D_HEAD, D_HEAD)


def _const_spec(shape):
    nd = len(shape)
    return pl.BlockSpec(shape, lambda i: (0,) * nd)


def _mixer_call(x2d, ret0, hg0, params, *, seg_len, carry, block_offset, total_blocks, shared=()):
    (gmix, win, lbl, hgn, wout, gffn, wr, br) = params
    T = x2d.shape[0]
    LS = seg_len
    n_seg = TB // LS
    NCH = LS // CHUNK
    n_states = ret0.shape[0]
    n_own = T // TB
    n_fill = 0 if shared else total_blocks - n_own - block_offset
    grid = n_own + n_fill

    lg = jnp.log1p(-(2.0 ** (-5.0 - jnp.arange(N_HEADS, dtype=F32))))
    tt = jnp.arange(LS, dtype=F32)
    ch = np.arange(LS) // CHUNK
    dret = jnp.exp(jnp.abs(tt[:, None] - tt[None, :])[None] * lg[:, None, None])
    dret = jnp.where(jnp.asarray(ch[None, :] <= ch[:, None])[None], dret, 0.0)
    lg_lane = jnp.repeat(lg, D_HEAD)[None, :]
    xi = jnp.exp((tt[:, None] + 1.0) * lg_lane)
    zeta = jnp.exp((LS - 1.0 - tt)[:, None] * lg_lane)
    gls = jnp.exp(LS * lg_lane)
    half = D_HEAD // 2
    inv_freq = ROPE_BASE ** (-jnp.arange(half, dtype=F32) / half)
    invf = jnp.tile(inv_freq, PAIR_W // half)[None, :]
    sgn = jnp.asarray(np.where((np.arange(PAIR_W) % D_HEAD) < half, -1.0, 1.0), F32)[None, :]
    tri = jnp.asarray(np.tril(np.ones((LS, LS), np.float32)), BF16)
    lane_head = np.arange(256) // D_HEAD
    ones_bd = jnp.asarray((lane_head[:, None] == lane_head[None, :]).astype(np.float32), BF16)
    cmask = jnp.asarray((ch[:, None] == ch[None, :]).astype(np.float32))

    if carry:
        state_spec = pl.BlockSpec((1, N_PAIRS, PAIR_W, PAIR_W), lambda i: (0, 0, 0, 0))
    else:
        state_spec = pl.BlockSpec((n_seg, N_PAIRS, PAIR_W, PAIR_W), lambda i: (i, 0, 0, 0))
    stri = jnp.asarray(np.tril(np.ones((TB, TB), np.float32), -1), BF16)
    in_row_spec = pl.BlockSpec((TB, D_MODEL), lambda i: (jnp.minimum(i, n_own - 1), 0))
    out_row_spec = lambda rows, w: pl.BlockSpec((rows, w), lambda i: (i + block_offset, 0))
    nw = (N_SUB - 1) * N_PAIRS * PAIR_W
    nx = max(NCH - 1, 1) * N_PAIRS * PAIR_W
    n_in = N_MIXER_INPUTS

    kern = functools.partial(_mixer_kernel, (LS, n_seg, carry, n_own, n_fill, len(shared)))
    return pl.pallas_call(
        kern,
        grid=(grid,),
        in_specs=[
            in_row_spec,
            _const_spec((1, D_MODEL)), _const_spec(win.shape), _const_spec((1, PAIR_W)), _const_spec((1, PAIR_W)),
            _const_spec(lbl.shape), _const_spec((1, GROUP_W)), _const_spec(wout.shape), _const_spec((1, D_MODEL)),
            _const_spec(wr.shape), _const_spec((1, ROUTER_W)),
            _const_spec((LS, LS)), _const_spec((N_HEADS, LS, LS)), _const_spec((LS, GROUP_W)),
            _const_spec((LS, GROUP_W)), _const_spec((1, GROUP_W)), _const_spec((256, 256)), _const_spec((LS, LS)),
            _const_spec((TB, TB)),
            state_spec, state_spec,
        ] + [pl.BlockSpec(memory_space=pl.ANY)] * len(shared),
        out_specs=[out_row_spec(TB, D_MODEL), out_row_spec(SORT_ROWS, D_MODEL), out_row_spec(SORT_ROWS, ROUTER_W),
                   out_row_spec(TB, ROUTER_W),
                   pl.BlockSpec((1, 8, ROUTER_W), lambda i: (i + block_offset, 0, 0)),
                   state_spec, state_spec],
        out_shape=[
            jax.ShapeDtypeStruct((total_blocks * TB, D_MODEL), F32),
            jax.ShapeDtypeStruct((total_blocks * SORT_ROWS, D_MODEL), BF16),
            jax.ShapeDtypeStruct((total_blocks * SORT_ROWS, ROUTER_W), F32),
            jax.ShapeDtypeStruct((total_blocks * TB, ROUTER_W), F32),
            jax.ShapeDtypeStruct((total_blocks, 8, ROUTER_W), jnp.int32),
            jax.ShapeDtypeStruct((n_states, N_PAIRS, PAIR_W, PAIR_W), F32),
            jax.ShapeDtypeStruct((n_states, N_PAIRS, PAIR_W, PAIR_W), F32),
        ],
        input_output_aliases={n_in + k: k for k in range(len(shared))},
        scratch_shapes=[
            pltpu.VMEM((TB, 8 * GROUP_W), F32),
            pltpu.VMEM((TB, GROUP_W), F32),
            pltpu.VMEM((TB, 2 * GROUP_W), F32),
            pltpu.VMEM((TB, nw), BF16), pltpu.VMEM((TB, nw), BF16),
            pltpu.VMEM((TB, nx), BF16), pltpu.VMEM((TB, nx), BF16),
            pltpu.VMEM((CHUNK * SUB, GROUP_W), BF16),
            pltpu.VMEM((1 if carry else n_seg, N_PAIRS, PAIR_W, PAIR_W), F32),
            pltpu.VMEM((1 if carry else n_seg, N_PAIRS, PAIR_W, PAIR_W), F32),
        ],
        compiler_params=pltpu.CompilerParams(
            dimension_semantics=("arbitrary",), vmem_limit_bytes=V7X_VMEM_LIMIT),
    )(x2d, gmix, win, invf, sgn, lbl, hgn, wout, gffn, wr, br, tri, dret, xi, zeta, gls, ones_bd, cmask, stri,
      ret0, hg0, *shared)


def _dispatch_tables(info, n_tiles_max):
    slab_group = info[:, 0, :SLABS_PER_BLOCK].reshape(-1)
    n_slabs = slab_group.shape[0]
    valid = slab_group >= 0
    onehot = (slab_group[:, None] == jnp.arange(N_GROUPS, dtype=jnp.int32)[None, :]).astype(jnp.int32)
    within = jnp.cumsum(onehot, axis=0) - onehot
    tiles = (jnp.sum(onehot, axis=0) + TILE_SLABS - 1) // TILE_SLABS
    tile_end = jnp.cumsum(tiles)
    g = jnp.clip(slab_group, 0, N_GROUPS - 1)
    dest = (tile_end - tiles)[g] * TILE_SLABS + jnp.take_along_axis(within, g[:, None], axis=1)[:, 0]
    dest = jnp.where(valid, dest, -1).astype(jnp.int32)
    n_slots = n_tiles_max * TILE_SLABS
    src = jnp.full((n_slots,), -1, jnp.int32).at[jnp.where(valid, dest, n_slots)].set(
        jnp.arange(n_slabs, dtype=jnp.int32), mode="drop")
    tile_group = jnp.sum(jnp.arange(n_tiles_max, dtype=jnp.int32)[:, None] >= tile_end[None, :], axis=1)
    tile_group = jnp.minimum(tile_group, N_GROUPS - 1).astype(jnp.int32)
    return src, dest, tile_group, tile_end[-1:].astype(jnp.int32)


def _moe_call(src, tile_group, n_tiles, h2s, gs, wgu, wd):
    n_tiles_max = tile_group.shape[0]
    return pl.pallas_call(
        _moe_kernel,
        grid_spec=pltpu.PrefetchScalarGridSpec(
            num_scalar_prefetch=3,
            grid=(n_tiles_max,),
            in_specs=[
                pl.BlockSpec(memory_space=pl.ANY), pl.BlockSpec(memory_space=pl.ANY),
                pl.BlockSpec((EXPERTS_PER_GROUP, D_MODEL, 2 * D_EXPERT), lambda i, s, tg, nt: (tg[i], 0, 0)),
                pl.BlockSpec((EXPERTS_PER_GROUP, D_EXPERT, D_MODEL), lambda i, s, tg, nt: (tg[i], 0, 0)),
            ],
            out_specs=pl.BlockSpec((TM, D_MODEL), lambda i, s, tg, nt: (i, 0)),
            scratch_shapes=[
                pltpu.VMEM((2, TM, D_MODEL), BF16), pltpu.VMEM((2, TM, ROUTER_W), F32),
                pltpu.SemaphoreType.DMA((2,)), pltpu.SemaphoreType.DMA((2,)),
            ],
        ),
        out_shape=jax.ShapeDtypeStruct((n_tiles_max * TM, D_MODEL), F32),
        compiler_params=pltpu.CompilerParams(
            dimension_semantics=("arbitrary",), vmem_limit_bytes=V7X_VMEM_LIMIT),
    )(src, tile_group, n_tiles, h2s, gs, wgu, wd)


def _final_call(dest, x1, rinfo, ys, gfin, blocks_p):
    n_blocks = x1.shape[0] // TB
    return pl.pallas_call(
        functools.partial(_final_kernel, blocks_p),
        grid_spec=pltpu.PrefetchScalarGridSpec(
            num_scalar_prefetch=1,
            grid=(n_blocks,),
            in_specs=[
                pl.BlockSpec((TB, D_MODEL), lambda i, d: (i, 0)),
                pl.BlockSpec((TB, ROUTER_W), lambda i, d: (i, 0)),
                pl.BlockSpec(memory_space=pl.ANY),
                pl.BlockSpec((1, D_MODEL), lambda i, d: (0, 0)),
            ],
            out_specs=[
                pl.BlockSpec((TB, D_MODEL), lambda i, d: (jnp.minimum(i, blocks_p - 1), 0)),
                pl.BlockSpec((TB, D_MODEL), lambda i, d: (jnp.maximum(i - blocks_p, 0), 0)),
            ],
            scratch_shapes=[pltpu.VMEM((2, SORT_W, D_MODEL), F32), pltpu.SemaphoreType.DMA((2,))],
        ),
        out_shape=[jax.ShapeDtypeStruct((blocks_p * TB, D_MODEL), F32),
                   jax.ShapeDtypeStruct(((n_blocks - blocks_p) * TB, D_MODEL), F32)],
        compiler_params=pltpu.CompilerParams(
            dimension_semantics=("arbitrary",), vmem_limit_bytes=V7X_VMEM_LIMIT),
    )(dest, x1, rinfo, ys, gfin)


def kernel(x_prompt, x_sample, state_ret, state_hgrn, norm_mix_g, w_in, hgrn_lb_logits, hgrn_norm_g, w_out,
           norm_ffn_g, w_router_group, b_router_group, w_router_expert, b_router_expert, w_exp_gate, w_exp_up,
           w_exp_down, norm_final_g):
    depth = w_in.shape[0]
    assert depth == 1 and hgrn_lb_logits.shape[0] == 2, "single-layer configuration only"
    bp, seq, d = x_prompt.shape
    db, dec_len, _ = x_sample.shape
    assert bp == 1 and d == D_MODEL and dec_len == CHUNK and seq % 256 == 0 and db % 4 == 0

    wr = jnp.zeros((D_MODEL, ROUTER_W), F32)
    wr = wr.at[:, :N_GROUPS].set(w_router_group[0]).at[:, N_GROUPS:N_GROUPS + N_EXPERTS].set(w_router_expert[0])
    br = jnp.zeros((1, ROUTER_W), F32)
    br = br.at[0, :N_GROUPS].set(b_router_group[0]).at[0, N_GROUPS:N_GROUPS + N_EXPERTS].set(b_router_expert[0])
    params = (norm_mix_g[0][None, :], w_in[0].astype(BF16), hgrn_lb_logits,
              jnp.tile(hgrn_norm_g[0], N_HEADS)[None, :], w_out[0].astype(BF16), norm_ffn_g[0][None, :], wr, br)

    zeros_state = jnp.zeros((1, N_PAIRS, PAIR_W, PAIR_W), F32)
    blocks_p = seq // TB
    blocks_s = db * dec_len // TB
    n_blocks = blocks_p + blocks_s
    *shared, ret_p, hg_p = _mixer_call(
        x_prompt.reshape(seq, d), zeros_state, zeros_state, params, seg_len=TB, carry=True,
        block_offset=0, total_blocks=n_blocks)
    ret0_s = _to_block_diag(state_ret[0])
    hg0_s = _to_block_diag(jnp.swapaxes(state_hgrn[0], -1, -2))
    x1, h2s, gs, rinfo, info, ret_s, hg_s = _mixer_call(
        x_sample.reshape(db * dec_len, d), ret0_s, hg0_s, params, seg_len=CHUNK, carry=False,
        block_offset=blocks_p, total_blocks=n_blocks, shared=tuple(shared))

    n_tiles_max = -(-n_blocks * SLABS_PER_BLOCK // TILE_SLABS) + N_GROUPS
    src, dest, tile_group, n_tiles = _dispatch_tables(info, n_tiles_max)
    wgu = jnp.concatenate([w_exp_gate[0], w_exp_up[0]], axis=-1).astype(BF16)
    ys = _moe_call(src, tile_group, n_tiles, h2s, gs, wgu, w_exp_down[0].astype(BF16))
    y_p, y_s = _final_call(dest, x1, rinfo, ys, norm_final_g[None, :], blocks_p)

    def states(r, s):
        return (_from_block_diag(r)[None], jnp.swapaxes(_from_block_diag(s), -1, -2)[None])

    rp, sp = states(ret_p, hg_p)
    rs, ss = states(ret_s, hg_s)
    return (y_p.reshape(bp, seq, d), y_s.reshape(db, dec_len, d), rp, sp, rs, ss)
```

```python
import functools

import numpy as np
import jax
import jax.numpy as jnp
from jax import lax
from jax.experimental import pallas as pl
from jax.experimental.pallas import tpu as pltpu

F32 = jnp.float32
BF16 = jnp.bfloat16

D_MODEL = 1024
N_HEADS = 8
D_HEAD = 64
GROUP_W = N_HEADS * D_HEAD
N_PAIRS = N_HEADS // 2
PAIR_W = 2 * D_HEAD
CHUNK = 64
SUB = 16
N_SUB = CHUNK // SUB
PAST_LEN = 2048
ROPE_BASE = 10000.0
RMS_EPS = 1e-6
LOG2E = 1.4426950408889634
N_GROUPS = 4
EXPERTS_PER_GROUP = 8
N_EXPERTS = N_GROUPS * EXPERTS_PER_GROUP
D_EXPERT = 256
ROUTER_W = 128
V7X_VMEM_LIMIT = 60 * 1024 * 1024
TB = 256
SLAB = 16
SLABS_PER_BLOCK = TB // SLAB + N_GROUPS
SORT_ROWS = SLABS_PER_BLOCK * SLAB
SORT_W = 384
TILE_SLABS = 32
TM = TILE_SLABS * SLAB

C_QR, C_KR, C_VR, C_GR, C_QG, C_FG, C_VG, C_GG = (i * GROUP_W for i in range(8))


def _dot(a, b):
    return jnp.dot(a, b, preferred_element_type=F32)


def _dot_nt(a, b):
    return lax.dot_general(a, b, (((1,), (1,)), ((), ())), preferred_element_type=F32)


def _dot_tn(a, b):
    return lax.dot_general(a, b, (((0,), (0,)), ((), ())), preferred_element_type=F32)


def _split(x, n):
    parts = []
    for _ in range(n):
        p = x.astype(BF16)
        parts.append(p)
        x = x - p.astype(F32)
    return parts


def _group_sum(x, ones_ref):
    xb = x.astype(BF16)
    return jnp.concatenate([_dot(xb[:, c * 256:(c + 1) * 256], ones_ref[...]) for c in range(2)], axis=1)


class _ColumnGroups:
    def __init__(self, bufs):
        self.bufs = bufs

    def _locate(self, idx):
        rows, cols = idx
        g = cols.start // GROUP_W
        assert (cols.stop - 1) // GROUP_W == g
        return self.bufs[g], (rows, slice(cols.start - g * GROUP_W, cols.stop - g * GROUP_W))

    def __getitem__(self, idx):
        buf, at = self._locate(idx)
        return buf[at]

    def __setitem__(self, idx, value):
        buf, at = self._locate(idx)
        buf[at] = value


def _silu(x):
    return x * jax.nn.sigmoid(x)


N_MIXER_INPUTS = 21


def _mixer_kernel(cfg, *refs):
    n_own, n_fill, n_alias = cfg[-3:]
    i = pl.program_id(0)

    @pl.when(i < n_own)
    def _():
        _mixer_body(cfg, *refs)

    if n_fill:
        @pl.when(i >= n_own)
        def _():
            for out_ref in refs[N_MIXER_INPUTS + n_alias:N_MIXER_INPUTS + n_alias + 5]:
                out_ref[...] = jnp.zeros_like(out_ref)


def _mixer_body(cfg, x_ref, gmix_ref, win_ref, invf_ref, sgn_ref, lbl_ref, hgn_ref, wout_ref, gffn_ref,
                wr_ref, br_ref, tri_ref, dret_ref, xi_ref, zeta_ref, gls_ref, ones_ref, cmask_ref, stri_ref,
                ret0_ref, hg0_ref, *rest):
    LS, NSEG, carry, _, _, n_alias = cfg
    (x1_ref, h2s_ref, gs_ref, rinfo_ref, info_ref, retout_ref, hgout_ref,
     *proj_bufs, kbuf, obuf, dbuf, qw, kw, qx, kx, qin_buf, kout_buf, pbuf, cos_t, sin_t, rst, sst) = rest[n_alias:]
    proj = _ColumnGroups(proj_bufs)
    NCH = LS // CHUNK
    i = pl.program_id(0)
    lane128 = lax.broadcasted_iota(jnp.int32, (1, PAIR_W), 1)
    head_a = lane128 < D_HEAD
    bd_mask = (lax.broadcasted_iota(jnp.int32, (PAIR_W, PAIR_W), 0) < D_HEAD) == head_a

    if carry:
        @pl.when(i == 0)
        def _():
            rst[...] = ret0_ref[...]
            sst[...] = hg0_ref[...]
    else:
        rst[...] = ret0_ref[...]
        sst[...] = hg0_ref[...]

    @pl.when(i == 0)
    def _():
        row = lax.broadcasted_iota(jnp.int32, (TB, 1), 0)
        ang_row = (row & (LS - 1)).astype(F32) * invf_ref[...]
        cos_t[...] = jnp.cos(ang_row)
        sin_t[...] = jnp.sin(ang_row)
        qw[...] = jnp.zeros_like(qw)
        kw[...] = jnp.zeros_like(kw)
        if NCH > 1:
            qx[...] = jnp.zeros_like(qx)
            kx[...] = jnp.zeros_like(kx)

    x = x_ref[...]
    h = x * lax.rsqrt(jnp.mean(x * x, axis=-1, keepdims=True) + RMS_EPS) * gmix_ref[...]
    hb = h.astype(BF16)

    def project(c0, c1):
        for c in range(c0, c1, GROUP_W):
            proj[:, c:c + GROUP_W] = _dot(hb, win_ref[:, c:c + GROUP_W])

    project(C_QG, C_GG)
    project(C_QR, C_VR)

    la = lbl_ref[0:1, :]
    lb_ = lbl_ref[1:2, :]
    lmax = jnp.maximum(la, lb_)
    ea = jnp.exp(la - lmax)
    lbv = ea / (ea + jnp.exp(lb_ - lmax))
    qg = proj[:, C_QG:C_QG + GROUP_W]
    proj[:, C_QG:C_QG + GROUP_W] = _silu(qg)
    f = lbv + (1.0 - lbv) * jax.nn.sigmoid(proj[:, C_FG:C_FG + GROUP_W])
    kbuf[...] = 1.0 - f
    logf = jnp.log(f)
    for sg in range(NSEG):
        rows = slice(sg * LS, (sg + 1) * LS)
        parts = _split(logf[rows], 3)
        proj[rows, C_FG:C_FG + GROUP_W] = sum(_dot(tri_ref[...], p) for p in parts)

    project(C_VR, C_QG)
    project(C_GG, C_GG + GROUP_W)

    start = jnp.full((8, PAIR_W), i * TB if carry else PAST_LEN, jnp.int32).astype(F32)
    ang0 = (start * invf_ref[...])[0:1]
    c0 = jnp.cos(ang0)
    s0 = jnp.sin(ang0)
    cos = cos_t[...] * c0 - sin_t[...] * s0
    sin = (sin_t[...] * c0 + cos_t[...] * s0) * sgn_ref[...]
    first_half = (lane128 & (D_HEAD - 1)) < (D_HEAD // 2)
    for blk in range(2 * N_PAIRS):
        cols = slice(blk * PAIR_W, (blk + 1) * PAIR_W)
        xx = proj[:, cols]
        partner = jnp.where(first_half, pltpu.roll(xx, PAIR_W - D_HEAD // 2, 1), pltpu.roll(xx, D_HEAD // 2, 1))
        r = xx * cos + partner * sin
        if blk < N_PAIRS:
            r = r * (D_HEAD ** -0.5)
        proj[:, cols] = r

    HALF = SUB // 2
    riota = lax.broadcasted_iota(jnp.int32, (HALF, 1), 0)
    P_ROWS = HALF * SUB + HALF * HALF

    def diag_chunk(c):
        for sb in range(N_SUB):
            rs = c * CHUNK + sb * SUB
            q16 = proj[rs:rs + SUB, C_QG:C_QG + GROUP_W]
            b16 = proj[rs:rs + SUB, C_FG:C_FG + GROUP_W] * LOG2E
            k16 = kbuf[rs:rs + SUB, :]
            base = sb * P_ROWS
            upper = []
            for s in range(SUB):
                ks = k16[s:s + 1]
                bs = b16[s:s + 1]
                hi = q16[HALF:] * ks * jnp.exp2(b16[HALF:] - bs)
                if s < HALF:
                    lo = q16[:HALF] * ks * jnp.exp2(b16[:HALF] - bs)
                    lo = jnp.where(riota >= s, lo, 0.0)
                    pbuf[base + s * SUB:base + (s + 1) * SUB, :] = jnp.concatenate([lo, hi], axis=0).astype(BF16)
                else:
                    upper.append(jnp.where(riota >= s - HALF, hi, 0.0))
                    if len(upper) == 2:
                        at = base + HALF * SUB + (s - HALF - 1) * HALF
                        pbuf[at:at + SUB, :] = jnp.concatenate(upper, axis=0).astype(BF16)
                        upper = []
        sc = [_dot(pbuf[:, hh * 256:(hh + 1) * 256], ones_ref[...]) for hh in range(2)]
        for sb in range(N_SUB):
            rs = c * CHUNK + sb * SUB
            v16 = proj[rs:rs + SUB, C_VG:C_VG + GROUP_W]
            base = sb * P_ROWS
            acc_lo = [jnp.zeros((HALF, 256), F32) for _ in range(2)]
            acc_hi = [jnp.zeros((HALF, 256), F32) for _ in range(2)]
            for s in range(SUB):
                for hh in range(2):
                    vs = v16[s:s + 1, hh * 256:(hh + 1) * 256]
                    if s < HALF:
                        at = base + s * SUB
                        acc_lo[hh] = acc_lo[hh] + sc[hh][at:at + HALF] * vs
                        acc_hi[hh] = acc_hi[hh] + sc[hh][at + HALF:at + SUB] * vs
                    else:
                        at = base + HALF * SUB + (s - HALF) * HALF
                        acc_hi[hh] = acc_hi[hh] + sc[hh][at:at + HALF] * vs
            dbuf[rs:rs + HALF, :] = jnp.concatenate(acc_lo, axis=1)
            dbuf[rs + HALF:rs + SUB, :] = jnp.concatenate(acc_hi, axis=1)

    for sg in range(NSEG):
        r0 = sg * LS
        rows = slice(r0, r0 + LS)
        st = 0 if carry else sg
        qg_s = proj[rows, C_QG:C_QG + GROUP_W]
        kg_s = kbuf[rows, :]
        b_s = proj[rows, C_FG:C_FG + GROUP_W]

        for c in range(NCH):
            for j in range(N_SUB - 1):
                e = c * CHUNK + j * SUB + SUB - 1
                bref = b_s[e:e + 1, :]
                q_rows = slice(e + 1, (c + 1) * CHUNK)
                k_rows = slice(e + 1 - SUB, e + 1)
                qt = (qg_s[q_rows] * jnp.exp(b_s[q_rows] - bref)).astype(BF16)
                kt = (kg_s[k_rows] * jnp.exp(bref - b_s[k_rows])).astype(BF16)
                for p in range(N_PAIRS):
                    dst = slice((p * (N_SUB - 1) + j) * PAIR_W, (p * (N_SUB - 1) + j + 1) * PAIR_W)
                    src = slice(p * PAIR_W, (p + 1) * PAIR_W)
                    qw[r0 + q_rows.start:r0 + q_rows.stop, dst] = qt[:, src]
                    kw[r0 + k_rows.start:r0 + k_rows.stop, dst] = kt[:, src]
        for c in range(NCH - 1):
            e = c * CHUNK + CHUNK - 1
            bref = b_s[e:e + 1, :]
            q_rows = slice(e + 1, LS)
            k_rows = slice(c * CHUNK, e + 1)
            qt = (qg_s[q_rows] * jnp.exp(b_s[q_rows] - bref)).astype(BF16)
            kt = (kg_s[k_rows] * jnp.exp(bref - b_s[k_rows])).astype(BF16)
            for p in range(N_PAIRS):
                dst = slice((p * (NCH - 1) + c) * PAIR_W, (p * (NCH - 1) + c + 1) * PAIR_W)
                src = slice(p * PAIR_W, (p + 1) * PAIR_W)
                qx[r0 + q_rows.start:r0 + q_rows.stop, dst] = qt[:, src]
                kx[r0 + k_rows.start:r0 + k_rows.stop, dst] = kt[:, src]

        b_last = b_s[LS - 1:LS, :]
        qin_buf[rows, :] = (qg_s * jnp.exp(b_s)).astype(BF16)
        kout_buf[rows, :] = (kg_s * jnp.exp(b_last - b_s)).astype(BF16)

    def pair_unit(sg, p):
        r0 = sg * LS
        rows = slice(r0, r0 + LS)
        st = 0 if carry else sg
        pc = slice(p * PAIR_W, (p + 1) * PAIR_W)

        qb = proj[rows, C_QR + p * PAIR_W:C_QR + (p + 1) * PAIR_W].astype(BF16)
        kr = proj[rows, C_KR + p * PAIR_W:C_KR + (p + 1) * PAIR_W]
        kb = kr.astype(BF16)
        vb = proj[rows, C_VR + p * PAIR_W:C_VR + (p + 1) * PAIR_W].astype(BF16)
        zero = jnp.zeros_like(qb)
        s_a = _dot_nt(jnp.where(head_a, qb, zero), kb) * dret_ref[2 * p]
        s_b = _dot_nt(jnp.where(head_a, zero, qb), kb) * dret_ref[2 * p + 1]
        o_r = jnp.where(head_a, _dot(s_a.astype(BF16), vb), _dot(s_b.astype(BF16), vb))
        r_old = rst[st, p]
        o_r = o_r + _dot(qb, r_old.astype(BF16)) * xi_ref[:, pc]
        u = _dot_tn((kr * zeta_ref[:, pc]).astype(BF16), vb)
        r_new = r_old * gls_ref[:, pc] + jnp.where(bd_mask, u, 0.0)
        obuf[rows, pc] = o_r

        vg = proj[rows, C_VG + p * PAIR_W:C_VG + (p + 1) * PAIR_W].astype(BF16)
        wc = slice(p * (N_SUB - 1) * PAIR_W, (p + 1) * (N_SUB - 1) * PAIR_W)
        qw_p = qw[rows, wc]
        kw_p = kw[rows, wc]
        lane_w = lax.broadcasted_iota(jnp.int32, (1, qw_p.shape[1]), 1)
        head_a_w = (lane_w & (PAIR_W - 1)) < D_HEAD
        zw = jnp.zeros_like(qw_p)
        g_a = _dot_nt(jnp.where(head_a_w, qw_p, zw), kw_p)
        g_b = _dot_nt(jnp.where(head_a_w, zw, qw_p), kw_p)
        if NCH > 1:
            g_a = g_a * cmask_ref[...]
            g_b = g_b * cmask_ref[...]
            xc = slice(p * (NCH - 1) * PAIR_W, (p + 1) * (NCH - 1) * PAIR_W)
            qx_p = qx[rows, xc]
            kx_p = kx[rows, xc]
            lane_x = lax.broadcasted_iota(jnp.int32, (1, qx_p.shape[1]), 1)
            head_a_x = (lane_x & (PAIR_W - 1)) < D_HEAD
            zx = jnp.zeros_like(qx_p)
            g_a = g_a + _dot_nt(jnp.where(head_a_x, qx_p, zx), kx_p)
            g_b = g_b + _dot_nt(jnp.where(head_a_x, zx, qx_p), kx_p)
        o_g = jnp.where(head_a, _dot(g_a.astype(BF16), vg), _dot(g_b.astype(BF16), vg))
        s_old = sst[st, p]
        o_g = o_g + _dot_nt(qin_buf[rows, pc], s_old.astype(BF16))
        ut = _dot_tn(vg, kout_buf[rows, pc])
        s_decay = jnp.exp(proj[r0 + LS - 1:r0 + LS, C_FG + p * PAIR_W:C_FG + (p + 1) * PAIR_W])
        s_new = s_old * s_decay + jnp.where(bd_mask, ut, 0.0)
        gc = slice(GROUP_W + p * PAIR_W, GROUP_W + (p + 1) * PAIR_W)
        obuf[rows, gc] = o_g

        if carry:
            rst[st, p] = r_new
            sst[st, p] = s_new
        retout_ref[st, p] = r_new
        hgout_ref[st, p] = s_new

    units = [(sg, p) for sg in range(NSEG) for p in range(N_PAIRS)]
    per_chunk = len(units) // (TB // CHUNK)
    for c in range(TB // CHUNK):
        for sg, p in units[c * per_chunk:(c + 1) * per_chunk]:
            pair_unit(sg, p)
        diag_chunk(c)

    o_r = obuf[:, 0:GROUP_W]
    mu = _group_sum(o_r, ones_ref) * (1.0 / D_HEAD)
    dlt = o_r - mu
    var = _group_sum(dlt * dlt, ones_ref) * (1.0 / D_HEAD)
    y_r = dlt * lax.rsqrt(var + RMS_EPS) * _silu(proj[:, C_GR:C_GR + GROUP_W])
    o_g = obuf[:, GROUP_W:2 * GROUP_W] + dbuf[...]
    ms = _group_sum(o_g * o_g, ones_ref) * (1.0 / D_HEAD)
    y_g = o_g * lax.rsqrt(ms + RMS_EPS) * hgn_ref[...] * _silu(proj[:, C_GG:C_GG + GROUP_W])
    mix = jnp.concatenate([y_r, y_g], axis=1).astype(BF16)
    x1 = x + _dot(mix, wout_ref[...])
    x1_ref[...] = x1

    h2 = x1 * lax.rsqrt(jnp.mean(x1 * x1, axis=-1, keepdims=True) + RMS_EPS) * gffn_ref[...]
    h_hi, h_lo = _split(h2, 2)
    w_hi, w_lo = _split(wr_ref[...], 2)
    logits = _dot(h_hi, w_hi) + _dot(h_hi, w_lo) + _dot(h_lo, w_hi) + br_ref[...]
    lane = lax.broadcasted_iota(jnp.int32, (1, ROUTER_W), 1).astype(F32)
    neg = -jnp.inf
    no_lane = float(ROUTER_W)
    gl = jnp.where(lane < N_GROUPS, logits, neg)
    gmax = jnp.max(gl, axis=-1, keepdims=True)
    g_idx = jnp.min(jnp.where(gl == gmax, lane, no_lane), axis=-1, keepdims=True)
    prob_g = 1.0 / jnp.sum(jnp.exp(gl - gmax), axis=-1, keepdims=True)
    e_lo = N_GROUPS + EXPERTS_PER_GROUP * g_idx
    el = jnp.where((lane >= e_lo) & (lane < e_lo + EXPERTS_PER_GROUP), logits, neg)
    v1 = jnp.max(el, axis=-1, keepdims=True)
    i1 = jnp.min(jnp.where(el == v1, lane, no_lane), axis=-1, keepdims=True)
    el2 = jnp.where(lane == i1, neg, el)
    v2 = jnp.max(el2, axis=-1, keepdims=True)
    i2 = jnp.min(jnp.where(el2 == v2, lane, no_lane), axis=-1, keepdims=True)
    t = jnp.exp(v2 - v1)
    p1 = 1.0 / (1.0 + t)
    p2 = t * p1
    gate = jnp.where(lane == i1, prob_g * p1, 0.0) + jnp.where(lane == i2, prob_g * p2, 0.0)

    onehot = jnp.where(lane == g_idx, 1.0, 0.0)
    before = _dot(stri_ref[...], onehot.astype(BF16))
    count = jnp.sum(onehot, axis=0, keepdims=True)
    padded = jnp.floor((count + (SLAB - 1.0)) * (1.0 / SLAB)) * SLAB
    padded8 = jnp.broadcast_to(padded, (8, ROUTER_W))
    seg_start = sum(jnp.where(lane >= k, pltpu.roll(padded8, k, 1), 0.0) for k in range(1, N_GROUPS))[0:1]
    rank = jnp.sum(onehot * (seg_start + before), axis=-1, keepdims=True)
    col = lax.broadcasted_iota(jnp.int32, (1, SORT_W), 1).astype(F32)
    place = jnp.where(col == rank, 1.0, 0.0).astype(BF16)
    h2s_ref[...] = _dot_tn(place, h2.astype(BF16))[:SORT_ROWS].astype(BF16)
    gs_ref[...] = sum(_dot_tn(place, part) for part in _split(gate, 3))[:SORT_ROWS]
    rinfo_ref[...] = jnp.where(lane == 0, rank, 0.0)
    slab_lo = (lane * SLAB).astype(F32)
    slab_group = jnp.full((1, ROUTER_W), -1.0, F32)
    for g in range(N_GROUPS):
        s_g = jnp.sum(jnp.where(lane == g, seg_start, 0.0), axis=-1, keepdims=True)
        n_g = jnp.sum(jnp.where(lane == g, padded, 0.0), axis=-1, keepdims=True)
        slab_group = jnp.where((slab_lo >= s_g) & (slab_lo < s_g + n_g), float(g), slab_group)
    info_ref[0] = jnp.broadcast_to(slab_group.astype(jnp.int32), (8, ROUTER_W))


def _slab_gather(table_ref, first, n_slabs, srcs, bufs, sems, slot, *, wait):
    for j in range(n_slabs):
        n = table_ref[first + j]
        row = pl.multiple_of(jnp.maximum(n, 0) * SLAB, SLAB)
        dst_rows = pl.ds(j * SLAB, SLAB)

        @pl.when(n >= 0)
        def _():
            for src, buf, sem in zip(srcs, bufs, sems):
                cp = pltpu.make_async_copy(src.at[pl.ds(row, SLAB)], buf.at[slot, dst_rows], sem.at[slot])
                if wait:
                    cp.wait()
                else:
                    cp.start()

        if not wait:
            @pl.when(n < 0)
            def _():
                for buf in bufs:
                    buf[slot, dst_rows] = jnp.zeros((SLAB,) + buf.shape[2:], buf.dtype)


def _moe_kernel(src_ref, tg_ref, nt_ref, h2s_hbm, gs_hbm, wgu_ref, wd_ref, ys_ref, hbuf, gbuf, sem_h, sem_g):
    i = pl.program_id(0)
    n_tiles = nt_ref[0]
    slot = lax.rem(i, 2)
    gather = functools.partial(_slab_gather, src_ref, srcs=(h2s_hbm, gs_hbm), bufs=(hbuf, gbuf),
                               sems=(sem_h, sem_g), n_slabs=TILE_SLABS)

    @pl.when(i == 0)
    def _():
        gather(first=0, slot=0, wait=False)

    @pl.when(i < n_tiles)
    def _():
        gather(first=i * TILE_SLABS, slot=slot, wait=True)

        @pl.when(i + 1 < n_tiles)
        def _():
            gather(first=(i + 1) * TILE_SLABS, slot=1 - slot, wait=False)

        hb = hbuf[slot]
        gates = gbuf[slot]
        lane = lax.broadcasted_iota(jnp.int32, (1, ROUTER_W), 1)
        first_lane = N_GROUPS + tg_ref[i] * EXPERTS_PER_GROUP
        acc = jnp.zeros((TM, D_MODEL), F32)
        for e in range(EXPERTS_PER_GROUP):
            ab = _dot(hb, wgu_ref[e])
            he = (_silu(ab[:, :D_EXPERT]) * ab[:, D_EXPERT:]).astype(BF16)
            gcol = jnp.sum(jnp.where(lane == first_lane + e, gates, 0.0), axis=-1, keepdims=True)
            acc = acc + gcol * _dot(he, wd_ref[e])
        ys_ref[...] = acc.astype(BF16)

    @pl.when(i >= n_tiles)
    def _():
        ys_ref[...] = jnp.zeros_like(ys_ref)


def _final_kernel(blocks_p, dest_ref, x1_ref, rinfo_ref, ys_hbm, gfin_ref, yp_ref, ysmp_ref, ybuf, sem):
    i = pl.program_id(0)
    slot = lax.rem(i, 2)
    gather = functools.partial(_slab_gather, dest_ref, srcs=(ys_hbm,), bufs=(ybuf,), sems=(sem,),
                               n_slabs=SLABS_PER_BLOCK)

    @pl.when(i == 0)
    def _():
        ybuf[:, SORT_ROWS:SORT_W, :] = jnp.zeros((2, SORT_W - SORT_ROWS, D_MODEL), BF16)
        gather(first=0, slot=0, wait=False)

    gather(first=i * SLABS_PER_BLOCK, slot=slot, wait=True)

    @pl.when(i + 1 < pl.num_programs(0))
    def _():
        gather(first=(i + 1) * SLABS_PER_BLOCK, slot=1 - slot, wait=False)

    rank = rinfo_ref[:, 0:1]
    col = lax.broadcasted_iota(jnp.int32, (1, SORT_W), 1).astype(F32)
    place = jnp.where(col == rank, 1.0, 0.0).astype(BF16)
    moe = _dot(place, ybuf[slot])
    xo = x1_ref[...] + moe
    y = xo * lax.rsqrt(jnp.mean(xo * xo, axis=-1, keepdims=True) + RMS_EPS) * gfin_ref[...]

    @pl.when(i < blocks_p)
    def _():
        yp_ref[...] = y

    @pl.when(i >= blocks_p)
    def _():
        ysmp_ref[...] = y


def _to_block_diag(s):
    n = s.shape[0]
    s = s.reshape(n, N_PAIRS, 2, D_HEAD, D_HEAD)
    z = jnp.zeros_like(s[:, :, 0])
    top = jnp.concatenate([s[:, :, 0], z], axis=-1)
    bot = jnp.concatenate([z, s[:, :, 1]], axis=-1)
    return jnp.concatenate([top, bot], axis=-2)


def _from_block_diag(b):
    n = b.shape[0]
    return jnp.stack([b[:, :, :D_HEAD, :D_HEAD], b[:, :, D_HEAD:, D_HEAD:]], axis=2).reshape(
        n, N_HEADS, D_HEAD, D_HEAD)


def _const_spec(shape):
    nd = len(shape)
    return pl.BlockSpec(shape, lambda i: (0,) * nd)


def _mixer_call(x2d, ret0, hg0, params, *, seg_len, carry, block_offset, total_blocks, shared=()):
    (gmix, win, lbl, hgn, wout, gffn, wr, br) = params
    T = x2d.shape[0]
    LS = seg_len
    n_seg = TB // LS
    NCH = LS // CHUNK
    n_states = ret0.shape[0]
    n_own = T // TB
    n_fill = 0 if shared else total_blocks - n_own - block_offset
    grid = n_own + n_fill

    f32 = np.float32
    lg = np.log1p(-(f32(2.0) ** (f32(-5.0) - np.arange(N_HEADS, dtype=f32)))).astype(f32)
    tt = np.arange(LS, dtype=f32)
    ch = np.arange(LS) // CHUNK
    dret = np.exp(np.abs(tt[:, None] - tt[None, :])[None] * lg[:, None, None]).astype(f32)
    dret = np.where((ch[None, :] <= ch[:, None])[None], dret, f32(0.0))
    lg_lane = np.repeat(lg, D_HEAD)[None, :]
    xi = np.exp((tt[:, None] + f32(1.0)) * lg_lane).astype(f32)
    zeta = np.exp((f32(LS - 1.0) - tt)[:, None] * lg_lane).astype(f32)
    gls = np.exp(f32(LS) * lg_lane).astype(f32)
    half = D_HEAD // 2
    inv_freq = ROPE_BASE ** (-jnp.arange(half, dtype=F32) / half)
    invf = jnp.tile(inv_freq, PAIR_W // half)[None, :]
    sgn = np.where((np.arange(PAIR_W) % D_HEAD) < half, f32(-1.0), f32(1.0))[None, :]
    tri = jnp.asarray(np.tril(np.ones((LS, LS), np.float32)), BF16)
    lane_head = np.arange(256) // D_HEAD
    ones_bd = jnp.asarray((lane_head[:, None] == lane_head[None, :]).astype(np.float32), BF16)
    cmask = jnp.asarray((ch[:, None] == ch[None, :]).astype(np.float32))

    if carry:
        state_spec = pl.BlockSpec((1, N_PAIRS, PAIR_W, PAIR_W), lambda i: (0, 0, 0, 0))
    else:
        state_spec = pl.BlockSpec((n_seg, N_PAIRS, PAIR_W, PAIR_W), lambda i: (i, 0, 0, 0))
    stri = jnp.asarray(np.tril(np.ones((TB, TB), np.float32), -1), BF16)
    in_row_spec = pl.BlockSpec((TB, D_MODEL), lambda i: (jnp.minimum(i, n_own - 1), 0))
    out_row_spec = lambda rows, w: pl.BlockSpec((rows, w), lambda i: (i + block_offset, 0))
    nw = (N_SUB - 1) * N_PAIRS * PAIR_W
    nx = max(NCH - 1, 1) * N_PAIRS * PAIR_W
    n_in = N_MIXER_INPUTS

    kern = functools.partial(_mixer_kernel, (LS, n_seg, carry, n_own, n_fill, len(shared)))
    return pl.pallas_call(
        kern,
        grid=(grid,),
        in_specs=[
            in_row_spec,
            _const_spec((1, D_MODEL)), _const_spec(win.shape), _const_spec((1, PAIR_W)), _const_spec((1, PAIR_W)),
            _const_spec(lbl.shape), _const_spec((1, GROUP_W)), _const_spec(wout.shape), _const_spec((1, D_MODEL)),
            _const_spec(wr.shape), _const_spec((1, ROUTER_W)),
            _const_spec((LS, LS)), _const_spec((N_HEADS, LS, LS)), _const_spec((LS, GROUP_W)),
            _const_spec((LS, GROUP_W)), _const_spec((1, GROUP_W)), _const_spec((256, 256)), _const_spec((LS, LS)),
            _const_spec((TB, TB)),
            state_spec, state_spec,
        ] + [pl.BlockSpec(memory_space=pl.ANY)] * len(shared),
        out_specs=[out_row_spec(TB, D_MODEL), out_row_spec(SORT_ROWS, D_MODEL), out_row_spec(SORT_ROWS, ROUTER_W),
                   out_row_spec(TB, ROUTER_W),
                   pl.BlockSpec((1, 8, ROUTER_W), lambda i: (i + block_offset, 0, 0)),
                   state_spec, state_spec],
        out_shape=[
            jax.ShapeDtypeStruct((total_blocks * TB, D_MODEL), F32),
            jax.ShapeDtypeStruct((total_blocks * SORT_ROWS, D_MODEL), BF16),
            jax.ShapeDtypeStruct((total_blocks * SORT_ROWS, ROUTER_W), F32),
            jax.ShapeDtypeStruct((total_blocks * TB, ROUTER_W), F32),
            jax.ShapeDtypeStruct((total_blocks, 8, ROUTER_W), jnp.int32),
            jax.ShapeDtypeStruct((n_states, N_PAIRS, PAIR_W, PAIR_W), F32),
            jax.ShapeDtypeStruct((n_states, N_PAIRS, PAIR_W, PAIR_W), F32),
        ],
        input_output_aliases={n_in + k: k for k in range(len(shared))},
        scratch_shapes=[
            *[pltpu.VMEM((TB, GROUP_W), F32) for _ in range(8)],
            pltpu.VMEM((TB, GROUP_W), F32),
            pltpu.VMEM((TB, 2 * GROUP_W), F32),
            pltpu.VMEM((TB, GROUP_W), F32),
            pltpu.VMEM((TB, nw), BF16), pltpu.VMEM((TB, nw), BF16),
            pltpu.VMEM((TB, nx), BF16), pltpu.VMEM((TB, nx), BF16),
            pltpu.VMEM((TB, GROUP_W), BF16), pltpu.VMEM((TB, GROUP_W), BF16),
            pltpu.VMEM((N_SUB * (SUB // 2) * (SUB + SUB // 2), GROUP_W), BF16),
            pltpu.VMEM((TB, PAIR_W), F32), pltpu.VMEM((TB, PAIR_W), F32),
            pltpu.VMEM((1 if carry else n_seg, N_PAIRS, PAIR_W, PAIR_W), F32),
            pltpu.VMEM((1 if carry else n_seg, N_PAIRS, PAIR_W, PAIR_W), F32),
        ],
        compiler_params=pltpu.CompilerParams(
            dimension_semantics=("arbitrary",), vmem_limit_bytes=V7X_VMEM_LIMIT),
    )(x2d, gmix, win, invf, sgn, lbl, hgn, wout, gffn, wr, br, tri, dret, xi, zeta, gls, ones_bd, cmask, stri,
      ret0, hg0, *shared)


def _dispatch_tables(info, n_tiles_max):
    slab_group = info[:, 0, :SLABS_PER_BLOCK].reshape(-1)
    n_slabs = slab_group.shape[0]
    valid = slab_group >= 0
    onehot = (slab_group[:, None] == jnp.arange(N_GROUPS, dtype=jnp.int32)[None, :]).astype(jnp.int32)
    within = jnp.cumsum(onehot, axis=0) - onehot
    tiles = (jnp.sum(onehot, axis=0) + TILE_SLABS - 1) // TILE_SLABS
    tile_end = jnp.cumsum(tiles)
    g = jnp.clip(slab_group, 0, N_GROUPS - 1)
    dest = (tile_end - tiles)[g] * TILE_SLABS + jnp.take_along_axis(within, g[:, None], axis=1)[:, 0]
    dest = jnp.where(valid, dest, -1).astype(jnp.int32)
    n_slots = n_tiles_max * TILE_SLABS
    src = jnp.full((n_slots,), -1, jnp.int32).at[jnp.where(valid, dest, n_slots)].set(
        jnp.arange(n_slabs, dtype=jnp.int32), mode="drop")
    tile_group = jnp.sum(jnp.arange(n_tiles_max, dtype=jnp.int32)[:, None] >= tile_end[None, :], axis=1)
    tile_group = jnp.minimum(tile_group, N_GROUPS - 1).astype(jnp.int32)
    return src, dest, tile_group, tile_end[-1:].astype(jnp.int32)


def _moe_call(src, tile_group, n_tiles, h2s, gs, wgu, wd):
    n_tiles_max = tile_group.shape[0]
    return pl.pallas_call(
        _moe_kernel,
        grid_spec=pltpu.PrefetchScalarGridSpec(
            num_scalar_prefetch=3,
            grid=(n_tiles_max,),
            in_specs=[
                pl.BlockSpec(memory_space=pl.ANY), pl.BlockSpec(memory_space=pl.ANY),
                pl.BlockSpec((EXPERTS_PER_GROUP, D_MODEL, 2 * D_EXPERT), lambda i, s, tg, nt: (tg[i], 0, 0)),
                pl.BlockSpec((EXPERTS_PER_GROUP, D_EXPERT, D_MODEL), lambda i, s, tg, nt: (tg[i], 0, 0)),
            ],
            out_specs=pl.BlockSpec((TM, D_MODEL), lambda i, s, tg, nt: (i, 0)),
            scratch_shapes=[
                pltpu.VMEM((2, TM, D_MODEL), BF16), pltpu.VMEM((2, TM, ROUTER_W), F32),
                pltpu.SemaphoreType.DMA((2,)), pltpu.SemaphoreType.DMA((2,)),
            ],
        ),
        out_shape=jax.ShapeDtypeStruct((n_tiles_max * TM, D_MODEL), BF16),
        compiler_params=pltpu.CompilerParams(
            dimension_semantics=("arbitrary",), vmem_limit_bytes=V7X_VMEM_LIMIT),
    )(src, tile_group, n_tiles, h2s, gs, wgu, wd)


def _final_call(dest, x1, rinfo, ys, gfin, blocks_p):
    n_blocks = x1.shape[0] // TB
    return pl.pallas_call(
        functools.partial(_final_kernel, blocks_p),
        grid_spec=pltpu.PrefetchScalarGridSpec(
            num_scalar_prefetch=1,
            grid=(n_blocks,),
            in_specs=[
                pl.BlockSpec((TB, D_MODEL), lambda i, d: (i, 0)),
                pl.BlockSpec((TB, ROUTER_W), lambda i, d: (i, 0)),
                pl.BlockSpec(memory_space=pl.ANY),
                pl.BlockSpec((1, D_MODEL), lambda i, d: (0, 0)),
            ],
            out_specs=[
                pl.BlockSpec((TB, D_MODEL), lambda i, d: (jnp.minimum(i, blocks_p - 1), 0)),
                pl.BlockSpec((TB, D_MODEL), lambda i, d: (jnp.maximum(i - blocks_p, 0), 0)),
            ],
            scratch_shapes=[pltpu.VMEM((2, SORT_W, D_MODEL), BF16), pltpu.SemaphoreType.DMA((2,))],
        ),
        out_shape=[jax.ShapeDtypeStruct((blocks_p * TB, D_MODEL), F32),
                   jax.ShapeDtypeStruct(((n_blocks - blocks_p) * TB, D_MODEL), F32)],
        compiler_params=pltpu.CompilerParams(
            dimension_semantics=("arbitrary",), vmem_limit_bytes=V7X_VMEM_LIMIT),
    )(dest, x1, rinfo, ys, gfin)


def kernel(x_prompt, x_sample, state_ret, state_hgrn, norm_mix_g, w_in, hgrn_lb_logits, hgrn_norm_g, w_out,
           norm_ffn_g, w_router_group, b_router_group, w_router_expert, b_router_expert, w_exp_gate, w_exp_up,
           w_exp_down, norm_final_g):
    depth = w_in.shape[0]
    assert depth == 1 and hgrn_lb_logits.shape[0] == 2, "single-layer configuration only"
    bp, seq, d = x_prompt.shape
    db, dec_len, _ = x_sample.shape
    assert bp == 1 and d == D_MODEL and dec_len == CHUNK and seq % 256 == 0 and db % 4 == 0

    wr = jnp.zeros((D_MODEL, ROUTER_W), F32)
    wr = wr.at[:, :N_GROUPS].set(w_router_group[0]).at[:, N_GROUPS:N_GROUPS + N_EXPERTS].set(w_router_expert[0])
    br = jnp.zeros((1, ROUTER_W), F32)
    br = br.at[0, :N_GROUPS].set(b_router_group[0]).at[0, N_GROUPS:N_GROUPS + N_EXPERTS].set(b_router_expert[0])
    params = (norm_mix_g[0][None, :], w_in[0].astype(BF16), hgrn_lb_logits,
              jnp.tile(hgrn_norm_g[0], N_HEADS)[None, :], w_out[0].astype(BF16), norm_ffn_g[0][None, :], wr, br)

    zeros_state = jnp.zeros((1, N_PAIRS, PAIR_W, PAIR_W), F32)
    blocks_p = seq // TB
    blocks_s = db * dec_len // TB
    n_blocks = blocks_p + blocks_s
    *shared, ret_p, hg_p = _mixer_call(
        x_prompt.reshape(seq, d), zeros_state, zeros_state, params, seg_len=TB, carry=True,
        block_offset=0, total_blocks=n_blocks)
    ret0_s = _to_block_diag(state_ret[0])
    hg0_s = _to_block_diag(jnp.swapaxes(state_hgrn[0], -1, -2))
    x1, h2s, gs, rinfo, info, ret_s, hg_s = _mixer_call(
        x_sample.reshape(db * dec_len, d), ret0_s, hg0_s, params, seg_len=CHUNK, carry=False,
        block_offset=blocks_p, total_blocks=n_blocks, shared=tuple(shared))

    n_tiles_max = -(-n_blocks * SLABS_PER_BLOCK // TILE_SLABS) + N_GROUPS
    src, dest, tile_group, n_tiles = _dispatch_tables(info, n_tiles_max)
    wgu = jnp.concatenate([w_exp_gate[0], w_exp_up[0]], axis=-1).astype(BF16)
    ys = _moe_call(src, tile_group, n_tiles, h2s, gs, wgu, w_exp_down[0].astype(BF16))
    y_p, y_s = _final_call(dest, x1, rinfo, ys, norm_final_g[None, :], blocks_p)

    def states(r, s):
        return (_from_block_diag(r)[None], jnp.swapaxes(_from_block_diag(s), -1, -2)[None])

    rp, sp = states(ret_p, hg_p)
    rs, ss = states(ret_s, hg_s)
    return (y_p.reshape(bp, seq, d), y_s.reshape(db, dec_len, d), rp, sp, rs, ss)
```

```python
import functools

import numpy as np
import jax
import jax.numpy as jnp
from jax import lax
from jax.experimental import pallas as pl
from jax.experimental.pallas import tpu as pltpu

F32 = jnp.float32
BF16 = jnp.bfloat16

D_MODEL = 1024
N_HEADS = 8
D_HEAD = 64
GROUP_W = N_HEADS * D_HEAD
N_PAIRS = N_HEADS // 2
PAIR_W = 2 * D_HEAD
CHUNK = 64
SUB = 16
N_SUB = CHUNK // SUB
PAST_LEN = 2048
ROPE_BASE = 10000.0
RMS_EPS = 1e-6
LOG2E = 1.4426950408889634
N_GROUPS = 4
EXPERTS_PER_GROUP = 8
N_EXPERTS = N_GROUPS * EXPERTS_PER_GROUP
D_EXPERT = 256
ROUTER_W = 128
V7X_VMEM_LIMIT = 60 * 1024 * 1024
TB = 256
SLAB = 16
SLABS_PER_BLOCK = TB // SLAB + N_GROUPS
SORT_ROWS = SLABS_PER_BLOCK * SLAB
SORT_W = 384
TILE_SLABS = 32
TM = TILE_SLABS * SLAB

C_QR, C_KR, C_VR, C_GR, C_QG, C_FG, C_VG, C_GG = (i * GROUP_W for i in range(8))


def _dot(a, b):
    return jnp.dot(a, b, preferred_element_type=F32)


def _dot_nt(a, b):
    return lax.dot_general(a, b, (((1,), (1,)), ((), ())), preferred_element_type=F32)


def _dot_tn(a, b):
    return lax.dot_general(a, b, (((0,), (0,)), ((), ())), preferred_element_type=F32)


def _split(x, n):
    parts = []
    for _ in range(n):
        p = x.astype(BF16)
        parts.append(p)
        x = x - p.astype(F32)
    return parts


def _group_sum(x, ones_ref):
    xb = x.astype(BF16)
    return jnp.concatenate([_dot(xb[:, c * 256:(c + 1) * 256], ones_ref[...]) for c in range(2)], axis=1)


class _ColumnGroups:
    def __init__(self, bufs):
        self.bufs = bufs

    def _locate(self, idx):
        rows, cols = idx
        g = cols.start // GROUP_W
        assert (cols.stop - 1) // GROUP_W == g
        return self.bufs[g], (rows, slice(cols.start - g * GROUP_W, cols.stop - g * GROUP_W))

    def __getitem__(self, idx):
        buf, at = self._locate(idx)
        return buf[at]

    def __setitem__(self, idx, value):
        buf, at = self._locate(idx)
        buf[at] = value


def _silu(x):
    return x * jax.nn.sigmoid(x)


N_MIXER_INPUTS = 21


def _mixer_kernel(cfg, *refs):
    n_own, n_fill, n_alias = cfg[-3:]
    i = pl.program_id(0)

    @pl.when(i < n_own)
    def _():
        _mixer_body(cfg, *refs)

    if n_fill:
        @pl.when(i >= n_own)
        def _():
            for out_ref in refs[N_MIXER_INPUTS + n_alias:N_MIXER_INPUTS + n_alias + 5]:
                out_ref[...] = jnp.zeros_like(out_ref)


def _mixer_body(cfg, x_ref, gmix_ref, win_ref, invf_ref, sgn_ref, lbl_ref, hgn_ref, wout_ref, gffn_ref,
                wr_ref, br_ref, tri_ref, dret_ref, xi_ref, zeta_ref, gls_ref, ones_ref, cmask_ref, stri_ref,
                ret0_ref, hg0_ref, *rest):
    LS, NSEG, carry, _, _, n_alias = cfg
    (x1_ref, h2s_ref, gs_ref, rinfo_ref, info_ref, retout_ref, hgout_ref,
     *proj_bufs, kbuf, obuf, dbuf, qw, kw, qx, kx, qin_buf, kout_buf, pbuf, cos_t, sin_t, win_bf, wout_bf,
     rst, sst) = rest[n_alias:]
    proj = _ColumnGroups(proj_bufs)
    NCH = LS // CHUNK
    i = pl.program_id(0)
    lane128 = lax.broadcasted_iota(jnp.int32, (1, PAIR_W), 1)
    head_a = lane128 < D_HEAD
    bd_mask = (lax.broadcasted_iota(jnp.int32, (PAIR_W, PAIR_W), 0) < D_HEAD) == head_a

    def pair_state(ref, sg, p):
        z = jnp.zeros((D_HEAD, D_HEAD), F32)
        return jnp.concatenate([jnp.concatenate([ref[sg, 2 * p], z], axis=1),
                                jnp.concatenate([z, ref[sg, 2 * p + 1]], axis=1)], axis=0)

    @pl.when(i == 0)
    def _():
        for c in range(0, 8 * GROUP_W, GROUP_W):
            win_bf[:, c:c + GROUP_W] = win_ref[:, c:c + GROUP_W].astype(BF16)
        wout_bf[...] = wout_ref[...].astype(BF16)
        if carry:
            for p in range(N_PAIRS):
                rst[0, p] = pair_state(ret0_ref, 0, p)
                sst[0, p] = pair_state(hg0_ref, 0, p)
        row = lax.broadcasted_iota(jnp.int32, (TB, 1), 0)
        ang_row = (row & (LS - 1)).astype(F32) * invf_ref[...]
        cos_t[...] = jnp.cos(ang_row)
        sin_t[...] = jnp.sin(ang_row)
        qw[...] = jnp.zeros_like(qw)
        kw[...] = jnp.zeros_like(kw)
        if NCH > 1:
            qx[...] = jnp.zeros_like(qx)
            kx[...] = jnp.zeros_like(kx)

    x = x_ref[...]
    h = x * lax.rsqrt(jnp.mean(x * x, axis=-1, keepdims=True) + RMS_EPS) * gmix_ref[...]
    hb = h.astype(BF16)

    def project(c0, c1):
        for c in range(c0, c1, GROUP_W):
            proj[:, c:c + GROUP_W] = _dot(hb, win_bf[:, c:c + GROUP_W])

    project(C_QG, C_GG)
    project(C_QR, C_VR)

    la = lbl_ref[0:1, :]
    lb_ = lbl_ref[1:2, :]
    lmax = jnp.maximum(la, lb_)
    ea = jnp.exp(la - lmax)
    lbv = ea / (ea + jnp.exp(lb_ - lmax))
    qg = proj[:, C_QG:C_QG + GROUP_W]
    proj[:, C_QG:C_QG + GROUP_W] = _silu(qg)
    f = lbv + (1.0 - lbv) * jax.nn.sigmoid(proj[:, C_FG:C_FG + GROUP_W])
    kbuf[...] = 1.0 - f
    logf = jnp.log(f)
    for sg in range(NSEG):
        rows = slice(sg * LS, (sg + 1) * LS)
        parts = _split(logf[rows], 3)
        proj[rows, C_FG:C_FG + GROUP_W] = sum(_dot(tri_ref[...], p) for p in parts)

    project(C_VR, C_QG)
    project(C_GG, C_GG + GROUP_W)

    start = jnp.full((8, PAIR_W), i * TB if carry else PAST_LEN, jnp.int32).astype(F32)
    ang0 = (start * invf_ref[...])[0:1]
    c0 = jnp.cos(ang0)
    s0 = jnp.sin(ang0)
    cos = cos_t[...] * c0 - sin_t[...] * s0
    sin = (sin_t[...] * c0 + cos_t[...] * s0) * sgn_ref[...]
    first_half = (lane128 & (D_HEAD - 1)) < (D_HEAD // 2)
    for blk in range(2 * N_PAIRS):
        cols = slice(blk * PAIR_W, (blk + 1) * PAIR_W)
        xx = proj[:, cols]
        partner = jnp.where(first_half, pltpu.roll(xx, PAIR_W - D_HEAD // 2, 1), pltpu.roll(xx, D_HEAD // 2, 1))
        r = xx * cos + partner * sin
        if blk < N_PAIRS:
            r = r * (D_HEAD ** -0.5)
        proj[:, cols] = r

    HALF = SUB // 2
    riota = lax.broadcasted_iota(jnp.int32, (HALF, 1), 0)
    P_ROWS = HALF * SUB + HALF * HALF

    def diag_chunk(c):
        for sb in range(N_SUB):
            rs = c * CHUNK + sb * SUB
            q16 = proj[rs:rs + SUB, C_QG:C_QG + GROUP_W]
            b16 = proj[rs:rs + SUB, C_FG:C_FG + GROUP_W] * LOG2E
            k16 = kbuf[rs:rs + SUB, :]
            base = sb * P_ROWS
            upper = []
            for s in range(SUB):
                ks = k16[s:s + 1]
                bs = b16[s:s + 1]
                hi = q16[HALF:] * ks * jnp.exp2(b16[HALF:] - bs)
                if s < HALF:
                    lo = q16[:HALF] * ks * jnp.exp2(b16[:HALF] - bs)
                    lo = jnp.where(riota >= s, lo, 0.0)
                    pbuf[base + s * SUB:base + (s + 1) * SUB, :] = jnp.concatenate([lo, hi], axis=0).astype(BF16)
                else:
                    upper.append(jnp.where(riota >= s - HALF, hi, 0.0))
                    if len(upper) == 2:
                        at = base + HALF * SUB + (s - HALF - 1) * HALF
                        pbuf[at:at + SUB, :] = jnp.concatenate(upper, axis=0).astype(BF16)
                        upper = []
        sc = [_dot(pbuf[:, hh * 256:(hh + 1) * 256], ones_ref[...]) for hh in range(2)]
        for sb in range(N_SUB):
            rs = c * CHUNK + sb * SUB
            v16 = proj[rs:rs + SUB, C_VG:C_VG + GROUP_W]
            base = sb * P_ROWS
            acc_lo = [jnp.zeros((HALF, 256), F32) for _ in range(2)]
            acc_hi = [jnp.zeros((HALF, 256), F32) for _ in range(2)]
            for s in range(SUB):
                for hh in range(2):
                    vs = v16[s:s + 1, hh * 256:(hh + 1) * 256]
                    if s < HALF:
                        at = base + s * SUB
                        acc_lo[hh] = acc_lo[hh] + sc[hh][at:at + HALF] * vs
                        acc_hi[hh] = acc_hi[hh] + sc[hh][at + HALF:at + SUB] * vs
                    else:
                        at = base + HALF * SUB + (s - HALF) * HALF
                        acc_hi[hh] = acc_hi[hh] + sc[hh][at:at + HALF] * vs
            dbuf[rs:rs + HALF, :] = jnp.concatenate(acc_lo, axis=1)
            dbuf[rs + HALF:rs + SUB, :] = jnp.concatenate(acc_hi, axis=1)

    for sg in range(NSEG):
        r0 = sg * LS
        rows = slice(r0, r0 + LS)
        st = 0 if carry else sg
        qg_s = proj[rows, C_QG:C_QG + GROUP_W]
        kg_s = kbuf[rows, :]
        b_s = proj[rows, C_FG:C_FG + GROUP_W]

        for c in range(NCH):
            for j in range(N_SUB - 1):
                e = c * CHUNK + j * SUB + SUB - 1
                bref = b_s[e:e + 1, :]
                q_rows = slice(e + 1, (c + 1) * CHUNK)
                k_rows = slice(e + 1 - SUB, e + 1)
                qt = (qg_s[q_rows] * jnp.exp(b_s[q_rows] - bref)).astype(BF16)
                kt = (kg_s[k_rows] * jnp.exp(bref - b_s[k_rows])).astype(BF16)
                for p in range(N_PAIRS):
                    dst = slice((p * (N_SUB - 1) + j) * PAIR_W, (p * (N_SUB - 1) + j + 1) * PAIR_W)
                    src = slice(p * PAIR_W, (p + 1) * PAIR_W)
                    qw[r0 + q_rows.start:r0 + q_rows.stop, dst] = qt[:, src]
                    kw[r0 + k_rows.start:r0 + k_rows.stop, dst] = kt[:, src]
        for c in range(NCH - 1):
            e = c * CHUNK + CHUNK - 1
            bref = b_s[e:e + 1, :]
            q_rows = slice(e + 1, LS)
            k_rows = slice(c * CHUNK, e + 1)
            qt = (qg_s[q_rows] * jnp.exp(b_s[q_rows] - bref)).astype(BF16)
            kt = (kg_s[k_rows] * jnp.exp(bref - b_s[k_rows])).astype(BF16)
            for p in range(N_PAIRS):
                dst = slice((p * (NCH - 1) + c) * PAIR_W, (p * (NCH - 1) + c + 1) * PAIR_W)
                src = slice(p * PAIR_W, (p + 1) * PAIR_W)
                qx[r0 + q_rows.start:r0 + q_rows.stop, dst] = qt[:, src]
                kx[r0 + k_rows.start:r0 + k_rows.stop, dst] = kt[:, src]

        b_last = b_s[LS - 1:LS, :]
        qin_buf[rows, :] = (qg_s * jnp.exp(b_s)).astype(BF16)
        kout_buf[rows, :] = (kg_s * jnp.exp(b_last - b_s)).astype(BF16)

    def pair_unit(sg, p):
        r0 = sg * LS
        rows = slice(r0, r0 + LS)
        st = 0 if carry else sg
        pc = slice(p * PAIR_W, (p + 1) * PAIR_W)

        qb = proj[rows, C_QR + p * PAIR_W:C_QR + (p + 1) * PAIR_W].astype(BF16)
        kr = proj[rows, C_KR + p * PAIR_W:C_KR + (p + 1) * PAIR_W]
        kb = kr.astype(BF16)
        vb = proj[rows, C_VR + p * PAIR_W:C_VR + (p + 1) * PAIR_W].astype(BF16)
        zero = jnp.zeros_like(qb)
        s_a = _dot_nt(jnp.where(head_a, qb, zero), kb) * dret_ref[2 * p]
        s_b = _dot_nt(jnp.where(head_a, zero, qb), kb) * dret_ref[2 * p + 1]
        o_r = jnp.where(head_a, _dot(s_a.astype(BF16), vb), _dot(s_b.astype(BF16), vb))
        r_old = rst[0, p] if carry else pair_state(ret0_ref, sg, p)
        o_r = o_r + _dot(qb, r_old.astype(BF16)) * xi_ref[:, pc]
        u = _dot_tn((kr * zeta_ref[:, pc]).astype(BF16), vb)
        r_new = r_old * gls_ref[:, pc] + jnp.where(bd_mask, u, 0.0)
        obuf[rows, pc] = o_r

        vg = proj[rows, C_VG + p * PAIR_W:C_VG + (p + 1) * PAIR_W].astype(BF16)
        wc = slice(p * (N_SUB - 1) * PAIR_W, (p + 1) * (N_SUB - 1) * PAIR_W)
        qw_p = qw[rows, wc]
        kw_p = kw[rows, wc]
        lane_w = lax.broadcasted_iota(jnp.int32, (1, qw_p.shape[1]), 1)
        head_a_w = (lane_w & (PAIR_W - 1)) < D_HEAD
        zw = jnp.zeros_like(qw_p)
        g_a = _dot_nt(jnp.where(head_a_w, qw_p, zw), kw_p)
        g_b = _dot_nt(jnp.where(head_a_w, zw, qw_p), kw_p)
        if NCH > 1:
            g_a = g_a * cmask_ref[...]
            g_b = g_b * cmask_ref[...]
            xc = slice(p * (NCH - 1) * PAIR_W, (p + 1) * (NCH - 1) * PAIR_W)
            qx_p = qx[rows, xc]
            kx_p = kx[rows, xc]
            lane_x = lax.broadcasted_iota(jnp.int32, (1, qx_p.shape[1]), 1)
            head_a_x = (lane_x & (PAIR_W - 1)) < D_HEAD
            zx = jnp.zeros_like(qx_p)
            g_a = g_a + _dot_nt(jnp.where(head_a_x, qx_p, zx), kx_p)
            g_b = g_b + _dot_nt(jnp.where(head_a_x, zx, qx_p), kx_p)
        o_g = jnp.where(head_a, _dot(g_a.astype(BF16), vg), _dot(g_b.astype(BF16), vg))
        s_old = sst[0, p] if carry else pair_state(hg0_ref, sg, p)
        o_g = o_g + _dot(qin_buf[rows, pc], s_old.astype(BF16))
        ut = _dot_tn(kout_buf[rows, pc], vg)
        b_last = proj[r0 + LS - 1:r0 + LS, C_FG + p * PAIR_W:C_FG + (p + 1) * PAIR_W]
        s_decay = jnp.exp(jnp.broadcast_to(b_last, (PAIR_W, PAIR_W)).T)
        s_new = s_old * s_decay + jnp.where(bd_mask, ut, 0.0)
        gc = slice(GROUP_W + p * PAIR_W, GROUP_W + (p + 1) * PAIR_W)
        obuf[rows, gc] = o_g

        if carry:
            rst[0, p] = r_new
            sst[0, p] = s_new
        for out_ref, new in ((retout_ref, r_new), (hgout_ref, s_new)):
            out_ref[st, 2 * p] = new[:D_HEAD, :D_HEAD]
            out_ref[st, 2 * p + 1] = new[D_HEAD:, D_HEAD:]

    units = [(sg, p) for sg in range(NSEG) for p in range(N_PAIRS)]
    per_chunk = len(units) // (TB // CHUNK)
    for c in range(TB // CHUNK):
        for sg, p in units[c * per_chunk:(c + 1) * per_chunk]:
            pair_unit(sg, p)
        diag_chunk(c)

    o_r = obuf[:, 0:GROUP_W]
    mu = _group_sum(o_r, ones_ref) * (1.0 / D_HEAD)
    dlt = o_r - mu
    var = _group_sum(dlt * dlt, ones_ref) * (1.0 / D_HEAD)
    y_r = dlt * lax.rsqrt(var + RMS_EPS) * _silu(proj[:, C_GR:C_GR + GROUP_W])
    o_g = obuf[:, GROUP_W:2 * GROUP_W] + dbuf[...]
    ms = _group_sum(o_g * o_g, ones_ref) * (1.0 / D_HEAD)
    y_g = o_g * lax.rsqrt(ms + RMS_EPS) * hgn_ref[...] * _silu(proj[:, C_GG:C_GG + GROUP_W])
    mix = jnp.concatenate([y_r, y_g], axis=1).astype(BF16)
    x1 = x + _dot(mix, wout_bf[...])
    x1_ref[...] = x1

    h2 = x1 * lax.rsqrt(jnp.mean(x1 * x1, axis=-1, keepdims=True) + RMS_EPS) * gffn_ref[...]
    h_hi, h_lo = _split(h2, 2)
    w_hi, w_lo = _split(wr_ref[...], 2)
    logits = _dot(h_hi, w_hi) + _dot(h_hi, w_lo) + _dot(h_lo, w_hi) + br_ref[...]
    lane = lax.broadcasted_iota(jnp.int32, (1, ROUTER_W), 1).astype(F32)
    neg = -jnp.inf
    no_lane = float(ROUTER_W)
    gl = jnp.where(lane < N_GROUPS, logits, neg)
    gmax = jnp.max(gl, axis=-1, keepdims=True)
    g_idx = jnp.min(jnp.where(gl == gmax, lane, no_lane), axis=-1, keepdims=True)
    prob_g = 1.0 / jnp.sum(jnp.exp(gl - gmax), axis=-1, keepdims=True)
    e_lo = N_GROUPS + EXPERTS_PER_GROUP * g_idx
    el = jnp.where((lane >= e_lo) & (lane < e_lo + EXPERTS_PER_GROUP), logits, neg)
    v1 = jnp.max(el, axis=-1, keepdims=True)
    i1 = jnp.min(jnp.where(el == v1, lane, no_lane), axis=-1, keepdims=True)
    el2 = jnp.where(lane == i1, neg, el)
    v2 = jnp.max(el2, axis=-1, keepdims=True)
    i2 = jnp.min(jnp.where(el2 == v2, lane, no_lane), axis=-1, keepdims=True)
    t = jnp.exp(v2 - v1)
    p1 = 1.0 / (1.0 + t)
    p2 = t * p1
    gate = jnp.where(lane == i1, prob_g * p1, 0.0) + jnp.where(lane == i2, prob_g * p2, 0.0)

    onehot = jnp.where(lane == g_idx, 1.0, 0.0)
    before = _dot(stri_ref[...], onehot.astype(BF16))
    count = jnp.sum(onehot, axis=0, keepdims=True)
    padded = jnp.floor((count + (SLAB - 1.0)) * (1.0 / SLAB)) * SLAB
    padded8 = jnp.broadcast_to(padded, (8, ROUTER_W))
    seg_start = sum(jnp.where(lane >= k, pltpu.roll(padded8, k, 1), 0.0) for k in range(1, N_GROUPS))[0:1]
    rank = jnp.sum(onehot * (seg_start + before), axis=-1, keepdims=True)
    col = lax.broadcasted_iota(jnp.int32, (1, SORT_W), 1).astype(F32)
    place = jnp.where(col == rank, 1.0, 0.0).astype(BF16)
    h2s_ref[...] = _dot_tn(place, h2.astype(BF16))[:SORT_ROWS].astype(BF16)
    gs_ref[...] = sum(_dot_tn(place, part) for part in _split(gate, 3))[:SORT_ROWS]
    rinfo_ref[...] = jnp.where(lane == 0, rank, 0.0)
    slab_lo = (lane * SLAB).astype(F32)
    slab_group = jnp.full((1, ROUTER_W), -1.0, F32)
    for g in range(N_GROUPS):
        s_g = jnp.sum(jnp.where(lane == g, seg_start, 0.0), axis=-1, keepdims=True)
        n_g = jnp.sum(jnp.where(lane == g, padded, 0.0), axis=-1, keepdims=True)
        slab_group = jnp.where((slab_lo >= s_g) & (slab_lo < s_g + n_g), float(g), slab_group)
    info_ref[0] = jnp.broadcast_to(slab_group.astype(jnp.int32), (8, ROUTER_W))


def _slab_gather(table_ref, first, n_slabs, srcs, bufs, sems, slot, *, wait):
    for j in range(n_slabs):
        n = table_ref[first + j]
        row = pl.multiple_of(jnp.maximum(n, 0) * SLAB, SLAB)
        dst_rows = pl.ds(j * SLAB, SLAB)

        @pl.when(n >= 0)
        def _():
            for src, buf, sem in zip(srcs, bufs, sems):
                cp = pltpu.make_async_copy(src.at[pl.ds(row, SLAB)], buf.at[slot, dst_rows], sem.at[slot])
                if wait:
                    cp.wait()
                else:
                    cp.start()

        if not wait:
            @pl.when(n < 0)
            def _():
                for buf in bufs:
                    buf[slot, dst_rows] = jnp.zeros((SLAB,) + buf.shape[2:], buf.dtype)


def _moe_kernel(src_ref, tg_ref, nt_ref, h2s_hbm, gs_hbm, wg_ref, wu_ref, wd_ref, ys_ref, hbuf, gbuf, wgu_bf, wd_bf,
                sem_h, sem_g):
    i = pl.program_id(0)
    n_tiles = nt_ref[0]
    slot = lax.rem(i, 2)
    gather = functools.partial(_slab_gather, src_ref, srcs=(h2s_hbm, gs_hbm), bufs=(hbuf, gbuf),
                               sems=(sem_h, sem_g), n_slabs=TILE_SLABS)

    @pl.when(i == 0)
    def _():
        gather(first=0, slot=0, wait=False)

    @pl.when((i == 0) | (tg_ref[i] != tg_ref[jnp.maximum(i - 1, 0)]))
    def _():
        for e in range(EXPERTS_PER_GROUP):
            wgu_bf[e, :, :D_EXPERT] = wg_ref[e].astype(BF16)
            wgu_bf[e, :, D_EXPERT:] = wu_ref[e].astype(BF16)
            wd_bf[e] = wd_ref[e].astype(BF16)

    @pl.when(i < n_tiles)
    def _():
        gather(first=i * TILE_SLABS, slot=slot, wait=True)

        @pl.when(i + 1 < n_tiles)
        def _():
            gather(first=(i + 1) * TILE_SLABS, slot=1 - slot, wait=False)

        hb = hbuf[slot]
        gates = gbuf[slot]
        lane = lax.broadcasted_iota(jnp.int32, (1, ROUTER_W), 1)
        first_lane = N_GROUPS + tg_ref[i] * EXPERTS_PER_GROUP
        acc = jnp.zeros((TM, D_MODEL), F32)
        for e in range(EXPERTS_PER_GROUP):
            ab = _dot(hb, wgu_bf[e])
            he = (_silu(ab[:, :D_EXPERT]) * ab[:, D_EXPERT:]).astype(BF16)
            gcol = jnp.sum(jnp.where(lane == first_lane + e, gates, 0.0), axis=-1, keepdims=True)
            acc = acc + gcol * _dot(he, wd_bf[e])
        ys_ref[...] = acc.astype(BF16)

    @pl.when(i >= n_tiles)
    def _():
        ys_ref[...] = jnp.zeros_like(ys_ref)


def _final_kernel(blocks_p, dest_ref, x1_ref, rinfo_ref, ys_hbm, gfin_ref, yp_ref, ysmp_ref, ybuf, sem):
    i = pl.program_id(0)
    slot = lax.rem(i, 2)
    gather = functools.partial(_slab_gather, dest_ref, srcs=(ys_hbm,), bufs=(ybuf,), sems=(sem,),
                               n_slabs=SLABS_PER_BLOCK)

    @pl.when(i == 0)
    def _():
        ybuf[:, SORT_ROWS:SORT_W, :] = jnp.zeros((2, SORT_W - SORT_ROWS, D_MODEL), BF16)
        gather(first=0, slot=0, wait=False)

    gather(first=i * SLABS_PER_BLOCK, slot=slot, wait=True)

    @pl.when(i + 1 < pl.num_programs(0))
    def _():
        gather(first=(i + 1) * SLABS_PER_BLOCK, slot=1 - slot, wait=False)

    rank = rinfo_ref[:, 0:1]
    col = lax.broadcasted_iota(jnp.int32, (1, SORT_W), 1).astype(F32)
    place = jnp.where(col == rank, 1.0, 0.0).astype(BF16)
    moe = _dot(place, ybuf[slot])
    xo = x1_ref[...] + moe
    y = xo * lax.rsqrt(jnp.mean(xo * xo, axis=-1, keepdims=True) + RMS_EPS) * gfin_ref[...]

    @pl.when(i < blocks_p)
    def _():
        yp_ref[...] = y

    @pl.when(i >= blocks_p)
    def _():
        ysmp_ref[...] = y


def _const_spec(shape, pipeline_mode=None):
    nd = len(shape)
    return pl.BlockSpec(shape, lambda i: (0,) * nd, pipeline_mode=pipeline_mode)


def _mixer_call(x2d, ret0, hg0, params, *, seg_len, carry, block_offset, total_blocks, shared=()):
    (gmix, win, lbl, hgn, wout, gffn, wr, br) = params
    T = x2d.shape[0]
    LS = seg_len
    n_seg = TB // LS
    NCH = LS // CHUNK
    n_states = ret0.shape[0]
    n_own = T // TB
    n_fill = 0 if shared else total_blocks - n_own - block_offset
    grid = n_own + n_fill

    f32 = np.float32
    lg = np.log1p(-(f32(2.0) ** (f32(-5.0) - np.arange(N_HEADS, dtype=f32)))).astype(f32)
    tt = np.arange(LS, dtype=f32)
    ch = np.arange(LS) // CHUNK
    dret = np.exp(np.abs(tt[:, None] - tt[None, :])[None] * lg[:, None, None]).astype(f32)
    dret = np.where((ch[None, :] <= ch[:, None])[None], dret, f32(0.0))
    lg_lane = np.repeat(lg, D_HEAD)[None, :]
    xi = np.exp((tt[:, None] + f32(1.0)) * lg_lane).astype(f32)
    zeta = np.exp((f32(LS - 1.0) - tt)[:, None] * lg_lane).astype(f32)
    gls = np.exp(f32(LS) * lg_lane).astype(f32)
    half = D_HEAD // 2
    inv_freq = ROPE_BASE ** (-jnp.arange(half, dtype=F32) / half)
    invf = jnp.tile(inv_freq, PAIR_W // half)[None, :]
    sgn = np.where((np.arange(PAIR_W) % D_HEAD) < half, f32(-1.0), f32(1.0))[None, :]
    tri = jnp.asarray(np.tril(np.ones((LS, LS), np.float32)), BF16)
    lane_head = np.arange(256) // D_HEAD
    ones_bd = jnp.asarray((lane_head[:, None] == lane_head[None, :]).astype(np.float32), BF16)
    cmask = jnp.asarray((ch[:, None] == ch[None, :]).astype(np.float32))

    if carry:
        state_spec = pl.BlockSpec((1, N_HEADS, D_HEAD, D_HEAD), lambda i: (0, 0, 0, 0))
    else:
        state_spec = pl.BlockSpec((n_seg, N_HEADS, D_HEAD, D_HEAD), lambda i: (i, 0, 0, 0))
    single = pl.Buffered(1)
    stri = jnp.asarray(np.tril(np.ones((TB, TB), np.float32), -1), BF16)
    in_row_spec = pl.BlockSpec((TB, D_MODEL), lambda i: (jnp.minimum(i, n_own - 1), 0))
    out_row_spec = lambda rows, w: pl.BlockSpec((rows, w), lambda i: (i + block_offset, 0))
    nw = (N_SUB - 1) * N_PAIRS * PAIR_W
    nx = max(NCH - 1, 1) * N_PAIRS * PAIR_W
    n_in = N_MIXER_INPUTS

    kern = functools.partial(_mixer_kernel, (LS, n_seg, carry, n_own, n_fill, len(shared)))
    return pl.pallas_call(
        kern,
        grid=(grid,),
        in_specs=[
            in_row_spec,
            _const_spec((1, D_MODEL)), _const_spec(win.shape, single), _const_spec((1, PAIR_W)),
            _const_spec((1, PAIR_W)), _const_spec(lbl.shape), _const_spec((1, GROUP_W)),
            _const_spec(wout.shape, single), _const_spec((1, D_MODEL)),
            _const_spec(wr.shape), _const_spec((1, ROUTER_W)),
            _const_spec((LS, LS)), _const_spec((N_HEADS, LS, LS)), _const_spec((LS, GROUP_W)),
            _const_spec((LS, GROUP_W)), _const_spec((1, GROUP_W)), _const_spec((256, 256)), _const_spec((LS, LS)),
            _const_spec((TB, TB)),
            state_spec, state_spec,
        ] + [pl.BlockSpec(memory_space=pl.ANY)] * len(shared),
        out_specs=[out_row_spec(TB, D_MODEL), out_row_spec(SORT_ROWS, D_MODEL), out_row_spec(SORT_ROWS, ROUTER_W),
                   out_row_spec(TB, ROUTER_W),
                   pl.BlockSpec((1, 8, ROUTER_W), lambda i: (i + block_offset, 0, 0)),
                   state_spec, state_spec],
        out_shape=[
            jax.ShapeDtypeStruct((total_blocks * TB, D_MODEL), F32),
            jax.ShapeDtypeStruct((total_blocks * SORT_ROWS, D_MODEL), BF16),
            jax.ShapeDtypeStruct((total_blocks * SORT_ROWS, ROUTER_W), F32),
            jax.ShapeDtypeStruct((total_blocks * TB, ROUTER_W), F32),
            jax.ShapeDtypeStruct((total_blocks, 8, ROUTER_W), jnp.int32),
            jax.ShapeDtypeStruct((n_states, N_HEADS, D_HEAD, D_HEAD), F32),
            jax.ShapeDtypeStruct((n_states, N_HEADS, D_HEAD, D_HEAD), F32),
        ],
        input_output_aliases={n_in + k: k for k in range(len(shared))},
        scratch_shapes=[
            *[pltpu.VMEM((TB, GROUP_W), F32) for _ in range(8)],
            pltpu.VMEM((TB, GROUP_W), F32),
            pltpu.VMEM((TB, 2 * GROUP_W), F32),
            pltpu.VMEM((TB, GROUP_W), F32),
            pltpu.VMEM((TB, nw), BF16), pltpu.VMEM((TB, nw), BF16),
            pltpu.VMEM((TB, nx), BF16), pltpu.VMEM((TB, nx), BF16),
            pltpu.VMEM((TB, GROUP_W), BF16), pltpu.VMEM((TB, GROUP_W), BF16),
            pltpu.VMEM((N_SUB * (SUB // 2) * (SUB + SUB // 2), GROUP_W), BF16),
            pltpu.VMEM((TB, PAIR_W), F32), pltpu.VMEM((TB, PAIR_W), F32),
            pltpu.VMEM(win.shape, BF16), pltpu.VMEM(wout.shape, BF16),
            pltpu.VMEM((1, N_PAIRS, PAIR_W, PAIR_W), F32),
            pltpu.VMEM((1, N_PAIRS, PAIR_W, PAIR_W), F32),
        ],
        compiler_params=pltpu.CompilerParams(
            dimension_semantics=("arbitrary",), vmem_limit_bytes=V7X_VMEM_LIMIT),
    )(x2d, gmix, win, invf, sgn, lbl, hgn, wout, gffn, wr, br, tri, dret, xi, zeta, gls, ones_bd, cmask, stri,
      ret0, hg0, *shared)


def _dispatch_tables(info, n_tiles_max):
    slab_group = info[:, 0, :SLABS_PER_BLOCK].reshape(-1)
    n_slabs = slab_group.shape[0]
    valid = slab_group >= 0
    onehot = (slab_group[:, None] == jnp.arange(N_GROUPS, dtype=jnp.int32)[None, :]).astype(jnp.int32)
    within = jnp.cumsum(onehot, axis=0) - onehot
    tiles = (jnp.sum(onehot, axis=0) + TILE_SLABS - 1) // TILE_SLABS
    tile_end = jnp.cumsum(tiles)
    g = jnp.clip(slab_group, 0, N_GROUPS - 1)
    dest = (tile_end - tiles)[g] * TILE_SLABS + jnp.take_along_axis(within, g[:, None], axis=1)[:, 0]
    dest = jnp.where(valid, dest, -1).astype(jnp.int32)
    n_slots = n_tiles_max * TILE_SLABS
    src = jnp.full((n_slots,), -1, jnp.int32).at[jnp.where(valid, dest, n_slots)].set(
        jnp.arange(n_slabs, dtype=jnp.int32), mode="drop")
    tile_group = jnp.sum(jnp.arange(n_tiles_max, dtype=jnp.int32)[:, None] >= tile_end[None, :], axis=1)
    tile_group = jnp.minimum(tile_group, N_GROUPS - 1).astype(jnp.int32)
    return src, dest, tile_group, tile_end[-1:].astype(jnp.int32)


def _moe_call(src, tile_group, n_tiles, h2s, gs, w_gate, w_up, w_down):
    n_tiles_max = tile_group.shape[0]
    group_of_tile = lambda i, s, tg, nt: (tg[i], 0, 0)
    single = pl.Buffered(1)
    return pl.pallas_call(
        _moe_kernel,
        grid_spec=pltpu.PrefetchScalarGridSpec(
            num_scalar_prefetch=3,
            grid=(n_tiles_max,),
            in_specs=[
                pl.BlockSpec(memory_space=pl.ANY), pl.BlockSpec(memory_space=pl.ANY),
                pl.BlockSpec((EXPERTS_PER_GROUP, D_MODEL, D_EXPERT), group_of_tile, pipeline_mode=single),
                pl.BlockSpec((EXPERTS_PER_GROUP, D_MODEL, D_EXPERT), group_of_tile, pipeline_mode=single),
                pl.BlockSpec((EXPERTS_PER_GROUP, D_EXPERT, D_MODEL), group_of_tile, pipeline_mode=single),
            ],
            out_specs=pl.BlockSpec((TM, D_MODEL), lambda i, s, tg, nt: (i, 0)),
            scratch_shapes=[
                pltpu.VMEM((2, TM, D_MODEL), BF16), pltpu.VMEM((2, TM, ROUTER_W), F32),
                pltpu.VMEM((EXPERTS_PER_GROUP, D_MODEL, 2 * D_EXPERT), BF16),
                pltpu.VMEM((EXPERTS_PER_GROUP, D_EXPERT, D_MODEL), BF16),
                pltpu.SemaphoreType.DMA((2,)), pltpu.SemaphoreType.DMA((2,)),
            ],
        ),
        out_shape=jax.ShapeDtypeStruct((n_tiles_max * TM, D_MODEL), BF16),
        compiler_params=pltpu.CompilerParams(
            dimension_semantics=("arbitrary",), vmem_limit_bytes=V7X_VMEM_LIMIT),
    )(src, tile_group, n_tiles, h2s, gs, w_gate, w_up, w_down)


def _final_call(dest, x1, rinfo, ys, gfin, blocks_p):
    n_blocks = x1.shape[0] // TB
    return pl.pallas_call(
        functools.partial(_final_kernel, blocks_p),
        grid_spec=pltpu.PrefetchScalarGridSpec(
            num_scalar_prefetch=1,
            grid=(n_blocks,),
            in_specs=[
                pl.BlockSpec((TB, D_MODEL), lambda i, d: (i, 0)),
                pl.BlockSpec((TB, ROUTER_W), lambda i, d: (i, 0)),
                pl.BlockSpec(memory_space=pl.ANY),
                pl.BlockSpec((1, D_MODEL), lambda i, d: (0, 0)),
            ],
            out_specs=[
                pl.BlockSpec((TB, D_MODEL), lambda i, d: (jnp.minimum(i, blocks_p - 1), 0)),
                pl.BlockSpec((TB, D_MODEL), lambda i, d: (jnp.maximum(i - blocks_p, 0), 0)),
            ],
            scratch_shapes=[pltpu.VMEM((2, SORT_W, D_MODEL), BF16), pltpu.SemaphoreType.DMA((2,))],
        ),
        out_shape=[jax.ShapeDtypeStruct((blocks_p * TB, D_MODEL), F32),
                   jax.ShapeDtypeStruct(((n_blocks - blocks_p) * TB, D_MODEL), F32)],
        compiler_params=pltpu.CompilerParams(
            dimension_semantics=("arbitrary",), vmem_limit_bytes=V7X_VMEM_LIMIT),
    )(dest, x1, rinfo, ys, gfin)


def kernel(x_prompt, x_sample, state_ret, state_hgrn, norm_mix_g, w_in, hgrn_lb_logits, hgrn_norm_g, w_out,
           norm_ffn_g, w_router_group, b_router_group, w_router_expert, b_router_expert, w_exp_gate, w_exp_up,
           w_exp_down, norm_final_g):
    depth = w_in.shape[0]
    assert depth == 1 and hgrn_lb_logits.shape[0] == 2, "single-layer configuration only"
    bp, seq, d = x_prompt.shape
    db, dec_len, _ = x_sample.shape
    assert bp == 1 and d == D_MODEL and dec_len == CHUNK and seq % 256 == 0 and db % 4 == 0

    pad = ROUTER_W - N_GROUPS - N_EXPERTS
    wr = jnp.concatenate([w_router_group[0], w_router_expert[0], jnp.zeros((D_MODEL, pad), F32)], axis=1)
    br = jnp.concatenate([b_router_group[0], b_router_expert[0], jnp.zeros((pad,), F32)])[None, :]
    params = (norm_mix_g[0][None, :], w_in[0], hgrn_lb_logits,
              jnp.tile(hgrn_norm_g[0], N_HEADS)[None, :], w_out[0], norm_ffn_g[0][None, :], wr, br)

    zeros_state = jnp.zeros((1, N_HEADS, D_HEAD, D_HEAD), F32)
    blocks_p = seq // TB
    blocks_s = db * dec_len // TB
    n_blocks = blocks_p + blocks_s
    *shared, ret_p, hg_p = _mixer_call(
        x_prompt.reshape(seq, d), zeros_state, zeros_state, params, seg_len=TB, carry=True,
        block_offset=0, total_blocks=n_blocks)
    x1, h2s, gs, rinfo, info, ret_s, hg_s = _mixer_call(
        x_sample.reshape(db * dec_len, d), state_ret[0], state_hgrn[0], params, seg_len=CHUNK, carry=False,
        block_offset=blocks_p, total_blocks=n_blocks, shared=tuple(shared))

    n_tiles_max = -(-n_blocks * SLABS_PER_BLOCK // TILE_SLABS) + N_GROUPS
    src, dest, tile_group, n_tiles = _dispatch_tables(info, n_tiles_max)
    ys = _moe_call(src, tile_group, n_tiles, h2s, gs, w_exp_gate[0], w_exp_up[0], w_exp_down[0])
    y_p, y_s = _final_call(dest, x1, rinfo, ys, norm_final_g[None, :], blocks_p)
    return (y_p.reshape(bp, seq, d), y_s.reshape(db, dec_len, d), ret_p[None], hg_p[None], ret_s[None], hg_s[None])
```

```python
import functools

import numpy as np
import jax
import jax.numpy as jnp
from jax import lax
from jax.experimental import pallas as pl
from jax.experimental.pallas import tpu as pltpu

F32 = jnp.float32
BF16 = jnp.bfloat16

D_MODEL = 1024
N_HEADS = 8
D_HEAD = 64
GROUP_W = N_HEADS * D_HEAD
N_PAIRS = N_HEADS // 2
PAIR_W = 2 * D_HEAD
CHUNK = 64
SUB = 16
N_SUB = CHUNK // SUB
PAST_LEN = 2048
ROPE_BASE = 10000.0
RMS_EPS = 1e-6
LOG2E = 1.4426950408889634
N_GROUPS = 4
EXPERTS_PER_GROUP = 8
N_EXPERTS = N_GROUPS * EXPERTS_PER_GROUP
D_EXPERT = 256
ROUTER_W = 128
V7X_VMEM_LIMIT = 60 * 1024 * 1024
TB = 256
SLAB = 16
TOP_K = 2
SLABS_PER_BLOCK = (TB * TOP_K + N_EXPERTS * (SLAB - 1)) // SLAB
SORT_ROWS = SLABS_PER_BLOCK * SLAB
SORT_W = 1024
TILE_SLABS = 32
TM = TILE_SLABS * SLAB

C_QR, C_KR, C_VR, C_GR, C_QG, C_FG, C_VG, C_GG = (i * GROUP_W for i in range(8))


def _dot(a, b):
    return jnp.dot(a, b, preferred_element_type=F32)


def _dot_nt(a, b):
    return lax.dot_general(a, b, (((1,), (1,)), ((), ())), preferred_element_type=F32)


def _dot_tn(a, b):
    return lax.dot_general(a, b, (((0,), (0,)), ((), ())), preferred_element_type=F32)


def _split(x, n):
    parts = []
    for _ in range(n):
        p = x.astype(BF16)
        parts.append(p)
        x = x - p.astype(F32)
    return parts


def _group_sum(x, ones_ref):
    xb = x.astype(BF16)
    return jnp.concatenate([_dot(xb[:, c * 256:(c + 1) * 256], ones_ref[...]) for c in range(2)], axis=1)


class _ColumnGroups:
    def __init__(self, bufs):
        self.bufs = bufs

    def _locate(self, idx):
        rows, cols = idx
        g = cols.start // GROUP_W
        assert (cols.stop - 1) // GROUP_W == g
        return self.bufs[g], (rows, slice(cols.start - g * GROUP_W, cols.stop - g * GROUP_W))

    def __getitem__(self, idx):
        buf, at = self._locate(idx)
        return buf[at]

    def __setitem__(self, idx, value):
        buf, at = self._locate(idx)
        buf[at] = value


def _silu(x):
    return x * jax.nn.sigmoid(x)


N_MIXER_INPUTS = 22
N_SHARED_OUTPUTS = 4


def _mixer_kernel(cfg, *refs):
    n_own, n_fill, n_alias = cfg[-3:]
    i = pl.program_id(0)

    @pl.when(i < n_own)
    def _():
        _mixer_body(cfg, *refs)

    if n_fill:
        @pl.when(i >= n_own)
        def _():
            for out_ref in refs[N_MIXER_INPUTS + n_alias:N_MIXER_INPUTS + n_alias + N_SHARED_OUTPUTS]:
                out_ref[...] = jnp.zeros_like(out_ref)


def _mixer_body(cfg, x_ref, gmix_ref, win_ref, invf_ref, sgn_ref, lbl_ref, hgn_ref, wout_ref, gffn_ref,
                wr_ref, br_ref, tri_ref, dret_ref, xi_ref, zeta_ref, gls_ref, ones_ref, cmask_ref, stri_ref,
                sut_ref, ret0_ref, hg0_ref, *rest):
    LS, NSEG, carry, _, _, n_alias = cfg
    (x1_ref, h2s_ref, rinfo_ref, info_ref, retout_ref, hgout_ref,
     *proj_bufs, kbuf, obuf, dbuf, qw, kw, qx, kx, qin_buf, kout_buf, pbuf, cos_t, sin_t, win_bf, wout_bf,
     rst, sst) = rest[n_alias:]
    proj = _ColumnGroups(proj_bufs)
    NCH = LS // CHUNK
    i = pl.program_id(0)
    lane128 = lax.broadcasted_iota(jnp.int32, (1, PAIR_W), 1)
    head_a = lane128 < D_HEAD
    bd_mask = (lax.broadcasted_iota(jnp.int32, (PAIR_W, PAIR_W), 0) < D_HEAD) == head_a

    def pair_state(ref, sg, p):
        z = jnp.zeros((D_HEAD, D_HEAD), F32)
        return jnp.concatenate([jnp.concatenate([ref[sg, 2 * p], z], axis=1),
                                jnp.concatenate([z, ref[sg, 2 * p + 1]], axis=1)], axis=0)

    @pl.when(i == 0)
    def _():
        for c in range(0, 8 * GROUP_W, GROUP_W):
            win_bf[:, c:c + GROUP_W] = win_ref[:, c:c + GROUP_W].astype(BF16)
        wout_bf[...] = wout_ref[...].astype(BF16)
        if carry:
            for p in range(N_PAIRS):
                rst[0, p] = pair_state(ret0_ref, 0, p)
                sst[0, p] = pair_state(hg0_ref, 0, p)
        row = lax.broadcasted_iota(jnp.int32, (TB, 1), 0)
        ang_row = (row & (LS - 1)).astype(F32) * invf_ref[...]
        cos_t[...] = jnp.cos(ang_row)
        sin_t[...] = jnp.sin(ang_row)
        qw[...] = jnp.zeros_like(qw)
        kw[...] = jnp.zeros_like(kw)
        if NCH > 1:
            qx[...] = jnp.zeros_like(qx)
            kx[...] = jnp.zeros_like(kx)

    x = x_ref[...]
    h = x * lax.rsqrt(jnp.mean(x * x, axis=-1, keepdims=True) + RMS_EPS) * gmix_ref[...]
    hb = h.astype(BF16)

    def project(c0, c1):
        for c in range(c0, c1, GROUP_W):
            proj[:, c:c + GROUP_W] = _dot(hb, win_bf[:, c:c + GROUP_W])

    project(C_QG, C_GG)
    project(C_QR, C_VR)

    la = lbl_ref[0:1, :]
    lb_ = lbl_ref[1:2, :]
    lmax = jnp.maximum(la, lb_)
    ea = jnp.exp(la - lmax)
    lbv = ea / (ea + jnp.exp(lb_ - lmax))
    qg = proj[:, C_QG:C_QG + GROUP_W]
    proj[:, C_QG:C_QG + GROUP_W] = _silu(qg)
    f = lbv + (1.0 - lbv) * jax.nn.sigmoid(proj[:, C_FG:C_FG + GROUP_W])
    kbuf[...] = 1.0 - f
    logf = jnp.log(f)
    for sg in range(NSEG):
        rows = slice(sg * LS, (sg + 1) * LS)
        parts = _split(logf[rows], 3)
        proj[rows, C_FG:C_FG + GROUP_W] = sum(_dot(tri_ref[...], p) for p in parts)

    project(C_VR, C_QG)
    project(C_GG, C_GG + GROUP_W)

    start = jnp.full((8, PAIR_W), i * TB if carry else PAST_LEN, jnp.int32).astype(F32)
    ang0 = (start * invf_ref[...])[0:1]
    c0 = jnp.cos(ang0)
    s0 = jnp.sin(ang0)
    cos = cos_t[...] * c0 - sin_t[...] * s0
    sin = (sin_t[...] * c0 + cos_t[...] * s0) * sgn_ref[...]
    first_half = (lane128 & (D_HEAD - 1)) < (D_HEAD // 2)
    for blk in range(2 * N_PAIRS):
        cols = slice(blk * PAIR_W, (blk + 1) * PAIR_W)
        xx = proj[:, cols]
        partner = jnp.where(first_half, pltpu.roll(xx, PAIR_W - D_HEAD // 2, 1), pltpu.roll(xx, D_HEAD // 2, 1))
        r = xx * cos + partner * sin
        if blk < N_PAIRS:
            r = r * (D_HEAD ** -0.5)
        proj[:, cols] = r

    HALF = SUB // 2
    riota = lax.broadcasted_iota(jnp.int32, (HALF, 1), 0)
    P_ROWS = HALF * SUB + HALF * HALF

    def diag_chunk(c):
        for sb in range(N_SUB):
            rs = c * CHUNK + sb * SUB
            q16 = proj[rs:rs + SUB, C_QG:C_QG + GROUP_W]
            b16 = proj[rs:rs + SUB, C_FG:C_FG + GROUP_W] * LOG2E
            k16 = kbuf[rs:rs + SUB, :]
            base = sb * P_ROWS
            upper = []
            for s in range(SUB):
                ks = k16[s:s + 1]
                bs = b16[s:s + 1]
                hi = q16[HALF:] * ks * jnp.exp2(b16[HALF:] - bs)
                if s < HALF:
                    lo = q16[:HALF] * ks * jnp.exp2(b16[:HALF] - bs)
                    lo = jnp.where(riota >= s, lo, 0.0)
                    pbuf[base + s * SUB:base + (s + 1) * SUB, :] = jnp.concatenate([lo, hi], axis=0).astype(BF16)
                else:
                    upper.append(jnp.where(riota >= s - HALF, hi, 0.0))
                    if len(upper) == 2:
                        at = base + HALF * SUB + (s - HALF - 1) * HALF
                        pbuf[at:at + SUB, :] = jnp.concatenate(upper, axis=0).astype(BF16)
                        upper = []
        sc = [_dot(pbuf[:, hh * 256:(hh + 1) * 256], ones_ref[...]) for hh in range(2)]
        for sb in range(N_SUB):
            rs = c * CHUNK + sb * SUB
            v16 = proj[rs:rs + SUB, C_VG:C_VG + GROUP_W]
            base = sb * P_ROWS
            acc_lo = [jnp.zeros((HALF, 256), F32) for _ in range(2)]
            acc_hi = [jnp.zeros((HALF, 256), F32) for _ in range(2)]
            for s in range(SUB):
                for hh in range(2):
                    vs = v16[s:s + 1, hh * 256:(hh + 1) * 256]
                    if s < HALF:
                        at = base + s * SUB
                        acc_lo[hh] = acc_lo[hh] + sc[hh][at:at + HALF] * vs
                        acc_hi[hh] = acc_hi[hh] + sc[hh][at + HALF:at + SUB] * vs
                    else:
                        at = base + HALF * SUB + (s - HALF) * HALF
                        acc_hi[hh] = acc_hi[hh] + sc[hh][at:at + HALF] * vs
            dbuf[rs:rs + HALF, :] = jnp.concatenate(acc_lo, axis=1)
            dbuf[rs + HALF:rs + SUB, :] = jnp.concatenate(acc_hi, axis=1)

    for sg in range(NSEG):
        r0 = sg * LS
        rows = slice(r0, r0 + LS)
        st = 0 if carry else sg
        qg_s = proj[rows, C_QG:C_QG + GROUP_W]
        kg_s = kbuf[rows, :]
        b_s = proj[rows, C_FG:C_FG + GROUP_W]

        for c in range(NCH):
            for j in range(N_SUB - 1):
                e = c * CHUNK + j * SUB + SUB - 1
                bref = b_s[e:e + 1, :]
                q_rows = slice(e + 1, (c + 1) * CHUNK)
                k_rows = slice(e + 1 - SUB, e + 1)
                qt = (qg_s[q_rows] * jnp.exp(b_s[q_rows] - bref)).astype(BF16)
                kt = (kg_s[k_rows] * jnp.exp(bref - b_s[k_rows])).astype(BF16)
                for p in range(N_PAIRS):
                    dst = slice((p * (N_SUB - 1) + j) * PAIR_W, (p * (N_SUB - 1) + j + 1) * PAIR_W)
                    src = slice(p * PAIR_W, (p + 1) * PAIR_W)
                    qw[r0 + q_rows.start:r0 + q_rows.stop, dst] = qt[:, src]
                    kw[r0 + k_rows.start:r0 + k_rows.stop, dst] = kt[:, src]
        for c in range(NCH - 1):
            e = c * CHUNK + CHUNK - 1
            bref = b_s[e:e + 1, :]
            q_rows = slice(e + 1, LS)
            k_rows = slice(c * CHUNK, e + 1)
            qt = (qg_s[q_rows] * jnp.exp(b_s[q_rows] - bref)).astype(BF16)
            kt = (kg_s[k_rows] * jnp.exp(bref - b_s[k_rows])).astype(BF16)
            for p in range(N_PAIRS):
                dst = slice((p * (NCH - 1) + c) * PAIR_W, (p * (NCH - 1) + c + 1) * PAIR_W)
                src = slice(p * PAIR_W, (p + 1) * PAIR_W)
                qx[r0 + q_rows.start:r0 + q_rows.stop, dst] = qt[:, src]
                kx[r0 + k_rows.start:r0 + k_rows.stop, dst] = kt[:, src]

        b_last = b_s[LS - 1:LS, :]
        qin_buf[rows, :] = (qg_s * jnp.exp(b_s)).astype(BF16)
        kout_buf[rows, :] = (kg_s * jnp.exp(b_last - b_s)).astype(BF16)

    def pair_unit(sg, p):
        r0 = sg * LS
        rows = slice(r0, r0 + LS)
        st = 0 if carry else sg
        pc = slice(p * PAIR_W, (p + 1) * PAIR_W)

        qb = proj[rows, C_QR + p * PAIR_W:C_QR + (p + 1) * PAIR_W].astype(BF16)
        kr = proj[rows, C_KR + p * PAIR_W:C_KR + (p + 1) * PAIR_W]
        kb = kr.astype(BF16)
        vb = proj[rows, C_VR + p * PAIR_W:C_VR + (p + 1) * PAIR_W].astype(BF16)
        zero = jnp.zeros_like(qb)
        s_a = _dot_nt(jnp.where(head_a, qb, zero), kb) * dret_ref[2 * p]
        s_b = _dot_nt(jnp.where(head_a, zero, qb), kb) * dret_ref[2 * p + 1]
        o_r = jnp.where(head_a, _dot(s_a.astype(BF16), vb), _dot(s_b.astype(BF16), vb))
        r_old = rst[0, p] if carry else pair_state(ret0_ref, sg, p)
        o_r = o_r + _dot(qb, r_old.astype(BF16)) * xi_ref[:, pc]
        u = _dot_tn((kr * zeta_ref[:, pc]).astype(BF16), vb)
        r_new = r_old * gls_ref[:, pc] + jnp.where(bd_mask, u, 0.0)
        obuf[rows, pc] = o_r

        vg = proj[rows, C_VG + p * PAIR_W:C_VG + (p + 1) * PAIR_W].astype(BF16)
        wc = slice(p * (N_SUB - 1) * PAIR_W, (p + 1) * (N_SUB - 1) * PAIR_W)
        qw_p = qw[rows, wc]
        kw_p = kw[rows, wc]
        lane_w = lax.broadcasted_iota(jnp.int32, (1, qw_p.shape[1]), 1)
        head_a_w = (lane_w & (PAIR_W - 1)) < D_HEAD
        zw = jnp.zeros_like(qw_p)
        g_a = _dot_nt(jnp.where(head_a_w, qw_p, zw), kw_p)
        g_b = _dot_nt(jnp.where(head_a_w, zw, qw_p), kw_p)
        if NCH > 1:
            g_a = g_a * cmask_ref[...]
            g_b = g_b * cmask_ref[...]
            xc = slice(p * (NCH - 1) * PAIR_W, (p + 1) * (NCH - 1) * PAIR_W)
            qx_p = qx[rows, xc]
            kx_p = kx[rows, xc]
            lane_x = lax.broadcasted_iota(jnp.int32, (1, qx_p.shape[1]), 1)
            head_a_x = (lane_x & (PAIR_W - 1)) < D_HEAD
            zx = jnp.zeros_like(qx_p)
            g_a = g_a + _dot_nt(jnp.where(head_a_x, qx_p, zx), kx_p)
            g_b = g_b + _dot_nt(jnp.where(head_a_x, zx, qx_p), kx_p)
        o_g = jnp.where(head_a, _dot(g_a.astype(BF16), vg), _dot(g_b.astype(BF16), vg))
        s_old = sst[0, p] if carry else pair_state(hg0_ref, sg, p)
        o_g = o_g + _dot(qin_buf[rows, pc], s_old.astype(BF16))
        ut = _dot_tn(kout_buf[rows, pc], vg)
        b_last = proj[r0 + LS - 1:r0 + LS, C_FG + p * PAIR_W:C_FG + (p + 1) * PAIR_W]
        s_decay = jnp.exp(jnp.broadcast_to(b_last, (PAIR_W, PAIR_W)).T)
        s_new = s_old * s_decay + jnp.where(bd_mask, ut, 0.0)
        gc = slice(GROUP_W + p * PAIR_W, GROUP_W + (p + 1) * PAIR_W)
        obuf[rows, gc] = o_g

        if carry:
            rst[0, p] = r_new
            sst[0, p] = s_new
        for out_ref, new in ((retout_ref, r_new), (hgout_ref, s_new)):
            out_ref[st, 2 * p] = new[:D_HEAD, :D_HEAD]
            out_ref[st, 2 * p + 1] = new[D_HEAD:, D_HEAD:]

    units = [(sg, p) for sg in range(NSEG) for p in range(N_PAIRS)]
    per_chunk = len(units) // (TB // CHUNK)
    for c in range(TB // CHUNK):
        for sg, p in units[c * per_chunk:(c + 1) * per_chunk]:
            pair_unit(sg, p)
        diag_chunk(c)

    o_r = obuf[:, 0:GROUP_W]
    mu = _group_sum(o_r, ones_ref) * (1.0 / D_HEAD)
    dlt = o_r - mu
    var = _group_sum(dlt * dlt, ones_ref) * (1.0 / D_HEAD)
    y_r = dlt * lax.rsqrt(var + RMS_EPS) * _silu(proj[:, C_GR:C_GR + GROUP_W])
    o_g = obuf[:, GROUP_W:2 * GROUP_W] + dbuf[...]
    ms = _group_sum(o_g * o_g, ones_ref) * (1.0 / D_HEAD)
    y_g = o_g * lax.rsqrt(ms + RMS_EPS) * hgn_ref[...] * _silu(proj[:, C_GG:C_GG + GROUP_W])
    mix = jnp.concatenate([y_r, y_g], axis=1).astype(BF16)
    x1 = x + _dot(mix, wout_bf[...])
    x1_ref[...] = x1

    h2 = x1 * lax.rsqrt(jnp.mean(x1 * x1, axis=-1, keepdims=True) + RMS_EPS) * gffn_ref[...]
    h_hi, h_lo = _split(h2, 2)
    w_hi, w_lo = _split(wr_ref[...], 2)
    logits = _dot(h_hi, w_hi) + _dot(h_hi, w_lo) + _dot(h_lo, w_hi) + br_ref[...]
    lane = lax.broadcasted_iota(jnp.int32, (1, ROUTER_W), 1).astype(F32)
    neg = -jnp.inf
    no_lane = float(ROUTER_W)
    gl = jnp.where(lane < N_GROUPS, logits, neg)
    gmax = jnp.max(gl, axis=-1, keepdims=True)
    g_idx = jnp.min(jnp.where(gl == gmax, lane, no_lane), axis=-1, keepdims=True)
    prob_g = 1.0 / jnp.sum(jnp.exp(gl - gmax), axis=-1, keepdims=True)
    e_lo = N_GROUPS + EXPERTS_PER_GROUP * g_idx
    el = jnp.where((lane >= e_lo) & (lane < e_lo + EXPERTS_PER_GROUP), logits, neg)
    v1 = jnp.max(el, axis=-1, keepdims=True)
    i1 = jnp.min(jnp.where(el == v1, lane, no_lane), axis=-1, keepdims=True)
    el2 = jnp.where(lane == i1, neg, el)
    v2 = jnp.max(el2, axis=-1, keepdims=True)
    i2 = jnp.min(jnp.where(el2 == v2, lane, no_lane), axis=-1, keepdims=True)
    t = jnp.exp(v2 - v1)
    p1 = 1.0 / (1.0 + t)
    p2 = t * p1
    e1 = i1 - N_GROUPS
    e2 = i2 - N_GROUPS
    pick1 = lane == e1
    pick2 = lane == e2
    onehot = jnp.where(pick1 | pick2, 1.0, 0.0)
    before = _dot(stri_ref[...], onehot.astype(BF16))
    count = jnp.sum(onehot, axis=0, keepdims=True)
    padded = jnp.floor((count + (SLAB - 1.0)) * (1.0 / SLAB)) * SLAB
    padded8 = jnp.broadcast_to(padded, (8, ROUTER_W)).astype(BF16)
    seg_start = _dot(padded8, sut_ref[...])[0:1]
    at = seg_start + before
    rank1 = jnp.sum(jnp.where(pick1, at, 0.0), axis=-1, keepdims=True)
    rank2 = jnp.sum(jnp.where(pick2, at, 0.0), axis=-1, keepdims=True)
    col = lax.broadcasted_iota(jnp.int32, (1, SORT_W), 1).astype(F32)
    place = jnp.where((col == rank1) | (col == rank2), 1.0, 0.0).astype(BF16)
    h2s_ref[...] = _dot_tn(place, h2.astype(BF16))[:SORT_ROWS].astype(BF16)
    rinfo_ref[...] = (jnp.where(lane == 0, rank1, 0.0) + jnp.where(lane == 1, rank2, 0.0)
                      + jnp.where(lane == 2, prob_g * p1, 0.0) + jnp.where(lane == 3, prob_g * p2, 0.0))
    seg_end = jnp.broadcast_to(seg_start + padded, (ROUTER_W, ROUTER_W)).T
    sub = lax.broadcasted_iota(jnp.int32, (ROUTER_W, 1), 0)
    slab_lo = lane * SLAB
    ended = jnp.where((seg_end <= slab_lo) & (sub < N_EXPERTS), 1.0, 0.0)
    slab_expert = jnp.sum(ended, axis=0, keepdims=True)
    slab_expert = jnp.where(slab_expert < N_EXPERTS, slab_expert, -1.0)
    info_ref[0] = jnp.broadcast_to(slab_expert.astype(jnp.int32), (8, ROUTER_W))


def _slab_gather(table_ref, first, n_slabs, srcs, bufs, sems, slot, *, wait):
    for j in range(n_slabs):
        n = table_ref[first + j]
        row = pl.multiple_of(jnp.maximum(n, 0) * SLAB, SLAB)
        dst_rows = pl.ds(j * SLAB, SLAB)

        @pl.when(n >= 0)
        def _():
            for src, buf, sem in zip(srcs, bufs, sems):
                cp = pltpu.make_async_copy(src.at[pl.ds(row, SLAB)], buf.at[slot, dst_rows], sem.at[slot])
                if wait:
                    cp.wait()
                else:
                    cp.start()

        if not wait:
            @pl.when(n < 0)
            def _():
                for buf in bufs:
                    buf[slot, dst_rows] = jnp.zeros((SLAB,) + buf.shape[2:], buf.dtype)


def _moe_kernel(src_ref, te_ref, nt_ref, h2s_hbm, wg_ref, wu_ref, wd_ref, ys_ref, hbuf, wgu_bf, wd_bf, sem_h):
    i = pl.program_id(0)
    n_tiles = nt_ref[0]
    slot = lax.rem(i, 2)
    gather = functools.partial(_slab_gather, src_ref, srcs=(h2s_hbm,), bufs=(hbuf,), sems=(sem_h,),
                               n_slabs=TILE_SLABS)

    @pl.when(i == 0)
    def _():
        gather(first=0, slot=0, wait=False)

    @pl.when((i == 0) | (te_ref[i] != te_ref[jnp.maximum(i - 1, 0)]))
    def _():
        wgu_bf[:, :D_EXPERT] = wg_ref[0].astype(BF16)
        wgu_bf[:, D_EXPERT:] = wu_ref[0].astype(BF16)
        wd_bf[...] = wd_ref[0].astype(BF16)

    @pl.when(i < n_tiles)
    def _():
        gather(first=i * TILE_SLABS, slot=slot, wait=True)

        @pl.when(i + 1 < n_tiles)
        def _():
            gather(first=(i + 1) * TILE_SLABS, slot=1 - slot, wait=False)

        ab = _dot(hbuf[slot], wgu_bf[...])
        he = (_silu(ab[:, :D_EXPERT]) * ab[:, D_EXPERT:]).astype(BF16)
        ys_ref[...] = _dot(he, wd_bf[...]).astype(BF16)

    @pl.when(i >= n_tiles)
    def _():
        ys_ref[...] = jnp.zeros_like(ys_ref)


def _final_kernel(blocks_p, dest_ref, x1_ref, rinfo_ref, ys_hbm, gfin_ref, yp_ref, ysmp_ref, ybuf, sem):
    i = pl.program_id(0)
    slot = lax.rem(i, 2)
    gather = functools.partial(_slab_gather, dest_ref, srcs=(ys_hbm,), bufs=(ybuf,), sems=(sem,),
                               n_slabs=SLABS_PER_BLOCK)

    @pl.when(i == 0)
    def _():
        ybuf[:, SORT_ROWS:SORT_W, :] = jnp.zeros((2, SORT_W - SORT_ROWS, D_MODEL), BF16)
        gather(first=0, slot=0, wait=False)

    gather(first=i * SLABS_PER_BLOCK, slot=slot, wait=True)

    @pl.when(i + 1 < pl.num_programs(0))
    def _():
        gather(first=(i + 1) * SLABS_PER_BLOCK, slot=1 - slot, wait=False)

    col = lax.broadcasted_iota(jnp.int32, (1, SORT_W), 1).astype(F32)
    yb = ybuf[slot]
    moe = jnp.zeros((TB, D_MODEL), F32)
    for k in range(TOP_K):
        place = jnp.where(col == rinfo_ref[:, k:k + 1], 1.0, 0.0).astype(BF16)
        moe = moe + rinfo_ref[:, TOP_K + k:TOP_K + k + 1] * _dot(place, yb)
    xo = x1_ref[...] + moe
    y = xo * lax.rsqrt(jnp.mean(xo * xo, axis=-1, keepdims=True) + RMS_EPS) * gfin_ref[...]

    @pl.when(i < blocks_p)
    def _():
        yp_ref[...] = y

    @pl.when(i >= blocks_p)
    def _():
        ysmp_ref[...] = y


def _const_spec(shape, pipeline_mode=None):
    nd = len(shape)
    return pl.BlockSpec(shape, lambda i: (0,) * nd, pipeline_mode=pipeline_mode)


def _mixer_call(x2d, ret0, hg0, params, *, seg_len, carry, block_offset, total_blocks, shared=()):
    (gmix, win, lbl, hgn, wout, gffn, wr, br) = params
    T = x2d.shape[0]
    LS = seg_len
    n_seg = TB // LS
    NCH = LS // CHUNK
    n_states = ret0.shape[0]
    n_own = T // TB
    n_fill = 0 if shared else total_blocks - n_own - block_offset
    grid = n_own + n_fill

    f32 = np.float32
    lg = np.log1p(-(f32(2.0) ** (f32(-5.0) - np.arange(N_HEADS, dtype=f32)))).astype(f32)
    tt = np.arange(LS, dtype=f32)
    ch = np.arange(LS) // CHUNK
    dret = np.exp(np.abs(tt[:, None] - tt[None, :])[None] * lg[:, None, None]).astype(f32)
    dret = np.where((ch[None, :] <= ch[:, None])[None], dret, f32(0.0))
    lg_lane = np.repeat(lg, D_HEAD)[None, :]
    xi = np.exp((tt[:, None] + f32(1.0)) * lg_lane).astype(f32)
    zeta = np.exp((f32(LS - 1.0) - tt)[:, None] * lg_lane).astype(f32)
    gls = np.exp(f32(LS) * lg_lane).astype(f32)
    half = D_HEAD // 2
    inv_freq = ROPE_BASE ** (-jnp.arange(half, dtype=F32) / half)
    invf = jnp.tile(inv_freq, PAIR_W // half)[None, :]
    sgn = np.where((np.arange(PAIR_W) % D_HEAD) < half, f32(-1.0), f32(1.0))[None, :]
    tri = jnp.asarray(np.tril(np.ones((LS, LS), np.float32)), BF16)
    lane_head = np.arange(256) // D_HEAD
    ones_bd = jnp.asarray((lane_head[:, None] == lane_head[None, :]).astype(np.float32), BF16)
    cmask = jnp.asarray((ch[:, None] == ch[None, :]).astype(np.float32))

    if carry:
        state_spec = pl.BlockSpec((1, N_HEADS, D_HEAD, D_HEAD), lambda i: (0, 0, 0, 0))
    else:
        state_spec = pl.BlockSpec((n_seg, N_HEADS, D_HEAD, D_HEAD), lambda i: (i, 0, 0, 0))
    single = pl.Buffered(1)
    stri = jnp.asarray(np.tril(np.ones((TB, TB), np.float32), -1), BF16)
    sut = jnp.asarray(np.triu(np.ones((ROUTER_W, ROUTER_W), np.float32), 1), BF16)
    in_row_spec = pl.BlockSpec((TB, D_MODEL), lambda i: (jnp.minimum(i, n_own - 1), 0))
    out_row_spec = lambda rows, w: pl.BlockSpec((rows, w), lambda i: (i + block_offset, 0))
    nw = (N_SUB - 1) * N_PAIRS * PAIR_W
    nx = max(NCH - 1, 1) * N_PAIRS * PAIR_W
    n_in = N_MIXER_INPUTS

    kern = functools.partial(_mixer_kernel, (LS, n_seg, carry, n_own, n_fill, len(shared)))
    return pl.pallas_call(
        kern,
        grid=(grid,),
        in_specs=[
            in_row_spec,
            _const_spec((1, D_MODEL)), _const_spec(win.shape, single), _const_spec((1, PAIR_W)),
            _const_spec((1, PAIR_W)), _const_spec(lbl.shape), _const_spec((1, GROUP_W)),
            _const_spec(wout.shape, single), _const_spec((1, D_MODEL)),
            _const_spec(wr.shape), _const_spec((1, ROUTER_W)),
            _const_spec((LS, LS)), _const_spec((N_HEADS, LS, LS)), _const_spec((LS, GROUP_W)),
            _const_spec((LS, GROUP_W)), _const_spec((1, GROUP_W)), _const_spec((256, 256)), _const_spec((LS, LS)),
            _const_spec((TB, TB)), _const_spec((ROUTER_W, ROUTER_W)),
            state_spec, state_spec,
        ] + [pl.BlockSpec(memory_space=pl.ANY)] * len(shared),
        out_specs=[out_row_spec(TB, D_MODEL), out_row_spec(SORT_ROWS, D_MODEL), out_row_spec(TB, ROUTER_W),
                   pl.BlockSpec((1, 8, ROUTER_W), lambda i: (i + block_offset, 0, 0)),
                   state_spec, state_spec],
        out_shape=[
            jax.ShapeDtypeStruct((total_blocks * TB, D_MODEL), F32),
            jax.ShapeDtypeStruct((total_blocks * SORT_ROWS, D_MODEL), BF16),
            jax.ShapeDtypeStruct((total_blocks * TB, ROUTER_W), F32),
            jax.ShapeDtypeStruct((total_blocks, 8, ROUTER_W), jnp.int32),
            jax.ShapeDtypeStruct((n_states, N_HEADS, D_HEAD, D_HEAD), F32),
            jax.ShapeDtypeStruct((n_states, N_HEADS, D_HEAD, D_HEAD), F32),
        ],
        input_output_aliases={n_in + k: k for k in range(len(shared))},
        scratch_shapes=[
            *[pltpu.VMEM((TB, GROUP_W), F32) for _ in range(8)],
            pltpu.VMEM((TB, GROUP_W), F32),
            pltpu.VMEM((TB, 2 * GROUP_W), F32),
            pltpu.VMEM((TB, GROUP_W), F32),
            pltpu.VMEM((TB, nw), BF16), pltpu.VMEM((TB, nw), BF16),
            pltpu.VMEM((TB, nx), BF16), pltpu.VMEM((TB, nx), BF16),
            pltpu.VMEM((TB, GROUP_W), BF16), pltpu.VMEM((TB, GROUP_W), BF16),
            pltpu.VMEM((N_SUB * (SUB // 2) * (SUB + SUB // 2), GROUP_W), BF16),
            pltpu.VMEM((TB, PAIR_W), F32), pltpu.VMEM((TB, PAIR_W), F32),
            pltpu.VMEM(win.shape, BF16), pltpu.VMEM(wout.shape, BF16),
            pltpu.VMEM((1, N_PAIRS, PAIR_W, PAIR_W), F32),
            pltpu.VMEM((1, N_PAIRS, PAIR_W, PAIR_W), F32),
        ],
        compiler_params=pltpu.CompilerParams(
            dimension_semantics=("arbitrary",), vmem_limit_bytes=V7X_VMEM_LIMIT),
    )(x2d, gmix, win, invf, sgn, lbl, hgn, wout, gffn, wr, br, tri, dret, xi, zeta, gls, ones_bd, cmask, stri, sut,
      ret0, hg0, *shared)


def _dispatch_tables(info, n_tiles_max):
    slab_group = info[:, 0, :SLABS_PER_BLOCK].reshape(-1)
    n_slabs = slab_group.shape[0]
    valid = slab_group >= 0
    onehot = (slab_group[:, None] == jnp.arange(N_EXPERTS, dtype=jnp.int32)[None, :]).astype(jnp.int32)
    within = jnp.cumsum(onehot, axis=0) - onehot
    tiles = (jnp.sum(onehot, axis=0) + TILE_SLABS - 1) // TILE_SLABS
    tile_end = jnp.cumsum(tiles)
    g = jnp.clip(slab_group, 0, N_EXPERTS - 1)
    dest = (tile_end - tiles)[g] * TILE_SLABS + jnp.take_along_axis(within, g[:, None], axis=1)[:, 0]
    dest = jnp.where(valid, dest, -1).astype(jnp.int32)
    n_slots = n_tiles_max * TILE_SLABS
    src = jnp.full((n_slots,), -1, jnp.int32).at[jnp.where(valid, dest, n_slots)].set(
        jnp.arange(n_slabs, dtype=jnp.int32), mode="drop")
    tile_group = jnp.sum(jnp.arange(n_tiles_max, dtype=jnp.int32)[:, None] >= tile_end[None, :], axis=1)
    tile_group = jnp.minimum(tile_group, N_EXPERTS - 1).astype(jnp.int32)
    return src, dest, tile_group, tile_end[-1:].astype(jnp.int32)


def _moe_call(src, tile_expert, n_tiles, h2s, w_gate, w_up, w_down):
    n_tiles_max = tile_expert.shape[0]
    expert_of_tile = lambda i, s, te, nt: (te[i], 0, 0)
    return pl.pallas_call(
        _moe_kernel,
        grid_spec=pltpu.PrefetchScalarGridSpec(
            num_scalar_prefetch=3,
            grid=(n_tiles_max,),
            in_specs=[
                pl.BlockSpec(memory_space=pl.ANY),
                pl.BlockSpec((1, D_MODEL, D_EXPERT), expert_of_tile),
                pl.BlockSpec((1, D_MODEL, D_EXPERT), expert_of_tile),
                pl.BlockSpec((1, D_EXPERT, D_MODEL), expert_of_tile),
            ],
            out_specs=pl.BlockSpec((TM, D_MODEL), lambda i, s, te, nt: (i, 0)),
            scratch_shapes=[
                pltpu.VMEM((2, TM, D_MODEL), BF16),
                pltpu.VMEM((D_MODEL, 2 * D_EXPERT), BF16),
                pltpu.VMEM((D_EXPERT, D_MODEL), BF16),
                pltpu.SemaphoreType.DMA((2,)),
            ],
        ),
        out_shape=jax.ShapeDtypeStruct((n_tiles_max * TM, D_MODEL), BF16),
        compiler_params=pltpu.CompilerParams(
            dimension_semantics=("arbitrary",), vmem_limit_bytes=V7X_VMEM_LIMIT),
    )(src, tile_expert, n_tiles, h2s, w_gate, w_up, w_down)


def _final_call(dest, x1, rinfo, ys, gfin, blocks_p):
    n_blocks = x1.shape[0] // TB
    return pl.pallas_call(
        functools.partial(_final_kernel, blocks_p),
        grid_spec=pltpu.PrefetchScalarGridSpec(
            num_scalar_prefetch=1,
            grid=(n_blocks,),
            in_specs=[
                pl.BlockSpec((TB, D_MODEL), lambda i, d: (i, 0)),
                pl.BlockSpec((TB, ROUTER_W), lambda i, d: (i, 0)),
                pl.BlockSpec(memory_space=pl.ANY),
                pl.BlockSpec((1, D_MODEL), lambda i, d: (0, 0)),
            ],
            out_specs=[
                pl.BlockSpec((TB, D_MODEL), lambda i, d: (jnp.minimum(i, blocks_p - 1), 0)),
                pl.BlockSpec((TB, D_MODEL), lambda i, d: (jnp.maximum(i - blocks_p, 0), 0)),
            ],
            scratch_shapes=[pltpu.VMEM((2, SORT_W, D_MODEL), BF16), pltpu.SemaphoreType.DMA((2,))],
        ),
        out_shape=[jax.ShapeDtypeStruct((blocks_p * TB, D_MODEL), F32),
                   jax.ShapeDtypeStruct(((n_blocks - blocks_p) * TB, D_MODEL), F32)],
        compiler_params=pltpu.CompilerParams(
            dimension_semantics=("arbitrary",), vmem_limit_bytes=V7X_VMEM_LIMIT),
    )(dest, x1, rinfo, ys, gfin)


def kernel(x_prompt, x_sample, state_ret, state_hgrn, norm_mix_g, w_in, hgrn_lb_logits, hgrn_norm_g, w_out,
           norm_ffn_g, w_router_group, b_router_group, w_router_expert, b_router_expert, w_exp_gate, w_exp_up,
           w_exp_down, norm_final_g):
    depth = w_in.shape[0]
    assert depth == 1 and hgrn_lb_logits.shape[0] == 2, "single-layer configuration only"
    bp, seq, d = x_prompt.shape
    db, dec_len, _ = x_sample.shape
    assert bp == 1 and d == D_MODEL and dec_len == CHUNK and seq % 256 == 0 and db % 4 == 0

    pad = ROUTER_W - N_GROUPS - N_EXPERTS
    wr = jnp.concatenate([w_router_group[0], w_router_expert[0], jnp.zeros((D_MODEL, pad), F32)], axis=1)
    br = jnp.concatenate([b_router_group[0], b_router_expert[0], jnp.zeros((pad,), F32)])[None, :]
    params = (norm_mix_g[0][None, :], w_in[0], hgrn_lb_logits,
              jnp.tile(hgrn_norm_g[0], N_HEADS)[None, :], w_out[0], norm_ffn_g[0][None, :], wr, br)

    zeros_state = jnp.zeros((1, N_HEADS, D_HEAD, D_HEAD), F32)
    blocks_p = seq // TB
    blocks_s = db * dec_len // TB
    n_blocks = blocks_p + blocks_s
    *shared, ret_p, hg_p = _mixer_call(
        x_prompt.reshape(seq, d), zeros_state, zeros_state, params, seg_len=TB, carry=True,
        block_offset=0, total_blocks=n_blocks)
    x1, h2s, rinfo, info, ret_s, hg_s = _mixer_call(
        x_sample.reshape(db * dec_len, d), state_ret[0], state_hgrn[0], params, seg_len=CHUNK, carry=False,
        block_offset=blocks_p, total_blocks=n_blocks, shared=tuple(shared))

    n_tiles_max = -(-n_blocks * SLABS_PER_BLOCK // TILE_SLABS) + N_EXPERTS
    src, dest, tile_expert, n_tiles = _dispatch_tables(info, n_tiles_max)
    ys = _moe_call(src, tile_expert, n_tiles, h2s, w_exp_gate[0], w_exp_up[0], w_exp_down[0])
    y_p, y_s = _final_call(dest, x1, rinfo, ys, norm_final_g[None, :], blocks_p)
    return (y_p.reshape(bp, seq, d), y_s.reshape(db, dec_len, d), ret_p[None], hg_p[None], ret_s[None], hg_s[None])
```

```python
import functools

import numpy as np
import jax
import jax.numpy as jnp
from jax import lax
from jax.experimental import pallas as pl
from jax.experimental.pallas import tpu as pltpu

F32 = jnp.float32
BF16 = jnp.bfloat16

D_MODEL = 1024
N_HEADS = 8
D_HEAD = 64
GROUP_W = N_HEADS * D_HEAD
N_PAIRS = N_HEADS // 2
PAIR_W = 2 * D_HEAD
CHUNK = 64
SUB = 16
N_SUB = CHUNK // SUB
PAST_LEN = 2048
ROPE_BASE = 10000.0
RMS_EPS = 1e-6
LOG2E = 1.4426950408889634
N_GROUPS = 4
EXPERTS_PER_GROUP = 8
N_EXPERTS = N_GROUPS * EXPERTS_PER_GROUP
D_EXPERT = 256
ROUTER_W = 128
V7X_VMEM_LIMIT = 60 * 1024 * 1024
TB = 256
SLAB = 16
SLABS_PER_BLOCK = TB // SLAB + N_GROUPS
SORT_ROWS = SLABS_PER_BLOCK * SLAB
SORT_W = 384
TILE_SLABS = 32
TM = TILE_SLABS * SLAB

C_QR, C_KR, C_VR, C_GR, C_QG, C_FG, C_VG, C_GG = (i * GROUP_W for i in range(8))


def _dot(a, b):
    return jnp.dot(a, b, preferred_element_type=F32)


def _dot_nt(a, b):
    return lax.dot_general(a, b, (((1,), (1,)), ((), ())), preferred_element_type=F32)


def _dot_tn(a, b):
    return lax.dot_general(a, b, (((0,), (0,)), ((), ())), preferred_element_type=F32)


def _split(x, n):
    parts = []
    for _ in range(n):
        p = x.astype(BF16)
        parts.append(p)
        x = x - p.astype(F32)
    return parts


def _group_sum(x, ones_ref):
    xb = x.astype(BF16)
    return jnp.concatenate([_dot(xb[:, c * 256:(c + 1) * 256], ones_ref[...]) for c in range(2)], axis=1)


class _ColumnGroups:
    def __init__(self, bufs):
        self.bufs = bufs

    def _locate(self, idx):
        rows, cols = idx
        g = cols.start // GROUP_W
        assert (cols.stop - 1) // GROUP_W == g
        return self.bufs[g], (rows, slice(cols.start - g * GROUP_W, cols.stop - g * GROUP_W))

    def __getitem__(self, idx):
        buf, at = self._locate(idx)
        return buf[at]

    def __setitem__(self, idx, value):
        buf, at = self._locate(idx)
        buf[at] = value


def _silu(x):
    return x * jax.nn.sigmoid(x)


N_MIXER_INPUTS = 21


def _mixer_kernel(cfg, *refs):
    pipelined, n_own, n_fill, n_alias = cfg[-4:]
    n_body = n_own + (1 if pipelined else 0)
    i = pl.program_id(0)

    @pl.when(i < n_body)
    def _():
        _mixer_body(cfg, *refs)

    if n_fill:
        @pl.when(i >= n_body)
        def _():
            for out_ref in refs[N_MIXER_INPUTS + n_alias:N_MIXER_INPUTS + n_alias + 5]:
                out_ref[...] = jnp.zeros_like(out_ref)


def _mixer_body(cfg, x_ref, gmix_ref, win_ref, invf_ref, sgn_ref, lbl_ref, hgn_ref, wout_ref, gffn_ref,
                wr_ref, br_ref, tri_ref, dret_ref, xi_ref, zeta_ref, gls_ref, ones_ref, cmask_ref, stri_ref,
                ret0_ref, hg0_ref, *rest):
    LS, NSEG, carry, pipelined, n_own, _, n_alias = cfg
    (x1_ref, h2s_ref, gs_ref, rinfo_ref, info_ref, retout_ref, hgout_ref,
     *proj_bufs, kbuf, obuf, dbuf, xprev, qw, kw, qx, kx, qin_buf, kout_buf, pbuf, cos_t, sin_t, win_bf, wout_bf,
     rst, sst) = rest[n_alias:]
    proj = _ColumnGroups(proj_bufs)
    NCH = LS // CHUNK
    i = pl.program_id(0)
    lane128 = lax.broadcasted_iota(jnp.int32, (1, PAIR_W), 1)
    head_a = lane128 < D_HEAD
    bd_mask = (lax.broadcasted_iota(jnp.int32, (PAIR_W, PAIR_W), 0) < D_HEAD) == head_a

    def pair_state(ref, sg, p):
        z = jnp.zeros((D_HEAD, D_HEAD), F32)
        return jnp.concatenate([jnp.concatenate([ref[sg, 2 * p], z], axis=1),
                                jnp.concatenate([z, ref[sg, 2 * p + 1]], axis=1)], axis=0)

    @pl.when(i == 0)
    def _():
        for c in range(0, 8 * GROUP_W, GROUP_W):
            win_bf[:, c:c + GROUP_W] = win_ref[:, c:c + GROUP_W].astype(BF16)
        wout_bf[...] = wout_ref[...].astype(BF16)
        if carry:
            for p in range(N_PAIRS):
                rst[0, p] = pair_state(ret0_ref, 0, p)
                sst[0, p] = pair_state(hg0_ref, 0, p)
        row = lax.broadcasted_iota(jnp.int32, (TB, 1), 0)
        ang_row = (row & (LS - 1)).astype(F32) * invf_ref[...]
        cos_t[...] = jnp.cos(ang_row)
        sin_t[...] = jnp.sin(ang_row)
        qw[...] = jnp.zeros_like(qw)
        kw[...] = jnp.zeros_like(kw)
        if NCH > 1:
            qx[...] = jnp.zeros_like(qx)
            kx[...] = jnp.zeros_like(kx)
        if pipelined:
            for buf in (obuf, dbuf, xprev, proj_bufs[C_GR // GROUP_W], proj_bufs[C_GG // GROUP_W]):
                buf[...] = jnp.zeros_like(buf)

    if pipelined:
        _mixer_back(proj, obuf, dbuf, xprev, hgn_ref, ones_ref, wout_bf, gffn_ref, wr_ref, br_ref, stri_ref,
                    x1_ref, h2s_ref, gs_ref, rinfo_ref, info_ref)
    keep = i < n_own

    x = x_ref[...]
    h = x * lax.rsqrt(jnp.mean(x * x, axis=-1, keepdims=True) + RMS_EPS) * gmix_ref[...]
    hb = h.astype(BF16)

    def project(c0, c1):
        for c in range(c0, c1, GROUP_W):
            proj[:, c:c + GROUP_W] = _dot(hb, win_bf[:, c:c + GROUP_W])

    project(C_QG, C_GG)
    project(C_QR, C_VR)

    la = lbl_ref[0:1, :]
    lb_ = lbl_ref[1:2, :]
    lmax = jnp.maximum(la, lb_)
    ea = jnp.exp(la - lmax)
    lbv = ea / (ea + jnp.exp(lb_ - lmax))
    qg = proj[:, C_QG:C_QG + GROUP_W]
    proj[:, C_QG:C_QG + GROUP_W] = _silu(qg)
    f = lbv + (1.0 - lbv) * jax.nn.sigmoid(proj[:, C_FG:C_FG + GROUP_W])
    kbuf[...] = 1.0 - f
    logf = jnp.log(f)
    for sg in range(NSEG):
        rows = slice(sg * LS, (sg + 1) * LS)
        parts = _split(logf[rows], 3)
        proj[rows, C_FG:C_FG + GROUP_W] = sum(_dot(tri_ref[...], p) for p in parts)

    project(C_VR, C_QG)
    project(C_GG, C_GG + GROUP_W)

    start = jnp.full((8, PAIR_W), i * TB if carry else PAST_LEN, jnp.int32).astype(F32)
    ang0 = (start * invf_ref[...])[0:1]
    c0 = jnp.cos(ang0)
    s0 = jnp.sin(ang0)
    cos = cos_t[...] * c0 - sin_t[...] * s0
    sin = (sin_t[...] * c0 + cos_t[...] * s0) * sgn_ref[...]
    first_half = (lane128 & (D_HEAD - 1)) < (D_HEAD // 2)
    for blk in range(2 * N_PAIRS):
        cols = slice(blk * PAIR_W, (blk + 1) * PAIR_W)
        xx = proj[:, cols]
        partner = jnp.where(first_half, pltpu.roll(xx, PAIR_W - D_HEAD // 2, 1), pltpu.roll(xx, D_HEAD // 2, 1))
        r = xx * cos + partner * sin
        if blk < N_PAIRS:
            r = r * (D_HEAD ** -0.5)
        proj[:, cols] = r

    HALF = SUB // 2
    riota = lax.broadcasted_iota(jnp.int32, (HALF, 1), 0)
    P_ROWS = HALF * SUB + HALF * HALF

    def diag_chunk(c):
        for sb in range(N_SUB):
            rs = c * CHUNK + sb * SUB
            q16 = proj[rs:rs + SUB, C_QG:C_QG + GROUP_W]
            b16 = proj[rs:rs + SUB, C_FG:C_FG + GROUP_W] * LOG2E
            k16 = kbuf[rs:rs + SUB, :]
            base = sb * P_ROWS
            upper = []
            for s in range(SUB):
                ks = k16[s:s + 1]
                bs = b16[s:s + 1]
                hi = q16[HALF:] * ks * jnp.exp2(b16[HALF:] - bs)
                if s < HALF:
                    lo = q16[:HALF] * ks * jnp.exp2(b16[:HALF] - bs)
                    lo = jnp.where(riota >= s, lo, 0.0)
                    pbuf[base + s * SUB:base + (s + 1) * SUB, :] = jnp.concatenate([lo, hi], axis=0).astype(BF16)
                else:
                    upper.append(jnp.where(riota >= s - HALF, hi, 0.0))
                    if len(upper) == 2:
                        at = base + HALF * SUB + (s - HALF - 1) * HALF
                        pbuf[at:at + SUB, :] = jnp.concatenate(upper, axis=0).astype(BF16)
                        upper = []
        sc = [_dot(pbuf[:, hh * 256:(hh + 1) * 256], ones_ref[...]) for hh in range(2)]
        for sb in range(N_SUB):
            rs = c * CHUNK + sb * SUB
            v16 = proj[rs:rs + SUB, C_VG:C_VG + GROUP_W]
            base = sb * P_ROWS
            acc_lo = [jnp.zeros((HALF, 256), F32) for _ in range(2)]
            acc_hi = [jnp.zeros((HALF, 256), F32) for _ in range(2)]
            for s in range(SUB):
                for hh in range(2):
                    vs = v16[s:s + 1, hh * 256:(hh + 1) * 256]
                    if s < HALF:
                        at = base + s * SUB
                        acc_lo[hh] = acc_lo[hh] + sc[hh][at:at + HALF] * vs
                        acc_hi[hh] = acc_hi[hh] + sc[hh][at + HALF:at + SUB] * vs
                    else:
                        at = base + HALF * SUB + (s - HALF) * HALF
                        acc_hi[hh] = acc_hi[hh] + sc[hh][at:at + HALF] * vs
            dbuf[rs:rs + HALF, :] = jnp.concatenate(acc_lo, axis=1)
            dbuf[rs + HALF:rs + SUB, :] = jnp.concatenate(acc_hi, axis=1)

    for sg in range(NSEG):
        r0 = sg * LS
        rows = slice(r0, r0 + LS)
        st = 0 if carry else sg
        qg_s = proj[rows, C_QG:C_QG + GROUP_W]
        kg_s = kbuf[rows, :]
        b_s = proj[rows, C_FG:C_FG + GROUP_W]

        for c in range(NCH):
            for j in range(N_SUB - 1):
                e = c * CHUNK + j * SUB + SUB - 1
                bref = b_s[e:e + 1, :]
                q_rows = slice(e + 1, (c + 1) * CHUNK)
                k_rows = slice(e + 1 - SUB, e + 1)
                qt = (qg_s[q_rows] * jnp.exp(b_s[q_rows] - bref)).astype(BF16)
                kt = (kg_s[k_rows] * jnp.exp(bref - b_s[k_rows])).astype(BF16)
                for p in range(N_PAIRS):
                    dst = slice((p * (N_SUB - 1) + j) * PAIR_W, (p * (N_SUB - 1) + j + 1) * PAIR_W)
                    src = slice(p * PAIR_W, (p + 1) * PAIR_W)
                    qw[r0 + q_rows.start:r0 + q_rows.stop, dst] = qt[:, src]
                    kw[r0 + k_rows.start:r0 + k_rows.stop, dst] = kt[:, src]
        for c in range(NCH - 1):
            e = c * CHUNK + CHUNK - 1
            bref = b_s[e:e + 1, :]
            q_rows = slice(e + 1, LS)
            k_rows = slice(c * CHUNK, e + 1)
            qt = (qg_s[q_rows] * jnp.exp(b_s[q_rows] - bref)).astype(BF16)
            kt = (kg_s[k_rows] * jnp.exp(bref - b_s[k_rows])).astype(BF16)
            for p in range(N_PAIRS):
                dst = slice((p * (NCH - 1) + c) * PAIR_W, (p * (NCH - 1) + c + 1) * PAIR_W)
                src = slice(p * PAIR_W, (p + 1) * PAIR_W)
                qx[r0 + q_rows.start:r0 + q_rows.stop, dst] = qt[:, src]
                kx[r0 + k_rows.start:r0 + k_rows.stop, dst] = kt[:, src]

        b_last = b_s[LS - 1:LS, :]
        qin_buf[rows, :] = (qg_s * jnp.exp(b_s)).astype(BF16)
        kout_buf[rows, :] = (kg_s * jnp.exp(b_last - b_s)).astype(BF16)

    def pair_unit(sg, p):
        r0 = sg * LS
        rows = slice(r0, r0 + LS)
        st = 0 if carry else sg
        pc = slice(p * PAIR_W, (p + 1) * PAIR_W)

        qb = proj[rows, C_QR + p * PAIR_W:C_QR + (p + 1) * PAIR_W].astype(BF16)
        kr = proj[rows, C_KR + p * PAIR_W:C_KR + (p + 1) * PAIR_W]
        kb = kr.astype(BF16)
        vb = proj[rows, C_VR + p * PAIR_W:C_VR + (p + 1) * PAIR_W].astype(BF16)
        zero = jnp.zeros_like(qb)
        s_a = _dot_nt(jnp.where(head_a, qb, zero), kb) * dret_ref[2 * p]
        s_b = _dot_nt(jnp.where(head_a, zero, qb), kb) * dret_ref[2 * p + 1]
        o_r = jnp.where(head_a, _dot(s_a.astype(BF16), vb), _dot(s_b.astype(BF16), vb))
        r_old = rst[0, p] if carry else pair_state(ret0_ref, sg, p)
        o_r = o_r + _dot(qb, r_old.astype(BF16)) * xi_ref[:, pc]
        u = _dot_tn((kr * zeta_ref[:, pc]).astype(BF16), vb)
        r_new = r_old * gls_ref[:, pc] + jnp.where(bd_mask, u, 0.0)
        if pipelined:
            r_new = jnp.where(keep, r_new, r_old)
        obuf[rows, pc] = o_r

        vg = proj[rows, C_VG + p * PAIR_W:C_VG + (p + 1) * PAIR_W].astype(BF16)
        wc = slice(p * (N_SUB - 1) * PAIR_W, (p + 1) * (N_SUB - 1) * PAIR_W)
        qw_p = qw[rows, wc]
        kw_p = kw[rows, wc]
        lane_w = lax.broadcasted_iota(jnp.int32, (1, qw_p.shape[1]), 1)
        head_a_w = (lane_w & (PAIR_W - 1)) < D_HEAD
        zw = jnp.zeros_like(qw_p)
        g_a = _dot_nt(jnp.where(head_a_w, qw_p, zw), kw_p)
        g_b = _dot_nt(jnp.where(head_a_w, zw, qw_p), kw_p)
        if NCH > 1:
            g_a = g_a * cmask_ref[...]
            g_b = g_b * cmask_ref[...]
            xc = slice(p * (NCH - 1) * PAIR_W, (p + 1) * (NCH - 1) * PAIR_W)
            qx_p = qx[rows, xc]
            kx_p = kx[rows, xc]
            lane_x = lax.broadcasted_iota(jnp.int32, (1, qx_p.shape[1]), 1)
            head_a_x = (lane_x & (PAIR_W - 1)) < D_HEAD
            zx = jnp.zeros_like(qx_p)
            g_a = g_a + _dot_nt(jnp.where(head_a_x, qx_p, zx), kx_p)
            g_b = g_b + _dot_nt(jnp.where(head_a_x, zx, qx_p), kx_p)
        o_g = jnp.where(head_a, _dot(g_a.astype(BF16), vg), _dot(g_b.astype(BF16), vg))
        s_old = sst[0, p] if carry else pair_state(hg0_ref, sg, p)
        o_g = o_g + _dot(qin_buf[rows, pc], s_old.astype(BF16))
        ut = _dot_tn(kout_buf[rows, pc], vg)
        b_last = proj[r0 + LS - 1:r0 + LS, C_FG + p * PAIR_W:C_FG + (p + 1) * PAIR_W]
        s_decay = jnp.exp(jnp.broadcast_to(b_last, (PAIR_W, PAIR_W)).T)
        s_new = s_old * s_decay + jnp.where(bd_mask, ut, 0.0)
        if pipelined:
            s_new = jnp.where(keep, s_new, s_old)
        gc = slice(GROUP_W + p * PAIR_W, GROUP_W + (p + 1) * PAIR_W)
        obuf[rows, gc] = o_g

        if carry:
            rst[0, p] = r_new
            sst[0, p] = s_new
        for out_ref, new in ((retout_ref, r_new), (hgout_ref, s_new)):
            out_ref[st, 2 * p] = new[:D_HEAD, :D_HEAD]
            out_ref[st, 2 * p + 1] = new[D_HEAD:, D_HEAD:]

    units = [(sg, p) for sg in range(NSEG) for p in range(N_PAIRS)]
    per_chunk = len(units) // (TB // CHUNK)
    for c in range(TB // CHUNK):
        for sg, p in units[c * per_chunk:(c + 1) * per_chunk]:
            pair_unit(sg, p)
        diag_chunk(c)

    xprev[...] = x
    if not pipelined:
        _mixer_back(proj, obuf, dbuf, xprev, hgn_ref, ones_ref, wout_bf, gffn_ref, wr_ref, br_ref, stri_ref,
                    x1_ref, h2s_ref, gs_ref, rinfo_ref, info_ref)


def _mixer_back(proj, obuf, dbuf, xprev, hgn_ref, ones_ref, wout_bf, gffn_ref, wr_ref, br_ref, stri_ref,
                x1_ref, h2s_ref, gs_ref, rinfo_ref, info_ref):
    o_r = obuf[:, 0:GROUP_W]
    mu = _group_sum(o_r, ones_ref) * (1.0 / D_HEAD)
    dlt = o_r - mu
    var = _group_sum(dlt * dlt, ones_ref) * (1.0 / D_HEAD)
    y_r = dlt * lax.rsqrt(var + RMS_EPS) * _silu(proj[:, C_GR:C_GR + GROUP_W])
    o_g = obuf[:, GROUP_W:2 * GROUP_W] + dbuf[...]
    ms = _group_sum(o_g * o_g, ones_ref) * (1.0 / D_HEAD)
    y_g = o_g * lax.rsqrt(ms + RMS_EPS) * hgn_ref[...] * _silu(proj[:, C_GG:C_GG + GROUP_W])
    mix = jnp.concatenate([y_r, y_g], axis=1).astype(BF16)
    x1 = xprev[...] + _dot(mix, wout_bf[...])
    x1_ref[...] = x1

    h2 = x1 * lax.rsqrt(jnp.mean(x1 * x1, axis=-1, keepdims=True) + RMS_EPS) * gffn_ref[...]
    h_hi, h_lo = _split(h2, 2)
    w_hi, w_lo = _split(wr_ref[...], 2)
    logits = _dot(h_hi, w_hi) + _dot(h_hi, w_lo) + _dot(h_lo, w_hi) + br_ref[...]
    lane = lax.broadcasted_iota(jnp.int32, (1, ROUTER_W), 1).astype(F32)
    neg = -jnp.inf
    no_lane = float(ROUTER_W)
    gl = jnp.where(lane < N_GROUPS, logits, neg)
    gmax = jnp.max(gl, axis=-1, keepdims=True)
    g_idx = jnp.min(jnp.where(gl == gmax, lane, no_lane), axis=-1, keepdims=True)
    prob_g = 1.0 / jnp.sum(jnp.exp(gl - gmax), axis=-1, keepdims=True)
    e_lo = N_GROUPS + EXPERTS_PER_GROUP * g_idx
    el = jnp.where((lane >= e_lo) & (lane < e_lo + EXPERTS_PER_GROUP), logits, neg)
    v1 = jnp.max(el, axis=-1, keepdims=True)
    i1 = jnp.min(jnp.where(el == v1, lane, no_lane), axis=-1, keepdims=True)
    el2 = jnp.where(lane == i1, neg, el)
    v2 = jnp.max(el2, axis=-1, keepdims=True)
    i2 = jnp.min(jnp.where(el2 == v2, lane, no_lane), axis=-1, keepdims=True)
    t = jnp.exp(v2 - v1)
    p1 = 1.0 / (1.0 + t)
    p2 = t * p1
    gate = jnp.where(lane == i1, prob_g * p1, 0.0) + jnp.where(lane == i2, prob_g * p2, 0.0)

    onehot = jnp.where(lane == g_idx, 1.0, 0.0)
    before = _dot(stri_ref[...], onehot.astype(BF16))
    count = jnp.sum(onehot, axis=0, keepdims=True)
    padded = jnp.floor((count + (SLAB - 1.0)) * (1.0 / SLAB)) * SLAB
    padded8 = jnp.broadcast_to(padded, (8, ROUTER_W))
    seg_start = sum(jnp.where(lane >= k, pltpu.roll(padded8, k, 1), 0.0) for k in range(1, N_GROUPS))[0:1]
    rank = jnp.sum(onehot * (seg_start + before), axis=-1, keepdims=True)
    col = lax.broadcasted_iota(jnp.int32, (1, SORT_W), 1).astype(F32)
    place = jnp.where(col == rank, 1.0, 0.0).astype(BF16)
    h2s_ref[...] = _dot_tn(place, h2.astype(BF16))[:SORT_ROWS].astype(BF16)
    gs_ref[...] = sum(_dot_tn(place, part) for part in _split(gate, 3))[:SORT_ROWS]
    rinfo_ref[...] = jnp.where(lane == 0, rank, 0.0)
    slab_lo = (lane * SLAB).astype(F32)
    slab_group = jnp.full((1, ROUTER_W), -1.0, F32)
    for g in range(N_GROUPS):
        s_g = jnp.sum(jnp.where(lane == g, seg_start, 0.0), axis=-1, keepdims=True)
        n_g = jnp.sum(jnp.where(lane == g, padded, 0.0), axis=-1, keepdims=True)
        slab_group = jnp.where((slab_lo >= s_g) & (slab_lo < s_g + n_g), float(g), slab_group)
    info_ref[0] = jnp.broadcast_to(slab_group.astype(jnp.int32), (8, ROUTER_W))


def _slab_gather(table_ref, first, n_slabs, srcs, bufs, sems, slot, *, wait):
    for j in range(n_slabs):
        n = table_ref[first + j]
        row = pl.multiple_of(jnp.maximum(n, 0) * SLAB, SLAB)
        dst_rows = pl.ds(j * SLAB, SLAB)

        @pl.when(n >= 0)
        def _():
            for src, buf, sem in zip(srcs, bufs, sems):
                cp = pltpu.make_async_copy(src.at[pl.ds(row, SLAB)], buf.at[slot, dst_rows], sem.at[slot])
                if wait:
                    cp.wait()
                else:
                    cp.start()

        if not wait:
            @pl.when(n < 0)
            def _():
                for buf in bufs:
                    buf[slot, dst_rows] = jnp.zeros((SLAB,) + buf.shape[2:], buf.dtype)


def _moe_kernel(src_ref, tg_ref, nt_ref, h2s_hbm, gs_hbm, wg_ref, wu_ref, wd_ref, ys_ref, hbuf, gbuf, wgu_bf, wd_bf,
                sem_h, sem_g):
    i = pl.program_id(0)
    n_tiles = nt_ref[0]
    slot = lax.rem(i, 2)
    gather = functools.partial(_slab_gather, src_ref, srcs=(h2s_hbm, gs_hbm), bufs=(hbuf, gbuf),
                               sems=(sem_h, sem_g), n_slabs=TILE_SLABS)

    @pl.when(i == 0)
    def _():
        gather(first=0, slot=0, wait=False)

    @pl.when((i == 0) | (tg_ref[i] != tg_ref[jnp.maximum(i - 1, 0)]))
    def _():
        for e in range(EXPERTS_PER_GROUP):
            wgu_bf[e, :, :D_EXPERT] = wg_ref[e].astype(BF16)
            wgu_bf[e, :, D_EXPERT:] = wu_ref[e].astype(BF16)
            wd_bf[e] = wd_ref[e].astype(BF16)

    @pl.when(i < n_tiles)
    def _():
        gather(first=i * TILE_SLABS, slot=slot, wait=True)

        @pl.when(i + 1 < n_tiles)
        def _():
            gather(first=(i + 1) * TILE_SLABS, slot=1 - slot, wait=False)

        hb = hbuf[slot]
        gates = gbuf[slot]
        lane = lax.broadcasted_iota(jnp.int32, (1, ROUTER_W), 1)
        first_lane = N_GROUPS + tg_ref[i] * EXPERTS_PER_GROUP
        acc = jnp.zeros((TM, D_MODEL), F32)
        for e in range(EXPERTS_PER_GROUP):
            ab = _dot(hb, wgu_bf[e])
            he = (_silu(ab[:, :D_EXPERT]) * ab[:, D_EXPERT:]).astype(BF16)
            gcol = jnp.sum(jnp.where(lane == first_lane + e, gates, 0.0), axis=-1, keepdims=True)
            acc = acc + gcol * _dot(he, wd_bf[e])
        ys_ref[...] = acc.astype(BF16)

    @pl.when(i >= n_tiles)
    def _():
        ys_ref[...] = jnp.zeros_like(ys_ref)


def _final_kernel(blocks_p, dest_ref, x1_ref, rinfo_ref, ys_hbm, gfin_ref, yp_ref, ysmp_ref, ybuf, sem):
    i = pl.program_id(0)
    slot = lax.rem(i, 2)
    gather = functools.partial(_slab_gather, dest_ref, srcs=(ys_hbm,), bufs=(ybuf,), sems=(sem,),
                               n_slabs=SLABS_PER_BLOCK)

    @pl.when(i == 0)
    def _():
        ybuf[:, SORT_ROWS:SORT_W, :] = jnp.zeros((2, SORT_W - SORT_ROWS, D_MODEL), BF16)
        gather(first=0, slot=0, wait=False)

    gather(first=i * SLABS_PER_BLOCK, slot=slot, wait=True)

    @pl.when(i + 1 < pl.num_programs(0))
    def _():
        gather(first=(i + 1) * SLABS_PER_BLOCK, slot=1 - slot, wait=False)

    rank = rinfo_ref[:, 0:1]
    col = lax.broadcasted_iota(jnp.int32, (1, SORT_W), 1).astype(F32)
    place = jnp.where(col == rank, 1.0, 0.0).astype(BF16)
    moe = _dot(place, ybuf[slot])
    xo = x1_ref[...] + moe
    y = xo * lax.rsqrt(jnp.mean(xo * xo, axis=-1, keepdims=True) + RMS_EPS) * gfin_ref[...]

    @pl.when(i < blocks_p)
    def _():
        yp_ref[...] = y

    @pl.when(i >= blocks_p)
    def _():
        ysmp_ref[...] = y


def _const_spec(shape, pipeline_mode=None):
    nd = len(shape)
    return pl.BlockSpec(shape, lambda i: (0,) * nd, pipeline_mode=pipeline_mode)


def _mixer_call(x2d, ret0, hg0, params, *, seg_len, carry, pipelined, block_offset, total_blocks, shared=()):
    (gmix, win, lbl, hgn, wout, gffn, wr, br) = params
    T = x2d.shape[0]
    LS = seg_len
    n_seg = TB // LS
    NCH = LS // CHUNK
    n_states = ret0.shape[0]
    n_own = T // TB
    n_fill = 0 if shared else total_blocks - n_own - block_offset
    lag = 1 if pipelined else 0
    grid = n_own + lag + n_fill

    f32 = np.float32
    lg = np.log1p(-(f32(2.0) ** (f32(-5.0) - np.arange(N_HEADS, dtype=f32)))).astype(f32)
    tt = np.arange(LS, dtype=f32)
    ch = np.arange(LS) // CHUNK
    dret = np.exp(np.abs(tt[:, None] - tt[None, :])[None] * lg[:, None, None]).astype(f32)
    dret = np.where((ch[None, :] <= ch[:, None])[None], dret, f32(0.0))
    lg_lane = np.repeat(lg, D_HEAD)[None, :]
    xi = np.exp((tt[:, None] + f32(1.0)) * lg_lane).astype(f32)
    zeta = np.exp((f32(LS - 1.0) - tt)[:, None] * lg_lane).astype(f32)
    gls = np.exp(f32(LS) * lg_lane).astype(f32)
    half = D_HEAD // 2
    inv_freq = ROPE_BASE ** (-jnp.arange(half, dtype=F32) / half)
    invf = jnp.tile(inv_freq, PAIR_W // half)[None, :]
    sgn = np.where((np.arange(PAIR_W) % D_HEAD) < half, f32(-1.0), f32(1.0))[None, :]
    tri = jnp.asarray(np.tril(np.ones((LS, LS), np.float32)), BF16)
    lane_head = np.arange(256) // D_HEAD
    ones_bd = jnp.asarray((lane_head[:, None] == lane_head[None, :]).astype(np.float32), BF16)
    cmask = jnp.asarray((ch[:, None] == ch[None, :]).astype(np.float32))

    if carry:
        state_spec = pl.BlockSpec((1, N_HEADS, D_HEAD, D_HEAD), lambda i: (0, 0, 0, 0))
    else:
        state_spec = pl.BlockSpec((n_seg, N_HEADS, D_HEAD, D_HEAD), lambda i: (i, 0, 0, 0))
    single = pl.Buffered(1)
    stri = jnp.asarray(np.tril(np.ones((TB, TB), np.float32), -1), BF16)
    in_row_spec = pl.BlockSpec((TB, D_MODEL), lambda i: (jnp.minimum(i, n_own - 1), 0))
    out_block = lambda i: jnp.maximum(i - lag, 0) + block_offset
    out_row_spec = lambda rows, w: pl.BlockSpec((rows, w), lambda i: (out_block(i), 0))
    nw = (N_SUB - 1) * N_PAIRS * PAIR_W
    nx = max(NCH - 1, 1) * N_PAIRS * PAIR_W
    n_in = N_MIXER_INPUTS

    kern = functools.partial(_mixer_kernel, (LS, n_seg, carry, pipelined, n_own, n_fill, len(shared)))
    return pl.pallas_call(
        kern,
        grid=(grid,),
        in_specs=[
            in_row_spec,
            _const_spec((1, D_MODEL)), _const_spec(win.shape, single), _const_spec((1, PAIR_W)),
            _const_spec((1, PAIR_W)), _const_spec(lbl.shape), _const_spec((1, GROUP_W)),
            _const_spec(wout.shape, single), _const_spec((1, D_MODEL)),
            _const_spec(wr.shape), _const_spec((1, ROUTER_W)),
            _const_spec((LS, LS)), _const_spec((N_HEADS, LS, LS)), _const_spec((LS, GROUP_W)),
            _const_spec((LS, GROUP_W)), _const_spec((1, GROUP_W)), _const_spec((256, 256)), _const_spec((LS, LS)),
            _const_spec((TB, TB)),
            state_spec, state_spec,
        ] + [pl.BlockSpec(memory_space=pl.ANY)] * len(shared),
        out_specs=[out_row_spec(TB, D_MODEL), out_row_spec(SORT_ROWS, D_MODEL), out_row_spec(SORT_ROWS, ROUTER_W),
                   out_row_spec(TB, ROUTER_W),
                   pl.BlockSpec((1, 8, ROUTER_W), lambda i: (out_block(i), 0, 0)),
                   state_spec, state_spec],
        out_shape=[
            jax.ShapeDtypeStruct((total_blocks * TB, D_MODEL), F32),
            jax.ShapeDtypeStruct((total_blocks * SORT_ROWS, D_MODEL), BF16),
            jax.ShapeDtypeStruct((total_blocks * SORT_ROWS, ROUTER_W), F32),
            jax.ShapeDtypeStruct((total_blocks * TB, ROUTER_W), F32),
            jax.ShapeDtypeStruct((total_blocks, 8, ROUTER_W), jnp.int32),
            jax.ShapeDtypeStruct((n_states, N_HEADS, D_HEAD, D_HEAD), F32),
            jax.ShapeDtypeStruct((n_states, N_HEADS, D_HEAD, D_HEAD), F32),
        ],
        input_output_aliases={n_in + k: k for k in range(len(shared))},
        scratch_shapes=[
            *[pltpu.VMEM((TB, GROUP_W), F32) for _ in range(8)],
            pltpu.VMEM((TB, GROUP_W), F32),
            pltpu.VMEM((TB, 2 * GROUP_W), F32),
            pltpu.VMEM((TB, GROUP_W), F32),
            pltpu.VMEM((TB, D_MODEL), F32),
            pltpu.VMEM((TB, nw), BF16), pltpu.VMEM((TB, nw), BF16),
            pltpu.VMEM((TB, nx), BF16), pltpu.VMEM((TB, nx), BF16),
            pltpu.VMEM((TB, GROUP_W), BF16), pltpu.VMEM((TB, GROUP_W), BF16),
            pltpu.VMEM((N_SUB * (SUB // 2) * (SUB + SUB // 2), GROUP_W), BF16),
            pltpu.VMEM((TB, PAIR_W), F32), pltpu.VMEM((TB, PAIR_W), F32),
            pltpu.VMEM(win.shape, BF16), pltpu.VMEM(wout.shape, BF16),
            pltpu.VMEM((1, N_PAIRS, PAIR_W, PAIR_W), F32),
            pltpu.VMEM((1, N_PAIRS, PAIR_W, PAIR_W), F32),
        ],
        compiler_params=pltpu.CompilerParams(
            dimension_semantics=("arbitrary",), vmem_limit_bytes=V7X_VMEM_LIMIT),
    )(x2d, gmix, win, invf, sgn, lbl, hgn, wout, gffn, wr, br, tri, dret, xi, zeta, gls, ones_bd, cmask, stri,
      ret0, hg0, *shared)


def _dispatch_tables(info, n_tiles_max):
    slab_group = info[:, 0, :SLABS_PER_BLOCK].reshape(-1)
    n_slabs = slab_group.shape[0]
    valid = slab_group >= 0
    onehot = (slab_group[:, None] == jnp.arange(N_GROUPS, dtype=jnp.int32)[None, :]).astype(jnp.int32)
    within = jnp.cumsum(onehot, axis=0) - onehot
    tiles = (jnp.sum(onehot, axis=0) + TILE_SLABS - 1) // TILE_SLABS
    tile_end = jnp.cumsum(tiles)
    g = jnp.clip(slab_group, 0, N_GROUPS - 1)
    dest = (tile_end - tiles)[g] * TILE_SLABS + jnp.take_along_axis(within, g[:, None], axis=1)[:, 0]
    dest = jnp.where(valid, dest, -1).astype(jnp.int32)
    n_slots = n_tiles_max * TILE_SLABS
    src = jnp.full((n_slots,), -1, jnp.int32).at[jnp.where(valid, dest, n_slots)].set(
        jnp.arange(n_slabs, dtype=jnp.int32), mode="drop")
    tile_group = jnp.sum(jnp.arange(n_tiles_max, dtype=jnp.int32)[:, None] >= tile_end[None, :], axis=1)
    tile_group = jnp.minimum(tile_group, N_GROUPS - 1).astype(jnp.int32)
    return src, dest, tile_group, tile_end[-1:].astype(jnp.int32)


def _moe_call(src, tile_group, n_tiles, h2s, gs, w_gate, w_up, w_down):
    n_tiles_max = tile_group.shape[0]
    group_of_tile = lambda i, s, tg, nt: (tg[i], 0, 0)
    single = pl.Buffered(1)
    return pl.pallas_call(
        _moe_kernel,
        grid_spec=pltpu.PrefetchScalarGridSpec(
            num_scalar_prefetch=3,
            grid=(n_tiles_max,),
            in_specs=[
                pl.BlockSpec(memory_space=pl.ANY), pl.BlockSpec(memory_space=pl.ANY),
                pl.BlockSpec((EXPERTS_PER_GROUP, D_MODEL, D_EXPERT), group_of_tile, pipeline_mode=single),
                pl.BlockSpec((EXPERTS_PER_GROUP, D_MODEL, D_EXPERT), group_of_tile, pipeline_mode=single),
                pl.BlockSpec((EXPERTS_PER_GROUP, D_EXPERT, D_MODEL), group_of_tile, pipeline_mode=single),
            ],
            out_specs=pl.BlockSpec((TM, D_MODEL), lambda i, s, tg, nt: (i, 0)),
            scratch_shapes=[
                pltpu.VMEM((2, TM, D_MODEL), BF16), pltpu.VMEM((2, TM, ROUTER_W), F32),
                pltpu.VMEM((EXPERTS_PER_GROUP, D_MODEL, 2 * D_EXPERT), BF16),
                pltpu.VMEM((EXPERTS_PER_GROUP, D_EXPERT, D_MODEL), BF16),
                pltpu.SemaphoreType.DMA((2,)), pltpu.SemaphoreType.DMA((2,)),
            ],
        ),
        out_shape=jax.ShapeDtypeStruct((n_tiles_max * TM, D_MODEL), BF16),
        compiler_params=pltpu.CompilerParams(
            dimension_semantics=("arbitrary",), vmem_limit_bytes=V7X_VMEM_LIMIT),
    )(src, tile_group, n_tiles, h2s, gs, w_gate, w_up, w_down)


def _final_call(dest, x1, rinfo, ys, gfin, blocks_p):
    n_blocks = x1.shape[0] // TB
    return pl.pallas_call(
        functools.partial(_final_kernel, blocks_p),
        grid_spec=pltpu.PrefetchScalarGridSpec(
            num_scalar_prefetch=1,
            grid=(n_blocks,),
            in_specs=[
                pl.BlockSpec((TB, D_MODEL), lambda i, d: (i, 0)),
                pl.BlockSpec((TB, ROUTER_W), lambda i, d: (i, 0)),
                pl.BlockSpec(memory_space=pl.ANY),
                pl.BlockSpec((1, D_MODEL), lambda i, d: (0, 0)),
            ],
            out_specs=[
                pl.BlockSpec((TB, D_MODEL), lambda i, d: (jnp.minimum(i, blocks_p - 1), 0)),
                pl.BlockSpec((TB, D_MODEL), lambda i, d: (jnp.maximum(i - blocks_p, 0), 0)),
            ],
            scratch_shapes=[pltpu.VMEM((2, SORT_W, D_MODEL), BF16), pltpu.SemaphoreType.DMA((2,))],
        ),
        out_shape=[jax.ShapeDtypeStruct((blocks_p * TB, D_MODEL), F32),
                   jax.ShapeDtypeStruct(((n_blocks - blocks_p) * TB, D_MODEL), F32)],
        compiler_params=pltpu.CompilerParams(
            dimension_semantics=("arbitrary",), vmem_limit_bytes=V7X_VMEM_LIMIT),
    )(dest, x1, rinfo, ys, gfin)


def kernel(x_prompt, x_sample, state_ret, state_hgrn, norm_mix_g, w_in, hgrn_lb_logits, hgrn_norm_g, w_out,
           norm_ffn_g, w_router_group, b_router_group, w_router_expert, b_router_expert, w_exp_gate, w_exp_up,
           w_exp_down, norm_final_g):
    depth = w_in.shape[0]
    assert depth == 1 and hgrn_lb_logits.shape[0] == 2, "single-layer configuration only"
    bp, seq, d = x_prompt.shape
    db, dec_len, _ = x_sample.shape
    assert bp == 1 and d == D_MODEL and dec_len == CHUNK and seq % 256 == 0 and db % 4 == 0

    pad = ROUTER_W - N_GROUPS - N_EXPERTS
    wr = jnp.concatenate([w_router_group[0], w_router_expert[0], jnp.zeros((D_MODEL, pad), F32)], axis=1)
    br = jnp.concatenate([b_router_group[0], b_router_expert[0], jnp.zeros((pad,), F32)])[None, :]
    params = (norm_mix_g[0][None, :], w_in[0], hgrn_lb_logits,
              jnp.tile(hgrn_norm_g[0], N_HEADS)[None, :], w_out[0], norm_ffn_g[0][None, :], wr, br)

    zeros_state = jnp.zeros((1, N_HEADS, D_HEAD, D_HEAD), F32)
    blocks_p = seq // TB
    blocks_s = db * dec_len // TB
    n_blocks = blocks_p + blocks_s
    *shared, ret_p, hg_p = _mixer_call(
        x_prompt.reshape(seq, d), zeros_state, zeros_state, params, seg_len=TB, carry=True, pipelined=True,
        block_offset=0, total_blocks=n_blocks)
    x1, h2s, gs, rinfo, info, ret_s, hg_s = _mixer_call(
        x_sample.reshape(db * dec_len, d), state_ret[0], state_hgrn[0], params, seg_len=CHUNK, carry=False,
        pipelined=False,
        block_offset=blocks_p, total_blocks=n_blocks, shared=tuple(shared))

    n_tiles_max = -(-n_blocks * SLABS_PER_BLOCK // TILE_SLABS) + N_GROUPS
    src, dest, tile_group, n_tiles = _dispatch_tables(info, n_tiles_max)
    ys = _moe_call(src, tile_group, n_tiles, h2s, gs, w_exp_gate[0], w_exp_up[0], w_exp_down[0])
    y_p, y_s = _final_call(dest, x1, rinfo, ys, norm_final_g[None, :], blocks_p)
    return (y_p.reshape(bp, seq, d), y_s.reshape(db, dec_len, d), ret_p[None], hg_p[None], ret_s[None], hg_s[None])
```

```python
import functools

import numpy as np
import jax
import jax.numpy as jnp
from jax import lax
from jax.experimental import pallas as pl
from jax.experimental.pallas import tpu as pltpu

F32 = jnp.float32
BF16 = jnp.bfloat16

D_MODEL = 1024
N_HEADS = 8
D_HEAD = 64
GROUP_W = N_HEADS * D_HEAD
N_PAIRS = N_HEADS // 2
PAIR_W = 2 * D_HEAD
CHUNK = 64
SUB = 16
N_SUB = CHUNK // SUB
PAST_LEN = 2048
ROPE_BASE = 10000.0
RMS_EPS = 1e-6
LOG2E = 1.4426950408889634
N_GROUPS = 4
EXPERTS_PER_GROUP = 8
N_EXPERTS = N_GROUPS * EXPERTS_PER_GROUP
D_EXPERT = 256
ROUTER_W = 128
V7X_VMEM_LIMIT = 60 * 1024 * 1024
TB = 256
SLAB = 16
SLABS_PER_BLOCK = TB // SLAB + N_GROUPS
SORT_ROWS = SLABS_PER_BLOCK * SLAB
SORT_W = 384
TILE_SLABS = 32
TM = TILE_SLABS * SLAB

C_QR, C_KR, C_VR, C_GR, C_QG, C_FG, C_VG, C_GG = (i * GROUP_W for i in range(8))


def _dot(a, b):
    return jnp.dot(a, b, preferred_element_type=F32)


def _dot_nt(a, b):
    return lax.dot_general(a, b, (((1,), (1,)), ((), ())), preferred_element_type=F32)


def _dot_tn(a, b):
    return lax.dot_general(a, b, (((0,), (0,)), ((), ())), preferred_element_type=F32)


def _split(x, n):
    parts = []
    for _ in range(n):
        p = x.astype(BF16)
        parts.append(p)
        x = x - p.astype(F32)
    return parts


def _group_sum(x, ones_ref):
    xb = x.astype(BF16)
    return jnp.concatenate([_dot(xb[:, c * 256:(c + 1) * 256], ones_ref[...]) for c in range(2)], axis=1)


class _ColumnGroups:
    def __init__(self, bufs):
        self.bufs = bufs

    def _locate(self, idx):
        rows, cols = idx
        g = cols.start // GROUP_W
        assert (cols.stop - 1) // GROUP_W == g
        return self.bufs[g], (rows, slice(cols.start - g * GROUP_W, cols.stop - g * GROUP_W))

    def __getitem__(self, idx):
        buf, at = self._locate(idx)
        return buf[at]

    def __setitem__(self, idx, value):
        buf, at = self._locate(idx)
        buf[at] = value


def _silu(x):
    return x * jax.nn.sigmoid(x)


N_MIXER_INPUTS = 21


def _mixer_kernel(cfg, *refs):
    pipelined, n_own, n_fill, n_alias, cast_w = cfg[-5:]
    n_body = n_own + (1 if pipelined else 0)
    n_in = N_MIXER_INPUTS + (3 if cast_w else 0)
    i = pl.program_id(0)

    @pl.when(i < n_body)
    def _():
        _mixer_body(cfg, *refs)

    if n_fill:
        @pl.when(i >= n_body)
        def _():
            for out_ref in refs[n_in + n_alias:n_in + n_alias + 5]:
                out_ref[...] = jnp.zeros_like(out_ref)


def _mixer_body(cfg, x_ref, gmix_ref, win_ref, invf_ref, sgn_ref, lbl_ref, hgn_ref, wout_ref, gffn_ref,
                wr_ref, br_ref, tri_ref, dret_ref, xi_ref, zeta_ref, gls_ref, ones_ref, cmask_ref, stri_ref,
                ret0_ref, hg0_ref, *rest):
    LS, NSEG, carry, pipelined, n_own, _, n_alias, cast_w = cfg
    w_chunks = ()
    if cast_w:
        w_chunks = tuple(zip(rest[:3], rest[3 + n_alias + 7:3 + n_alias + 10]))
        rest = rest[3:3 + n_alias + 7] + rest[3 + n_alias + 10:]
    (x1_ref, h2s_ref, gs_ref, rinfo_ref, info_ref, retout_ref, hgout_ref,
     *proj_bufs, kbuf, obuf, dbuf, xprev, qw, kw, qx, kx, qin_buf, kout_buf, pbuf, cos_t, sin_t, win_bf, wout_bf,
     rst, sst) = rest[n_alias:]
    proj = _ColumnGroups(proj_bufs)
    NCH = LS // CHUNK
    i = pl.program_id(0)
    lane128 = lax.broadcasted_iota(jnp.int32, (1, PAIR_W), 1)
    head_a = lane128 < D_HEAD
    bd_mask = (lax.broadcasted_iota(jnp.int32, (PAIR_W, PAIR_W), 0) < D_HEAD) == head_a

    def pair_state(ref, sg, p):
        z = jnp.zeros((D_HEAD, D_HEAD), F32)
        return jnp.concatenate([jnp.concatenate([ref[sg, 2 * p], z], axis=1),
                                jnp.concatenate([z, ref[sg, 2 * p + 1]], axis=1)], axis=0)

    @pl.when(i == 0)
    def _():
        for c in range(0, 8 * GROUP_W, GROUP_W):
            win_bf[:, c:c + GROUP_W] = win_ref[:, c:c + GROUP_W].astype(BF16)
        wout_bf[...] = wout_ref[...].astype(BF16)
        if carry:
            for p in range(N_PAIRS):
                rst[0, p] = pair_state(ret0_ref, 0, p)
                sst[0, p] = pair_state(hg0_ref, 0, p)
        row = lax.broadcasted_iota(jnp.int32, (TB, 1), 0)
        ang_row = (row & (LS - 1)).astype(F32) * invf_ref[...]
        cos_t[...] = jnp.cos(ang_row)
        sin_t[...] = jnp.sin(ang_row)
        qw[...] = jnp.zeros_like(qw)
        kw[...] = jnp.zeros_like(kw)
        if NCH > 1:
            qx[...] = jnp.zeros_like(qx)
            kx[...] = jnp.zeros_like(kx)
        if pipelined:
            for buf in (obuf, dbuf, xprev, proj_bufs[C_GR // GROUP_W], proj_bufs[C_GG // GROUP_W]):
                buf[...] = jnp.zeros_like(buf)

    if pipelined:
        _mixer_back(proj, obuf, dbuf, xprev, hgn_ref, ones_ref, wout_bf, gffn_ref, wr_ref, br_ref, stri_ref,
                    x1_ref, h2s_ref, gs_ref, rinfo_ref, info_ref)
    keep = i < n_own

    for w_ref, w_bf_ref in w_chunks:
        w_bf_ref[...] = w_ref[...].astype(BF16)

    x = x_ref[...]
    h = x * lax.rsqrt(jnp.mean(x * x, axis=-1, keepdims=True) + RMS_EPS) * gmix_ref[...]
    hb = h.astype(BF16)

    def project(c0, c1):
        for c in range(c0, c1, GROUP_W):
            proj[:, c:c + GROUP_W] = _dot(hb, win_bf[:, c:c + GROUP_W])

    project(C_QG, C_GG)
    project(C_QR, C_VR)

    la = lbl_ref[0:1, :]
    lb_ = lbl_ref[1:2, :]
    lmax = jnp.maximum(la, lb_)
    ea = jnp.exp(la - lmax)
    lbv = ea / (ea + jnp.exp(lb_ - lmax))
    qg = proj[:, C_QG:C_QG + GROUP_W]
    proj[:, C_QG:C_QG + GROUP_W] = _silu(qg)
    f = lbv + (1.0 - lbv) * jax.nn.sigmoid(proj[:, C_FG:C_FG + GROUP_W])
    kbuf[...] = 1.0 - f
    logf = jnp.log(f)
    for sg in range(NSEG):
        rows = slice(sg * LS, (sg + 1) * LS)
        parts = _split(logf[rows], 3)
        proj[rows, C_FG:C_FG + GROUP_W] = sum(_dot(tri_ref[...], p) for p in parts)

    project(C_VR, C_QG)
    project(C_GG, C_GG + GROUP_W)

    start = jnp.full((8, PAIR_W), i * TB if carry else PAST_LEN, jnp.int32).astype(F32)
    ang0 = (start * invf_ref[...])[0:1]
    c0 = jnp.cos(ang0)
    s0 = jnp.sin(ang0)
    cos = cos_t[...] * c0 - sin_t[...] * s0
    sin = (sin_t[...] * c0 + cos_t[...] * s0) * sgn_ref[...]
    first_half = (lane128 & (D_HEAD - 1)) < (D_HEAD // 2)
    for blk in range(2 * N_PAIRS):
        cols = slice(blk * PAIR_W, (blk + 1) * PAIR_W)
        xx = proj[:, cols]
        partner = jnp.where(first_half, pltpu.roll(xx, PAIR_W - D_HEAD // 2, 1), pltpu.roll(xx, D_HEAD // 2, 1))
        r = xx * cos + partner * sin
        if blk < N_PAIRS:
            r = r * (D_HEAD ** -0.5)
        proj[:, cols] = r

    HALF = SUB // 2
    riota = lax.broadcasted_iota(jnp.int32, (HALF, 1), 0)
    P_ROWS = HALF * SUB + HALF * HALF

    def diag_chunk(c):
        for sb in range(N_SUB):
            rs = c * CHUNK + sb * SUB
            q16 = proj[rs:rs + SUB, C_QG:C_QG + GROUP_W]
            b16 = proj[rs:rs + SUB, C_FG:C_FG + GROUP_W] * LOG2E
            k16 = kbuf[rs:rs + SUB, :]
            base = sb * P_ROWS
            upper = []
            for s in range(SUB):
                ks = k16[s:s + 1]
                bs = b16[s:s + 1]
                hi = q16[HALF:] * ks * jnp.exp2(b16[HALF:] - bs)
                if s < HALF:
                    lo = q16[:HALF] * ks * jnp.exp2(b16[:HALF] - bs)
                    lo = jnp.where(riota >= s, lo, 0.0)
                    pbuf[base + s * SUB:base + (s + 1) * SUB, :] = jnp.concatenate([lo, hi], axis=0).astype(BF16)
                else:
                    upper.append(jnp.where(riota >= s - HALF, hi, 0.0))
                    if len(upper) == 2:
                        at = base + HALF * SUB + (s - HALF - 1) * HALF
                        pbuf[at:at + SUB, :] = jnp.concatenate(upper, axis=0).astype(BF16)
                        upper = []
        sc = [_dot(pbuf[:, hh * 256:(hh + 1) * 256], ones_ref[...]) for hh in range(2)]
        for sb in range(N_SUB):
            rs = c * CHUNK + sb * SUB
            v16 = proj[rs:rs + SUB, C_VG:C_VG + GROUP_W]
            base = sb * P_ROWS
            acc_lo = [jnp.zeros((HALF, 256), F32) for _ in range(2)]
            acc_hi = [jnp.zeros((HALF, 256), F32) for _ in range(2)]
            for s in range(SUB):
                for hh in range(2):
                    vs = v16[s:s + 1, hh * 256:(hh + 1) * 256]
                    if s < HALF:
                        at = base + s * SUB
                        acc_lo[hh] = acc_lo[hh] + sc[hh][at:at + HALF] * vs
                        acc_hi[hh] = acc_hi[hh] + sc[hh][at + HALF:at + SUB] * vs
                    else:
                        at = base + HALF * SUB + (s - HALF) * HALF
                        acc_hi[hh] = acc_hi[hh] + sc[hh][at:at + HALF] * vs
            dbuf[rs:rs + HALF, :] = jnp.concatenate(acc_lo, axis=1)
            dbuf[rs + HALF:rs + SUB, :] = jnp.concatenate(acc_hi, axis=1)

    for sg in range(NSEG):
        r0 = sg * LS
        rows = slice(r0, r0 + LS)
        st = 0 if carry else sg
        qg_s = proj[rows, C_QG:C_QG + GROUP_W]
        kg_s = kbuf[rows, :]
        b_s = proj[rows, C_FG:C_FG + GROUP_W]

        for c in range(NCH):
            for j in range(N_SUB - 1):
                e = c * CHUNK + j * SUB + SUB - 1
                bref = b_s[e:e + 1, :]
                q_rows = slice(e + 1, (c + 1) * CHUNK)
                k_rows = slice(e + 1 - SUB, e + 1)
                qt = (qg_s[q_rows] * jnp.exp(b_s[q_rows] - bref)).astype(BF16)
                kt = (kg_s[k_rows] * jnp.exp(bref - b_s[k_rows])).astype(BF16)
                for p in range(N_PAIRS):
                    dst = slice((p * (N_SUB - 1) + j) * PAIR_W, (p * (N_SUB - 1) + j + 1) * PAIR_W)
                    src = slice(p * PAIR_W, (p + 1) * PAIR_W)
                    qw[r0 + q_rows.start:r0 + q_rows.stop, dst] = qt[:, src]
                    kw[r0 + k_rows.start:r0 + k_rows.stop, dst] = kt[:, src]
        for c in range(NCH - 1):
            e = c * CHUNK + CHUNK - 1
            bref = b_s[e:e + 1, :]
            q_rows = slice(e + 1, LS)
            k_rows = slice(c * CHUNK, e + 1)
            qt = (qg_s[q_rows] * jnp.exp(b_s[q_rows] - bref)).astype(BF16)
            kt = (kg_s[k_rows] * jnp.exp(bref - b_s[k_rows])).astype(BF16)
            for p in range(N_PAIRS):
                dst = slice((p * (NCH - 1) + c) * PAIR_W, (p * (NCH - 1) + c + 1) * PAIR_W)
                src = slice(p * PAIR_W, (p + 1) * PAIR_W)
                qx[r0 + q_rows.start:r0 + q_rows.stop, dst] = qt[:, src]
                kx[r0 + k_rows.start:r0 + k_rows.stop, dst] = kt[:, src]

        b_last = b_s[LS - 1:LS, :]
        qin_buf[rows, :] = (qg_s * jnp.exp(b_s)).astype(BF16)
        kout_buf[rows, :] = (kg_s * jnp.exp(b_last - b_s)).astype(BF16)

    def pair_unit(sg, p):
        r0 = sg * LS
        rows = slice(r0, r0 + LS)
        st = 0 if carry else sg
        pc = slice(p * PAIR_W, (p + 1) * PAIR_W)

        qb = proj[rows, C_QR + p * PAIR_W:C_QR + (p + 1) * PAIR_W].astype(BF16)
        kr = proj[rows, C_KR + p * PAIR_W:C_KR + (p + 1) * PAIR_W]
        kb = kr.astype(BF16)
        vb = proj[rows, C_VR + p * PAIR_W:C_VR + (p + 1) * PAIR_W].astype(BF16)
        zero = jnp.zeros_like(qb)
        s_a = _dot_nt(jnp.where(head_a, qb, zero), kb) * dret_ref[2 * p]
        s_b = _dot_nt(jnp.where(head_a, zero, qb), kb) * dret_ref[2 * p + 1]
        o_r = jnp.where(head_a, _dot(s_a.astype(BF16), vb), _dot(s_b.astype(BF16), vb))
        r_old = rst[0, p] if carry else pair_state(ret0_ref, sg, p)
        o_r = o_r + _dot(qb, r_old.astype(BF16)) * xi_ref[:, pc]
        u = _dot_tn((kr * zeta_ref[:, pc]).astype(BF16), vb)
        r_new = r_old * gls_ref[:, pc] + jnp.where(bd_mask, u, 0.0)
        if pipelined:
            r_new = jnp.where(keep, r_new, r_old)
        obuf[rows, pc] = o_r

        vg = proj[rows, C_VG + p * PAIR_W:C_VG + (p + 1) * PAIR_W].astype(BF16)
        wc = slice(p * (N_SUB - 1) * PAIR_W, (p + 1) * (N_SUB - 1) * PAIR_W)
        qw_p = qw[rows, wc]
        kw_p = kw[rows, wc]
        lane_w = lax.broadcasted_iota(jnp.int32, (1, qw_p.shape[1]), 1)
        head_a_w = (lane_w & (PAIR_W - 1)) < D_HEAD
        zw = jnp.zeros_like(qw_p)
        g_a = _dot_nt(jnp.where(head_a_w, qw_p, zw), kw_p)
        g_b = _dot_nt(jnp.where(head_a_w, zw, qw_p), kw_p)
        if NCH > 1:
            g_a = g_a * cmask_ref[...]
            g_b = g_b * cmask_ref[...]
            xc = slice(p * (NCH - 1) * PAIR_W, (p + 1) * (NCH - 1) * PAIR_W)
            qx_p = qx[rows, xc]
            kx_p = kx[rows, xc]
            lane_x = lax.broadcasted_iota(jnp.int32, (1, qx_p.shape[1]), 1)
            head_a_x = (lane_x & (PAIR_W - 1)) < D_HEAD
            zx = jnp.zeros_like(qx_p)
            g_a = g_a + _dot_nt(jnp.where(head_a_x, qx_p, zx), kx_p)
            g_b = g_b + _dot_nt(jnp.where(head_a_x, zx, qx_p), kx_p)
        o_g = jnp.where(head_a, _dot(g_a.astype(BF16), vg), _dot(g_b.astype(BF16), vg))
        s_old = sst[0, p] if carry else pair_state(hg0_ref, sg, p)
        o_g = o_g + _dot(qin_buf[rows, pc], s_old.astype(BF16))
        ut = _dot_tn(kout_buf[rows, pc], vg)
        b_last = proj[r0 + LS - 1:r0 + LS, C_FG + p * PAIR_W:C_FG + (p + 1) * PAIR_W]
        s_decay = jnp.exp(jnp.broadcast_to(b_last, (PAIR_W, PAIR_W)).T)
        s_new = s_old * s_decay + jnp.where(bd_mask, ut, 0.0)
        if pipelined:
            s_new = jnp.where(keep, s_new, s_old)
        gc = slice(GROUP_W + p * PAIR_W, GROUP_W + (p + 1) * PAIR_W)
        obuf[rows, gc] = o_g

        if carry:
            rst[0, p] = r_new
            sst[0, p] = s_new
        for out_ref, new in ((retout_ref, r_new), (hgout_ref, s_new)):
            out_ref[st, 2 * p] = new[:D_HEAD, :D_HEAD]
            out_ref[st, 2 * p + 1] = new[D_HEAD:, D_HEAD:]

    units = [(sg, p) for sg in range(NSEG) for p in range(N_PAIRS)]
    per_chunk = len(units) // (TB // CHUNK)
    for c in range(TB // CHUNK):
        for sg, p in units[c * per_chunk:(c + 1) * per_chunk]:
            pair_unit(sg, p)
        diag_chunk(c)

    xprev[...] = x
    if not pipelined:
        _mixer_back(proj, obuf, dbuf, xprev, hgn_ref, ones_ref, wout_bf, gffn_ref, wr_ref, br_ref, stri_ref,
                    x1_ref, h2s_ref, gs_ref, rinfo_ref, info_ref)


def _mixer_back(proj, obuf, dbuf, xprev, hgn_ref, ones_ref, wout_bf, gffn_ref, wr_ref, br_ref, stri_ref,
                x1_ref, h2s_ref, gs_ref, rinfo_ref, info_ref):
    o_r = obuf[:, 0:GROUP_W]
    mu = _group_sum(o_r, ones_ref) * (1.0 / D_HEAD)
    dlt = o_r - mu
    var = _group_sum(dlt * dlt, ones_ref) * (1.0 / D_HEAD)
    y_r = dlt * lax.rsqrt(var + RMS_EPS) * _silu(proj[:, C_GR:C_GR + GROUP_W])
    o_g = obuf[:, GROUP_W:2 * GROUP_W] + dbuf[...]
    ms = _group_sum(o_g * o_g, ones_ref) * (1.0 / D_HEAD)
    y_g = o_g * lax.rsqrt(ms + RMS_EPS) * hgn_ref[...] * _silu(proj[:, C_GG:C_GG + GROUP_W])
    mix = jnp.concatenate([y_r, y_g], axis=1).astype(BF16)
    x1 = xprev[...] + _dot(mix, wout_bf[...])
    x1_ref[...] = x1

    h2 = x1 * lax.rsqrt(jnp.mean(x1 * x1, axis=-1, keepdims=True) + RMS_EPS) * gffn_ref[...]
    h_hi, h_lo = _split(h2, 2)
    w_hi, w_lo = _split(wr_ref[...], 2)
    logits = _dot(h_hi, w_hi) + _dot(h_hi, w_lo) + _dot(h_lo, w_hi) + br_ref[...]
    lane = lax.broadcasted_iota(jnp.int32, (1, ROUTER_W), 1).astype(F32)
    neg = -jnp.inf
    no_lane = float(ROUTER_W)
    gl = jnp.where(lane < N_GROUPS, logits, neg)
    gmax = jnp.max(gl, axis=-1, keepdims=True)
    g_idx = jnp.min(jnp.where(gl == gmax, lane, no_lane), axis=-1, keepdims=True)
    prob_g = 1.0 / jnp.sum(jnp.exp(gl - gmax), axis=-1, keepdims=True)
    e_lo = N_GROUPS + EXPERTS_PER_GROUP * g_idx
    el = jnp.where((lane >= e_lo) & (lane < e_lo + EXPERTS_PER_GROUP), logits, neg)
    v1 = jnp.max(el, axis=-1, keepdims=True)
    i1 = jnp.min(jnp.where(el == v1, lane, no_lane), axis=-1, keepdims=True)
    el2 = jnp.where(lane == i1, neg, el)
    v2 = jnp.max(el2, axis=-1, keepdims=True)
    i2 = jnp.min(jnp.where(el2 == v2, lane, no_lane), axis=-1, keepdims=True)
    t = jnp.exp(v2 - v1)
    p1 = 1.0 / (1.0 + t)
    p2 = t * p1
    gate = jnp.where(lane == i1, prob_g * p1, 0.0) + jnp.where(lane == i2, prob_g * p2, 0.0)

    onehot = jnp.where(lane == g_idx, 1.0, 0.0)
    before = _dot(stri_ref[...], onehot.astype(BF16))
    count = jnp.sum(onehot, axis=0, keepdims=True)
    padded = jnp.floor((count + (SLAB - 1.0)) * (1.0 / SLAB)) * SLAB
    padded8 = jnp.broadcast_to(padded, (8, ROUTER_W))
    seg_start = sum(jnp.where(lane >= k, pltpu.roll(padded8, k, 1), 0.0) for k in range(1, N_GROUPS))[0:1]
    rank = jnp.sum(onehot * (seg_start + before), axis=-1, keepdims=True)
    col = lax.broadcasted_iota(jnp.int32, (1, SORT_W), 1).astype(F32)
    place = jnp.where(col == rank, 1.0, 0.0).astype(BF16)
    h2s_ref[...] = _dot_tn(place, h2.astype(BF16))[:SORT_ROWS].astype(BF16)
    gs_ref[...] = sum(_dot_tn(place, part) for part in _split(gate, 3))[:SORT_ROWS]
    rinfo_ref[...] = jnp.where(lane == 0, rank, 0.0)
    slab_lo = (lane * SLAB).astype(F32)
    slab_group = jnp.full((1, ROUTER_W), -1.0, F32)
    for g in range(N_GROUPS):
        s_g = jnp.sum(jnp.where(lane == g, seg_start, 0.0), axis=-1, keepdims=True)
        n_g = jnp.sum(jnp.where(lane == g, padded, 0.0), axis=-1, keepdims=True)
        slab_group = jnp.where((slab_lo >= s_g) & (slab_lo < s_g + n_g), float(g), slab_group)
    info_ref[0] = jnp.broadcast_to(slab_group.astype(jnp.int32), (8, ROUTER_W))


def _slab_gather(table_ref, first, n_slabs, srcs, bufs, sems, slot, *, wait):
    for j in range(n_slabs):
        n = table_ref[first + j]
        row = pl.multiple_of(jnp.maximum(n, 0) * SLAB, SLAB)
        dst_rows = pl.ds(j * SLAB, SLAB)

        @pl.when(n >= 0)
        def _():
            for src, buf, sem in zip(srcs, bufs, sems):
                cp = pltpu.make_async_copy(src.at[pl.ds(row, SLAB)], buf.at[slot, dst_rows], sem.at[slot])
                if wait:
                    cp.wait()
                else:
                    cp.start()

        if not wait:
            @pl.when(n < 0)
            def _():
                for buf in bufs:
                    buf[slot, dst_rows] = jnp.zeros((SLAB,) + buf.shape[2:], buf.dtype)


def _moe_kernel(src_ref, tg_ref, nt_ref, h2s_hbm, gs_hbm, wg_ref, wu_ref, wd_ref, ys_ref, hbuf, gbuf, sem_h, sem_g):
    i = pl.program_id(0)
    n_tiles = nt_ref[0]
    slot = lax.rem(i, 2)
    gather = functools.partial(_slab_gather, src_ref, srcs=(h2s_hbm, gs_hbm), bufs=(hbuf, gbuf),
                               sems=(sem_h, sem_g), n_slabs=TILE_SLABS)

    @pl.when(i == 0)
    def _():
        gather(first=0, slot=0, wait=False)

    @pl.when(i < n_tiles)
    def _():
        gather(first=i * TILE_SLABS, slot=slot, wait=True)

        @pl.when(i + 1 < n_tiles)
        def _():
            gather(first=(i + 1) * TILE_SLABS, slot=1 - slot, wait=False)

        hb = hbuf[slot]
        gates = gbuf[slot]
        lane = lax.broadcasted_iota(jnp.int32, (1, ROUTER_W), 1)
        first_lane = N_GROUPS + tg_ref[i] * EXPERTS_PER_GROUP
        acc = jnp.zeros((TM, D_MODEL), F32)
        for e in range(EXPERTS_PER_GROUP):
            he = (_silu(_dot(hb, wg_ref[e])) * _dot(hb, wu_ref[e])).astype(BF16)
            gcol = jnp.sum(jnp.where(lane == first_lane + e, gates, 0.0), axis=-1, keepdims=True)
            acc = acc + gcol * _dot(he, wd_ref[e])
        ys_ref[...] = acc.astype(BF16)

    @pl.when(i >= n_tiles)
    def _():
        ys_ref[...] = jnp.zeros_like(ys_ref)


def _final_kernel(blocks_p, dest_ref, x1_ref, rinfo_ref, ys_hbm, gfin_ref, yp_ref, ysmp_ref, ybuf, sem):
    i = pl.program_id(0)
    slot = lax.rem(i, 2)
    gather = functools.partial(_slab_gather, dest_ref, srcs=(ys_hbm,), bufs=(ybuf,), sems=(sem,),
                               n_slabs=SLABS_PER_BLOCK)

    @pl.when(i == 0)
    def _():
        ybuf[:, SORT_ROWS:SORT_W, :] = jnp.zeros((2, SORT_W - SORT_ROWS, D_MODEL), BF16)
        gather(first=0, slot=0, wait=False)

    gather(first=i * SLABS_PER_BLOCK, slot=slot, wait=True)

    @pl.when(i + 1 < pl.num_programs(0))
    def _():
        gather(first=(i + 1) * SLABS_PER_BLOCK, slot=1 - slot, wait=False)

    rank = rinfo_ref[:, 0:1]
    col = lax.broadcasted_iota(jnp.int32, (1, SORT_W), 1).astype(F32)
    place = jnp.where(col == rank, 1.0, 0.0).astype(BF16)
    moe = _dot(place, ybuf[slot])
    xo = x1_ref[...] + moe
    y = xo * lax.rsqrt(jnp.mean(xo * xo, axis=-1, keepdims=True) + RMS_EPS) * gfin_ref[...]

    @pl.when(i < blocks_p)
    def _():
        yp_ref[...] = y

    @pl.when(i >= blocks_p)
    def _():
        ysmp_ref[...] = y


def _const_spec(shape, pipeline_mode=None):
    nd = len(shape)
    return pl.BlockSpec(shape, lambda i: (0,) * nd, pipeline_mode=pipeline_mode)


def _mixer_call(x2d, ret0, hg0, params, *, seg_len, carry, pipelined, block_offset, total_blocks, shared=(),
                expert_weights=()):
    (gmix, win, lbl, hgn, wout, gffn, wr, br) = params
    T = x2d.shape[0]
    LS = seg_len
    n_seg = TB // LS
    NCH = LS // CHUNK
    n_states = ret0.shape[0]
    n_own = T // TB
    n_fill = 0 if shared else total_blocks - n_own - block_offset
    lag = 1 if pipelined else 0
    grid = n_own + lag + n_fill

    f32 = np.float32
    lg = np.log1p(-(f32(2.0) ** (f32(-5.0) - np.arange(N_HEADS, dtype=f32)))).astype(f32)
    tt = np.arange(LS, dtype=f32)
    ch = np.arange(LS) // CHUNK
    dret = np.exp(np.abs(tt[:, None] - tt[None, :])[None] * lg[:, None, None]).astype(f32)
    dret = np.where((ch[None, :] <= ch[:, None])[None], dret, f32(0.0))
    lg_lane = np.repeat(lg, D_HEAD)[None, :]
    xi = np.exp((tt[:, None] + f32(1.0)) * lg_lane).astype(f32)
    zeta = np.exp((f32(LS - 1.0) - tt)[:, None] * lg_lane).astype(f32)
    gls = np.exp(f32(LS) * lg_lane).astype(f32)
    half = D_HEAD // 2
    inv_freq = ROPE_BASE ** (-jnp.arange(half, dtype=F32) / half)
    invf = jnp.tile(inv_freq, PAIR_W // half)[None, :]
    sgn = np.where((np.arange(PAIR_W) % D_HEAD) < half, f32(-1.0), f32(1.0))[None, :]
    tri = jnp.asarray(np.tril(np.ones((LS, LS), np.float32)), BF16)
    lane_head = np.arange(256) // D_HEAD
    ones_bd = jnp.asarray((lane_head[:, None] == lane_head[None, :]).astype(np.float32), BF16)
    cmask = jnp.asarray((ch[:, None] == ch[None, :]).astype(np.float32))

    if carry:
        state_spec = pl.BlockSpec((1, N_HEADS, D_HEAD, D_HEAD), lambda i: (0, 0, 0, 0))
    else:
        state_spec = pl.BlockSpec((n_seg, N_HEADS, D_HEAD, D_HEAD), lambda i: (i, 0, 0, 0))
    single = pl.Buffered(1)
    stri = jnp.asarray(np.tril(np.ones((TB, TB), np.float32), -1), BF16)
    in_row_spec = pl.BlockSpec((TB, D_MODEL), lambda i: (jnp.minimum(i, n_own - 1), 0))
    out_block = lambda i: jnp.maximum(i - lag, 0) + block_offset
    out_row_spec = lambda rows, w: pl.BlockSpec((rows, w), lambda i: (out_block(i), 0))
    nw = (N_SUB - 1) * N_PAIRS * PAIR_W
    nx = max(NCH - 1, 1) * N_PAIRS * PAIR_W

    cast_w = bool(expert_weights)
    w2d = [w.reshape(-1, w.shape[-1]) for w in expert_weights]
    w_specs = [pl.BlockSpec((w.shape[0] // n_own, w.shape[1]), lambda i: (jnp.minimum(i, n_own - 1), 0)) for w in w2d]
    w_shapes = [jax.ShapeDtypeStruct(w.shape, BF16) for w in w2d]
    n_in = N_MIXER_INPUTS + len(w2d)

    kern = functools.partial(_mixer_kernel, (LS, n_seg, carry, pipelined, n_own, n_fill, len(shared), cast_w))
    return pl.pallas_call(
        kern,
        grid=(grid,),
        in_specs=[
            in_row_spec,
            _const_spec((1, D_MODEL)), _const_spec(win.shape, single), _const_spec((1, PAIR_W)),
            _const_spec((1, PAIR_W)), _const_spec(lbl.shape), _const_spec((1, GROUP_W)),
            _const_spec(wout.shape, single), _const_spec((1, D_MODEL)),
            _const_spec(wr.shape), _const_spec((1, ROUTER_W)),
            _const_spec((LS, LS)), _const_spec((N_HEADS, LS, LS)), _const_spec((LS, GROUP_W)),
            _const_spec((LS, GROUP_W)), _const_spec((1, GROUP_W)), _const_spec((256, 256)), _const_spec((LS, LS)),
            _const_spec((TB, TB)),
            state_spec, state_spec,
        ] + w_specs + [pl.BlockSpec(memory_space=pl.ANY)] * len(shared),
        out_specs=[out_row_spec(TB, D_MODEL), out_row_spec(SORT_ROWS, D_MODEL), out_row_spec(SORT_ROWS, ROUTER_W),
                   out_row_spec(TB, ROUTER_W),
                   pl.BlockSpec((1, 8, ROUTER_W), lambda i: (out_block(i), 0, 0)),
                   state_spec, state_spec] + w_specs,
        out_shape=[
            jax.ShapeDtypeStruct((total_blocks * TB, D_MODEL), F32),
            jax.ShapeDtypeStruct((total_blocks * SORT_ROWS, D_MODEL), BF16),
            jax.ShapeDtypeStruct((total_blocks * SORT_ROWS, ROUTER_W), F32),
            jax.ShapeDtypeStruct((total_blocks * TB, ROUTER_W), F32),
            jax.ShapeDtypeStruct((total_blocks, 8, ROUTER_W), jnp.int32),
            jax.ShapeDtypeStruct((n_states, N_HEADS, D_HEAD, D_HEAD), F32),
            jax.ShapeDtypeStruct((n_states, N_HEADS, D_HEAD, D_HEAD), F32),
        ] + w_shapes,
        input_output_aliases={n_in + k: k for k in range(len(shared))},
        scratch_shapes=[
            *[pltpu.VMEM((TB, GROUP_W), F32) for _ in range(8)],
            pltpu.VMEM((TB, GROUP_W), F32),
            pltpu.VMEM((TB, 2 * GROUP_W), F32),
            pltpu.VMEM((TB, GROUP_W), F32),
            pltpu.VMEM((TB, D_MODEL), F32),
            pltpu.VMEM((TB, nw), BF16), pltpu.VMEM((TB, nw), BF16),
            pltpu.VMEM((TB, nx), BF16), pltpu.VMEM((TB, nx), BF16),
            pltpu.VMEM((TB, GROUP_W), BF16), pltpu.VMEM((TB, GROUP_W), BF16),
            pltpu.VMEM((N_SUB * (SUB // 2) * (SUB + SUB // 2), GROUP_W), BF16),
            pltpu.VMEM((TB, PAIR_W), F32), pltpu.VMEM((TB, PAIR_W), F32),
            pltpu.VMEM(win.shape, BF16), pltpu.VMEM(wout.shape, BF16),
            pltpu.VMEM((1, N_PAIRS, PAIR_W, PAIR_W), F32),
            pltpu.VMEM((1, N_PAIRS, PAIR_W, PAIR_W), F32),
        ],
        compiler_params=pltpu.CompilerParams(
            dimension_semantics=("arbitrary",), vmem_limit_bytes=V7X_VMEM_LIMIT),
    )(x2d, gmix, win, invf, sgn, lbl, hgn, wout, gffn, wr, br, tri, dret, xi, zeta, gls, ones_bd, cmask, stri,
      ret0, hg0, *w2d, *shared)


def _dispatch_tables(info, n_tiles_max):
    slab_group = info[:, 0, :SLABS_PER_BLOCK].reshape(-1)
    n_slabs = slab_group.shape[0]
    valid = slab_group >= 0
    onehot = (slab_group[:, None] == jnp.arange(N_GROUPS, dtype=jnp.int32)[None, :]).astype(jnp.int32)
    within = jnp.cumsum(onehot, axis=0) - onehot
    tiles = (jnp.sum(onehot, axis=0) + TILE_SLABS - 1) // TILE_SLABS
    tile_end = jnp.cumsum(tiles)
    g = jnp.clip(slab_group, 0, N_GROUPS - 1)
    dest = (tile_end - tiles)[g] * TILE_SLABS + jnp.take_along_axis(within, g[:, None], axis=1)[:, 0]
    dest = jnp.where(valid, dest, -1).astype(jnp.int32)
    n_slots = n_tiles_max * TILE_SLABS
    src = jnp.full((n_slots,), -1, jnp.int32).at[jnp.where(valid, dest, n_slots)].set(
        jnp.arange(n_slabs, dtype=jnp.int32), mode="drop")
    tile_group = jnp.sum(jnp.arange(n_tiles_max, dtype=jnp.int32)[:, None] >= tile_end[None, :], axis=1)
    tile_group = jnp.minimum(tile_group, N_GROUPS - 1).astype(jnp.int32)
    return src, dest, tile_group, tile_end[-1:].astype(jnp.int32)


def _moe_call(src, tile_group, n_tiles, h2s, gs, w_gate, w_up, w_down):
    n_tiles_max = tile_group.shape[0]
    group_of_tile = lambda i, s, tg, nt: (tg[i], 0, 0)
    return pl.pallas_call(
        _moe_kernel,
        grid_spec=pltpu.PrefetchScalarGridSpec(
            num_scalar_prefetch=3,
            grid=(n_tiles_max,),
            in_specs=[
                pl.BlockSpec(memory_space=pl.ANY), pl.BlockSpec(memory_space=pl.ANY),
                pl.BlockSpec((EXPERTS_PER_GROUP, D_MODEL, D_EXPERT), group_of_tile),
                pl.BlockSpec((EXPERTS_PER_GROUP, D_MODEL, D_EXPERT), group_of_tile),
                pl.BlockSpec((EXPERTS_PER_GROUP, D_EXPERT, D_MODEL), group_of_tile),
            ],
            out_specs=pl.BlockSpec((TM, D_MODEL), lambda i, s, tg, nt: (i, 0)),
            scratch_shapes=[
                pltpu.VMEM((2, TM, D_MODEL), BF16), pltpu.VMEM((2, TM, ROUTER_W), F32),
                pltpu.SemaphoreType.DMA((2,)), pltpu.SemaphoreType.DMA((2,)),
            ],
        ),
        out_shape=jax.ShapeDtypeStruct((n_tiles_max * TM, D_MODEL), BF16),
        compiler_params=pltpu.CompilerParams(
            dimension_semantics=("arbitrary",), vmem_limit_bytes=V7X_VMEM_LIMIT),
    )(src, tile_group, n_tiles, h2s, gs, w_gate, w_up, w_down)


def _final_call(dest, x1, rinfo, ys, gfin, blocks_p):
    n_blocks = x1.shape[0] // TB
    return pl.pallas_call(
        functools.partial(_final_kernel, blocks_p),
        grid_spec=pltpu.PrefetchScalarGridSpec(
            num_scalar_prefetch=1,
            grid=(n_blocks,),
            in_specs=[
                pl.BlockSpec((TB, D_MODEL), lambda i, d: (i, 0)),
                pl.BlockSpec((TB, ROUTER_W), lambda i, d: (i, 0)),
                pl.BlockSpec(memory_space=pl.ANY),
                pl.BlockSpec((1, D_MODEL), lambda i, d: (0, 0)),
            ],
            out_specs=[
                pl.BlockSpec((TB, D_MODEL), lambda i, d: (jnp.minimum(i, blocks_p - 1), 0)),
                pl.BlockSpec((TB, D_MODEL), lambda i, d: (jnp.maximum(i - blocks_p, 0), 0)),
            ],
            scratch_shapes=[pltpu.VMEM((2, SORT_W, D_MODEL), BF16), pltpu.SemaphoreType.DMA((2,))],
        ),
        out_shape=[jax.ShapeDtypeStruct((blocks_p * TB, D_MODEL), F32),
                   jax.ShapeDtypeStruct(((n_blocks - blocks_p) * TB, D_MODEL), F32)],
        compiler_params=pltpu.CompilerParams(
            dimension_semantics=("arbitrary",), vmem_limit_bytes=V7X_VMEM_LIMIT),
    )(dest, x1, rinfo, ys, gfin)


def kernel(x_prompt, x_sample, state_ret, state_hgrn, norm_mix_g, w_in, hgrn_lb_logits, hgrn_norm_g, w_out,
           norm_ffn_g, w_router_group, b_router_group, w_router_expert, b_router_expert, w_exp_gate, w_exp_up,
           w_exp_down, norm_final_g):
    depth = w_in.shape[0]
    assert depth == 1 and hgrn_lb_logits.shape[0] == 2, "single-layer configuration only"
    bp, seq, d = x_prompt.shape
    db, dec_len, _ = x_sample.shape
    assert bp == 1 and d == D_MODEL and dec_len == CHUNK and seq % 256 == 0 and db % 4 == 0

    pad = ROUTER_W - N_GROUPS - N_EXPERTS
    wr = jnp.concatenate([w_router_group[0], w_router_expert[0], jnp.zeros((D_MODEL, pad), F32)], axis=1)
    br = jnp.concatenate([b_router_group[0], b_router_expert[0], jnp.zeros((pad,), F32)])[None, :]
    params = (norm_mix_g[0][None, :], w_in[0], hgrn_lb_logits,
              jnp.tile(hgrn_norm_g[0], N_HEADS)[None, :], w_out[0], norm_ffn_g[0][None, :], wr, br)

    zeros_state = jnp.zeros((1, N_HEADS, D_HEAD, D_HEAD), F32)
    blocks_p = seq // TB
    blocks_s = db * dec_len // TB
    n_blocks = blocks_p + blocks_s
    *shared, ret_p, hg_p, wg_bf, wu_bf, wd_bf = _mixer_call(
        x_prompt.reshape(seq, d), zeros_state, zeros_state, params, seg_len=TB, carry=True, pipelined=True,
        block_offset=0, total_blocks=n_blocks, expert_weights=(w_exp_gate[0], w_exp_up[0], w_exp_down[0]))
    x1, h2s, gs, rinfo, info, ret_s, hg_s = _mixer_call(
        x_sample.reshape(db * dec_len, d), state_ret[0], state_hgrn[0], params, seg_len=CHUNK, carry=False,
        pipelined=False,
        block_offset=blocks_p, total_blocks=n_blocks, shared=tuple(shared))

    n_tiles_max = -(-n_blocks * SLABS_PER_BLOCK // TILE_SLABS) + N_GROUPS
    src, dest, tile_group, n_tiles = _dispatch_tables(info, n_tiles_max)
    ys = _moe_call(src, tile_group, n_tiles, h2s, gs, wg_bf.reshape(w_exp_gate[0].shape),
                   wu_bf.reshape(w_exp_up[0].shape), wd_bf.reshape(w_exp_down[0].shape))
    y_p, y_s = _final_call(dest, x1, rinfo, ys, norm_final_g[None, :], blocks_p)
    return (y_p.reshape(bp, seq, d), y_s.reshape(db, dec_len, d), ret_p[None], hg_p[None], ret_s[None], hg_s[None])
```

```python
import functools

import numpy as np
import jax
import jax.numpy as jnp
from jax import lax
from jax.experimental import pallas as pl
from jax.experimental.pallas import tpu as pltpu

F32 = jnp.float32
BF16 = jnp.bfloat16

D_MODEL = 1024
N_HEADS = 8
D_HEAD = 64
GROUP_W = N_HEADS * D_HEAD
N_PAIRS = N_HEADS // 2
PAIR_W = 2 * D_HEAD
CHUNK = 64
SUB = 16
N_SUB = CHUNK // SUB
PAST_LEN = 2048
ROPE_BASE = 10000.0
RMS_EPS = 1e-6
LOG2E = 1.4426950408889634
N_GROUPS = 4
EXPERTS_PER_GROUP = 8
N_EXPERTS = N_GROUPS * EXPERTS_PER_GROUP
D_EXPERT = 256
ROUTER_W = 128
V7X_VMEM_LIMIT = 60 * 1024 * 1024
TB = 256
SLAB = 16
SLABS_PER_BLOCK = TB // SLAB + N_GROUPS
SORT_ROWS = SLABS_PER_BLOCK * SLAB
SORT_W = 384
TILE_SLABS = 32
TM = TILE_SLABS * SLAB

C_QR, C_KR, C_VR, C_GR, C_QG, C_FG, C_VG, C_GG = (i * GROUP_W for i in range(8))


def _dot(a, b):
    return jnp.dot(a, b, preferred_element_type=F32)


def _dot_nt(a, b):
    return lax.dot_general(a, b, (((1,), (1,)), ((), ())), preferred_element_type=F32)


def _dot_tn(a, b):
    return lax.dot_general(a, b, (((0,), (0,)), ((), ())), preferred_element_type=F32)


def _split(x, n):
    parts = []
    for _ in range(n):
        p = x.astype(BF16)
        parts.append(p)
        x = x - p.astype(F32)
    return parts


def _group_sum(x, ones_ref):
    xb = x.astype(BF16)
    return jnp.concatenate([_dot(xb[:, c * 256:(c + 1) * 256], ones_ref[...]) for c in range(2)], axis=1)


class _ColumnGroups:
    def __init__(self, bufs):
        self.bufs = bufs

    def _locate(self, idx):
        rows, cols = idx
        g = cols.start // GROUP_W
        assert (cols.stop - 1) // GROUP_W == g
        return self.bufs[g], (rows, slice(cols.start - g * GROUP_W, cols.stop - g * GROUP_W))

    def __getitem__(self, idx):
        buf, at = self._locate(idx)
        return buf[at]

    def __setitem__(self, idx, value):
        buf, at = self._locate(idx)
        buf[at] = value


def _silu(x):
    return x * jax.nn.sigmoid(x)


N_MIXER_INPUTS = 21


def _mixer_kernel(cfg, *refs):
    pipelined, n_own, n_fill, n_alias, cast_w = cfg[-5:]
    n_body = n_own + (1 if pipelined else 0)
    n_in = N_MIXER_INPUTS + (3 if cast_w else 0)
    i = pl.program_id(0)

    @pl.when(i < n_body)
    def _():
        _mixer_body(cfg, *refs)

    if n_fill:
        @pl.when(i >= n_body)
        def _():
            for out_ref in refs[n_in + n_alias:n_in + n_alias + 5]:
                out_ref[...] = jnp.zeros_like(out_ref)


def _mixer_body(cfg, x_ref, gmix_ref, win_ref, invf_ref, sgn_ref, lbl_ref, hgn_ref, wout_ref, gffn_ref,
                wr_ref, br_ref, tri_ref, dret_ref, xi_ref, zeta_ref, gls_ref, ones_ref, cmask_ref, stri_ref,
                ret0_ref, hg0_ref, *rest):
    LS, NSEG, carry, pipelined, n_own, _, n_alias, cast_w = cfg
    w_chunks = ()
    if cast_w:
        w_chunks = tuple(zip(rest[:3], rest[3 + n_alias + 7:3 + n_alias + 10]))
        rest = rest[3:3 + n_alias + 7] + rest[3 + n_alias + 10:]
    (x1_ref, h2s_ref, gs_ref, rinfo_ref, info_ref, retout_ref, hgout_ref,
     *proj_bufs, kbuf, obuf, dbuf, xprev, qw, kw, qx, kx, qin_buf, kout_buf, pbuf, cos_t, sin_t, win_bf, wout_bf,
     rst, sst) = rest[n_alias:]
    proj = _ColumnGroups(proj_bufs)
    NCH = LS // CHUNK
    i = pl.program_id(0)
    lane128 = lax.broadcasted_iota(jnp.int32, (1, PAIR_W), 1)
    head_a = lane128 < D_HEAD
    bd_mask = (lax.broadcasted_iota(jnp.int32, (PAIR_W, PAIR_W), 0) < D_HEAD) == head_a

    def pair_state(ref, sg, p):
        z = jnp.zeros((D_HEAD, D_HEAD), F32)
        return jnp.concatenate([jnp.concatenate([ref[sg, 2 * p], z], axis=1),
                                jnp.concatenate([z, ref[sg, 2 * p + 1]], axis=1)], axis=0)

    @pl.when(i == 0)
    def _():
        for c in range(0, 8 * GROUP_W, GROUP_W):
            win_bf[:, c:c + GROUP_W] = win_ref[:, c:c + GROUP_W].astype(BF16)
        wout_bf[...] = wout_ref[...].astype(BF16)
        if carry:
            for p in range(N_PAIRS):
                rst[0, p] = pair_state(ret0_ref, 0, p)
                sst[0, p] = pair_state(hg0_ref, 0, p)
        row = lax.broadcasted_iota(jnp.int32, (TB, 1), 0)
        ang_row = (row & (LS - 1)).astype(F32) * invf_ref[...]
        cos_t[...] = jnp.cos(ang_row)
        sin_t[...] = jnp.sin(ang_row)
        qw[...] = jnp.zeros_like(qw)
        kw[...] = jnp.zeros_like(kw)
        if NCH > 1:
            qx[...] = jnp.zeros_like(qx)
            kx[...] = jnp.zeros_like(kx)
        if pipelined:
            for buf in (obuf, dbuf, xprev, proj_bufs[C_GR // GROUP_W], proj_bufs[C_GG // GROUP_W]):
                buf[...] = jnp.zeros_like(buf)

    if pipelined:
        _mixer_back(proj, obuf, dbuf, xprev, hgn_ref, ones_ref, wout_bf, gffn_ref, wr_ref, br_ref, stri_ref,
                    x1_ref, h2s_ref, gs_ref, rinfo_ref, info_ref)
    keep = i < n_own

    for w_ref, w_bf_ref in w_chunks:
        w_bf_ref[...] = w_ref[...].astype(BF16)

    x = x_ref[...]
    h = x * lax.rsqrt(jnp.mean(x * x, axis=-1, keepdims=True) + RMS_EPS) * gmix_ref[...]
    hb = h.astype(BF16)

    def project(c0, c1):
        for c in range(c0, c1, GROUP_W):
            proj[:, c:c + GROUP_W] = _dot(hb, win_bf[:, c:c + GROUP_W])

    project(C_QG, C_GG)
    project(C_QR, C_VR)

    la = lbl_ref[0:1, :]
    lb_ = lbl_ref[1:2, :]
    lmax = jnp.maximum(la, lb_)
    ea = jnp.exp(la - lmax)
    lbv = ea / (ea + jnp.exp(lb_ - lmax))
    qg = proj[:, C_QG:C_QG + GROUP_W]
    proj[:, C_QG:C_QG + GROUP_W] = _silu(qg)
    f = lbv + (1.0 - lbv) * jax.nn.sigmoid(proj[:, C_FG:C_FG + GROUP_W])
    kbuf[...] = 1.0 - f
    logf = jnp.log(f)
    for sg in range(NSEG):
        rows = slice(sg * LS, (sg + 1) * LS)
        parts = _split(logf[rows], 3)
        proj[rows, C_FG:C_FG + GROUP_W] = sum(_dot(tri_ref[...], p) for p in parts)

    project(C_VR, C_QG)
    project(C_GG, C_GG + GROUP_W)

    start = jnp.full((8, PAIR_W), i * TB if carry else PAST_LEN, jnp.int32).astype(F32)
    ang0 = (start * invf_ref[...])[0:1]
    c0 = jnp.cos(ang0)
    s0 = jnp.sin(ang0)
    cos = cos_t[...] * c0 - sin_t[...] * s0
    sin = (sin_t[...] * c0 + cos_t[...] * s0) * sgn_ref[...]
    first_half = (lane128 & (D_HEAD - 1)) < (D_HEAD // 2)
    for blk in range(2 * N_PAIRS):
        cols = slice(blk * PAIR_W, (blk + 1) * PAIR_W)
        xx = proj[:, cols]
        partner = jnp.where(first_half, pltpu.roll(xx, PAIR_W - D_HEAD // 2, 1), pltpu.roll(xx, D_HEAD // 2, 1))
        r = xx * cos + partner * sin
        if blk < N_PAIRS:
            r = r * (D_HEAD ** -0.5)
        proj[:, cols] = r

    HALF = SUB // 2
    riota = lax.broadcasted_iota(jnp.int32, (HALF, 1), 0)
    P_ROWS = HALF * SUB + HALF * HALF

    def diag_chunk(c):
        for sb in range(N_SUB):
            rs = c * CHUNK + sb * SUB
            q16 = proj[rs:rs + SUB, C_QG:C_QG + GROUP_W]
            b16 = proj[rs:rs + SUB, C_FG:C_FG + GROUP_W] * LOG2E
            k16 = kbuf[rs:rs + SUB, :]
            base = sb * P_ROWS
            upper = []
            for s in range(SUB):
                ks = k16[s:s + 1]
                bs = b16[s:s + 1]
                hi = q16[HALF:] * ks * jnp.exp2(b16[HALF:] - bs)
                if s < HALF:
                    lo = q16[:HALF] * ks * jnp.exp2(b16[:HALF] - bs)
                    lo = jnp.where(riota >= s, lo, 0.0)
                    pbuf[base + s * SUB:base + (s + 1) * SUB, :] = jnp.concatenate([lo, hi], axis=0).astype(BF16)
                else:
                    upper.append(jnp.where(riota >= s - HALF, hi, 0.0))
                    if len(upper) == 2:
                        at = base + HALF * SUB + (s - HALF - 1) * HALF
                        pbuf[at:at + SUB, :] = jnp.concatenate(upper, axis=0).astype(BF16)
                        upper = []
        sc = [_dot(pbuf[:, hh * 256:(hh + 1) * 256], ones_ref[...]) for hh in range(2)]
        for sb in range(N_SUB):
            rs = c * CHUNK + sb * SUB
            v16 = proj[rs:rs + SUB, C_VG:C_VG + GROUP_W]
            base = sb * P_ROWS
            acc_lo = [jnp.zeros((HALF, 256), F32) for _ in range(2)]
            acc_hi = [jnp.zeros((HALF, 256), F32) for _ in range(2)]
            for s in range(SUB):
                for hh in range(2):
                    vs = v16[s:s + 1, hh * 256:(hh + 1) * 256]
                    if s < HALF:
                        at = base + s * SUB
                        acc_lo[hh] = acc_lo[hh] + sc[hh][at:at + HALF] * vs
                        acc_hi[hh] = acc_hi[hh] + sc[hh][at + HALF:at + SUB] * vs
                    else:
                        at = base + HALF * SUB + (s - HALF) * HALF
                        acc_hi[hh] = acc_hi[hh] + sc[hh][at:at + HALF] * vs
            dbuf[rs:rs + HALF, :] = jnp.concatenate(acc_lo, axis=1)
            dbuf[rs + HALF:rs + SUB, :] = jnp.concatenate(acc_hi, axis=1)

    for sg in range(NSEG):
        r0 = sg * LS
        rows = slice(r0, r0 + LS)
        st = 0 if carry else sg
        qg_s = proj[rows, C_QG:C_QG + GROUP_W]
        kg_s = kbuf[rows, :]
        b_s = proj[rows, C_FG:C_FG + GROUP_W]

        for c in range(NCH):
            for j in range(N_SUB - 1):
                e = c * CHUNK + j * SUB + SUB - 1
                bref = b_s[e:e + 1, :]
                q_rows = slice(e + 1, (c + 1) * CHUNK)
                k_rows = slice(e + 1 - SUB, e + 1)
                qt = (qg_s[q_rows] * jnp.exp(b_s[q_rows] - bref)).astype(BF16)
                kt = (kg_s[k_rows] * jnp.exp(bref - b_s[k_rows])).astype(BF16)
                for p in range(N_PAIRS):
                    dst = slice((p * (N_SUB - 1) + j) * PAIR_W, (p * (N_SUB - 1) + j + 1) * PAIR_W)
                    src = slice(p * PAIR_W, (p + 1) * PAIR_W)
                    qw[r0 + q_rows.start:r0 + q_rows.stop, dst] = qt[:, src]
                    kw[r0 + k_rows.start:r0 + k_rows.stop, dst] = kt[:, src]
        for c in range(NCH - 1):
            e = c * CHUNK + CHUNK - 1
            bref = b_s[e:e + 1, :]
            q_rows = slice(e + 1, LS)
            k_rows = slice(c * CHUNK, e + 1)
            qt = (qg_s[q_rows] * jnp.exp(b_s[q_rows] - bref)).astype(BF16)
            kt = (kg_s[k_rows] * jnp.exp(bref - b_s[k_rows])).astype(BF16)
            for p in range(N_PAIRS):
                dst = slice((p * (NCH - 1) + c) * PAIR_W, (p * (NCH - 1) + c + 1) * PAIR_W)
                src = slice(p * PAIR_W, (p + 1) * PAIR_W)
                qx[r0 + q_rows.start:r0 + q_rows.stop, dst] = qt[:, src]
                kx[r0 + k_rows.start:r0 + k_rows.stop, dst] = kt[:, src]

        b_last = b_s[LS - 1:LS, :]
        qin_buf[rows, :] = (qg_s * jnp.exp(b_s)).astype(BF16)
        kout_buf[rows, :] = (kg_s * jnp.exp(b_last - b_s)).astype(BF16)

    def pair_unit(sg, p):
        r0 = sg * LS
        rows = slice(r0, r0 + LS)
        st = 0 if carry else sg
        pc = slice(p * PAIR_W, (p + 1) * PAIR_W)

        qb = proj[rows, C_QR + p * PAIR_W:C_QR + (p + 1) * PAIR_W].astype(BF16)
        kr = proj[rows, C_KR + p * PAIR_W:C_KR + (p + 1) * PAIR_W]
        kb = kr.astype(BF16)
        vb = proj[rows, C_VR + p * PAIR_W:C_VR + (p + 1) * PAIR_W].astype(BF16)
        zero = jnp.zeros_like(qb)
        s_a = _dot_nt(jnp.where(head_a, qb, zero), kb) * dret_ref[2 * p]
        s_b = _dot_nt(jnp.where(head_a, zero, qb), kb) * dret_ref[2 * p + 1]
        o_r = jnp.where(head_a, _dot(s_a.astype(BF16), vb), _dot(s_b.astype(BF16), vb))
        r_old = rst[0, p] if carry else pair_state(ret0_ref, sg, p)
        o_r = o_r + _dot(qb, r_old.astype(BF16)) * xi_ref[:, pc]
        u = _dot_tn((kr * zeta_ref[:, pc]).astype(BF16), vb)
        r_new = r_old * gls_ref[:, pc] + jnp.where(bd_mask, u, 0.0)
        if pipelined:
            r_new = jnp.where(keep, r_new, r_old)
        obuf[rows, pc] = o_r

        vg = proj[rows, C_VG + p * PAIR_W:C_VG + (p + 1) * PAIR_W].astype(BF16)
        wc = slice(p * (N_SUB - 1) * PAIR_W, (p + 1) * (N_SUB - 1) * PAIR_W)
        qw_p = qw[rows, wc]
        kw_p = kw[rows, wc]
        lane_w = lax.broadcasted_iota(jnp.int32, (1, qw_p.shape[1]), 1)
        head_a_w = (lane_w & (PAIR_W - 1)) < D_HEAD
        zw = jnp.zeros_like(qw_p)
        g_a = _dot_nt(jnp.where(head_a_w, qw_p, zw), kw_p)
        g_b = _dot_nt(jnp.where(head_a_w, zw, qw_p), kw_p)
        if NCH > 1:
            g_a = g_a * cmask_ref[...]
            g_b = g_b * cmask_ref[...]
            xc = slice(p * (NCH - 1) * PAIR_W, (p + 1) * (NCH - 1) * PAIR_W)
            qx_p = qx[rows, xc]
            kx_p = kx[rows, xc]
            lane_x = lax.broadcasted_iota(jnp.int32, (1, qx_p.shape[1]), 1)
            head_a_x = (lane_x & (PAIR_W - 1)) < D_HEAD
            zx = jnp.zeros_like(qx_p)
            g_a = g_a + _dot_nt(jnp.where(head_a_x, qx_p, zx), kx_p)
            g_b = g_b + _dot_nt(jnp.where(head_a_x, zx, qx_p), kx_p)
        o_g = jnp.where(head_a, _dot(g_a.astype(BF16), vg), _dot(g_b.astype(BF16), vg))
        s_old = sst[0, p] if carry else pair_state(hg0_ref, sg, p)
        o_g = o_g + _dot(qin_buf[rows, pc], s_old.astype(BF16))
        ut = _dot_tn(kout_buf[rows, pc], vg)
        b_last = proj[r0 + LS - 1:r0 + LS, C_FG + p * PAIR_W:C_FG + (p + 1) * PAIR_W]
        s_decay = jnp.exp(jnp.broadcast_to(b_last, (PAIR_W, PAIR_W)).T)
        s_new = s_old * s_decay + jnp.where(bd_mask, ut, 0.0)
        if pipelined:
            s_new = jnp.where(keep, s_new, s_old)
        gc = slice(GROUP_W + p * PAIR_W, GROUP_W + (p + 1) * PAIR_W)
        obuf[rows, gc] = o_g

        if carry:
            rst[0, p] = r_new
            sst[0, p] = s_new
        for out_ref, new in ((retout_ref, r_new), (hgout_ref, s_new)):
            out_ref[st, 2 * p] = new[:D_HEAD, :D_HEAD]
            out_ref[st, 2 * p + 1] = new[D_HEAD:, D_HEAD:]

    units = [(sg, p) for sg in range(NSEG) for p in range(N_PAIRS)]
    per_chunk = len(units) // (TB // CHUNK)
    for c in range(TB // CHUNK):
        for sg, p in units[c * per_chunk:(c + 1) * per_chunk]:
            pair_unit(sg, p)
        diag_chunk(c)

    xprev[...] = x
    if not pipelined:
        _mixer_back(proj, obuf, dbuf, xprev, hgn_ref, ones_ref, wout_bf, gffn_ref, wr_ref, br_ref, stri_ref,
                    x1_ref, h2s_ref, gs_ref, rinfo_ref, info_ref)


def _mixer_back(proj, obuf, dbuf, xprev, hgn_ref, ones_ref, wout_bf, gffn_ref, wr_ref, br_ref, stri_ref,
                x1_ref, h2s_ref, gs_ref, rinfo_ref, info_ref):
    o_r = obuf[:, 0:GROUP_W]
    mu = _group_sum(o_r, ones_ref) * (1.0 / D_HEAD)
    dlt = o_r - mu
    var = _group_sum(dlt * dlt, ones_ref) * (1.0 / D_HEAD)
    y_r = dlt * lax.rsqrt(var + RMS_EPS) * _silu(proj[:, C_GR:C_GR + GROUP_W])
    o_g = obuf[:, GROUP_W:2 * GROUP_W] + dbuf[...]
    ms = _group_sum(o_g * o_g, ones_ref) * (1.0 / D_HEAD)
    y_g = o_g * lax.rsqrt(ms + RMS_EPS) * hgn_ref[...] * _silu(proj[:, C_GG:C_GG + GROUP_W])
    mix = jnp.concatenate([y_r, y_g], axis=1).astype(BF16)
    x1 = xprev[...] + _dot(mix, wout_bf[...])
    x1_ref[...] = x1

    h2 = x1 * lax.rsqrt(jnp.mean(x1 * x1, axis=-1, keepdims=True) + RMS_EPS) * gffn_ref[...]
    h_hi, h_lo = _split(h2, 2)
    w_hi, w_lo = _split(wr_ref[...], 2)
    logits = _dot(h_hi, w_hi) + _dot(h_hi, w_lo) + _dot(h_lo, w_hi) + br_ref[...]
    lane = lax.broadcasted_iota(jnp.int32, (1, ROUTER_W), 1).astype(F32)
    neg = -jnp.inf
    no_lane = float(ROUTER_W)
    gl = jnp.where(lane < N_GROUPS, logits, neg)
    gmax = jnp.max(gl, axis=-1, keepdims=True)
    g_idx = jnp.min(jnp.where(gl == gmax, lane, no_lane), axis=-1, keepdims=True)
    prob_g = 1.0 / jnp.sum(jnp.exp(gl - gmax), axis=-1, keepdims=True)
    e_lo = N_GROUPS + EXPERTS_PER_GROUP * g_idx
    el = jnp.where((lane >= e_lo) & (lane < e_lo + EXPERTS_PER_GROUP), logits, neg)
    v1 = jnp.max(el, axis=-1, keepdims=True)
    i1 = jnp.min(jnp.where(el == v1, lane, no_lane), axis=-1, keepdims=True)
    el2 = jnp.where(lane == i1, neg, el)
    v2 = jnp.max(el2, axis=-1, keepdims=True)
    i2 = jnp.min(jnp.where(el2 == v2, lane, no_lane), axis=-1, keepdims=True)
    t = jnp.exp(v2 - v1)
    p1 = 1.0 / (1.0 + t)
    p2 = t * p1
    gate = jnp.where(lane == i1, prob_g * p1, 0.0) + jnp.where(lane == i2, prob_g * p2, 0.0)

    onehot = jnp.where(lane == g_idx, 1.0, 0.0)
    before = _dot(stri_ref[...], onehot.astype(BF16))
    count = jnp.sum(onehot, axis=0, keepdims=True)
    padded = jnp.floor((count + (SLAB - 1.0)) * (1.0 / SLAB)) * SLAB
    padded8 = jnp.broadcast_to(padded, (8, ROUTER_W))
    seg_start = sum(jnp.where(lane >= k, pltpu.roll(padded8, k, 1), 0.0) for k in range(1, N_GROUPS))[0:1]
    rank = jnp.sum(onehot * (seg_start + before), axis=-1, keepdims=True)
    col = lax.broadcasted_iota(jnp.int32, (1, SORT_W), 1).astype(F32)
    place = jnp.where(col == rank, 1.0, 0.0).astype(BF16)
    h2s_ref[...] = _dot_tn(place, h2.astype(BF16))[:SORT_ROWS].astype(BF16)
    gs_ref[...] = sum(_dot_tn(place, part) for part in _split(gate, 3))[:SORT_ROWS]
    rinfo_ref[...] = jnp.where(lane == 0, rank, 0.0)
    slab_lo = (lane * SLAB).astype(F32)
    slab_group = jnp.full((1, ROUTER_W), -1.0, F32)
    for g in range(N_GROUPS):
        s_g = jnp.sum(jnp.where(lane == g, seg_start, 0.0), axis=-1, keepdims=True)
        n_g = jnp.sum(jnp.where(lane == g, padded, 0.0), axis=-1, keepdims=True)
        slab_group = jnp.where((slab_lo >= s_g) & (slab_lo < s_g + n_g), float(g), slab_group)
    info_ref[0] = jnp.broadcast_to(slab_group.astype(jnp.int32), (8, ROUTER_W))


def _slab_gather(table_ref, first, n_slabs, srcs, bufs, sems, slot, *, wait):
    n_rows = n_slabs * SLAB
    for src, buf, sem in zip(srcs, bufs, sems):
        if wait:
            pltpu.make_async_copy(src.at[pl.ds(0, n_rows)], buf.at[slot, pl.ds(0, n_rows)], sem.at[slot]).wait()
            continue
        for j in range(n_slabs):
            row = pl.multiple_of(jnp.maximum(table_ref[first + j], 0) * SLAB, SLAB)
            pltpu.make_async_copy(src.at[pl.ds(row, SLAB)], buf.at[slot, pl.ds(j * SLAB, SLAB)],
                                  sem.at[slot]).start()


def _moe_kernel(src_ref, tg_ref, nt_ref, h2s_hbm, gs_hbm, wg_ref, wu_ref, wd_ref, ys_ref, hbuf, gbuf, sem_h, sem_g):
    i = pl.program_id(0)
    n_tiles = nt_ref[0]
    slot = lax.rem(i, 2)
    gather = functools.partial(_slab_gather, src_ref, srcs=(h2s_hbm, gs_hbm), bufs=(hbuf, gbuf),
                               sems=(sem_h, sem_g), n_slabs=TILE_SLABS)

    @pl.when(i == 0)
    def _():
        gather(first=0, slot=0, wait=False)

    @pl.when(i < n_tiles)
    def _():
        gather(first=i * TILE_SLABS, slot=slot, wait=True)

        @pl.when(i + 1 < n_tiles)
        def _():
            gather(first=(i + 1) * TILE_SLABS, slot=1 - slot, wait=False)

        hb = hbuf[slot]
        gates = gbuf[slot]
        lane = lax.broadcasted_iota(jnp.int32, (1, ROUTER_W), 1)
        first_lane = N_GROUPS + tg_ref[i] * EXPERTS_PER_GROUP
        acc = jnp.zeros((TM, D_MODEL), F32)
        for e in range(EXPERTS_PER_GROUP):
            he = (_silu(_dot(hb, wg_ref[e])) * _dot(hb, wu_ref[e])).astype(BF16)
            gcol = jnp.sum(jnp.where(lane == first_lane + e, gates, 0.0), axis=-1, keepdims=True)
            acc = acc + gcol * _dot(he, wd_ref[e])
        ys_ref[...] = acc.astype(BF16)

    @pl.when(i >= n_tiles)
    def _():
        ys_ref[...] = jnp.zeros_like(ys_ref)


def _final_kernel(blocks_p, dest_ref, x1_ref, rinfo_ref, ys_hbm, gfin_ref, yp_ref, ysmp_ref, ybuf, sem):
    i = pl.program_id(0)
    slot = lax.rem(i, 2)
    gather = functools.partial(_slab_gather, dest_ref, srcs=(ys_hbm,), bufs=(ybuf,), sems=(sem,),
                               n_slabs=SLABS_PER_BLOCK)

    @pl.when(i == 0)
    def _():
        ybuf[:, SORT_ROWS:SORT_W, :] = jnp.zeros((2, SORT_W - SORT_ROWS, D_MODEL), BF16)
        gather(first=0, slot=0, wait=False)

    gather(first=i * SLABS_PER_BLOCK, slot=slot, wait=True)

    @pl.when(i + 1 < pl.num_programs(0))
    def _():
        gather(first=(i + 1) * SLABS_PER_BLOCK, slot=1 - slot, wait=False)

    rank = rinfo_ref[:, 0:1]
    col = lax.broadcasted_iota(jnp.int32, (1, SORT_W), 1).astype(F32)
    place = jnp.where(col == rank, 1.0, 0.0).astype(BF16)
    moe = _dot(place, ybuf[slot])
    xo = x1_ref[...] + moe
    y = xo * lax.rsqrt(jnp.mean(xo * xo, axis=-1, keepdims=True) + RMS_EPS) * gfin_ref[...]

    @pl.when(i < blocks_p)
    def _():
        yp_ref[...] = y

    @pl.when(i >= blocks_p)
    def _():
        ysmp_ref[...] = y


def _const_spec(shape, pipeline_mode=None):
    nd = len(shape)
    return pl.BlockSpec(shape, lambda i: (0,) * nd, pipeline_mode=pipeline_mode)


def _mixer_call(x2d, ret0, hg0, params, *, seg_len, carry, pipelined, block_offset, total_blocks, shared=(),
                expert_weights=()):
    (gmix, win, lbl, hgn, wout, gffn, wr, br) = params
    T = x2d.shape[0]
    LS = seg_len
    n_seg = TB // LS
    NCH = LS // CHUNK
    n_states = ret0.shape[0]
    n_own = T // TB
    n_fill = 0 if shared else total_blocks - n_own - block_offset
    lag = 1 if pipelined else 0
    grid = n_own + lag + n_fill

    f32 = np.float32
    lg = np.log1p(-(f32(2.0) ** (f32(-5.0) - np.arange(N_HEADS, dtype=f32)))).astype(f32)
    tt = np.arange(LS, dtype=f32)
    ch = np.arange(LS) // CHUNK
    dret = np.exp(np.abs(tt[:, None] - tt[None, :])[None] * lg[:, None, None]).astype(f32)
    dret = np.where((ch[None, :] <= ch[:, None])[None], dret, f32(0.0))
    lg_lane = np.repeat(lg, D_HEAD)[None, :]
    xi = np.exp((tt[:, None] + f32(1.0)) * lg_lane).astype(f32)
    zeta = np.exp((f32(LS - 1.0) - tt)[:, None] * lg_lane).astype(f32)
    gls = np.exp(f32(LS) * lg_lane).astype(f32)
    half = D_HEAD // 2
    inv_freq = ROPE_BASE ** (-jnp.arange(half, dtype=F32) / half)
    invf = jnp.tile(inv_freq, PAIR_W // half)[None, :]
    sgn = np.where((np.arange(PAIR_W) % D_HEAD) < half, f32(-1.0), f32(1.0))[None, :]
    tri = jnp.asarray(np.tril(np.ones((LS, LS), np.float32)), BF16)
    lane_head = np.arange(256) // D_HEAD
    ones_bd = jnp.asarray((lane_head[:, None] == lane_head[None, :]).astype(np.float32), BF16)
    cmask = jnp.asarray((ch[:, None] == ch[None, :]).astype(np.float32))

    if carry:
        state_spec = pl.BlockSpec((1, N_HEADS, D_HEAD, D_HEAD), lambda i: (0, 0, 0, 0))
    else:
        state_spec = pl.BlockSpec((n_seg, N_HEADS, D_HEAD, D_HEAD), lambda i: (i, 0, 0, 0))
    single = pl.Buffered(1)
    stri = jnp.asarray(np.tril(np.ones((TB, TB), np.float32), -1), BF16)
    in_row_spec = pl.BlockSpec((TB, D_MODEL), lambda i: (jnp.minimum(i, n_own - 1), 0))
    out_block = lambda i: jnp.maximum(i - lag, 0) + block_offset
    out_row_spec = lambda rows, w: pl.BlockSpec((rows, w), lambda i: (out_block(i), 0))
    nw = (N_SUB - 1) * N_PAIRS * PAIR_W
    nx = max(NCH - 1, 1) * N_PAIRS * PAIR_W

    cast_w = bool(expert_weights)
    w2d = [w.reshape(-1, w.shape[-1]) for w in expert_weights]
    w_specs = [pl.BlockSpec((w.shape[0] // n_own, w.shape[1]), lambda i: (jnp.minimum(i, n_own - 1), 0)) for w in w2d]
    w_shapes = [jax.ShapeDtypeStruct(w.shape, BF16) for w in w2d]
    n_in = N_MIXER_INPUTS + len(w2d)

    kern = functools.partial(_mixer_kernel, (LS, n_seg, carry, pipelined, n_own, n_fill, len(shared), cast_w))
    return pl.pallas_call(
        kern,
        grid=(grid,),
        in_specs=[
            in_row_spec,
            _const_spec((1, D_MODEL)), _const_spec(win.shape, single), _const_spec((1, PAIR_W)),
            _const_spec((1, PAIR_W)), _const_spec(lbl.shape), _const_spec((1, GROUP_W)),
            _const_spec(wout.shape, single), _const_spec((1, D_MODEL)),
            _const_spec(wr.shape), _const_spec((1, ROUTER_W)),
            _const_spec((LS, LS)), _const_spec((N_HEADS, LS, LS)), _const_spec((LS, GROUP_W)),
            _const_spec((LS, GROUP_W)), _const_spec((1, GROUP_W)), _const_spec((256, 256)), _const_spec((LS, LS)),
            _const_spec((TB, TB)),
            state_spec, state_spec,
        ] + w_specs + [pl.BlockSpec(memory_space=pl.ANY)] * len(shared),
        out_specs=[out_row_spec(TB, D_MODEL), out_row_spec(SORT_ROWS, D_MODEL), out_row_spec(SORT_ROWS, ROUTER_W),
                   out_row_spec(TB, ROUTER_W),
                   pl.BlockSpec((1, 8, ROUTER_W), lambda i: (out_block(i), 0, 0)),
                   state_spec, state_spec] + w_specs,
        out_shape=[
            jax.ShapeDtypeStruct((total_blocks * TB, D_MODEL), F32),
            jax.ShapeDtypeStruct((total_blocks * SORT_ROWS, D_MODEL), BF16),
            jax.ShapeDtypeStruct((total_blocks * SORT_ROWS, ROUTER_W), F32),
            jax.ShapeDtypeStruct((total_blocks * TB, ROUTER_W), F32),
            jax.ShapeDtypeStruct((total_blocks, 8, ROUTER_W), jnp.int32),
            jax.ShapeDtypeStruct((n_states, N_HEADS, D_HEAD, D_HEAD), F32),
            jax.ShapeDtypeStruct((n_states, N_HEADS, D_HEAD, D_HEAD), F32),
        ] + w_shapes,
        input_output_aliases={n_in + k: k for k in range(len(shared))},
        scratch_shapes=[
            *[pltpu.VMEM((TB, GROUP_W), F32) for _ in range(8)],
            pltpu.VMEM((TB, GROUP_W), F32),
            pltpu.VMEM((TB, 2 * GROUP_W), F32),
            pltpu.VMEM((TB, GROUP_W), F32),
            pltpu.VMEM((TB, D_MODEL), F32),
            pltpu.VMEM((TB, nw), BF16), pltpu.VMEM((TB, nw), BF16),
            pltpu.VMEM((TB, nx), BF16), pltpu.VMEM((TB, nx), BF16),
            pltpu.VMEM((TB, GROUP_W), BF16), pltpu.VMEM((TB, GROUP_W), BF16),
            pltpu.VMEM((N_SUB * (SUB // 2) * (SUB + SUB // 2), GROUP_W), BF16),
            pltpu.VMEM((TB, PAIR_W), F32), pltpu.VMEM((TB, PAIR_W), F32),
            pltpu.VMEM(win.shape, BF16), pltpu.VMEM(wout.shape, BF16),
            pltpu.VMEM((1, N_PAIRS, PAIR_W, PAIR_W), F32),
            pltpu.VMEM((1, N_PAIRS, PAIR_W, PAIR_W), F32),
        ],
        compiler_params=pltpu.CompilerParams(
            dimension_semantics=("arbitrary",), vmem_limit_bytes=V7X_VMEM_LIMIT),
    )(x2d, gmix, win, invf, sgn, lbl, hgn, wout, gffn, wr, br, tri, dret, xi, zeta, gls, ones_bd, cmask, stri,
      ret0, hg0, *w2d, *shared)


def _dispatch_tables(info, n_tiles_max):
    slab_group = info[:, 0, :SLABS_PER_BLOCK].reshape(-1)
    n_slabs = slab_group.shape[0]
    valid = slab_group >= 0
    onehot = (slab_group[:, None] == jnp.arange(N_GROUPS, dtype=jnp.int32)[None, :]).astype(jnp.int32)
    within = jnp.cumsum(onehot, axis=0) - onehot
    tiles = (jnp.sum(onehot, axis=0) + TILE_SLABS - 1) // TILE_SLABS
    tile_end = jnp.cumsum(tiles)
    g = jnp.clip(slab_group, 0, N_GROUPS - 1)
    dest = (tile_end - tiles)[g] * TILE_SLABS + jnp.take_along_axis(within, g[:, None], axis=1)[:, 0]
    dest = jnp.where(valid, dest, -1).astype(jnp.int32)
    n_slots = n_tiles_max * TILE_SLABS
    src = jnp.full((n_slots,), -1, jnp.int32).at[jnp.where(valid, dest, n_slots)].set(
        jnp.arange(n_slabs, dtype=jnp.int32), mode="drop")
    tile_group = jnp.sum(jnp.arange(n_tiles_max, dtype=jnp.int32)[:, None] >= tile_end[None, :], axis=1)
    tile_group = jnp.minimum(tile_group, N_GROUPS - 1).astype(jnp.int32)
    return src, dest, tile_group, tile_end[-1:].astype(jnp.int32)


def _moe_call(src, tile_group, n_tiles, h2s, gs, w_gate, w_up, w_down):
    n_tiles_max = tile_group.shape[0]
    group_of_tile = lambda i, s, tg, nt: (tg[i], 0, 0)
    return pl.pallas_call(
        _moe_kernel,
        grid_spec=pltpu.PrefetchScalarGridSpec(
            num_scalar_prefetch=3,
            grid=(n_tiles_max,),
            in_specs=[
                pl.BlockSpec(memory_space=pl.ANY), pl.BlockSpec(memory_space=pl.ANY),
                pl.BlockSpec((EXPERTS_PER_GROUP, D_MODEL, D_EXPERT), group_of_tile),
                pl.BlockSpec((EXPERTS_PER_GROUP, D_MODEL, D_EXPERT), group_of_tile),
                pl.BlockSpec((EXPERTS_PER_GROUP, D_EXPERT, D_MODEL), group_of_tile),
            ],
            out_specs=pl.BlockSpec((TM, D_MODEL), lambda i, s, tg, nt: (i, 0)),
            scratch_shapes=[
                pltpu.VMEM((2, TM, D_MODEL), BF16), pltpu.VMEM((2, TM, ROUTER_W), F32),
                pltpu.SemaphoreType.DMA((2,)), pltpu.SemaphoreType.DMA((2,)),
            ],
        ),
        out_shape=jax.ShapeDtypeStruct((n_tiles_max * TM, D_MODEL), BF16),
        compiler_params=pltpu.CompilerParams(
            dimension_semantics=("arbitrary",), vmem_limit_bytes=V7X_VMEM_LIMIT),
    )(src, tile_group, n_tiles, h2s, gs, w_gate, w_up, w_down)


def _final_call(dest, x1, rinfo, ys, gfin, blocks_p):
    n_blocks = x1.shape[0] // TB
    return pl.pallas_call(
        functools.partial(_final_kernel, blocks_p),
        grid_spec=pltpu.PrefetchScalarGridSpec(
            num_scalar_prefetch=1,
            grid=(n_blocks,),
            in_specs=[
                pl.BlockSpec((TB, D_MODEL), lambda i, d: (i, 0)),
                pl.BlockSpec((TB, ROUTER_W), lambda i, d: (i, 0)),
                pl.BlockSpec(memory_space=pl.ANY),
                pl.BlockSpec((1, D_MODEL), lambda i, d: (0, 0)),
            ],
            out_specs=[
                pl.BlockSpec((TB, D_MODEL), lambda i, d: (jnp.minimum(i, blocks_p - 1), 0)),
                pl.BlockSpec((TB, D_MODEL), lambda i, d: (jnp.maximum(i - blocks_p, 0), 0)),
            ],
            scratch_shapes=[pltpu.VMEM((2, SORT_W, D_MODEL), BF16), pltpu.SemaphoreType.DMA((2,))],
        ),
        out_shape=[jax.ShapeDtypeStruct((blocks_p * TB, D_MODEL), F32),
                   jax.ShapeDtypeStruct(((n_blocks - blocks_p) * TB, D_MODEL), F32)],
        compiler_params=pltpu.CompilerParams(
            dimension_semantics=("arbitrary",), vmem_limit_bytes=V7X_VMEM_LIMIT),
    )(dest, x1, rinfo, ys, gfin)


def kernel(x_prompt, x_sample, state_ret, state_hgrn, norm_mix_g, w_in, hgrn_lb_logits, hgrn_norm_g, w_out,
           norm_ffn_g, w_router_group, b_router_group, w_router_expert, b_router_expert, w_exp_gate, w_exp_up,
           w_exp_down, norm_final_g):
    depth = w_in.shape[0]
    assert depth == 1 and hgrn_lb_logits.shape[0] == 2, "single-layer configuration only"
    bp, seq, d = x_prompt.shape
    db, dec_len, _ = x_sample.shape
    assert bp == 1 and d == D_MODEL and dec_len == CHUNK and seq % 256 == 0 and db % 4 == 0

    pad = ROUTER_W - N_GROUPS - N_EXPERTS
    wr = jnp.concatenate([w_router_group[0], w_router_expert[0], jnp.zeros((D_MODEL, pad), F32)], axis=1)
    br = jnp.concatenate([b_router_group[0], b_router_expert[0], jnp.zeros((pad,), F32)])[None, :]
    params = (norm_mix_g[0][None, :], w_in[0], hgrn_lb_logits,
              jnp.tile(hgrn_norm_g[0], N_HEADS)[None, :], w_out[0], norm_ffn_g[0][None, :], wr, br)

    zeros_state = jnp.zeros((1, N_HEADS, D_HEAD, D_HEAD), F32)
    blocks_p = seq // TB
    blocks_s = db * dec_len // TB
    n_blocks = blocks_p + blocks_s
    *shared, ret_p, hg_p, wg_bf, wu_bf, wd_bf = _mixer_call(
        x_prompt.reshape(seq, d), zeros_state, zeros_state, params, seg_len=TB, carry=True, pipelined=True,
        block_offset=0, total_blocks=n_blocks, expert_weights=(w_exp_gate[0], w_exp_up[0], w_exp_down[0]))
    x1, h2s, gs, rinfo, info, ret_s, hg_s = _mixer_call(
        x_sample.reshape(db * dec_len, d), state_ret[0], state_hgrn[0], params, seg_len=CHUNK, carry=False,
        pipelined=False,
        block_offset=blocks_p, total_blocks=n_blocks, shared=tuple(shared))

    n_tiles_max = -(-n_blocks * SLABS_PER_BLOCK // TILE_SLABS) + N_GROUPS
    src, dest, tile_group, n_tiles = _dispatch_tables(info, n_tiles_max)
    ys = _moe_call(src, tile_group, n_tiles, h2s, gs, wg_bf.reshape(w_exp_gate[0].shape),
                   wu_bf.reshape(w_exp_up[0].shape), wd_bf.reshape(w_exp_down[0].shape))
    y_p, y_s = _final_call(dest, x1, rinfo, ys, norm_final_g[None, :], blocks_p)
    return (y_p.reshape(bp, seq, d), y_s.reshape(db, dec_len, d), ret_p[None], hg_p[None], ret_s[None], hg_s[None])
```

```python
import functools

import numpy as np
import jax
import jax.numpy as jnp
from jax import lax
from jax.experimental import pallas as pl
from jax.experimental.pallas import tpu as pltpu

F32 = jnp.float32
BF16 = jnp.bfloat16

D_MODEL = 1024
N_HEADS = 8
D_HEAD = 64
GROUP_W = N_HEADS * D_HEAD
N_PAIRS = N_HEADS // 2
PAIR_W = 2 * D_HEAD
CHUNK = 64
SUB = 16
N_SUB = CHUNK // SUB
PAST_LEN = 2048
ROPE_BASE = 10000.0
RMS_EPS = 1e-6
LOG2E = 1.4426950408889634
N_GROUPS = 4
EXPERTS_PER_GROUP = 8
N_EXPERTS = N_GROUPS * EXPERTS_PER_GROUP
D_EXPERT = 256
ROUTER_W = 128
V7X_VMEM_LIMIT = 60 * 1024 * 1024
TB = 256
SLAB = 16
SLABS_PER_BLOCK = TB // SLAB + N_GROUPS
SORT_ROWS = SLABS_PER_BLOCK * SLAB
SORT_W = 384
TILE_SLABS = 32
TM = TILE_SLABS * SLAB
FINAL_BLOCKS = 2

C_QR, C_KR, C_VR, C_GR, C_QG, C_FG, C_VG, C_GG = (i * GROUP_W for i in range(8))


def _dot(a, b):
    return jnp.dot(a, b, preferred_element_type=F32)


def _dot_nt(a, b):
    return lax.dot_general(a, b, (((1,), (1,)), ((), ())), preferred_element_type=F32)


def _dot_tn(a, b):
    return lax.dot_general(a, b, (((0,), (0,)), ((), ())), preferred_element_type=F32)


def _split(x, n):
    parts = []
    for _ in range(n):
        p = x.astype(BF16)
        parts.append(p)
        x = x - p.astype(F32)
    return parts


def _group_sum(x, ones_ref):
    xb = x.astype(BF16)
    return jnp.concatenate([_dot(xb[:, c * 256:(c + 1) * 256], ones_ref[...]) for c in range(2)], axis=1)


class _ColumnGroups:
    def __init__(self, bufs):
        self.bufs = bufs

    def _locate(self, idx):
        rows, cols = idx
        g = cols.start // GROUP_W
        assert (cols.stop - 1) // GROUP_W == g
        return self.bufs[g], (rows, slice(cols.start - g * GROUP_W, cols.stop - g * GROUP_W))

    def __getitem__(self, idx):
        buf, at = self._locate(idx)
        return buf[at]

    def __setitem__(self, idx, value):
        buf, at = self._locate(idx)
        buf[at] = value


def _silu(x):
    return x * jax.nn.sigmoid(x)


N_MIXER_INPUTS = 21


def _mixer_kernel(cfg, *refs):
    pipelined, n_own, n_fill, n_alias, cast_w = cfg[-5:]
    n_body = n_own + (1 if pipelined else 0)
    n_in = N_MIXER_INPUTS + (3 if cast_w else 0)
    i = pl.program_id(0)

    @pl.when(i < n_body)
    def _():
        _mixer_body(cfg, *refs)

    if n_fill:
        @pl.when(i >= n_body)
        def _():
            for out_ref in refs[n_in + n_alias:n_in + n_alias + 5]:
                out_ref[...] = jnp.zeros_like(out_ref)


def _mixer_body(cfg, x_ref, gmix_ref, win_ref, invf_ref, sgn_ref, lbl_ref, hgn_ref, wout_ref, gffn_ref,
                wr_ref, br_ref, tri_ref, dret_ref, xi_ref, zeta_ref, gls_ref, ones_ref, cmask_ref, stri_ref,
                ret0_ref, hg0_ref, *rest):
    LS, NSEG, carry, pipelined, n_own, _, n_alias, cast_w = cfg
    w_chunks = ()
    if cast_w:
        w_chunks = tuple(zip(rest[:3], rest[3 + n_alias + 7:3 + n_alias + 10]))
        rest = rest[3:3 + n_alias + 7] + rest[3 + n_alias + 10:]
    (x1_ref, h2s_ref, gs_ref, rinfo_ref, info_ref, retout_ref, hgout_ref,
     *proj_bufs, kbuf, obuf, dbuf, xprev, qw, kw, qx, kx, qin_buf, kout_buf, pbuf, cos_t, sin_t, win_bf, wout_bf,
     rst, sst) = rest[n_alias:]
    proj = _ColumnGroups(proj_bufs)
    NCH = LS // CHUNK
    i = pl.program_id(0)
    lane128 = lax.broadcasted_iota(jnp.int32, (1, PAIR_W), 1)
    head_a = lane128 < D_HEAD
    bd_mask = (lax.broadcasted_iota(jnp.int32, (PAIR_W, PAIR_W), 0) < D_HEAD) == head_a

    def pair_state(ref, sg, p):
        z = jnp.zeros((D_HEAD, D_HEAD), F32)
        return jnp.concatenate([jnp.concatenate([ref[sg, 2 * p], z], axis=1),
                                jnp.concatenate([z, ref[sg, 2 * p + 1]], axis=1)], axis=0)

    @pl.when(i == 0)
    def _():
        for c in range(0, 8 * GROUP_W, GROUP_W):
            win_bf[:, c:c + GROUP_W] = win_ref[:, c:c + GROUP_W].astype(BF16)
        wout_bf[...] = wout_ref[...].astype(BF16)
        if carry:
            for p in range(N_PAIRS):
                rst[0, p] = pair_state(ret0_ref, 0, p)
                sst[0, p] = pair_state(hg0_ref, 0, p)
        row = lax.broadcasted_iota(jnp.int32, (TB, 1), 0)
        ang_row = (row & (LS - 1)).astype(F32) * invf_ref[...]
        cos_t[...] = jnp.cos(ang_row)
        sin_t[...] = jnp.sin(ang_row)
        qw[...] = jnp.zeros_like(qw)
        kw[...] = jnp.zeros_like(kw)
        if NCH > 1:
            qx[...] = jnp.zeros_like(qx)
            kx[...] = jnp.zeros_like(kx)
        if pipelined:
            for buf in (obuf, dbuf, xprev, proj_bufs[C_GR // GROUP_W], proj_bufs[C_GG // GROUP_W]):
                buf[...] = jnp.zeros_like(buf)

    if pipelined:
        _mixer_back(proj, obuf, dbuf, xprev, hgn_ref, ones_ref, wout_bf, gffn_ref, wr_ref, br_ref, stri_ref,
                    x1_ref, h2s_ref, gs_ref, rinfo_ref, info_ref)
    keep = i < n_own

    for w_ref, w_bf_ref in w_chunks:
        w_bf_ref[...] = w_ref[...].astype(BF16)

    x = x_ref[...]
    h = x * lax.rsqrt(jnp.mean(x * x, axis=-1, keepdims=True) + RMS_EPS) * gmix_ref[...]
    hb = h.astype(BF16)

    def project(c0, c1):
        for c in range(c0, c1, GROUP_W):
            proj[:, c:c + GROUP_W] = _dot(hb, win_bf[:, c:c + GROUP_W])

    project(C_QG, C_GG)
    project(C_QR, C_VR)

    la = lbl_ref[0:1, :]
    lb_ = lbl_ref[1:2, :]
    lmax = jnp.maximum(la, lb_)
    ea = jnp.exp(la - lmax)
    lbv = ea / (ea + jnp.exp(lb_ - lmax))
    qg = proj[:, C_QG:C_QG + GROUP_W]
    proj[:, C_QG:C_QG + GROUP_W] = _silu(qg)
    f = lbv + (1.0 - lbv) * jax.nn.sigmoid(proj[:, C_FG:C_FG + GROUP_W])
    kbuf[...] = 1.0 - f
    logf = jnp.log(f)
    for sg in range(NSEG):
        rows = slice(sg * LS, (sg + 1) * LS)
        parts = _split(logf[rows], 3)
        proj[rows, C_FG:C_FG + GROUP_W] = sum(_dot(tri_ref[...], p) for p in parts)

    project(C_VR, C_QG)
    project(C_GG, C_GG + GROUP_W)

    start = jnp.full((8, PAIR_W), i * TB if carry else PAST_LEN, jnp.int32).astype(F32)
    ang0 = (start * invf_ref[...])[0:1]
    c0 = jnp.cos(ang0)
    s0 = jnp.sin(ang0)
    cos = cos_t[...] * c0 - sin_t[...] * s0
    sin = (sin_t[...] * c0 + cos_t[...] * s0) * sgn_ref[...]
    first_half = (lane128 & (D_HEAD - 1)) < (D_HEAD // 2)
    for blk in range(2 * N_PAIRS):
        cols = slice(blk * PAIR_W, (blk + 1) * PAIR_W)
        xx = proj[:, cols]
        partner = jnp.where(first_half, pltpu.roll(xx, PAIR_W - D_HEAD // 2, 1), pltpu.roll(xx, D_HEAD // 2, 1))
        r = xx * cos + partner * sin
        if blk < N_PAIRS:
            r = r * (D_HEAD ** -0.5)
        proj[:, cols] = r

    HALF = SUB // 2
    riota = lax.broadcasted_iota(jnp.int32, (HALF, 1), 0)
    P_ROWS = HALF * SUB + HALF * HALF

    def diag_chunk(c):
        for sb in range(N_SUB):
            rs = c * CHUNK + sb * SUB
            q16 = proj[rs:rs + SUB, C_QG:C_QG + GROUP_W]
            b16 = proj[rs:rs + SUB, C_FG:C_FG + GROUP_W] * LOG2E
            k16 = kbuf[rs:rs + SUB, :]
            base = sb * P_ROWS
            upper = []
            for s in range(SUB):
                ks = k16[s:s + 1]
                bs = b16[s:s + 1]
                hi = q16[HALF:] * ks * jnp.exp2(b16[HALF:] - bs)
                if s < HALF:
                    lo = q16[:HALF] * ks * jnp.exp2(b16[:HALF] - bs)
                    lo = jnp.where(riota >= s, lo, 0.0)
                    pbuf[base + s * SUB:base + (s + 1) * SUB, :] = jnp.concatenate([lo, hi], axis=0).astype(BF16)
                else:
                    upper.append(jnp.where(riota >= s - HALF, hi, 0.0))
                    if len(upper) == 2:
                        at = base + HALF * SUB + (s - HALF - 1) * HALF
                        pbuf[at:at + SUB, :] = jnp.concatenate(upper, axis=0).astype(BF16)
                        upper = []
        sc = [_dot(pbuf[:, hh * 256:(hh + 1) * 256], ones_ref[...]) for hh in range(2)]
        for sb in range(N_SUB):
            rs = c * CHUNK + sb * SUB
            v16 = proj[rs:rs + SUB, C_VG:C_VG + GROUP_W]
            base = sb * P_ROWS
            acc_lo = [jnp.zeros((HALF, 256), F32) for _ in range(2)]
            acc_hi = [jnp.zeros((HALF, 256), F32) for _ in range(2)]
            for s in range(SUB):
                for hh in range(2):
                    vs = v16[s:s + 1, hh * 256:(hh + 1) * 256]
                    if s < HALF:
                        at = base + s * SUB
                        acc_lo[hh] = acc_lo[hh] + sc[hh][at:at + HALF] * vs
                        acc_hi[hh] = acc_hi[hh] + sc[hh][at + HALF:at + SUB] * vs
                    else:
                        at = base + HALF * SUB + (s - HALF) * HALF
                        acc_hi[hh] = acc_hi[hh] + sc[hh][at:at + HALF] * vs
            dbuf[rs:rs + HALF, :] = jnp.concatenate(acc_lo, axis=1)
            dbuf[rs + HALF:rs + SUB, :] = jnp.concatenate(acc_hi, axis=1)

    for sg in range(NSEG):
        r0 = sg * LS
        rows = slice(r0, r0 + LS)
        st = 0 if carry else sg
        qg_s = proj[rows, C_QG:C_QG + GROUP_W]
        kg_s = kbuf[rows, :]
        b_s = proj[rows, C_FG:C_FG + GROUP_W]

        for c in range(NCH):
            for j in range(N_SUB - 1):
                e = c * CHUNK + j * SUB + SUB - 1
                bref = b_s[e:e + 1, :]
                q_rows = slice(e + 1, (c + 1) * CHUNK)
                k_rows = slice(e + 1 - SUB, e + 1)
                qt = (qg_s[q_rows] * jnp.exp(b_s[q_rows] - bref)).astype(BF16)
                kt = (kg_s[k_rows] * jnp.exp(bref - b_s[k_rows])).astype(BF16)
                for p in range(N_PAIRS):
                    dst = slice((p * (N_SUB - 1) + j) * PAIR_W, (p * (N_SUB - 1) + j + 1) * PAIR_W)
                    src = slice(p * PAIR_W, (p + 1) * PAIR_W)
                    qw[r0 + q_rows.start:r0 + q_rows.stop, dst] = qt[:, src]
                    kw[r0 + k_rows.start:r0 + k_rows.stop, dst] = kt[:, src]
        for c in range(NCH - 1):
            e = c * CHUNK + CHUNK - 1
            bref = b_s[e:e + 1, :]
            q_rows = slice(e + 1, LS)
            k_rows = slice(c * CHUNK, e + 1)
            qt = (qg_s[q_rows] * jnp.exp(b_s[q_rows] - bref)).astype(BF16)
            kt = (kg_s[k_rows] * jnp.exp(bref - b_s[k_rows])).astype(BF16)
            for p in range(N_PAIRS):
                dst = slice((p * (NCH - 1) + c) * PAIR_W, (p * (NCH - 1) + c + 1) * PAIR_W)
                src = slice(p * PAIR_W, (p + 1) * PAIR_W)
                qx[r0 + q_rows.start:r0 + q_rows.stop, dst] = qt[:, src]
                kx[r0 + k_rows.start:r0 + k_rows.stop, dst] = kt[:, src]

        b_last = b_s[LS - 1:LS, :]
        qin_buf[rows, :] = (qg_s * jnp.exp(b_s)).astype(BF16)
        kout_buf[rows, :] = (kg_s * jnp.exp(b_last - b_s)).astype(BF16)

    def pair_unit(sg, p):
        r0 = sg * LS
        rows = slice(r0, r0 + LS)
        st = 0 if carry else sg
        pc = slice(p * PAIR_W, (p + 1) * PAIR_W)

        qb = proj[rows, C_QR + p * PAIR_W:C_QR + (p + 1) * PAIR_W].astype(BF16)
        kr = proj[rows, C_KR + p * PAIR_W:C_KR + (p + 1) * PAIR_W]
        kb = kr.astype(BF16)
        vb = proj[rows, C_VR + p * PAIR_W:C_VR + (p + 1) * PAIR_W].astype(BF16)
        zero = jnp.zeros_like(qb)
        s_a = _dot_nt(jnp.where(head_a, qb, zero), kb) * dret_ref[2 * p]
        s_b = _dot_nt(jnp.where(head_a, zero, qb), kb) * dret_ref[2 * p + 1]
        o_r = jnp.where(head_a, _dot(s_a.astype(BF16), vb), _dot(s_b.astype(BF16), vb))
        r_old = rst[0, p] if carry else pair_state(ret0_ref, sg, p)
        o_r = o_r + _dot(qb, r_old.astype(BF16)) * xi_ref[:, pc]
        u = _dot_tn((kr * zeta_ref[:, pc]).astype(BF16), vb)
        r_new = r_old * gls_ref[:, pc] + jnp.where(bd_mask, u, 0.0)
        if pipelined:
            r_new = jnp.where(keep, r_new, r_old)
        obuf[rows, pc] = o_r

        vg = proj[rows, C_VG + p * PAIR_W:C_VG + (p + 1) * PAIR_W].astype(BF16)
        wc = slice(p * (N_SUB - 1) * PAIR_W, (p + 1) * (N_SUB - 1) * PAIR_W)
        qw_p = qw[rows, wc]
        kw_p = kw[rows, wc]
        lane_w = lax.broadcasted_iota(jnp.int32, (1, qw_p.shape[1]), 1)
        head_a_w = (lane_w & (PAIR_W - 1)) < D_HEAD
        zw = jnp.zeros_like(qw_p)
        g_a = _dot_nt(jnp.where(head_a_w, qw_p, zw), kw_p)
        g_b = _dot_nt(jnp.where(head_a_w, zw, qw_p), kw_p)
        if NCH > 1:
            g_a = g_a * cmask_ref[...]
            g_b = g_b * cmask_ref[...]
            xc = slice(p * (NCH - 1) * PAIR_W, (p + 1) * (NCH - 1) * PAIR_W)
            qx_p = qx[rows, xc]
            kx_p = kx[rows, xc]
            lane_x = lax.broadcasted_iota(jnp.int32, (1, qx_p.shape[1]), 1)
            head_a_x = (lane_x & (PAIR_W - 1)) < D_HEAD
            zx = jnp.zeros_like(qx_p)
            g_a = g_a + _dot_nt(jnp.where(head_a_x, qx_p, zx), kx_p)
            g_b = g_b + _dot_nt(jnp.where(head_a_x, zx, qx_p), kx_p)
        o_g = jnp.where(head_a, _dot(g_a.astype(BF16), vg), _dot(g_b.astype(BF16), vg))
        s_old = sst[0, p] if carry else pair_state(hg0_ref, sg, p)
        o_g = o_g + _dot(qin_buf[rows, pc], s_old.astype(BF16))
        ut = _dot_tn(kout_buf[rows, pc], vg)
        b_last = proj[r0 + LS - 1:r0 + LS, C_FG + p * PAIR_W:C_FG + (p + 1) * PAIR_W]
        s_decay = jnp.exp(jnp.broadcast_to(b_last, (PAIR_W, PAIR_W)).T)
        s_new = s_old * s_decay + jnp.where(bd_mask, ut, 0.0)
        if pipelined:
            s_new = jnp.where(keep, s_new, s_old)
        gc = slice(GROUP_W + p * PAIR_W, GROUP_W + (p + 1) * PAIR_W)
        obuf[rows, gc] = o_g

        if carry:
            rst[0, p] = r_new
            sst[0, p] = s_new
        for out_ref, new in ((retout_ref, r_new), (hgout_ref, s_new)):
            out_ref[st, 2 * p] = new[:D_HEAD, :D_HEAD]
            out_ref[st, 2 * p + 1] = new[D_HEAD:, D_HEAD:]

    units = [(sg, p) for sg in range(NSEG) for p in range(N_PAIRS)]
    per_chunk = len(units) // (TB // CHUNK)
    for c in range(TB // CHUNK):
        for sg, p in units[c * per_chunk:(c + 1) * per_chunk]:
            pair_unit(sg, p)
        diag_chunk(c)

    xprev[...] = x
    if not pipelined:
        _mixer_back(proj, obuf, dbuf, xprev, hgn_ref, ones_ref, wout_bf, gffn_ref, wr_ref, br_ref, stri_ref,
                    x1_ref, h2s_ref, gs_ref, rinfo_ref, info_ref)


def _mixer_back(proj, obuf, dbuf, xprev, hgn_ref, ones_ref, wout_bf, gffn_ref, wr_ref, br_ref, stri_ref,
                x1_ref, h2s_ref, gs_ref, rinfo_ref, info_ref):
    o_r = obuf[:, 0:GROUP_W]
    mu = _group_sum(o_r, ones_ref) * (1.0 / D_HEAD)
    dlt = o_r - mu
    var = _group_sum(dlt * dlt, ones_ref) * (1.0 / D_HEAD)
    y_r = dlt * lax.rsqrt(var + RMS_EPS) * _silu(proj[:, C_GR:C_GR + GROUP_W])
    o_g = obuf[:, GROUP_W:2 * GROUP_W] + dbuf[...]
    ms = _group_sum(o_g * o_g, ones_ref) * (1.0 / D_HEAD)
    y_g = o_g * lax.rsqrt(ms + RMS_EPS) * hgn_ref[...] * _silu(proj[:, C_GG:C_GG + GROUP_W])
    mix = jnp.concatenate([y_r, y_g], axis=1).astype(BF16)
    x1 = xprev[...] + _dot(mix, wout_bf[...])
    x1_ref[...] = x1

    h2 = x1 * lax.rsqrt(jnp.mean(x1 * x1, axis=-1, keepdims=True) + RMS_EPS) * gffn_ref[...]
    h_hi, h_lo = _split(h2, 2)
    w_hi, w_lo = _split(wr_ref[...], 2)
    hi_terms = _dot(h_hi, jnp.concatenate([w_hi, w_lo], axis=1))
    logits = hi_terms[:, :ROUTER_W] + hi_terms[:, ROUTER_W:] + _dot(h_lo, w_hi) + br_ref[...]
    lane = lax.broadcasted_iota(jnp.int32, (1, ROUTER_W), 1).astype(F32)
    neg = -jnp.inf
    no_lane = float(ROUTER_W)
    gl = jnp.where(lane < N_GROUPS, logits, neg)
    gmax = jnp.max(gl, axis=-1, keepdims=True)
    g_idx = jnp.min(jnp.where(gl == gmax, lane, no_lane), axis=-1, keepdims=True)
    prob_g = 1.0 / jnp.sum(jnp.exp(gl - gmax), axis=-1, keepdims=True)
    e_lo = N_GROUPS + EXPERTS_PER_GROUP * g_idx
    el = jnp.where((lane >= e_lo) & (lane < e_lo + EXPERTS_PER_GROUP), logits, neg)
    v1 = jnp.max(el, axis=-1, keepdims=True)
    i1 = jnp.min(jnp.where(el == v1, lane, no_lane), axis=-1, keepdims=True)
    el2 = jnp.where(lane == i1, neg, el)
    v2 = jnp.max(el2, axis=-1, keepdims=True)
    i2 = jnp.min(jnp.where(el2 == v2, lane, no_lane), axis=-1, keepdims=True)
    t = jnp.exp(v2 - v1)
    p1 = 1.0 / (1.0 + t)
    p2 = t * p1
    gate = jnp.where(lane == i1, prob_g * p1, 0.0) + jnp.where(lane == i2, prob_g * p2, 0.0)

    onehot = jnp.where(lane == g_idx, 1.0, 0.0)
    before = _dot(stri_ref[...], onehot.astype(BF16))
    count = jnp.sum(onehot, axis=0, keepdims=True)
    padded = jnp.floor((count + (SLAB - 1.0)) * (1.0 / SLAB)) * SLAB
    padded8 = jnp.broadcast_to(padded, (8, ROUTER_W))
    seg_start = sum(jnp.where(lane >= k, pltpu.roll(padded8, k, 1), 0.0) for k in range(1, N_GROUPS))[0:1]
    rank = jnp.sum(onehot * (seg_start + before), axis=-1, keepdims=True)
    col = lax.broadcasted_iota(jnp.int32, (1, SORT_W), 1).astype(F32)
    place = jnp.where(col == rank, 1.0, 0.0).astype(BF16)
    h2s_ref[...] = _dot_tn(place, h2.astype(BF16))[:SORT_ROWS].astype(BF16)
    gs_ref[...] = sum(_dot_tn(place, part) for part in _split(gate, 3))[:SORT_ROWS]
    rinfo_ref[...] = jnp.where(lane == 0, rank, 0.0)
    slab_lo = (lane * SLAB).astype(F32)
    slab_group = jnp.full((1, ROUTER_W), -1.0, F32)
    for g in range(N_GROUPS):
        s_g = jnp.sum(jnp.where(lane == g, seg_start, 0.0), axis=-1, keepdims=True)
        n_g = jnp.sum(jnp.where(lane == g, padded, 0.0), axis=-1, keepdims=True)
        slab_group = jnp.where((slab_lo >= s_g) & (slab_lo < s_g + n_g), float(g), slab_group)
    info_ref[0] = jnp.broadcast_to(slab_group.astype(jnp.int32), (8, ROUTER_W))


def _slab_gather(table_ref, first, n_slabs, srcs, bufs, sems, slot, *, wait):
    n_rows = n_slabs * SLAB
    for src, buf, sem in zip(srcs, bufs, sems):
        if wait:
            pltpu.make_async_copy(src.at[pl.ds(0, n_rows)], buf.at[slot, pl.ds(0, n_rows)], sem.at[slot]).wait()
            continue
        for j in range(n_slabs):
            row = pl.multiple_of(jnp.maximum(table_ref[first + j], 0) * SLAB, SLAB)
            pltpu.make_async_copy(src.at[pl.ds(row, SLAB)], buf.at[slot, pl.ds(j * SLAB, SLAB)],
                                  sem.at[slot]).start()


def _moe_kernel(src_ref, tg_ref, nt_ref, h2s_hbm, gs_hbm, wg_ref, wu_ref, wd_ref, ys_ref, hbuf, gbuf, sem_h, sem_g):
    i = pl.program_id(0)
    n_tiles = nt_ref[0]
    slot = lax.rem(i, 2)
    gather = functools.partial(_slab_gather, src_ref, srcs=(h2s_hbm, gs_hbm), bufs=(hbuf, gbuf),
                               sems=(sem_h, sem_g), n_slabs=TILE_SLABS)

    @pl.when(i == 0)
    def _():
        gather(first=0, slot=0, wait=False)

    @pl.when(i < n_tiles)
    def _():
        gather(first=i * TILE_SLABS, slot=slot, wait=True)

        @pl.when(i + 1 < n_tiles)
        def _():
            gather(first=(i + 1) * TILE_SLABS, slot=1 - slot, wait=False)

        hb = hbuf[slot]
        gates = gbuf[slot]
        lane = lax.broadcasted_iota(jnp.int32, (1, ROUTER_W), 1)
        first_lane = N_GROUPS + tg_ref[i] * EXPERTS_PER_GROUP
        acc = jnp.zeros((TM, D_MODEL), F32)
        for e in range(EXPERTS_PER_GROUP):
            he = (_silu(_dot(hb, wg_ref[e])) * _dot(hb, wu_ref[e])).astype(BF16)
            gcol = jnp.sum(jnp.where(lane == first_lane + e, gates, 0.0), axis=-1, keepdims=True)
            acc = acc + gcol * _dot(he, wd_ref[e])
        ys_ref[...] = acc.astype(BF16)

    @pl.when(i >= n_tiles)
    def _():
        ys_ref[...] = jnp.zeros_like(ys_ref)


def _final_kernel(steps_p, dest_ref, x1_ref, rinfo_ref, ys_hbm, gfin_ref, yp_ref, ysmp_ref, ybuf, sem):
    i = pl.program_id(0)
    slot = lax.rem(i, 2)
    n_slabs = FINAL_BLOCKS * SLABS_PER_BLOCK
    gather = functools.partial(_slab_gather, dest_ref, srcs=(ys_hbm,), bufs=(ybuf,), sems=(sem,), n_slabs=n_slabs)

    @pl.when(i == 0)
    def _():
        gather(first=0, slot=0, wait=False)

    gather(first=i * n_slabs, slot=slot, wait=True)

    @pl.when(i + 1 < pl.num_programs(0))
    def _():
        gather(first=(i + 1) * n_slabs, slot=1 - slot, wait=False)

    yb = ybuf[slot]
    col = lax.broadcasted_iota(jnp.int32, (1, FINAL_BLOCKS * SORT_ROWS), 1).astype(F32)
    moe = []
    for b in range(FINAL_BLOCKS):
        rank = rinfo_ref[b * TB:(b + 1) * TB, 0:1] + float(b * SORT_ROWS)
        moe.append(_dot(jnp.where(col == rank, 1.0, 0.0).astype(BF16), yb))
    xo = x1_ref[...] + jnp.concatenate(moe, axis=0)
    y = xo * lax.rsqrt(jnp.mean(xo * xo, axis=-1, keepdims=True) + RMS_EPS) * gfin_ref[...]

    @pl.when(i < steps_p)
    def _():
        yp_ref[...] = y

    @pl.when(i >= steps_p)
    def _():
        ysmp_ref[...] = y


def _const_spec(shape, pipeline_mode=None):
    nd = len(shape)
    return pl.BlockSpec(shape, lambda i: (0,) * nd, pipeline_mode=pipeline_mode)


def _mixer_call(x2d, ret0, hg0, params, *, seg_len, carry, pipelined, block_offset, total_blocks, shared=(),
                expert_weights=()):
    (gmix, win, lbl, hgn, wout, gffn, wr, br) = params
    T = x2d.shape[0]
    LS = seg_len
    n_seg = TB // LS
    NCH = LS // CHUNK
    n_states = ret0.shape[0]
    n_own = T // TB
    n_fill = 0 if shared else total_blocks - n_own - block_offset
    lag = 1 if pipelined else 0
    grid = n_own + lag + n_fill

    f32 = np.float32
    lg = np.log1p(-(f32(2.0) ** (f32(-5.0) - np.arange(N_HEADS, dtype=f32)))).astype(f32)
    tt = np.arange(LS, dtype=f32)
    ch = np.arange(LS) // CHUNK
    dret = np.exp(np.abs(tt[:, None] - tt[None, :])[None] * lg[:, None, None]).astype(f32)
    dret = np.where((ch[None, :] <= ch[:, None])[None], dret, f32(0.0))
    lg_lane = np.repeat(lg, D_HEAD)[None, :]
    xi = np.exp((tt[:, None] + f32(1.0)) * lg_lane).astype(f32)
    zeta = np.exp((f32(LS - 1.0) - tt)[:, None] * lg_lane).astype(f32)
    gls = np.exp(f32(LS) * lg_lane).astype(f32)
    half = D_HEAD // 2
    inv_freq = ROPE_BASE ** (-jnp.arange(half, dtype=F32) / half)
    invf = jnp.tile(inv_freq, PAIR_W // half)[None, :]
    sgn = np.where((np.arange(PAIR_W) % D_HEAD) < half, f32(-1.0), f32(1.0))[None, :]
    tri = jnp.asarray(np.tril(np.ones((LS, LS), np.float32)), BF16)
    lane_head = np.arange(256) // D_HEAD
    ones_bd = jnp.asarray((lane_head[:, None] == lane_head[None, :]).astype(np.float32), BF16)
    cmask = jnp.asarray((ch[:, None] == ch[None, :]).astype(np.float32))

    if carry:
        state_spec = pl.BlockSpec((1, N_HEADS, D_HEAD, D_HEAD), lambda i: (0, 0, 0, 0))
    else:
        state_spec = pl.BlockSpec((n_seg, N_HEADS, D_HEAD, D_HEAD), lambda i: (i, 0, 0, 0))
    single = pl.Buffered(1)
    stri = jnp.asarray(np.tril(np.ones((TB, TB), np.float32), -1), BF16)
    in_row_spec = pl.BlockSpec((TB, D_MODEL), lambda i: (jnp.minimum(i, n_own - 1), 0))
    out_block = lambda i: jnp.maximum(i - lag, 0) + block_offset
    out_row_spec = lambda rows, w: pl.BlockSpec((rows, w), lambda i: (out_block(i), 0))
    nw = (N_SUB - 1) * N_PAIRS * PAIR_W
    nx = max(NCH - 1, 1) * N_PAIRS * PAIR_W

    cast_w = bool(expert_weights)
    w2d = [w.reshape(-1, w.shape[-1]) for w in expert_weights]
    w_specs = [pl.BlockSpec((w.shape[0] // n_own, w.shape[1]), lambda i: (jnp.minimum(i, n_own - 1), 0)) for w in w2d]
    w_shapes = [jax.ShapeDtypeStruct(w.shape, BF16) for w in w2d]
    n_in = N_MIXER_INPUTS + len(w2d)

    kern = functools.partial(_mixer_kernel, (LS, n_seg, carry, pipelined, n_own, n_fill, len(shared), cast_w))
    return pl.pallas_call(
        kern,
        grid=(grid,),
        in_specs=[
            in_row_spec,
            _const_spec((1, D_MODEL)), _const_spec(win.shape, single), _const_spec((1, PAIR_W)),
            _const_spec((1, PAIR_W)), _const_spec(lbl.shape), _const_spec((1, GROUP_W)),
            _const_spec(wout.shape, single), _const_spec((1, D_MODEL)),
            _const_spec(wr.shape), _const_spec((1, ROUTER_W)),
            _const_spec((LS, LS)), _const_spec((N_HEADS, LS, LS)), _const_spec((LS, GROUP_W)),
            _const_spec((LS, GROUP_W)), _const_spec((1, GROUP_W)), _const_spec((256, 256)), _const_spec((LS, LS)),
            _const_spec((TB, TB)),
            state_spec, state_spec,
        ] + w_specs + [pl.BlockSpec(memory_space=pl.ANY)] * len(shared),
        out_specs=[out_row_spec(TB, D_MODEL), out_row_spec(SORT_ROWS, D_MODEL), out_row_spec(SORT_ROWS, ROUTER_W),
                   out_row_spec(TB, ROUTER_W),
                   pl.BlockSpec((1, 8, ROUTER_W), lambda i: (out_block(i), 0, 0)),
                   state_spec, state_spec] + w_specs,
        out_shape=[
            jax.ShapeDtypeStruct((total_blocks * TB, D_MODEL), F32),
            jax.ShapeDtypeStruct((total_blocks * SORT_ROWS, D_MODEL), BF16),
            jax.ShapeDtypeStruct((total_blocks * SORT_ROWS, ROUTER_W), F32),
            jax.ShapeDtypeStruct((total_blocks * TB, ROUTER_W), F32),
            jax.ShapeDtypeStruct((total_blocks, 8, ROUTER_W), jnp.int32),
            jax.ShapeDtypeStruct((n_states, N_HEADS, D_HEAD, D_HEAD), F32),
            jax.ShapeDtypeStruct((n_states, N_HEADS, D_HEAD, D_HEAD), F32),
        ] + w_shapes,
        input_output_aliases={n_in + k: k for k in range(len(shared))},
        scratch_shapes=[
            *[pltpu.VMEM((TB, GROUP_W), F32) for _ in range(8)],
            pltpu.VMEM((TB, GROUP_W), F32),
            pltpu.VMEM((TB, 2 * GROUP_W), F32),
            pltpu.VMEM((TB, GROUP_W), F32),
            pltpu.VMEM((TB, D_MODEL), F32),
            pltpu.VMEM((TB, nw), BF16), pltpu.VMEM((TB, nw), BF16),
            pltpu.VMEM((TB, nx), BF16), pltpu.VMEM((TB, nx), BF16),
            pltpu.VMEM((TB, GROUP_W), BF16), pltpu.VMEM((TB, GROUP_W), BF16),
            pltpu.VMEM((N_SUB * (SUB // 2) * (SUB + SUB // 2), GROUP_W), BF16),
            pltpu.VMEM((TB, PAIR_W), F32), pltpu.VMEM((TB, PAIR_W), F32),
            pltpu.VMEM(win.shape, BF16), pltpu.VMEM(wout.shape, BF16),
            pltpu.VMEM((1, N_PAIRS, PAIR_W, PAIR_W), F32),
            pltpu.VMEM((1, N_PAIRS, PAIR_W, PAIR_W), F32),
        ],
        compiler_params=pltpu.CompilerParams(
            dimension_semantics=("arbitrary",), vmem_limit_bytes=V7X_VMEM_LIMIT),
    )(x2d, gmix, win, invf, sgn, lbl, hgn, wout, gffn, wr, br, tri, dret, xi, zeta, gls, ones_bd, cmask, stri,
      ret0, hg0, *w2d, *shared)


def _dispatch_tables(info, n_tiles_max):
    slab_group = info[:, 0, :SLABS_PER_BLOCK].reshape(-1)
    n_slabs = slab_group.shape[0]
    valid = slab_group >= 0
    onehot = (slab_group[:, None] == jnp.arange(N_GROUPS, dtype=jnp.int32)[None, :]).astype(jnp.int32)
    within = jnp.cumsum(onehot, axis=0) - onehot
    tiles = (jnp.sum(onehot, axis=0) + TILE_SLABS - 1) // TILE_SLABS
    tile_end = jnp.cumsum(tiles)
    g = jnp.clip(slab_group, 0, N_GROUPS - 1)
    dest = (tile_end - tiles)[g] * TILE_SLABS + jnp.take_along_axis(within, g[:, None], axis=1)[:, 0]
    dest = jnp.where(valid, dest, -1).astype(jnp.int32)
    n_slots = n_tiles_max * TILE_SLABS
    src = jnp.full((n_slots,), -1, jnp.int32).at[jnp.where(valid, dest, n_slots)].set(
        jnp.arange(n_slabs, dtype=jnp.int32), mode="drop")
    tile_group = jnp.sum(jnp.arange(n_tiles_max, dtype=jnp.int32)[:, None] >= tile_end[None, :], axis=1)
    tile_group = jnp.minimum(tile_group, N_GROUPS - 1).astype(jnp.int32)
    return src, dest, tile_group, tile_end[-1:].astype(jnp.int32)


def _moe_call(src, tile_group, n_tiles, h2s, gs, w_gate, w_up, w_down):
    n_tiles_max = tile_group.shape[0]
    group_of_tile = lambda i, s, tg, nt: (tg[i], 0, 0)
    return pl.pallas_call(
        _moe_kernel,
        grid_spec=pltpu.PrefetchScalarGridSpec(
            num_scalar_prefetch=3,
            grid=(n_tiles_max,),
            in_specs=[
                pl.BlockSpec(memory_space=pl.ANY), pl.BlockSpec(memory_space=pl.ANY),
                pl.BlockSpec((EXPERTS_PER_GROUP, D_MODEL, D_EXPERT), group_of_tile),
                pl.BlockSpec((EXPERTS_PER_GROUP, D_MODEL, D_EXPERT), group_of_tile),
                pl.BlockSpec((EXPERTS_PER_GROUP, D_EXPERT, D_MODEL), group_of_tile),
            ],
            out_specs=pl.BlockSpec((TM, D_MODEL), lambda i, s, tg, nt: (i, 0)),
            scratch_shapes=[
                pltpu.VMEM((2, TM, D_MODEL), BF16), pltpu.VMEM((2, TM, ROUTER_W), F32),
                pltpu.SemaphoreType.DMA((2,)), pltpu.SemaphoreType.DMA((2,)),
            ],
        ),
        out_shape=jax.ShapeDtypeStruct((n_tiles_max * TM, D_MODEL), BF16),
        compiler_params=pltpu.CompilerParams(
            dimension_semantics=("arbitrary",), vmem_limit_bytes=V7X_VMEM_LIMIT),
    )(src, tile_group, n_tiles, h2s, gs, w_gate, w_up, w_down)


def _final_call(dest, x1, rinfo, ys, gfin, blocks_p):
    n_blocks = x1.shape[0] // TB
    assert blocks_p % FINAL_BLOCKS == 0 and n_blocks % FINAL_BLOCKS == 0
    steps_p = blocks_p // FINAL_BLOCKS
    rows = FINAL_BLOCKS * TB
    return pl.pallas_call(
        functools.partial(_final_kernel, steps_p),
        grid_spec=pltpu.PrefetchScalarGridSpec(
            num_scalar_prefetch=1,
            grid=(n_blocks // FINAL_BLOCKS,),
            in_specs=[
                pl.BlockSpec((rows, D_MODEL), lambda i, d: (i, 0)),
                pl.BlockSpec((rows, ROUTER_W), lambda i, d: (i, 0)),
                pl.BlockSpec(memory_space=pl.ANY),
                pl.BlockSpec((1, D_MODEL), lambda i, d: (0, 0)),
            ],
            out_specs=[
                pl.BlockSpec((rows, D_MODEL), lambda i, d: (jnp.minimum(i, steps_p - 1), 0)),
                pl.BlockSpec((rows, D_MODEL), lambda i, d: (jnp.maximum(i - steps_p, 0), 0)),
            ],
            scratch_shapes=[pltpu.VMEM((2, FINAL_BLOCKS * SORT_ROWS, D_MODEL), BF16), pltpu.SemaphoreType.DMA((2,))],
        ),
        out_shape=[jax.ShapeDtypeStruct((blocks_p * TB, D_MODEL), F32),
                   jax.ShapeDtypeStruct(((n_blocks - blocks_p) * TB, D_MODEL), F32)],
        compiler_params=pltpu.CompilerParams(
            dimension_semantics=("arbitrary",), vmem_limit_bytes=V7X_VMEM_LIMIT),
    )(dest, x1, rinfo, ys, gfin)


def kernel(x_prompt, x_sample, state_ret, state_hgrn, norm_mix_g, w_in, hgrn_lb_logits, hgrn_norm_g, w_out,
           norm_ffn_g, w_router_group, b_router_group, w_router_expert, b_router_expert, w_exp_gate, w_exp_up,
           w_exp_down, norm_final_g):
    depth = w_in.shape[0]
    assert depth == 1 and hgrn_lb_logits.shape[0] == 2, "single-layer configuration only"
    bp, seq, d = x_prompt.shape
    db, dec_len, _ = x_sample.shape
    assert bp == 1 and d == D_MODEL and dec_len == CHUNK and seq % 256 == 0 and db % 4 == 0

    pad = ROUTER_W - N_GROUPS - N_EXPERTS
    wr = jnp.concatenate([w_router_group[0], w_router_expert[0], jnp.zeros((D_MODEL, pad), F32)], axis=1)
    br = jnp.concatenate([b_router_group[0], b_router_expert[0], jnp.zeros((pad,), F32)])[None, :]
    params = (norm_mix_g[0][None, :], w_in[0], hgrn_lb_logits,
              jnp.tile(hgrn_norm_g[0], N_HEADS)[None, :], w_out[0], norm_ffn_g[0][None, :], wr, br)

    zeros_state = jnp.zeros((1, N_HEADS, D_HEAD, D_HEAD), F32)
    blocks_p = seq // TB
    blocks_s = db * dec_len // TB
    n_blocks = blocks_p + blocks_s
    *shared, ret_p, hg_p, wg_bf, wu_bf, wd_bf = _mixer_call(
        x_prompt.reshape(seq, d), zeros_state, zeros_state, params, seg_len=TB, carry=True, pipelined=True,
        block_offset=0, total_blocks=n_blocks, expert_weights=(w_exp_gate[0], w_exp_up[0], w_exp_down[0]))
    x1, h2s, gs, rinfo, info, ret_s, hg_s = _mixer_call(
        x_sample.reshape(db * dec_len, d), state_ret[0], state_hgrn[0], params, seg_len=CHUNK, carry=False,
        pipelined=False,
        block_offset=blocks_p, total_blocks=n_blocks, shared=tuple(shared))

    n_tiles_max = -(-n_blocks * SLABS_PER_BLOCK // TILE_SLABS) + N_GROUPS
    src, dest, tile_group, n_tiles = _dispatch_tables(info, n_tiles_max)
    ys = _moe_call(src, tile_group, n_tiles, h2s, gs, wg_bf.reshape(w_exp_gate[0].shape),
                   wu_bf.reshape(w_exp_up[0].shape), wd_bf.reshape(w_exp_down[0].shape))
    y_p, y_s = _final_call(dest, x1, rinfo, ys, norm_final_g[None, :], blocks_p)
    return (y_p.reshape(bp, seq, d), y_s.reshape(db, dec_len, d), ret_p[None], hg_p[None], ret_s[None], hg_s[None])
```

```python
import functools

import numpy as np
import jax
import jax.numpy as jnp
from jax import lax
from jax.experimental import pallas as pl
from jax.experimental.pallas import tpu as pltpu

F32 = jnp.float32
BF16 = jnp.bfloat16

D_MODEL = 1024
N_HEADS = 8
D_HEAD = 64
GROUP_W = N_HEADS * D_HEAD
N_PAIRS = N_HEADS // 2
PAIR_W = 2 * D_HEAD
CHUNK = 64
SUB = 16
N_SUB = CHUNK // SUB
PAST_LEN = 2048
ROPE_BASE = 10000.0
RMS_EPS = 1e-6
LOG2E = 1.4426950408889634
N_GROUPS = 4
EXPERTS_PER_GROUP = 8
N_EXPERTS = N_GROUPS * EXPERTS_PER_GROUP
D_EXPERT = 256
ROUTER_W = 128
V7X_VMEM_LIMIT = 60 * 1024 * 1024
TB = 256
SLAB = 16
SLABS_PER_BLOCK = TB // SLAB + N_GROUPS
SORT_ROWS = SLABS_PER_BLOCK * SLAB
SORT_W = 384
TILE_SLABS = 32
TM = TILE_SLABS * SLAB
FINAL_BLOCKS = 4

C_QR, C_KR, C_VR, C_GR, C_QG, C_FG, C_VG, C_GG = (i * GROUP_W for i in range(8))


def _dot(a, b):
    return jnp.dot(a, b, preferred_element_type=F32)


def _dot_nt(a, b):
    return lax.dot_general(a, b, (((1,), (1,)), ((), ())), preferred_element_type=F32)


def _dot_tn(a, b):
    return lax.dot_general(a, b, (((0,), (0,)), ((), ())), preferred_element_type=F32)


def _split(x, n):
    parts = []
    for _ in range(n):
        p = x.astype(BF16)
        parts.append(p)
        x = x - p.astype(F32)
    return parts


def _group_sum(x, ones_ref):
    xb = x.astype(BF16)
    return jnp.concatenate([_dot(xb[:, c * 256:(c + 1) * 256], ones_ref[...]) for c in range(2)], axis=1)


class _ColumnGroups:
    def __init__(self, bufs):
        self.bufs = bufs

    def _locate(self, idx):
        rows, cols = idx
        g = cols.start // GROUP_W
        assert (cols.stop - 1) // GROUP_W == g
        return self.bufs[g], (rows, slice(cols.start - g * GROUP_W, cols.stop - g * GROUP_W))

    def __getitem__(self, idx):
        buf, at = self._locate(idx)
        return buf[at]

    def __setitem__(self, idx, value):
        buf, at = self._locate(idx)
        buf[at] = value


def _silu(x):
    return x * jax.nn.sigmoid(x)


N_MIXER_INPUTS = 21


def _mixer_kernel(cfg, *refs):
    pipelined, n_own, n_fill, n_alias, cast_w = cfg[-5:]
    n_body = n_own + (1 if pipelined else 0)
    n_in = N_MIXER_INPUTS + (3 if cast_w else 0)
    i = pl.program_id(0)

    @pl.when(i < n_body)
    def _():
        _mixer_body(cfg, *refs)

    if n_fill:
        @pl.when(i >= n_body)
        def _():
            for out_ref in refs[n_in + n_alias:n_in + n_alias + 5]:
                out_ref[...] = jnp.zeros_like(out_ref)


def _mixer_body(cfg, x_ref, gmix_ref, win_ref, invf_ref, sgn_ref, lbl_ref, hgn_ref, wout_ref, gffn_ref,
                wr_ref, br_ref, tri_ref, dret_ref, xi_ref, zeta_ref, gls_ref, ones_ref, cmask_ref, stri_ref,
                ret0_ref, hg0_ref, *rest):
    LS, NSEG, carry, pipelined, n_own, _, n_alias, cast_w = cfg
    w_chunks = ()
    if cast_w:
        w_chunks = tuple(zip(rest[:3], rest[3 + n_alias + 7:3 + n_alias + 10]))
        rest = rest[3:3 + n_alias + 7] + rest[3 + n_alias + 10:]
    (x1_ref, h2s_ref, gs_ref, rinfo_ref, info_ref, retout_ref, hgout_ref,
     *proj_bufs, kbuf, obuf, dbuf, xprev, qw, kw, qx, kx, qin_buf, kout_buf, pbuf, cos_t, sin_t, win_bf, wout_bf,
     rst, sst) = rest[n_alias:]
    proj = _ColumnGroups(proj_bufs)
    NCH = LS // CHUNK
    i = pl.program_id(0)
    lane128 = lax.broadcasted_iota(jnp.int32, (1, PAIR_W), 1)
    head_a = lane128 < D_HEAD
    bd_mask = (lax.broadcasted_iota(jnp.int32, (PAIR_W, PAIR_W), 0) < D_HEAD) == head_a

    def pair_state(ref, sg, p):
        z = jnp.zeros((D_HEAD, D_HEAD), F32)
        return jnp.concatenate([jnp.concatenate([ref[sg, 2 * p], z], axis=1),
                                jnp.concatenate([z, ref[sg, 2 * p + 1]], axis=1)], axis=0)

    @pl.when(i == 0)
    def _():
        for c in range(0, 8 * GROUP_W, GROUP_W):
            win_bf[:, c:c + GROUP_W] = win_ref[:, c:c + GROUP_W].astype(BF16)
        wout_bf[...] = wout_ref[...].astype(BF16)
        if carry:
            for p in range(N_PAIRS):
                rst[0, p] = pair_state(ret0_ref, 0, p)
                sst[0, p] = pair_state(hg0_ref, 0, p)
        row = lax.broadcasted_iota(jnp.int32, (TB, 1), 0)
        ang_row = (row & (LS - 1)).astype(F32) * invf_ref[...]
        cos_t[...] = jnp.cos(ang_row)
        sin_t[...] = jnp.sin(ang_row)
        if pipelined:
            for buf in (obuf, dbuf, xprev, proj_bufs[C_GR // GROUP_W], proj_bufs[C_GG // GROUP_W]):
                buf[...] = jnp.zeros_like(buf)

    if pipelined:
        _mixer_back(proj, obuf, dbuf, xprev, hgn_ref, ones_ref, wout_bf, gffn_ref, wr_ref, br_ref, stri_ref,
                    x1_ref, h2s_ref, gs_ref, rinfo_ref, info_ref)
    keep = i < n_own

    for w_ref, w_bf_ref in w_chunks:
        w_bf_ref[...] = w_ref[...].astype(BF16)

    x = x_ref[...]
    h = x * lax.rsqrt(jnp.mean(x * x, axis=-1, keepdims=True) + RMS_EPS) * gmix_ref[...]
    hb = h.astype(BF16)

    def project(c0, c1):
        for c in range(c0, c1, GROUP_W):
            proj[:, c:c + GROUP_W] = _dot(hb, win_bf[:, c:c + GROUP_W])

    project(C_QG, C_GG)
    project(C_QR, C_VR)

    la = lbl_ref[0:1, :]
    lb_ = lbl_ref[1:2, :]
    lmax = jnp.maximum(la, lb_)
    ea = jnp.exp(la - lmax)
    lbv = ea / (ea + jnp.exp(lb_ - lmax))
    qg = proj[:, C_QG:C_QG + GROUP_W]
    proj[:, C_QG:C_QG + GROUP_W] = _silu(qg)
    f = lbv + (1.0 - lbv) * jax.nn.sigmoid(proj[:, C_FG:C_FG + GROUP_W])
    kbuf[...] = 1.0 - f
    logf = jnp.log(f)
    for sg in range(NSEG):
        rows = slice(sg * LS, (sg + 1) * LS)
        parts = _split(logf[rows], 3)
        proj[rows, C_FG:C_FG + GROUP_W] = sum(_dot(tri_ref[...], p) for p in parts)

    project(C_VR, C_QG)
    project(C_GG, C_GG + GROUP_W)

    start = jnp.full((8, PAIR_W), i * TB if carry else PAST_LEN, jnp.int32).astype(F32)
    ang0 = (start * invf_ref[...])[0:1]
    c0 = jnp.cos(ang0)
    s0 = jnp.sin(ang0)
    cos = cos_t[...] * c0 - sin_t[...] * s0
    sin = (sin_t[...] * c0 + cos_t[...] * s0) * sgn_ref[...]
    first_half = (lane128 & (D_HEAD - 1)) < (D_HEAD // 2)
    for blk in range(2 * N_PAIRS):
        cols = slice(blk * PAIR_W, (blk + 1) * PAIR_W)
        xx = proj[:, cols]
        partner = jnp.where(first_half, pltpu.roll(xx, PAIR_W - D_HEAD // 2, 1), pltpu.roll(xx, D_HEAD // 2, 1))
        r = xx * cos + partner * sin
        if blk < N_PAIRS:
            r = r * (D_HEAD ** -0.5)
        proj[:, cols] = r

    HALF = SUB // 2
    riota = lax.broadcasted_iota(jnp.int32, (HALF, 1), 0)
    P_ROWS = HALF * SUB + HALF * HALF

    def diag_chunk(c):
        for sb in range(N_SUB):
            rs = c * CHUNK + sb * SUB
            q16 = proj[rs:rs + SUB, C_QG:C_QG + GROUP_W]
            b16 = proj[rs:rs + SUB, C_FG:C_FG + GROUP_W] * LOG2E
            k16 = kbuf[rs:rs + SUB, :]
            base = sb * P_ROWS
            upper = []
            for s in range(SUB):
                ks = k16[s:s + 1]
                bs = b16[s:s + 1]
                hi = q16[HALF:] * ks * jnp.exp2(b16[HALF:] - bs)
                if s < HALF:
                    lo = q16[:HALF] * ks * jnp.exp2(b16[:HALF] - bs)
                    lo = jnp.where(riota >= s, lo, 0.0)
                    pbuf[base + s * SUB:base + (s + 1) * SUB, :] = jnp.concatenate([lo, hi], axis=0).astype(BF16)
                else:
                    upper.append(jnp.where(riota >= s - HALF, hi, 0.0))
                    if len(upper) == 2:
                        at = base + HALF * SUB + (s - HALF - 1) * HALF
                        pbuf[at:at + SUB, :] = jnp.concatenate(upper, axis=0).astype(BF16)
                        upper = []
        sc = [_dot(pbuf[:, hh * 256:(hh + 1) * 256], ones_ref[...]) for hh in range(2)]
        for sb in range(N_SUB):
            rs = c * CHUNK + sb * SUB
            v16 = proj[rs:rs + SUB, C_VG:C_VG + GROUP_W]
            base = sb * P_ROWS
            acc_lo = [jnp.zeros((HALF, 256), F32) for _ in range(2)]
            acc_hi = [jnp.zeros((HALF, 256), F32) for _ in range(2)]
            for s in range(SUB):
                for hh in range(2):
                    vs = v16[s:s + 1, hh * 256:(hh + 1) * 256]
                    if s < HALF:
                        at = base + s * SUB
                        acc_lo[hh] = acc_lo[hh] + sc[hh][at:at + HALF] * vs
                        acc_hi[hh] = acc_hi[hh] + sc[hh][at + HALF:at + SUB] * vs
                    else:
                        at = base + HALF * SUB + (s - HALF) * HALF
                        acc_hi[hh] = acc_hi[hh] + sc[hh][at:at + HALF] * vs
            dbuf[rs:rs + HALF, :] = jnp.concatenate(acc_lo, axis=1)
            dbuf[rs + HALF:rs + SUB, :] = jnp.concatenate(acc_hi, axis=1)

    for sg in range(NSEG):
        r0 = sg * LS
        rows = slice(r0, r0 + LS)
        st = 0 if carry else sg
        qg_s = proj[rows, C_QG:C_QG + GROUP_W]
        kg_s = kbuf[rows, :]
        b_s = proj[rows, C_FG:C_FG + GROUP_W]

        def factors(q_rows, k_rows, e, n_rows, base):
            bref = b_s[e:e + 1, :]
            qt = qg_s[q_rows] * jnp.exp(b_s[q_rows] - bref)
            kt = kg_s[k_rows] * jnp.exp(bref - b_s[k_rows])
            def pad(a, lo):
                hi = n_rows - lo - a.shape[0]
                parts = [jnp.zeros((n, GROUP_W), F32) for n in (lo,) if n] + [a] + [
                    jnp.zeros((n, GROUP_W), F32) for n in (hi,) if n]
                return jnp.concatenate(parts, axis=0) if len(parts) > 1 else a

            return pad(qt, q_rows.start - base), pad(kt, k_rows.start - base)

        def store_slots(dst_q, dst_k, rows_out, slots):
            for dst, which in ((dst_q, 0), (dst_k, 1)):
                for p in range(N_PAIRS):
                    pc = slice(p * PAIR_W, (p + 1) * PAIR_W)
                    s0, s1, s2 = (slots[j][which][:, pc] for j in range(3))
                    r1 = pltpu.roll(s1, D_HEAD, 1)
                    zero = jnp.zeros_like(s0)
                    blocks = ((jnp.where(head_a, s0, r1), jnp.where(head_a, s2, zero)),
                              (jnp.where(head_a, r1, s0), jnp.where(head_a, zero, s2)))
                    for hh in range(2):
                        c0 = (2 * p + hh) * 2 * PAIR_W
                        dst[rows_out, c0:c0 + PAIR_W] = blocks[hh][0].astype(BF16)
                        dst[rows_out, c0 + PAIR_W:c0 + 2 * PAIR_W] = blocks[hh][1].astype(BF16)

        for c in range(NCH):
            lo = c * CHUNK
            slots = [factors(slice(lo + (j + 1) * SUB, lo + CHUNK), slice(lo + j * SUB, lo + (j + 1) * SUB),
                             lo + (j + 1) * SUB - 1, CHUNK, lo) for j in range(N_SUB - 1)]
            store_slots(qw, kw, slice(r0 + lo, r0 + lo + CHUNK), slots)
        if NCH > 1:
            slots = [factors(slice((c + 1) * CHUNK, LS), slice(c * CHUNK, (c + 1) * CHUNK), (c + 1) * CHUNK - 1, LS, 0)
                     for c in range(NCH - 1)]
            store_slots(qx, kx, rows, slots)

        b_last = b_s[LS - 1:LS, :]
        qin_buf[rows, :] = (qg_s * jnp.exp(b_s)).astype(BF16)
        kout_buf[rows, :] = (kg_s * jnp.exp(b_last - b_s)).astype(BF16)

    def pair_unit(sg, p):
        r0 = sg * LS
        rows = slice(r0, r0 + LS)
        st = 0 if carry else sg
        pc = slice(p * PAIR_W, (p + 1) * PAIR_W)

        qb = proj[rows, C_QR + p * PAIR_W:C_QR + (p + 1) * PAIR_W].astype(BF16)
        kr = proj[rows, C_KR + p * PAIR_W:C_KR + (p + 1) * PAIR_W]
        kb = kr.astype(BF16)
        vb = proj[rows, C_VR + p * PAIR_W:C_VR + (p + 1) * PAIR_W].astype(BF16)
        zero = jnp.zeros_like(qb)
        s_a = _dot_nt(jnp.where(head_a, qb, zero), kb) * dret_ref[2 * p]
        s_b = _dot_nt(jnp.where(head_a, zero, qb), kb) * dret_ref[2 * p + 1]
        o_r = jnp.where(head_a, _dot(s_a.astype(BF16), vb), _dot(s_b.astype(BF16), vb))
        r_old = rst[0, p] if carry else pair_state(ret0_ref, sg, p)
        o_r = o_r + _dot(qb, r_old.astype(BF16)) * xi_ref[:, pc]
        u = _dot_tn((kr * zeta_ref[:, pc]).astype(BF16), vb)
        r_new = r_old * gls_ref[:, pc] + jnp.where(bd_mask, u, 0.0)
        if pipelined:
            r_new = jnp.where(keep, r_new, r_old)
        obuf[rows, pc] = o_r

        vg = proj[rows, C_VG + p * PAIR_W:C_VG + (p + 1) * PAIR_W].astype(BF16)
        scores = []
        for h in (2 * p, 2 * p + 1):
            hc = slice(h * 2 * PAIR_W, (h + 1) * 2 * PAIR_W)
            g = _dot_nt(qw[rows, hc], kw[rows, hc])
            if NCH > 1:
                g = g * cmask_ref[...] + _dot_nt(qx[rows, hc], kx[rows, hc])
            scores.append(g.astype(BF16))
        o_g = jnp.where(head_a, _dot(scores[0], vg), _dot(scores[1], vg))
        s_old = sst[0, p] if carry else pair_state(hg0_ref, sg, p)
        o_g = o_g + _dot(qin_buf[rows, pc], s_old.astype(BF16))
        ut = _dot_tn(kout_buf[rows, pc], vg)
        b_last = proj[r0 + LS - 1:r0 + LS, C_FG + p * PAIR_W:C_FG + (p + 1) * PAIR_W]
        s_decay = jnp.exp(jnp.broadcast_to(b_last, (PAIR_W, PAIR_W)).T)
        s_new = s_old * s_decay + jnp.where(bd_mask, ut, 0.0)
        if pipelined:
            s_new = jnp.where(keep, s_new, s_old)
        gc = slice(GROUP_W + p * PAIR_W, GROUP_W + (p + 1) * PAIR_W)
        obuf[rows, gc] = o_g

        if carry:
            rst[0, p] = r_new
            sst[0, p] = s_new
        for out_ref, new in ((retout_ref, r_new), (hgout_ref, s_new)):
            out_ref[st, 2 * p] = new[:D_HEAD, :D_HEAD]
            out_ref[st, 2 * p + 1] = new[D_HEAD:, D_HEAD:]

    units = [(sg, p) for sg in range(NSEG) for p in range(N_PAIRS)]
    per_chunk = len(units) // (TB // CHUNK)
    for c in range(TB // CHUNK):
        for sg, p in units[c * per_chunk:(c + 1) * per_chunk]:
            pair_unit(sg, p)
        diag_chunk(c)

    xprev[...] = x
    if not pipelined:
        _mixer_back(proj, obuf, dbuf, xprev, hgn_ref, ones_ref, wout_bf, gffn_ref, wr_ref, br_ref, stri_ref,
                    x1_ref, h2s_ref, gs_ref, rinfo_ref, info_ref)


def _mixer_back(proj, obuf, dbuf, xprev, hgn_ref, ones_ref, wout_bf, gffn_ref, wr_ref, br_ref, stri_ref,
                x1_ref, h2s_ref, gs_ref, rinfo_ref, info_ref):
    o_r = obuf[:, 0:GROUP_W]
    mu = _group_sum(o_r, ones_ref) * (1.0 / D_HEAD)
    dlt = o_r - mu
    var = _group_sum(dlt * dlt, ones_ref) * (1.0 / D_HEAD)
    y_r = dlt * lax.rsqrt(var + RMS_EPS) * _silu(proj[:, C_GR:C_GR + GROUP_W])
    o_g = obuf[:, GROUP_W:2 * GROUP_W] + dbuf[...]
    ms = _group_sum(o_g * o_g, ones_ref) * (1.0 / D_HEAD)
    y_g = o_g * lax.rsqrt(ms + RMS_EPS) * hgn_ref[...] * _silu(proj[:, C_GG:C_GG + GROUP_W])
    mix = jnp.concatenate([y_r, y_g], axis=1).astype(BF16)
    x1 = xprev[...] + _dot(mix, wout_bf[...])
    x1_ref[...] = x1

    h2 = x1 * lax.rsqrt(jnp.mean(x1 * x1, axis=-1, keepdims=True) + RMS_EPS) * gffn_ref[...]
    h_hi, h_lo = _split(h2, 2)
    w_hi, w_lo = _split(wr_ref[...], 2)
    hi_terms = _dot(h_hi, jnp.concatenate([w_hi, w_lo], axis=1))
    logits = hi_terms[:, :ROUTER_W] + hi_terms[:, ROUTER_W:] + _dot(h_lo, w_hi) + br_ref[...]
    lane = lax.broadcasted_iota(jnp.int32, (1, ROUTER_W), 1).astype(F32)
    neg = -jnp.inf
    no_lane = float(ROUTER_W)
    gl = jnp.where(lane < N_GROUPS, logits, neg)
    gmax = jnp.max(gl, axis=-1, keepdims=True)
    g_idx = jnp.min(jnp.where(gl == gmax, lane, no_lane), axis=-1, keepdims=True)
    prob_g = 1.0 / jnp.sum(jnp.exp(gl - gmax), axis=-1, keepdims=True)
    e_lo = N_GROUPS + EXPERTS_PER_GROUP * g_idx
    el = jnp.where((lane >= e_lo) & (lane < e_lo + EXPERTS_PER_GROUP), logits, neg)
    v1 = jnp.max(el, axis=-1, keepdims=True)
    i1 = jnp.min(jnp.where(el == v1, lane, no_lane), axis=-1, keepdims=True)
    el2 = jnp.where(lane == i1, neg, el)
    v2 = jnp.max(el2, axis=-1, keepdims=True)
    i2 = jnp.min(jnp.where(el2 == v2, lane, no_lane), axis=-1, keepdims=True)
    t = jnp.exp(v2 - v1)
    p1 = 1.0 / (1.0 + t)
    p2 = t * p1
    gate = jnp.where(lane == i1, prob_g * p1, 0.0) + jnp.where(lane == i2, prob_g * p2, 0.0)

    onehot = jnp.where(lane == g_idx, 1.0, 0.0)
    before = _dot(stri_ref[...], onehot.astype(BF16))
    count = jnp.sum(onehot, axis=0, keepdims=True)
    padded = jnp.floor((count + (SLAB - 1.0)) * (1.0 / SLAB)) * SLAB
    padded8 = jnp.broadcast_to(padded, (8, ROUTER_W))
    seg_start = sum(jnp.where(lane >= k, pltpu.roll(padded8, k, 1), 0.0) for k in range(1, N_GROUPS))[0:1]
    rank = jnp.sum(onehot * (seg_start + before), axis=-1, keepdims=True)
    col = lax.broadcasted_iota(jnp.int32, (1, SORT_W), 1).astype(F32)
    place = jnp.where(col == rank, 1.0, 0.0).astype(BF16)
    h2s_ref[...] = _dot_tn(place, h2.astype(BF16))[:SORT_ROWS].astype(BF16)
    gs_ref[...] = sum(_dot_tn(place, part) for part in _split(gate, 3))[:SORT_ROWS]
    rinfo_ref[...] = jnp.where(lane == 0, rank, 0.0)
    slab_lo = (lane * SLAB).astype(F32)
    slab_group = jnp.full((1, ROUTER_W), -1.0, F32)
    for g in range(N_GROUPS):
        s_g = jnp.sum(jnp.where(lane == g, seg_start, 0.0), axis=-1, keepdims=True)
        n_g = jnp.sum(jnp.where(lane == g, padded, 0.0), axis=-1, keepdims=True)
        slab_group = jnp.where((slab_lo >= s_g) & (slab_lo < s_g + n_g), float(g), slab_group)
    info_ref[0] = jnp.broadcast_to(slab_group.astype(jnp.int32), (8, ROUTER_W))


def _slab_gather(table_ref, first, n_slabs, srcs, bufs, sems, slot, *, wait):
    n_rows = n_slabs * SLAB
    for src, buf, sem in zip(srcs, bufs, sems):
        if wait:
            pltpu.make_async_copy(src.at[pl.ds(0, n_rows)], buf.at[slot, pl.ds(0, n_rows)], sem.at[slot]).wait()
            continue
        for j in range(n_slabs):
            row = pl.multiple_of(jnp.maximum(table_ref[first + j], 0) * SLAB, SLAB)
            pltpu.make_async_copy(src.at[pl.ds(row, SLAB)], buf.at[slot, pl.ds(j * SLAB, SLAB)],
                                  sem.at[slot]).start()


def _moe_kernel(src_ref, tg_ref, nt_ref, h2s_hbm, gs_hbm, wg_ref, wu_ref, wd_ref, ys_ref, hbuf, gbuf, sem_h, sem_g):
    i = pl.program_id(0)
    n_tiles = nt_ref[0]
    slot = lax.rem(i, 2)
    gather = functools.partial(_slab_gather, src_ref, srcs=(h2s_hbm, gs_hbm), bufs=(hbuf, gbuf),
                               sems=(sem_h, sem_g), n_slabs=TILE_SLABS)

    @pl.when(i == 0)
    def _():
        gather(first=0, slot=0, wait=False)

    @pl.when(i < n_tiles)
    def _():
        gather(first=i * TILE_SLABS, slot=slot, wait=True)

        @pl.when(i + 1 < n_tiles)
        def _():
            gather(first=(i + 1) * TILE_SLABS, slot=1 - slot, wait=False)

        hb = hbuf[slot]
        gates = gbuf[slot]
        lane = lax.broadcasted_iota(jnp.int32, (1, ROUTER_W), 1)
        first_lane = N_GROUPS + tg_ref[i] * EXPERTS_PER_GROUP
        acc = jnp.zeros((TM, D_MODEL), F32)
        for e in range(EXPERTS_PER_GROUP):
            he = (_silu(_dot(hb, wg_ref[e])) * _dot(hb, wu_ref[e])).astype(BF16)
            gcol = jnp.sum(jnp.where(lane == first_lane + e, gates, 0.0), axis=-1, keepdims=True)
            acc = acc + gcol * _dot(he, wd_ref[e])
        ys_ref[...] = acc.astype(BF16)

    @pl.when(i >= n_tiles)
    def _():
        ys_ref[...] = jnp.zeros_like(ys_ref)


def _final_kernel(steps_p, dest_ref, x1_ref, rinfo_ref, ys_hbm, gfin_ref, yp_ref, ysmp_ref, ybuf, sem):
    i = pl.program_id(0)
    slot = lax.rem(i, 2)
    n_slabs = FINAL_BLOCKS * SLABS_PER_BLOCK
    gather = functools.partial(_slab_gather, dest_ref, srcs=(ys_hbm,), bufs=(ybuf,), sems=(sem,), n_slabs=n_slabs)

    @pl.when(i == 0)
    def _():
        gather(first=0, slot=0, wait=False)

    gather(first=i * n_slabs, slot=slot, wait=True)

    @pl.when(i + 1 < pl.num_programs(0))
    def _():
        gather(first=(i + 1) * n_slabs, slot=1 - slot, wait=False)

    yb = ybuf[slot]
    col = lax.broadcasted_iota(jnp.int32, (1, FINAL_BLOCKS * SORT_ROWS), 1).astype(F32)
    moe = []
    for b in range(FINAL_BLOCKS):
        rank = rinfo_ref[b * TB:(b + 1) * TB, 0:1] + float(b * SORT_ROWS)
        moe.append(_dot(jnp.where(col == rank, 1.0, 0.0).astype(BF16), yb))
    xo = x1_ref[...] + jnp.concatenate(moe, axis=0)
    y = xo * lax.rsqrt(jnp.mean(xo * xo, axis=-1, keepdims=True) + RMS_EPS) * gfin_ref[...]

    @pl.when(i < steps_p)
    def _():
        yp_ref[...] = y

    @pl.when(i >= steps_p)
    def _():
        ysmp_ref[...] = y


def _const_spec(shape, pipeline_mode=None):
    nd = len(shape)
    return pl.BlockSpec(shape, lambda i: (0,) * nd, pipeline_mode=pipeline_mode)


def _mixer_call(x2d, ret0, hg0, params, *, seg_len, carry, pipelined, block_offset, total_blocks, shared=(),
                expert_weights=()):
    (gmix, win, lbl, hgn, wout, gffn, wr, br) = params
    T = x2d.shape[0]
    LS = seg_len
    n_seg = TB // LS
    NCH = LS // CHUNK
    n_states = ret0.shape[0]
    n_own = T // TB
    n_fill = 0 if shared else total_blocks - n_own - block_offset
    lag = 1 if pipelined else 0
    grid = n_own + lag + n_fill

    f32 = np.float32
    lg = np.log1p(-(f32(2.0) ** (f32(-5.0) - np.arange(N_HEADS, dtype=f32)))).astype(f32)
    tt = np.arange(LS, dtype=f32)
    ch = np.arange(LS) // CHUNK
    dret = np.exp(np.abs(tt[:, None] - tt[None, :])[None] * lg[:, None, None]).astype(f32)
    dret = np.where((ch[None, :] <= ch[:, None])[None], dret, f32(0.0))
    lg_lane = np.repeat(lg, D_HEAD)[None, :]
    xi = np.exp((tt[:, None] + f32(1.0)) * lg_lane).astype(f32)
    zeta = np.exp((f32(LS - 1.0) - tt)[:, None] * lg_lane).astype(f32)
    gls = np.exp(f32(LS) * lg_lane).astype(f32)
    half = D_HEAD // 2
    inv_freq = ROPE_BASE ** (-jnp.arange(half, dtype=F32) / half)
    invf = jnp.tile(inv_freq, PAIR_W // half)[None, :]
    sgn = np.where((np.arange(PAIR_W) % D_HEAD) < half, f32(-1.0), f32(1.0))[None, :]
    tri = jnp.asarray(np.tril(np.ones((LS, LS), np.float32)), BF16)
    lane_head = np.arange(256) // D_HEAD
    ones_bd = jnp.asarray((lane_head[:, None] == lane_head[None, :]).astype(np.float32), BF16)
    cmask = jnp.asarray((ch[:, None] == ch[None, :]).astype(np.float32))

    if carry:
        state_spec = pl.BlockSpec((1, N_HEADS, D_HEAD, D_HEAD), lambda i: (0, 0, 0, 0))
    else:
        state_spec = pl.BlockSpec((n_seg, N_HEADS, D_HEAD, D_HEAD), lambda i: (i, 0, 0, 0))
    single = pl.Buffered(1)
    stri = jnp.asarray(np.tril(np.ones((TB, TB), np.float32), -1), BF16)
    in_row_spec = pl.BlockSpec((TB, D_MODEL), lambda i: (jnp.minimum(i, n_own - 1), 0))
    out_block = lambda i: jnp.maximum(i - lag, 0) + block_offset
    out_row_spec = lambda rows, w: pl.BlockSpec((rows, w), lambda i: (out_block(i), 0))
    assert N_SUB - 1 == 3 and NCH - 1 in (0, 3)
    nw = nx = N_HEADS * 2 * PAIR_W

    cast_w = bool(expert_weights)
    w2d = [w.reshape(-1, w.shape[-1]) for w in expert_weights]
    w_specs = [pl.BlockSpec((w.shape[0] // n_own, w.shape[1]), lambda i: (jnp.minimum(i, n_own - 1), 0)) for w in w2d]
    w_shapes = [jax.ShapeDtypeStruct(w.shape, BF16) for w in w2d]
    n_in = N_MIXER_INPUTS + len(w2d)

    kern = functools.partial(_mixer_kernel, (LS, n_seg, carry, pipelined, n_own, n_fill, len(shared), cast_w))
    return pl.pallas_call(
        kern,
        grid=(grid,),
        in_specs=[
            in_row_spec,
            _const_spec((1, D_MODEL)), _const_spec(win.shape, single), _const_spec((1, PAIR_W)),
            _const_spec((1, PAIR_W)), _const_spec(lbl.shape), _const_spec((1, GROUP_W)),
            _const_spec(wout.shape, single), _const_spec((1, D_MODEL)),
            _const_spec(wr.shape), _const_spec((1, ROUTER_W)),
            _const_spec((LS, LS)), _const_spec((N_HEADS, LS, LS)), _const_spec((LS, GROUP_W)),
            _const_spec((LS, GROUP_W)), _const_spec((1, GROUP_W)), _const_spec((256, 256)), _const_spec((LS, LS)),
            _const_spec((TB, TB)),
            state_spec, state_spec,
        ] + w_specs + [pl.BlockSpec(memory_space=pl.ANY)] * len(shared),
        out_specs=[out_row_spec(TB, D_MODEL), out_row_spec(SORT_ROWS, D_MODEL), out_row_spec(SORT_ROWS, ROUTER_W),
                   out_row_spec(TB, ROUTER_W),
                   pl.BlockSpec((1, 8, ROUTER_W), lambda i: (out_block(i), 0, 0)),
                   state_spec, state_spec] + w_specs,
        out_shape=[
            jax.ShapeDtypeStruct((total_blocks * TB, D_MODEL), F32),
            jax.ShapeDtypeStruct((total_blocks * SORT_ROWS, D_MODEL), BF16),
            jax.ShapeDtypeStruct((total_blocks * SORT_ROWS, ROUTER_W), F32),
            jax.ShapeDtypeStruct((total_blocks * TB, ROUTER_W), F32),
            jax.ShapeDtypeStruct((total_blocks, 8, ROUTER_W), jnp.int32),
            jax.ShapeDtypeStruct((n_states, N_HEADS, D_HEAD, D_HEAD), F32),
            jax.ShapeDtypeStruct((n_states, N_HEADS, D_HEAD, D_HEAD), F32),
        ] + w_shapes,
        input_output_aliases={n_in + k: k for k in range(len(shared))},
        scratch_shapes=[
            *[pltpu.VMEM((TB, GROUP_W), F32) for _ in range(8)],
            pltpu.VMEM((TB, GROUP_W), F32),
            pltpu.VMEM((TB, 2 * GROUP_W), F32),
            pltpu.VMEM((TB, GROUP_W), F32),
            pltpu.VMEM((TB, D_MODEL), F32),
            pltpu.VMEM((TB, nw), BF16), pltpu.VMEM((TB, nw), BF16),
            pltpu.VMEM((TB, nx), BF16), pltpu.VMEM((TB, nx), BF16),
            pltpu.VMEM((TB, GROUP_W), BF16), pltpu.VMEM((TB, GROUP_W), BF16),
            pltpu.VMEM((N_SUB * (SUB // 2) * (SUB + SUB // 2), GROUP_W), BF16),
            pltpu.VMEM((TB, PAIR_W), F32), pltpu.VMEM((TB, PAIR_W), F32),
            pltpu.VMEM(win.shape, BF16), pltpu.VMEM(wout.shape, BF16),
            pltpu.VMEM((1, N_PAIRS, PAIR_W, PAIR_W), F32),
            pltpu.VMEM((1, N_PAIRS, PAIR_W, PAIR_W), F32),
        ],
        compiler_params=pltpu.CompilerParams(
            dimension_semantics=("arbitrary",), vmem_limit_bytes=V7X_VMEM_LIMIT),
    )(x2d, gmix, win, invf, sgn, lbl, hgn, wout, gffn, wr, br, tri, dret, xi, zeta, gls, ones_bd, cmask, stri,
      ret0, hg0, *w2d, *shared)


def _dispatch_tables(info, n_tiles_max):
    slab_group = info[:, 0, :SLABS_PER_BLOCK].reshape(-1)
    n_slabs = slab_group.shape[0]
    valid = slab_group >= 0
    onehot = (slab_group[:, None] == jnp.arange(N_GROUPS, dtype=jnp.int32)[None, :]).astype(jnp.int32)
    within = jnp.cumsum(onehot, axis=0) - onehot
    tiles = (jnp.sum(onehot, axis=0) + TILE_SLABS - 1) // TILE_SLABS
    tile_end = jnp.cumsum(tiles)
    g = jnp.clip(slab_group, 0, N_GROUPS - 1)
    dest = (tile_end - tiles)[g] * TILE_SLABS + jnp.take_along_axis(within, g[:, None], axis=1)[:, 0]
    dest = jnp.where(valid, dest, -1).astype(jnp.int32)
    n_slots = n_tiles_max * TILE_SLABS
    src = jnp.full((n_slots,), -1, jnp.int32).at[jnp.where(valid, dest, n_slots)].set(
        jnp.arange(n_slabs, dtype=jnp.int32), mode="drop")
    tile_group = jnp.sum(jnp.arange(n_tiles_max, dtype=jnp.int32)[:, None] >= tile_end[None, :], axis=1)
    tile_group = jnp.minimum(tile_group, N_GROUPS - 1).astype(jnp.int32)
    return src, dest, tile_group, tile_end[-1:].astype(jnp.int32)


def _moe_call(src, tile_group, n_tiles, h2s, gs, w_gate, w_up, w_down):
    n_tiles_max = tile_group.shape[0]
    group_of_tile = lambda i, s, tg, nt: (tg[i], 0, 0)
    return pl.pallas_call(
        _moe_kernel,
        grid_spec=pltpu.PrefetchScalarGridSpec(
            num_scalar_prefetch=3,
            grid=(n_tiles_max,),
            in_specs=[
                pl.BlockSpec(memory_space=pl.ANY), pl.BlockSpec(memory_space=pl.ANY),
                pl.BlockSpec((EXPERTS_PER_GROUP, D_MODEL, D_EXPERT), group_of_tile),
                pl.BlockSpec((EXPERTS_PER_GROUP, D_MODEL, D_EXPERT), group_of_tile),
                pl.BlockSpec((EXPERTS_PER_GROUP, D_EXPERT, D_MODEL), group_of_tile),
            ],
            out_specs=pl.BlockSpec((TM, D_MODEL), lambda i, s, tg, nt: (i, 0)),
            scratch_shapes=[
                pltpu.VMEM((2, TM, D_MODEL), BF16), pltpu.VMEM((2, TM, ROUTER_W), F32),
                pltpu.SemaphoreType.DMA((2,)), pltpu.SemaphoreType.DMA((2,)),
            ],
        ),
        out_shape=jax.ShapeDtypeStruct((n_tiles_max * TM, D_MODEL), BF16),
        compiler_params=pltpu.CompilerParams(
            dimension_semantics=("arbitrary",), vmem_limit_bytes=V7X_VMEM_LIMIT),
    )(src, tile_group, n_tiles, h2s, gs, w_gate, w_up, w_down)


def _final_call(dest, x1, rinfo, ys, gfin, blocks_p):
    n_blocks = x1.shape[0] // TB
    assert blocks_p % FINAL_BLOCKS == 0 and n_blocks % FINAL_BLOCKS == 0
    steps_p = blocks_p // FINAL_BLOCKS
    rows = FINAL_BLOCKS * TB
    return pl.pallas_call(
        functools.partial(_final_kernel, steps_p),
        grid_spec=pltpu.PrefetchScalarGridSpec(
            num_scalar_prefetch=1,
            grid=(n_blocks // FINAL_BLOCKS,),
            in_specs=[
                pl.BlockSpec((rows, D_MODEL), lambda i, d: (i, 0)),
                pl.BlockSpec((rows, ROUTER_W), lambda i, d: (i, 0)),
                pl.BlockSpec(memory_space=pl.ANY),
                pl.BlockSpec((1, D_MODEL), lambda i, d: (0, 0)),
            ],
            out_specs=[
                pl.BlockSpec((rows, D_MODEL), lambda i, d: (jnp.minimum(i, steps_p - 1), 0)),
                pl.BlockSpec((rows, D_MODEL), lambda i, d: (jnp.maximum(i - steps_p, 0), 0)),
            ],
            scratch_shapes=[pltpu.VMEM((2, FINAL_BLOCKS * SORT_ROWS, D_MODEL), BF16), pltpu.SemaphoreType.DMA((2,))],
        ),
        out_shape=[jax.ShapeDtypeStruct((blocks_p * TB, D_MODEL), F32),
                   jax.ShapeDtypeStruct(((n_blocks - blocks_p) * TB, D_MODEL), F32)],
        compiler_params=pltpu.CompilerParams(
            dimension_semantics=("arbitrary",), vmem_limit_bytes=V7X_VMEM_LIMIT),
    )(dest, x1, rinfo, ys, gfin)


def kernel(x_prompt, x_sample, state_ret, state_hgrn, norm_mix_g, w_in, hgrn_lb_logits, hgrn_norm_g, w_out,
           norm_ffn_g, w_router_group, b_router_group, w_router_expert, b_router_expert, w_exp_gate, w_exp_up,
           w_exp_down, norm_final_g):
    depth = w_in.shape[0]
    assert depth == 1 and hgrn_lb_logits.shape[0] == 2, "single-layer configuration only"
    bp, seq, d = x_prompt.shape
    db, dec_len, _ = x_sample.shape
    assert bp == 1 and d == D_MODEL and dec_len == CHUNK and seq % 256 == 0 and db % 4 == 0

    pad = ROUTER_W - N_GROUPS - N_EXPERTS
    wr = jnp.concatenate([w_router_group[0], w_router_expert[0], jnp.zeros((D_MODEL, pad), F32)], axis=1)
    br = jnp.concatenate([b_router_group[0], b_router_expert[0], jnp.zeros((pad,), F32)])[None, :]
    params = (norm_mix_g[0][None, :], w_in[0], hgrn_lb_logits,
              jnp.tile(hgrn_norm_g[0], N_HEADS)[None, :], w_out[0], norm_ffn_g[0][None, :], wr, br)

    zeros_state = jnp.zeros((1, N_HEADS, D_HEAD, D_HEAD), F32)
    blocks_p = seq // TB
    blocks_s = db * dec_len // TB
    n_blocks = blocks_p + blocks_s
    *shared, ret_p, hg_p, wg_bf, wu_bf, wd_bf = _mixer_call(
        x_prompt.reshape(seq, d), zeros_state, zeros_state, params, seg_len=TB, carry=True, pipelined=True,
        block_offset=0, total_blocks=n_blocks, expert_weights=(w_exp_gate[0], w_exp_up[0], w_exp_down[0]))
    x1, h2s, gs, rinfo, info, ret_s, hg_s = _mixer_call(
        x_sample.reshape(db * dec_len, d), state_ret[0], state_hgrn[0], params, seg_len=CHUNK, carry=False,
        pipelined=False,
        block_offset=blocks_p, total_blocks=n_blocks, shared=tuple(shared))

    n_tiles_max = -(-n_blocks * SLABS_PER_BLOCK // TILE_SLABS) + N_GROUPS
    src, dest, tile_group, n_tiles = _dispatch_tables(info, n_tiles_max)
    ys = _moe_call(src, tile_group, n_tiles, h2s, gs, wg_bf.reshape(w_exp_gate[0].shape),
                   wu_bf.reshape(w_exp_up[0].shape), wd_bf.reshape(w_exp_down[0].shape))
    y_p, y_s = _final_call(dest, x1, rinfo, ys, norm_final_g[None, :], blocks_p)
    return (y_p.reshape(bp, seq, d), y_s.reshape(db, dec_len, d), ret_p[None], hg_p[None], ret_s[None], hg_s[None])
```

```python
import functools

import numpy as np
import jax
import jax.numpy as jnp
from jax import lax
from jax.experimental import pallas as pl
from jax.experimental.pallas import tpu as pltpu

F32 = jnp.float32
BF16 = jnp.bfloat16

D_MODEL = 1024
N_HEADS = 8
D_HEAD = 64
GROUP_W = N_HEADS * D_HEAD
N_PAIRS = N_HEADS // 2
PAIR_W = 2 * D_HEAD
CHUNK = 64
SUB = 16
N_SUB = CHUNK // SUB
PAST_LEN = 2048
ROPE_BASE = 10000.0
RMS_EPS = 1e-6
LOG2E = 1.4426950408889634
N_GROUPS = 4
EXPERTS_PER_GROUP = 8
N_EXPERTS = N_GROUPS * EXPERTS_PER_GROUP
D_EXPERT = 256
ROUTER_W = 128
V7X_VMEM_LIMIT = 60 * 1024 * 1024
TB = 256
SLAB = 16
SLABS_PER_BLOCK = TB // SLAB + N_GROUPS
SORT_ROWS = SLABS_PER_BLOCK * SLAB
SORT_W = 384
TILE_SLABS = 32
TM = TILE_SLABS * SLAB
FINAL_BLOCKS = 2

C_QR, C_KR, C_VR, C_GR, C_QG, C_FG, C_VG, C_GG = (i * GROUP_W for i in range(8))


def _dot(a, b):
    return jnp.dot(a, b, preferred_element_type=F32)


def _dot_nt(a, b):
    return lax.dot_general(a, b, (((1,), (1,)), ((), ())), preferred_element_type=F32)


def _dot_tn(a, b):
    return lax.dot_general(a, b, (((0,), (0,)), ((), ())), preferred_element_type=F32)


def _split(x, n):
    parts = []
    for _ in range(n):
        p = x.astype(BF16)
        parts.append(p)
        x = x - p.astype(F32)
    return parts


def _group_sum(x, ones_ref):
    xb = x.astype(BF16)
    return jnp.concatenate([_dot(xb[:, c * 256:(c + 1) * 256], ones_ref[...]) for c in range(2)], axis=1)


class _ColumnGroups:
    def __init__(self, bufs):
        self.bufs = bufs

    def _locate(self, idx):
        rows, cols = idx
        g = cols.start // GROUP_W
        assert (cols.stop - 1) // GROUP_W == g
        return self.bufs[g], (rows, slice(cols.start - g * GROUP_W, cols.stop - g * GROUP_W))

    def __getitem__(self, idx):
        buf, at = self._locate(idx)
        return buf[at]

    def __setitem__(self, idx, value):
        buf, at = self._locate(idx)
        buf[at] = value


def _silu(x):
    return x * jax.nn.sigmoid(x)


N_MIXER_INPUTS = 21


def _mixer_kernel(cfg, *refs):
    pipelined, n_own, n_fill, n_alias, cast_w = cfg[-5:]
    n_body = n_own + (1 if pipelined else 0)
    n_in = N_MIXER_INPUTS + (3 if cast_w else 0)
    i = pl.program_id(0)

    @pl.when(i < n_body)
    def _():
        _mixer_body(cfg, *refs)

    if n_fill:
        @pl.when(i >= n_body)
        def _():
            for out_ref in refs[n_in + n_alias:n_in + n_alias + 5]:
                out_ref[...] = jnp.zeros_like(out_ref)


def _mixer_body(cfg, x_ref, gmix_ref, win_ref, invf_ref, sgn_ref, lbl_ref, hgn_ref, wout_ref, gffn_ref,
                wr_ref, br_ref, tri_ref, dret_ref, xi_ref, zeta_ref, gls_ref, ones_ref, cmask_ref, stri_ref,
                ret0_ref, hg0_ref, *rest):
    LS, NSEG, carry, pipelined, n_own, _, n_alias, cast_w = cfg
    w_chunks = ()
    if cast_w:
        w_chunks = tuple(zip(rest[:3], rest[3 + n_alias + 7:3 + n_alias + 10]))
        rest = rest[3:3 + n_alias + 7] + rest[3 + n_alias + 10:]
    (x1_ref, h2s_ref, gs_ref, rinfo_ref, info_ref, retout_ref, hgout_ref,
     *proj_bufs, kbuf, obuf, dbuf, xprev, qw, kw, qx, kx, qin_buf, kout_buf, pbuf, cos_t, sin_t, win_bf, wout_bf,
     rst, sst) = rest[n_alias:]
    proj = _ColumnGroups(proj_bufs)
    NCH = LS // CHUNK
    i = pl.program_id(0)
    lane128 = lax.broadcasted_iota(jnp.int32, (1, PAIR_W), 1)
    head_a = lane128 < D_HEAD
    bd_mask = (lax.broadcasted_iota(jnp.int32, (PAIR_W, PAIR_W), 0) < D_HEAD) == head_a

    def pair_state(ref, sg, p):
        z = jnp.zeros((D_HEAD, D_HEAD), F32)
        return jnp.concatenate([jnp.concatenate([ref[sg, 2 * p], z], axis=1),
                                jnp.concatenate([z, ref[sg, 2 * p + 1]], axis=1)], axis=0)

    @pl.when(i == 0)
    def _():
        for c in range(0, 8 * GROUP_W, GROUP_W):
            win_bf[:, c:c + GROUP_W] = win_ref[:, c:c + GROUP_W].astype(BF16)
        wout_bf[...] = wout_ref[...].astype(BF16)
        if carry:
            for p in range(N_PAIRS):
                rst[0, p] = pair_state(ret0_ref, 0, p)
                sst[0, p] = pair_state(hg0_ref, 0, p)
        row = lax.broadcasted_iota(jnp.int32, (TB, 1), 0)
        ang_row = (row & (LS - 1)).astype(F32) * invf_ref[...]
        cos_t[...] = jnp.cos(ang_row)
        sin_t[...] = jnp.sin(ang_row)
        qw[...] = jnp.zeros_like(qw)
        kw[...] = jnp.zeros_like(kw)
        if NCH > 1:
            qx[...] = jnp.zeros_like(qx)
            kx[...] = jnp.zeros_like(kx)
        if pipelined:
            for buf in (obuf, dbuf, xprev, proj_bufs[C_GR // GROUP_W], proj_bufs[C_GG // GROUP_W]):
                buf[...] = jnp.zeros_like(buf)

    if pipelined:
        _mixer_back(proj, obuf, dbuf, xprev, hgn_ref, ones_ref, wout_bf, gffn_ref, wr_ref, br_ref, stri_ref,
                    x1_ref, h2s_ref, gs_ref, rinfo_ref, info_ref)
    keep = i < n_own

    for w_ref, w_bf_ref in w_chunks:
        w_bf_ref[...] = w_ref[...].astype(BF16)

    x = x_ref[...]
    h = x * lax.rsqrt(jnp.mean(x * x, axis=-1, keepdims=True) + RMS_EPS) * gmix_ref[...]
    hb = h.astype(BF16)
    xprev[...] = x

    def project(c0, c1):
        for c in range(c0, c1, GROUP_W):
            proj[:, c:c + GROUP_W] = _dot(hb, win_bf[:, c:c + GROUP_W])

    project(C_QG, C_GG)
    project(C_QR, C_VR)

    la = lbl_ref[0:1, :]
    lb_ = lbl_ref[1:2, :]
    lmax = jnp.maximum(la, lb_)
    ea = jnp.exp(la - lmax)
    lbv = ea / (ea + jnp.exp(lb_ - lmax))
    qg = proj[:, C_QG:C_QG + GROUP_W]
    proj[:, C_QG:C_QG + GROUP_W] = _silu(qg)
    f = lbv + (1.0 - lbv) * jax.nn.sigmoid(proj[:, C_FG:C_FG + GROUP_W])
    kbuf[...] = 1.0 - f
    logf = jnp.log(f)
    for sg in range(NSEG):
        rows = slice(sg * LS, (sg + 1) * LS)
        parts = _split(logf[rows], 2)
        proj[rows, C_FG:C_FG + GROUP_W] = sum(_dot(tri_ref[...], p) for p in parts)

    project(C_VR, C_QG)
    project(C_GG, C_GG + GROUP_W)

    start = jnp.full((8, PAIR_W), i * TB if carry else PAST_LEN, jnp.int32).astype(F32)
    ang0 = (start * invf_ref[...])[0:1]
    c0 = jnp.cos(ang0)
    s0 = jnp.sin(ang0)
    cos = cos_t[...] * c0 - sin_t[...] * s0
    sin = (sin_t[...] * c0 + cos_t[...] * s0) * sgn_ref[...]
    first_half = (lane128 & (D_HEAD - 1)) < (D_HEAD // 2)
    for blk in range(2 * N_PAIRS):
        cols = slice(blk * PAIR_W, (blk + 1) * PAIR_W)
        xx = proj[:, cols]
        partner = jnp.where(first_half, pltpu.roll(xx, PAIR_W - D_HEAD // 2, 1), pltpu.roll(xx, D_HEAD // 2, 1))
        r = xx * cos + partner * sin
        if blk < N_PAIRS:
            r = r * (D_HEAD ** -0.5)
        proj[:, cols] = r

    HALF = SUB // 2
    riota = lax.broadcasted_iota(jnp.int32, (HALF, 1), 0)
    P_ROWS = HALF * SUB + HALF * HALF

    def diag_chunk(c):
        for sb in range(N_SUB):
            rs = c * CHUNK + sb * SUB
            q16 = proj[rs:rs + SUB, C_QG:C_QG + GROUP_W]
            b16 = proj[rs:rs + SUB, C_FG:C_FG + GROUP_W] * LOG2E
            k16 = kbuf[rs:rs + SUB, :]
            base = sb * P_ROWS
            upper = []
            for s in range(SUB):
                ks = k16[s:s + 1]
                bs = b16[s:s + 1]
                hi = q16[HALF:] * ks * jnp.exp2(b16[HALF:] - bs)
                if s < HALF:
                    lo = q16[:HALF] * ks * jnp.exp2(b16[:HALF] - bs)
                    lo = jnp.where(riota >= s, lo, 0.0)
                    pbuf[base + s * SUB:base + (s + 1) * SUB, :] = jnp.concatenate([lo, hi], axis=0).astype(BF16)
                else:
                    upper.append(jnp.where(riota >= s - HALF, hi, 0.0))
                    if len(upper) == 2:
                        at = base + HALF * SUB + (s - HALF - 1) * HALF
                        pbuf[at:at + SUB, :] = jnp.concatenate(upper, axis=0).astype(BF16)
                        upper = []
        sc = [_dot(pbuf[:, hh * 256:(hh + 1) * 256], ones_ref[...]) for hh in range(2)]
        for sb in range(N_SUB):
            rs = c * CHUNK + sb * SUB
            v16 = proj[rs:rs + SUB, C_VG:C_VG + GROUP_W]
            base = sb * P_ROWS
            acc_lo = [jnp.zeros((HALF, 256), F32) for _ in range(2)]
            acc_hi = [jnp.zeros((HALF, 256), F32) for _ in range(2)]
            for s in range(SUB):
                for hh in range(2):
                    vs = v16[s:s + 1, hh * 256:(hh + 1) * 256]
                    if s < HALF:
                        at = base + s * SUB
                        acc_lo[hh] = acc_lo[hh] + sc[hh][at:at + HALF] * vs
                        acc_hi[hh] = acc_hi[hh] + sc[hh][at + HALF:at + SUB] * vs
                    else:
                        at = base + HALF * SUB + (s - HALF) * HALF
                        acc_hi[hh] = acc_hi[hh] + sc[hh][at:at + HALF] * vs
            dbuf[rs:rs + HALF, :] = jnp.concatenate(acc_lo, axis=1)
            dbuf[rs + HALF:rs + SUB, :] = jnp.concatenate(acc_hi, axis=1)

    for sg in range(NSEG):
        r0 = sg * LS
        rows = slice(r0, r0 + LS)
        st = 0 if carry else sg
        qg_s = proj[rows, C_QG:C_QG + GROUP_W]
        kg_s = kbuf[rows, :]
        b_s = proj[rows, C_FG:C_FG + GROUP_W]

        for c in range(NCH):
            for j in range(N_SUB - 1):
                e = c * CHUNK + j * SUB + SUB - 1
                bref = b_s[e:e + 1, :]
                q_rows = slice(e + 1, (c + 1) * CHUNK)
                k_rows = slice(e + 1 - SUB, e + 1)
                qt = (qg_s[q_rows] * jnp.exp(b_s[q_rows] - bref)).astype(BF16)
                kt = (kg_s[k_rows] * jnp.exp(bref - b_s[k_rows])).astype(BF16)
                for p in range(N_PAIRS):
                    dst = slice((p * (N_SUB - 1) + j) * PAIR_W, (p * (N_SUB - 1) + j + 1) * PAIR_W)
                    src = slice(p * PAIR_W, (p + 1) * PAIR_W)
                    qw[r0 + q_rows.start:r0 + q_rows.stop, dst] = qt[:, src]
                    kw[r0 + k_rows.start:r0 + k_rows.stop, dst] = kt[:, src]
        for c in range(NCH - 1):
            e = c * CHUNK + CHUNK - 1
            bref = b_s[e:e + 1, :]
            q_rows = slice(e + 1, LS)
            k_rows = slice(c * CHUNK, e + 1)
            qt = (qg_s[q_rows] * jnp.exp(b_s[q_rows] - bref)).astype(BF16)
            kt = (kg_s[k_rows] * jnp.exp(bref - b_s[k_rows])).astype(BF16)
            for p in range(N_PAIRS):
                dst = slice((p * (NCH - 1) + c) * PAIR_W, (p * (NCH - 1) + c + 1) * PAIR_W)
                src = slice(p * PAIR_W, (p + 1) * PAIR_W)
                qx[r0 + q_rows.start:r0 + q_rows.stop, dst] = qt[:, src]
                kx[r0 + k_rows.start:r0 + k_rows.stop, dst] = kt[:, src]

        b_last = b_s[LS - 1:LS, :]
        qin_buf[rows, :] = (qg_s * jnp.exp(b_s)).astype(BF16)
        kout_buf[rows, :] = (kg_s * jnp.exp(b_last - b_s)).astype(BF16)

    def pair_unit(sg, p):
        r0 = sg * LS
        rows = slice(r0, r0 + LS)
        st = 0 if carry else sg
        pc = slice(p * PAIR_W, (p + 1) * PAIR_W)

        qb = proj[rows, C_QR + p * PAIR_W:C_QR + (p + 1) * PAIR_W].astype(BF16)
        kr = proj[rows, C_KR + p * PAIR_W:C_KR + (p + 1) * PAIR_W]
        kb = kr.astype(BF16)
        vb = proj[rows, C_VR + p * PAIR_W:C_VR + (p + 1) * PAIR_W].astype(BF16)
        zero = jnp.zeros_like(qb)
        s_a = _dot_nt(jnp.where(head_a, qb, zero), kb) * dret_ref[2 * p]
        s_b = _dot_nt(jnp.where(head_a, zero, qb), kb) * dret_ref[2 * p + 1]
        o_r = jnp.where(head_a, _dot(s_a.astype(BF16), vb), _dot(s_b.astype(BF16), vb))
        r_old = rst[0, p] if carry else pair_state(ret0_ref, sg, p)
        o_r = o_r + _dot(qb, r_old.astype(BF16)) * xi_ref[:, pc]
        u = _dot_tn((kr * zeta_ref[:, pc]).astype(BF16), vb)
        r_new = r_old * gls_ref[:, pc] + jnp.where(bd_mask, u, 0.0)
        if pipelined:
            r_new = jnp.where(keep, r_new, r_old)
        obuf[rows, pc] = o_r

        vg = proj[rows, C_VG + p * PAIR_W:C_VG + (p + 1) * PAIR_W].astype(BF16)
        wc = slice(p * (N_SUB - 1) * PAIR_W, (p + 1) * (N_SUB - 1) * PAIR_W)
        qw_p = qw[rows, wc]
        kw_p = kw[rows, wc]
        lane_w = lax.broadcasted_iota(jnp.int32, (1, qw_p.shape[1]), 1)
        head_a_w = (lane_w & (PAIR_W - 1)) < D_HEAD
        zw = jnp.zeros_like(qw_p)
        g_a = _dot_nt(jnp.where(head_a_w, qw_p, zw), kw_p)
        g_b = _dot_nt(jnp.where(head_a_w, zw, qw_p), kw_p)
        if NCH > 1:
            g_a = g_a * cmask_ref[...]
            g_b = g_b * cmask_ref[...]
            xc = slice(p * (NCH - 1) * PAIR_W, (p + 1) * (NCH - 1) * PAIR_W)
            qx_p = qx[rows, xc]
            kx_p = kx[rows, xc]
            lane_x = lax.broadcasted_iota(jnp.int32, (1, qx_p.shape[1]), 1)
            head_a_x = (lane_x & (PAIR_W - 1)) < D_HEAD
            zx = jnp.zeros_like(qx_p)
            g_a = g_a + _dot_nt(jnp.where(head_a_x, qx_p, zx), kx_p)
            g_b = g_b + _dot_nt(jnp.where(head_a_x, zx, qx_p), kx_p)
        o_g = jnp.where(head_a, _dot(g_a.astype(BF16), vg), _dot(g_b.astype(BF16), vg))
        s_old = sst[0, p] if carry else pair_state(hg0_ref, sg, p)
        o_g = o_g + _dot(qin_buf[rows, pc], s_old.astype(BF16))
        ut = _dot_tn(kout_buf[rows, pc], vg)
        b_last = proj[r0 + LS - 1:r0 + LS, C_FG + p * PAIR_W:C_FG + (p + 1) * PAIR_W]
        s_decay = jnp.exp(jnp.broadcast_to(b_last, (PAIR_W, PAIR_W)).T)
        s_new = s_old * s_decay + jnp.where(bd_mask, ut, 0.0)
        if pipelined:
            s_new = jnp.where(keep, s_new, s_old)
        gc = slice(GROUP_W + p * PAIR_W, GROUP_W + (p + 1) * PAIR_W)
        obuf[rows, gc] = o_g

        if carry:
            rst[0, p] = r_new
            sst[0, p] = s_new
        for out_ref, new in ((retout_ref, r_new), (hgout_ref, s_new)):
            out_ref[st, 2 * p] = new[:D_HEAD, :D_HEAD]
            out_ref[st, 2 * p + 1] = new[D_HEAD:, D_HEAD:]

    units = [(sg, p) for sg in range(NSEG) for p in range(N_PAIRS)]
    per_chunk = len(units) // (TB // CHUNK)
    for c in range(TB // CHUNK):
        for sg, p in units[c * per_chunk:(c + 1) * per_chunk]:
            pair_unit(sg, p)
        diag_chunk(c)

    if not pipelined:
        _mixer_back(proj, obuf, dbuf, xprev, hgn_ref, ones_ref, wout_bf, gffn_ref, wr_ref, br_ref, stri_ref,
                    x1_ref, h2s_ref, gs_ref, rinfo_ref, info_ref)


def _mixer_back(proj, obuf, dbuf, xprev, hgn_ref, ones_ref, wout_bf, gffn_ref, wr_ref, br_ref, stri_ref,
                x1_ref, h2s_ref, gs_ref, rinfo_ref, info_ref):
    o_r = obuf[:, 0:GROUP_W]
    mu = _group_sum(o_r, ones_ref) * (1.0 / D_HEAD)
    dlt = o_r - mu
    var = _group_sum(dlt * dlt, ones_ref) * (1.0 / D_HEAD)
    y_r = dlt * lax.rsqrt(var + RMS_EPS) * _silu(proj[:, C_GR:C_GR + GROUP_W])
    o_g = obuf[:, GROUP_W:2 * GROUP_W] + dbuf[...]
    ms = _group_sum(o_g * o_g, ones_ref) * (1.0 / D_HEAD)
    y_g = o_g * lax.rsqrt(ms + RMS_EPS) * hgn_ref[...] * _silu(proj[:, C_GG:C_GG + GROUP_W])
    mix = jnp.concatenate([y_r, y_g], axis=1).astype(BF16)
    x1 = xprev[...] + _dot(mix, wout_bf[...])
    x1_ref[...] = x1

    h2 = x1 * lax.rsqrt(jnp.mean(x1 * x1, axis=-1, keepdims=True) + RMS_EPS) * gffn_ref[...]
    h_hi, h_lo = _split(h2, 2)
    w_hi, w_lo = _split(wr_ref[...], 2)
    hi_terms = _dot(h_hi, jnp.concatenate([w_hi, w_lo], axis=1))
    logits = hi_terms[:, :ROUTER_W] + hi_terms[:, ROUTER_W:] + _dot(h_lo, w_hi) + br_ref[...]
    lane = lax.broadcasted_iota(jnp.int32, (1, ROUTER_W), 1).astype(F32)
    neg = -jnp.inf
    no_lane = float(ROUTER_W)
    gl = jnp.where(lane < N_GROUPS, logits, neg)
    gmax = jnp.max(gl, axis=-1, keepdims=True)
    g_idx = jnp.min(jnp.where(gl == gmax, lane, no_lane), axis=-1, keepdims=True)
    prob_g = 1.0 / jnp.sum(jnp.exp(gl - gmax), axis=-1, keepdims=True)
    e_lo = N_GROUPS + EXPERTS_PER_GROUP * g_idx
    el = jnp.where((lane >= e_lo) & (lane < e_lo + EXPERTS_PER_GROUP), logits, neg)
    v1 = jnp.max(el, axis=-1, keepdims=True)
    i1 = jnp.min(jnp.where(el == v1, lane, no_lane), axis=-1, keepdims=True)
    el2 = jnp.where(lane == i1, neg, el)
    v2 = jnp.max(el2, axis=-1, keepdims=True)
    i2 = jnp.min(jnp.where(el2 == v2, lane, no_lane), axis=-1, keepdims=True)
    t = jnp.exp(v2 - v1)
    p1 = 1.0 / (1.0 + t)
    p2 = t * p1
    gate = jnp.where(lane == i1, prob_g * p1, 0.0) + jnp.where(lane == i2, prob_g * p2, 0.0)

    onehot = jnp.where(lane == g_idx, 1.0, 0.0)
    before = _dot(stri_ref[...], onehot.astype(BF16))
    count = jnp.sum(onehot, axis=0, keepdims=True)
    padded = jnp.floor((count + (SLAB - 1.0)) * (1.0 / SLAB)) * SLAB
    padded8 = jnp.broadcast_to(padded, (8, ROUTER_W))
    seg_start = sum(jnp.where(lane >= k, pltpu.roll(padded8, k, 1), 0.0) for k in range(1, N_GROUPS))[0:1]
    rank = jnp.sum(onehot * (seg_start + before), axis=-1, keepdims=True)
    col = lax.broadcasted_iota(jnp.int32, (1, SORT_W), 1).astype(F32)
    place = jnp.where(col == rank, 1.0, 0.0).astype(BF16)
    h2s_ref[...] = _dot_tn(place, h2.astype(BF16))[:SORT_ROWS].astype(BF16)
    gs_ref[...] = sum(_dot_tn(place, part) for part in _split(gate, 3))[:SORT_ROWS]
    rinfo_ref[...] = jnp.where(lane == 0, rank, 0.0)
    slab_lo = (lane * SLAB).astype(F32)
    slab_group = jnp.full((1, ROUTER_W), -1.0, F32)
    for g in range(N_GROUPS):
        s_g = jnp.sum(jnp.where(lane == g, seg_start, 0.0), axis=-1, keepdims=True)
        n_g = jnp.sum(jnp.where(lane == g, padded, 0.0), axis=-1, keepdims=True)
        slab_group = jnp.where((slab_lo >= s_g) & (slab_lo < s_g + n_g), float(g), slab_group)
    info_ref[0] = jnp.broadcast_to(slab_group.astype(jnp.int32), (8, ROUTER_W))


def _slab_gather(table_ref, first, n_slabs, srcs, bufs, sems, slot, *, wait):
    n_rows = n_slabs * SLAB
    for src, buf, sem in zip(srcs, bufs, sems):
        if wait:
            pltpu.make_async_copy(src.at[pl.ds(0, n_rows)], buf.at[slot, pl.ds(0, n_rows)], sem.at[slot]).wait()
            continue
        for j in range(n_slabs):
            row = pl.multiple_of(jnp.maximum(table_ref[first + j], 0) * SLAB, SLAB)
            pltpu.make_async_copy(src.at[pl.ds(row, SLAB)], buf.at[slot, pl.ds(j * SLAB, SLAB)],
                                  sem.at[slot]).start()


def _moe_kernel(src_ref, tg_ref, nt_ref, h2s_hbm, gs_hbm, wg_ref, wu_ref, wd_ref, ys_ref, hbuf, gbuf, sem_h, sem_g):
    i = pl.program_id(0)
    n_tiles = nt_ref[0]
    slot = lax.rem(i, 2)
    gather = functools.partial(_slab_gather, src_ref, srcs=(h2s_hbm, gs_hbm), bufs=(hbuf, gbuf),
                               sems=(sem_h, sem_g), n_slabs=TILE_SLABS)

    @pl.when(i == 0)
    def _():
        gather(first=0, slot=0, wait=False)

    @pl.when(i < n_tiles)
    def _():
        gather(first=i * TILE_SLABS, slot=slot, wait=True)

        @pl.when(i + 1 < n_tiles)
        def _():
            gather(first=(i + 1) * TILE_SLABS, slot=1 - slot, wait=False)

        hb = hbuf[slot]
        gates = gbuf[slot]
        lane = lax.broadcasted_iota(jnp.int32, (1, ROUTER_W), 1)
        first_lane = N_GROUPS + tg_ref[i] * EXPERTS_PER_GROUP
        acc = jnp.zeros((TM, D_MODEL), F32)
        for e in range(EXPERTS_PER_GROUP):
            he = (_silu(_dot(hb, wg_ref[e])) * _dot(hb, wu_ref[e])).astype(BF16)
            gcol = jnp.sum(jnp.where(lane == first_lane + e, gates, 0.0), axis=-1, keepdims=True)
            acc = acc + gcol * _dot(he, wd_ref[e])
        ys_ref[...] = acc.astype(BF16)

    @pl.when(i >= n_tiles)
    def _():
        ys_ref[...] = jnp.zeros_like(ys_ref)


def _final_kernel(steps_p, dest_ref, x1_ref, rinfo_ref, ys_hbm, gfin_ref, yp_ref, ysmp_ref, ybuf, sem):
    i = pl.program_id(0)
    slot = lax.rem(i, 2)
    n_slabs = FINAL_BLOCKS * SLABS_PER_BLOCK
    gather = functools.partial(_slab_gather, dest_ref, srcs=(ys_hbm,), bufs=(ybuf,), sems=(sem,), n_slabs=n_slabs)

    @pl.when(i == 0)
    def _():
        gather(first=0, slot=0, wait=False)

    gather(first=i * n_slabs, slot=slot, wait=True)

    @pl.when(i + 1 < pl.num_programs(0))
    def _():
        gather(first=(i + 1) * n_slabs, slot=1 - slot, wait=False)

    yb = ybuf[slot]
    col = lax.broadcasted_iota(jnp.int32, (1, FINAL_BLOCKS * SORT_ROWS), 1).astype(F32)
    moe = []
    for b in range(FINAL_BLOCKS):
        rank = rinfo_ref[b * TB:(b + 1) * TB, 0:1] + float(b * SORT_ROWS)
        moe.append(_dot(jnp.where(col == rank, 1.0, 0.0).astype(BF16), yb))
    xo = x1_ref[...] + jnp.concatenate(moe, axis=0)
    y = xo * lax.rsqrt(jnp.mean(xo * xo, axis=-1, keepdims=True) + RMS_EPS) * gfin_ref[...]

    @pl.when(i < steps_p)
    def _():
        yp_ref[...] = y

    @pl.when(i >= steps_p)
    def _():
        ysmp_ref[...] = y


def _const_spec(shape, pipeline_mode=None):
    nd = len(shape)
    return pl.BlockSpec(shape, lambda i: (0,) * nd, pipeline_mode=pipeline_mode)


def _mixer_call(x2d, ret0, hg0, params, *, seg_len, carry, pipelined, block_offset, total_blocks, shared=(),
                expert_weights=()):
    (gmix, win, lbl, hgn, wout, gffn, wr, br) = params
    T = x2d.shape[0]
    LS = seg_len
    n_seg = TB // LS
    NCH = LS // CHUNK
    n_states = ret0.shape[0]
    n_own = T // TB
    n_fill = 0 if shared else total_blocks - n_own - block_offset
    lag = 1 if pipelined else 0
    grid = n_own + lag + n_fill

    f32 = np.float32
    lg = np.log1p(-(f32(2.0) ** (f32(-5.0) - np.arange(N_HEADS, dtype=f32)))).astype(f32)
    tt = np.arange(LS, dtype=f32)
    ch = np.arange(LS) // CHUNK
    dret = np.exp(np.abs(tt[:, None] - tt[None, :])[None] * lg[:, None, None]).astype(f32)
    dret = np.where((ch[None, :] <= ch[:, None])[None], dret, f32(0.0))
    lg_lane = np.repeat(lg, D_HEAD)[None, :]
    xi = np.exp((tt[:, None] + f32(1.0)) * lg_lane).astype(f32)
    zeta = np.exp((f32(LS - 1.0) - tt)[:, None] * lg_lane).astype(f32)
    gls = np.exp(f32(LS) * lg_lane).astype(f32)
    half = D_HEAD // 2
    inv_freq = ROPE_BASE ** (-jnp.arange(half, dtype=F32) / half)
    invf = jnp.tile(inv_freq, PAIR_W // half)[None, :]
    sgn = np.where((np.arange(PAIR_W) % D_HEAD) < half, f32(-1.0), f32(1.0))[None, :]
    tri = jnp.asarray(np.tril(np.ones((LS, LS), np.float32)), BF16)
    lane_head = np.arange(256) // D_HEAD
    ones_bd = jnp.asarray((lane_head[:, None] == lane_head[None, :]).astype(np.float32), BF16)
    cmask = jnp.asarray((ch[:, None] == ch[None, :]).astype(np.float32))

    if carry:
        state_spec = pl.BlockSpec((1, N_HEADS, D_HEAD, D_HEAD), lambda i: (0, 0, 0, 0))
    else:
        state_spec = pl.BlockSpec((n_seg, N_HEADS, D_HEAD, D_HEAD), lambda i: (i, 0, 0, 0))
    single = pl.Buffered(1)
    stri = jnp.asarray(np.tril(np.ones((TB, TB), np.float32), -1), BF16)
    in_row_spec = pl.BlockSpec((TB, D_MODEL), lambda i: (jnp.minimum(i, n_own - 1), 0))
    out_block = lambda i: jnp.maximum(i - lag, 0) + block_offset
    out_row_spec = lambda rows, w: pl.BlockSpec((rows, w), lambda i: (out_block(i), 0))
    nw = (N_SUB - 1) * N_PAIRS * PAIR_W
    nx = max(NCH - 1, 1) * N_PAIRS * PAIR_W

    cast_w = bool(expert_weights)
    w2d = [w.reshape(-1, w.shape[-1]) for w in expert_weights]
    w_specs = [pl.BlockSpec((w.shape[0] // n_own, w.shape[1]), lambda i: (jnp.minimum(i, n_own - 1), 0)) for w in w2d]
    w_shapes = [jax.ShapeDtypeStruct(w.shape, BF16) for w in w2d]
    n_in = N_MIXER_INPUTS + len(w2d)

    kern = functools.partial(_mixer_kernel, (LS, n_seg, carry, pipelined, n_own, n_fill, len(shared), cast_w))
    return pl.pallas_call(
        kern,
        grid=(grid,),
        in_specs=[
            in_row_spec,
            _const_spec((1, D_MODEL)), _const_spec(win.shape, single), _const_spec((1, PAIR_W)),
            _const_spec((1, PAIR_W)), _const_spec(lbl.shape), _const_spec((1, GROUP_W)),
            _const_spec(wout.shape, single), _const_spec((1, D_MODEL)),
            _const_spec(wr.shape), _const_spec((1, ROUTER_W)),
            _const_spec((LS, LS)), _const_spec((N_HEADS, LS, LS)), _const_spec((LS, GROUP_W)),
            _const_spec((LS, GROUP_W)), _const_spec((1, GROUP_W)), _const_spec((256, 256)), _const_spec((LS, LS)),
            _const_spec((TB, TB)),
            state_spec, state_spec,
        ] + w_specs + [pl.BlockSpec(memory_space=pl.ANY)] * len(shared),
        out_specs=[out_row_spec(TB, D_MODEL), out_row_spec(SORT_ROWS, D_MODEL), out_row_spec(SORT_ROWS, ROUTER_W),
                   out_row_spec(TB, ROUTER_W),
                   pl.BlockSpec((1, 8, ROUTER_W), lambda i: (out_block(i), 0, 0)),
                   state_spec, state_spec] + w_specs,
        out_shape=[
            jax.ShapeDtypeStruct((total_blocks * TB, D_MODEL), F32),
            jax.ShapeDtypeStruct((total_blocks * SORT_ROWS, D_MODEL), BF16),
            jax.ShapeDtypeStruct((total_blocks * SORT_ROWS, ROUTER_W), F32),
            jax.ShapeDtypeStruct((total_blocks * TB, ROUTER_W), F32),
            jax.ShapeDtypeStruct((total_blocks, 8, ROUTER_W), jnp.int32),
            jax.ShapeDtypeStruct((n_states, N_HEADS, D_HEAD, D_HEAD), F32),
            jax.ShapeDtypeStruct((n_states, N_HEADS, D_HEAD, D_HEAD), F32),
        ] + w_shapes,
        input_output_aliases={n_in + k: k for k in range(len(shared))},
        scratch_shapes=[
            *[pltpu.VMEM((TB, GROUP_W), F32) for _ in range(8)],
            pltpu.VMEM((TB, GROUP_W), F32),
            pltpu.VMEM((TB, 2 * GROUP_W), F32),
            pltpu.VMEM((TB, GROUP_W), F32),
            pltpu.VMEM((TB, D_MODEL), F32),
            pltpu.VMEM((TB, nw), BF16), pltpu.VMEM((TB, nw), BF16),
            pltpu.VMEM((TB, nx), BF16), pltpu.VMEM((TB, nx), BF16),
            pltpu.VMEM((TB, GROUP_W), BF16), pltpu.VMEM((TB, GROUP_W), BF16),
            pltpu.VMEM((N_SUB * (SUB // 2) * (SUB + SUB // 2), GROUP_W), BF16),
            pltpu.VMEM((TB, PAIR_W), F32), pltpu.VMEM((TB, PAIR_W), F32),
            pltpu.VMEM(win.shape, BF16), pltpu.VMEM(wout.shape, BF16),
            pltpu.VMEM((1, N_PAIRS, PAIR_W, PAIR_W), F32),
            pltpu.VMEM((1, N_PAIRS, PAIR_W, PAIR_W), F32),
        ],
        compiler_params=pltpu.CompilerParams(
            dimension_semantics=("arbitrary",), vmem_limit_bytes=V7X_VMEM_LIMIT),
    )(x2d, gmix, win, invf, sgn, lbl, hgn, wout, gffn, wr, br, tri, dret, xi, zeta, gls, ones_bd, cmask, stri,
      ret0, hg0, *w2d, *shared)


def _dispatch_tables(info, n_tiles_max):
    slab_group = info[:, 0, :SLABS_PER_BLOCK].reshape(-1)
    n_slabs = slab_group.shape[0]
    valid = slab_group >= 0
    onehot = (slab_group[:, None] == jnp.arange(N_GROUPS, dtype=jnp.int32)[None, :]).astype(jnp.int32)
    within = jnp.cumsum(onehot, axis=0) - onehot
    tiles = (jnp.sum(onehot, axis=0) + TILE_SLABS - 1) // TILE_SLABS
    tile_end = jnp.cumsum(tiles)
    g = jnp.clip(slab_group, 0, N_GROUPS - 1)
    dest = (tile_end - tiles)[g] * TILE_SLABS + jnp.take_along_axis(within, g[:, None], axis=1)[:, 0]
    dest = jnp.where(valid, dest, -1).astype(jnp.int32)
    n_slots = n_tiles_max * TILE_SLABS
    src = jnp.full((n_slots,), -1, jnp.int32).at[jnp.where(valid, dest, n_slots)].set(
        jnp.arange(n_slabs, dtype=jnp.int32), mode="drop")
    tile_group = jnp.sum(jnp.arange(n_tiles_max, dtype=jnp.int32)[:, None] >= tile_end[None, :], axis=1)
    tile_group = jnp.minimum(tile_group, N_GROUPS - 1).astype(jnp.int32)
    return src, dest, tile_group, tile_end[-1:].astype(jnp.int32)


def _moe_call(src, tile_group, n_tiles, h2s, gs, w_gate, w_up, w_down):
    n_tiles_max = tile_group.shape[0]
    group_of_tile = lambda i, s, tg, nt: (tg[i], 0, 0)
    return pl.pallas_call(
        _moe_kernel,
        grid_spec=pltpu.PrefetchScalarGridSpec(
            num_scalar_prefetch=3,
            grid=(n_tiles_max,),
            in_specs=[
                pl.BlockSpec(memory_space=pl.ANY), pl.BlockSpec(memory_space=pl.ANY),
                pl.BlockSpec((EXPERTS_PER_GROUP, D_MODEL, D_EXPERT), group_of_tile),
                pl.BlockSpec((EXPERTS_PER_GROUP, D_MODEL, D_EXPERT), group_of_tile),
                pl.BlockSpec((EXPERTS_PER_GROUP, D_EXPERT, D_MODEL), group_of_tile),
            ],
            out_specs=pl.BlockSpec((TM, D_MODEL), lambda i, s, tg, nt: (i, 0)),
            scratch_shapes=[
                pltpu.VMEM((2, TM, D_MODEL), BF16), pltpu.VMEM((2, TM, ROUTER_W), F32),
                pltpu.SemaphoreType.DMA((2,)), pltpu.SemaphoreType.DMA((2,)),
            ],
        ),
        out_shape=jax.ShapeDtypeStruct((n_tiles_max * TM, D_MODEL), BF16),
        compiler_params=pltpu.CompilerParams(
            dimension_semantics=("arbitrary",), vmem_limit_bytes=V7X_VMEM_LIMIT),
    )(src, tile_group, n_tiles, h2s, gs, w_gate, w_up, w_down)


def _final_call(dest, x1, rinfo, ys, gfin, blocks_p):
    n_blocks = x1.shape[0] // TB
    assert blocks_p % FINAL_BLOCKS == 0 and n_blocks % FINAL_BLOCKS == 0
    steps_p = blocks_p // FINAL_BLOCKS
    rows = FINAL_BLOCKS * TB
    return pl.pallas_call(
        functools.partial(_final_kernel, steps_p),
        grid_spec=pltpu.PrefetchScalarGridSpec(
            num_scalar_prefetch=1,
            grid=(n_blocks // FINAL_BLOCKS,),
            in_specs=[
                pl.BlockSpec((rows, D_MODEL), lambda i, d: (i, 0)),
                pl.BlockSpec((rows, ROUTER_W), lambda i, d: (i, 0)),
                pl.BlockSpec(memory_space=pl.ANY),
                pl.BlockSpec((1, D_MODEL), lambda i, d: (0, 0)),
            ],
            out_specs=[
                pl.BlockSpec((rows, D_MODEL), lambda i, d: (jnp.minimum(i, steps_p - 1), 0)),
                pl.BlockSpec((rows, D_MODEL), lambda i, d: (jnp.maximum(i - steps_p, 0), 0)),
            ],
            scratch_shapes=[pltpu.VMEM((2, FINAL_BLOCKS * SORT_ROWS, D_MODEL), BF16), pltpu.SemaphoreType.DMA((2,))],
        ),
        out_shape=[jax.ShapeDtypeStruct((blocks_p * TB, D_MODEL), F32),
                   jax.ShapeDtypeStruct(((n_blocks - blocks_p) * TB, D_MODEL), F32)],
        compiler_params=pltpu.CompilerParams(
            dimension_semantics=("arbitrary",), vmem_limit_bytes=V7X_VMEM_LIMIT),
    )(dest, x1, rinfo, ys, gfin)


def kernel(x_prompt, x_sample, state_ret, state_hgrn, norm_mix_g, w_in, hgrn_lb_logits, hgrn_norm_g, w_out,
           norm_ffn_g, w_router_group, b_router_group, w_router_expert, b_router_expert, w_exp_gate, w_exp_up,
           w_exp_down, norm_final_g):
    depth = w_in.shape[0]
    assert depth == 1 and hgrn_lb_logits.shape[0] == 2, "single-layer configuration only"
    bp, seq, d = x_prompt.shape
    db, dec_len, _ = x_sample.shape
    assert bp == 1 and d == D_MODEL and dec_len == CHUNK and seq % 256 == 0 and db % 4 == 0

    pad = ROUTER_W - N_GROUPS - N_EXPERTS
    wr = jnp.concatenate([w_router_group[0], w_router_expert[0], jnp.zeros((D_MODEL, pad), F32)], axis=1)
    br = jnp.concatenate([b_router_group[0], b_router_expert[0], jnp.zeros((pad,), F32)])[None, :]
    params = (norm_mix_g[0][None, :], w_in[0], hgrn_lb_logits,
              jnp.tile(hgrn_norm_g[0], N_HEADS)[None, :], w_out[0], norm_ffn_g[0][None, :], wr, br)

    zeros_state = jnp.zeros((1, N_HEADS, D_HEAD, D_HEAD), F32)
    blocks_p = seq // TB
    blocks_s = db * dec_len // TB
    n_blocks = blocks_p + blocks_s
    *shared, ret_p, hg_p, wg_bf, wu_bf, wd_bf = _mixer_call(
        x_prompt.reshape(seq, d), zeros_state, zeros_state, params, seg_len=TB, carry=True, pipelined=True,
        block_offset=0, total_blocks=n_blocks, expert_weights=(w_exp_gate[0], w_exp_up[0], w_exp_down[0]))
    x1, h2s, gs, rinfo, info, ret_s, hg_s = _mixer_call(
        x_sample.reshape(db * dec_len, d), state_ret[0], state_hgrn[0], params, seg_len=CHUNK, carry=False,
        pipelined=False,
        block_offset=blocks_p, total_blocks=n_blocks, shared=tuple(shared))

    n_tiles_max = -(-n_blocks * SLABS_PER_BLOCK // TILE_SLABS) + N_GROUPS
    src, dest, tile_group, n_tiles = _dispatch_tables(info, n_tiles_max)
    ys = _moe_call(src, tile_group, n_tiles, h2s, gs, wg_bf.reshape(w_exp_gate[0].shape),
                   wu_bf.reshape(w_exp_up[0].shape), wd_bf.reshape(w_exp_down[0].shape))
    y_p, y_s = _final_call(dest, x1, rinfo, ys, norm_final_g[None, :], blocks_p)
    return (y_p.reshape(bp, seq, d), y_s.reshape(db, dec_len, d), ret_p[None], hg_p[None], ret_s[None], hg_s[None])
```

```python
import functools

import numpy as np
import jax
import jax.numpy as jnp
from jax import lax
from jax.experimental import pallas as pl
from jax.experimental.pallas import tpu as pltpu

F32 = jnp.float32
BF16 = jnp.bfloat16

D_MODEL = 1024
N_HEADS = 8
D_HEAD = 64
GROUP_W = N_HEADS * D_HEAD
N_PAIRS = N_HEADS // 2
PAIR_W = 2 * D_HEAD
CHUNK = 64
SUB = 16
N_SUB = CHUNK // SUB
PAST_LEN = 2048
ROPE_BASE = 10000.0
RMS_EPS = 1e-6
LOG2E = 1.4426950408889634
SPAN_MAX = 80.0
N_GROUPS = 4
EXPERTS_PER_GROUP = 8
N_EXPERTS = N_GROUPS * EXPERTS_PER_GROUP
D_EXPERT = 256
ROUTER_W = 128
V7X_VMEM_LIMIT = 60 * 1024 * 1024
TB = 256
SLAB = 16
SLABS_PER_BLOCK = TB // SLAB + N_GROUPS
SORT_ROWS = SLABS_PER_BLOCK * SLAB
SORT_W = 384
TILE_SLABS = 32
TM = TILE_SLABS * SLAB
FINAL_BLOCKS = 2

C_QR, C_KR, C_VR, C_GR, C_QG, C_FG, C_VG, C_GG = (i * GROUP_W for i in range(8))


def _dot(a, b):
    return jnp.dot(a, b, preferred_element_type=F32)


def _dot_nt(a, b):
    return lax.dot_general(a, b, (((1,), (1,)), ((), ())), preferred_element_type=F32)


def _dot_tn(a, b):
    return lax.dot_general(a, b, (((0,), (0,)), ((), ())), preferred_element_type=F32)


def _split(x, n):
    parts = []
    for _ in range(n):
        p = x.astype(BF16)
        parts.append(p)
        x = x - p.astype(F32)
    return parts


def _group_sum(x, ones_ref):
    xb = x.astype(BF16)
    return jnp.concatenate([_dot(xb[:, c * 256:(c + 1) * 256], ones_ref[...]) for c in range(2)], axis=1)


class _ColumnGroups:
    def __init__(self, bufs):
        self.bufs = bufs

    def _locate(self, idx):
        rows, cols = idx
        g = cols.start // GROUP_W
        assert (cols.stop - 1) // GROUP_W == g
        return self.bufs[g], (rows, slice(cols.start - g * GROUP_W, cols.stop - g * GROUP_W))

    def __getitem__(self, idx):
        buf, at = self._locate(idx)
        return buf[at]

    def __setitem__(self, idx, value):
        buf, at = self._locate(idx)
        buf[at] = value


def _silu(x):
    return x * jax.nn.sigmoid(x)


N_MIXER_INPUTS = 22


def _mixer_kernel(cfg, *refs):
    pipelined, n_own, n_fill, n_alias, cast_w = cfg[-5:]
    n_body = n_own + (1 if pipelined else 0)
    n_in = N_MIXER_INPUTS + (3 if cast_w else 0)
    i = pl.program_id(0)

    @pl.when(i < n_body)
    def _():
        _mixer_body(cfg, *refs)

    if n_fill:
        @pl.when(i >= n_body)
        def _():
            for out_ref in refs[n_in + n_alias:n_in + n_alias + 5]:
                out_ref[...] = jnp.zeros_like(out_ref)


def _mixer_body(cfg, x_ref, gmix_ref, win_ref, invf_ref, sgn_ref, lbl_ref, hgn_ref, wout_ref, gffn_ref,
                wr_ref, br_ref, tri_ref, dret_ref, xi_ref, zeta_ref, gls_ref, ones_ref, cmask_ref, stri_ref,
                dmask_ref, ret0_ref, hg0_ref, *rest):
    LS, NSEG, carry, pipelined, n_own, _, n_alias, cast_w = cfg
    w_chunks = ()
    if cast_w:
        w_chunks = tuple(zip(rest[:3], rest[3 + n_alias + 7:3 + n_alias + 10]))
        rest = rest[3:3 + n_alias + 7] + rest[3 + n_alias + 10:]
    (x1_ref, h2s_ref, gs_ref, rinfo_ref, info_ref, retout_ref, hgout_ref,
     *proj_bufs, kbuf, obuf, dbuf, xprev, qw, kw, qx, kx, qin_buf, kout_buf, qd, kd, pbuf, cos_t, sin_t, win_bf, wout_bf,
     rst, sst) = rest[n_alias:]
    proj = _ColumnGroups(proj_bufs)
    NCH = LS // CHUNK
    i = pl.program_id(0)
    lane128 = lax.broadcasted_iota(jnp.int32, (1, PAIR_W), 1)
    head_a = lane128 < D_HEAD
    bd_mask = (lax.broadcasted_iota(jnp.int32, (PAIR_W, PAIR_W), 0) < D_HEAD) == head_a

    def pair_state(ref, sg, p):
        z = jnp.zeros((D_HEAD, D_HEAD), F32)
        return jnp.concatenate([jnp.concatenate([ref[sg, 2 * p], z], axis=1),
                                jnp.concatenate([z, ref[sg, 2 * p + 1]], axis=1)], axis=0)

    @pl.when(i == 0)
    def _():
        for c in range(0, 8 * GROUP_W, GROUP_W):
            win_bf[:, c:c + GROUP_W] = win_ref[:, c:c + GROUP_W].astype(BF16)
        wout_bf[...] = wout_ref[...].astype(BF16)
        if carry:
            for p in range(N_PAIRS):
                rst[0, p] = pair_state(ret0_ref, 0, p)
                sst[0, p] = pair_state(hg0_ref, 0, p)
        row = lax.broadcasted_iota(jnp.int32, (TB, 1), 0)
        ang_row = (row & (LS - 1)).astype(F32) * invf_ref[...]
        cos_t[...] = jnp.cos(ang_row)
        sin_t[...] = jnp.sin(ang_row)
        qw[...] = jnp.zeros_like(qw)
        kw[...] = jnp.zeros_like(kw)
        if NCH > 1:
            qx[...] = jnp.zeros_like(qx)
            kx[...] = jnp.zeros_like(kx)
        if pipelined:
            for buf in (obuf, dbuf, xprev, proj_bufs[C_GR // GROUP_W], proj_bufs[C_GG // GROUP_W]):
                buf[...] = jnp.zeros_like(buf)

    if pipelined:
        _mixer_back(proj, obuf, dbuf, xprev, hgn_ref, ones_ref, wout_bf, gffn_ref, wr_ref, br_ref, stri_ref,
                    x1_ref, h2s_ref, gs_ref, rinfo_ref, info_ref)
    keep = i < n_own

    for w_ref, w_bf_ref in w_chunks:
        w_bf_ref[...] = w_ref[...].astype(BF16)

    x = x_ref[...]
    h = x * lax.rsqrt(jnp.mean(x * x, axis=-1, keepdims=True) + RMS_EPS) * gmix_ref[...]
    hb = h.astype(BF16)

    def project(c0, c1):
        for c in range(c0, c1, GROUP_W):
            proj[:, c:c + GROUP_W] = _dot(hb, win_bf[:, c:c + GROUP_W])

    project(C_QG, C_GG)
    project(C_QR, C_VR)

    la = lbl_ref[0:1, :]
    lb_ = lbl_ref[1:2, :]
    lmax = jnp.maximum(la, lb_)
    ea = jnp.exp(la - lmax)
    lbv = ea / (ea + jnp.exp(lb_ - lmax))
    qg = proj[:, C_QG:C_QG + GROUP_W]
    proj[:, C_QG:C_QG + GROUP_W] = _silu(qg)
    f = lbv + (1.0 - lbv) * jax.nn.sigmoid(proj[:, C_FG:C_FG + GROUP_W])
    kbuf[...] = 1.0 - f
    logf = jnp.log(f)
    for sg in range(NSEG):
        rows = slice(sg * LS, (sg + 1) * LS)
        parts = _split(logf[rows], 3)
        proj[rows, C_FG:C_FG + GROUP_W] = sum(_dot(tri_ref[...], p) for p in parts)

    project(C_VR, C_QG)
    project(C_GG, C_GG + GROUP_W)

    start = jnp.full((8, PAIR_W), i * TB if carry else PAST_LEN, jnp.int32).astype(F32)
    ang0 = (start * invf_ref[...])[0:1]
    c0 = jnp.cos(ang0)
    s0 = jnp.sin(ang0)
    cos = cos_t[...] * c0 - sin_t[...] * s0
    sin = (sin_t[...] * c0 + cos_t[...] * s0) * sgn_ref[...]
    first_half = (lane128 & (D_HEAD - 1)) < (D_HEAD // 2)
    for blk in range(2 * N_PAIRS):
        cols = slice(blk * PAIR_W, (blk + 1) * PAIR_W)
        xx = proj[:, cols]
        partner = jnp.where(first_half, pltpu.roll(xx, PAIR_W - D_HEAD // 2, 1), pltpu.roll(xx, D_HEAD // 2, 1))
        r = xx * cos + partner * sin
        if blk < N_PAIRS:
            r = r * (D_HEAD ** -0.5)
        proj[:, cols] = r

    HALF = SUB // 2
    riota = lax.broadcasted_iota(jnp.int32, (HALF, 1), 0)
    P_ROWS = HALF * SUB + HALF * HALF

    def diag_chunk(c):
        for sb in range(N_SUB):
            rs = c * CHUNK + sb * SUB
            q16 = proj[rs:rs + SUB, C_QG:C_QG + GROUP_W]
            b16 = proj[rs:rs + SUB, C_FG:C_FG + GROUP_W] * LOG2E
            k16 = kbuf[rs:rs + SUB, :]
            base = sb * P_ROWS
            upper = []
            for s in range(SUB):
                ks = k16[s:s + 1]
                bs = b16[s:s + 1]
                hi = q16[HALF:] * ks * jnp.exp2(b16[HALF:] - bs)
                if s < HALF:
                    lo = q16[:HALF] * ks * jnp.exp2(b16[:HALF] - bs)
                    lo = jnp.where(riota >= s, lo, 0.0)
                    pbuf[base + s * SUB:base + (s + 1) * SUB, :] = jnp.concatenate([lo, hi], axis=0).astype(BF16)
                else:
                    upper.append(jnp.where(riota >= s - HALF, hi, 0.0))
                    if len(upper) == 2:
                        at = base + HALF * SUB + (s - HALF - 1) * HALF
                        pbuf[at:at + SUB, :] = jnp.concatenate(upper, axis=0).astype(BF16)
                        upper = []
        sc = [_dot(pbuf[:, hh * 256:(hh + 1) * 256], ones_ref[...]) for hh in range(2)]
        for sb in range(N_SUB):
            rs = c * CHUNK + sb * SUB
            v16 = proj[rs:rs + SUB, C_VG:C_VG + GROUP_W]
            base = sb * P_ROWS
            acc_lo = [jnp.zeros((HALF, 256), F32) for _ in range(2)]
            acc_hi = [jnp.zeros((HALF, 256), F32) for _ in range(2)]
            for s in range(SUB):
                for hh in range(2):
                    vs = v16[s:s + 1, hh * 256:(hh + 1) * 256]
                    if s < HALF:
                        at = base + s * SUB
                        acc_lo[hh] = acc_lo[hh] + sc[hh][at:at + HALF] * vs
                        acc_hi[hh] = acc_hi[hh] + sc[hh][at + HALF:at + SUB] * vs
                    else:
                        at = base + HALF * SUB + (s - HALF) * HALF
                        acc_hi[hh] = acc_hi[hh] + sc[hh][at:at + HALF] * vs
            dbuf[rs:rs + HALF, :] = jnp.concatenate(acc_lo, axis=1)
            dbuf[rs + HALF:rs + SUB, :] = jnp.concatenate(acc_hi, axis=1)

    b_all = proj[:, C_FG:C_FG + GROUP_W]
    first = jnp.concatenate([jnp.broadcast_to(b_all[r:r + 1, :], (SUB, GROUP_W)) for r in range(0, TB, SUB)], axis=0)
    span = first - b_all
    risky = jnp.max(span) > SPAN_MAX
    q_local = proj[:, C_QG:C_QG + GROUP_W] * jnp.exp(-span)
    qd[...] = jnp.where(risky, 0.0, q_local).astype(BF16)
    kd[...] = (kbuf[...] * jnp.exp(jnp.minimum(span, SPAN_MAX))).astype(BF16)

    for sg in range(NSEG):
        r0 = sg * LS
        rows = slice(r0, r0 + LS)
        st = 0 if carry else sg
        qg_s = proj[rows, C_QG:C_QG + GROUP_W]
        kg_s = kbuf[rows, :]
        b_s = proj[rows, C_FG:C_FG + GROUP_W]

        for c in range(NCH):
            for j in range(N_SUB - 1):
                e = c * CHUNK + j * SUB + SUB - 1
                bref = b_s[e:e + 1, :]
                q_rows = slice(e + 1, (c + 1) * CHUNK)
                k_rows = slice(e + 1 - SUB, e + 1)
                qt = (qg_s[q_rows] * jnp.exp(b_s[q_rows] - bref)).astype(BF16)
                kt = (kg_s[k_rows] * jnp.exp(bref - b_s[k_rows])).astype(BF16)
                for p in range(N_PAIRS):
                    dst = slice((p * (N_SUB - 1) + j) * PAIR_W, (p * (N_SUB - 1) + j + 1) * PAIR_W)
                    src = slice(p * PAIR_W, (p + 1) * PAIR_W)
                    qw[r0 + q_rows.start:r0 + q_rows.stop, dst] = qt[:, src]
                    kw[r0 + k_rows.start:r0 + k_rows.stop, dst] = kt[:, src]
        for c in range(NCH - 1):
            e = c * CHUNK + CHUNK - 1
            bref = b_s[e:e + 1, :]
            q_rows = slice(e + 1, LS)
            k_rows = slice(c * CHUNK, e + 1)
            qt = (qg_s[q_rows] * jnp.exp(b_s[q_rows] - bref)).astype(BF16)
            kt = (kg_s[k_rows] * jnp.exp(bref - b_s[k_rows])).astype(BF16)
            for p in range(N_PAIRS):
                dst = slice((p * (NCH - 1) + c) * PAIR_W, (p * (NCH - 1) + c + 1) * PAIR_W)
                src = slice(p * PAIR_W, (p + 1) * PAIR_W)
                qx[r0 + q_rows.start:r0 + q_rows.stop, dst] = qt[:, src]
                kx[r0 + k_rows.start:r0 + k_rows.stop, dst] = kt[:, src]

        b_last = b_s[LS - 1:LS, :]
        qin_buf[rows, :] = (qg_s * jnp.exp(b_s)).astype(BF16)
        kout_buf[rows, :] = (kg_s * jnp.exp(b_last - b_s)).astype(BF16)

    def pair_unit(sg, p):
        r0 = sg * LS
        rows = slice(r0, r0 + LS)
        st = 0 if carry else sg
        pc = slice(p * PAIR_W, (p + 1) * PAIR_W)

        qb = proj[rows, C_QR + p * PAIR_W:C_QR + (p + 1) * PAIR_W].astype(BF16)
        kr = proj[rows, C_KR + p * PAIR_W:C_KR + (p + 1) * PAIR_W]
        kb = kr.astype(BF16)
        vb = proj[rows, C_VR + p * PAIR_W:C_VR + (p + 1) * PAIR_W].astype(BF16)
        zero = jnp.zeros_like(qb)
        s_a = _dot_nt(jnp.where(head_a, qb, zero), kb) * dret_ref[2 * p]
        s_b = _dot_nt(jnp.where(head_a, zero, qb), kb) * dret_ref[2 * p + 1]
        o_r = jnp.where(head_a, _dot(s_a.astype(BF16), vb), _dot(s_b.astype(BF16), vb))
        r_old = rst[0, p] if carry else pair_state(ret0_ref, sg, p)
        o_r = o_r + _dot(qb, r_old.astype(BF16)) * xi_ref[:, pc]
        u = _dot_tn((kr * zeta_ref[:, pc]).astype(BF16), vb)
        r_new = r_old * gls_ref[:, pc] + jnp.where(bd_mask, u, 0.0)
        if pipelined:
            r_new = jnp.where(keep, r_new, r_old)
        obuf[rows, pc] = o_r

        vg = proj[rows, C_VG + p * PAIR_W:C_VG + (p + 1) * PAIR_W].astype(BF16)
        wc = slice(p * (N_SUB - 1) * PAIR_W, (p + 1) * (N_SUB - 1) * PAIR_W)
        qw_p = qw[rows, wc]
        kw_p = kw[rows, wc]
        lane_w = lax.broadcasted_iota(jnp.int32, (1, qw_p.shape[1]), 1)
        head_a_w = (lane_w & (PAIR_W - 1)) < D_HEAD
        zw = jnp.zeros_like(qw_p)
        g_a = _dot_nt(jnp.where(head_a_w, qw_p, zw), kw_p)
        g_b = _dot_nt(jnp.where(head_a_w, zw, qw_p), kw_p)
        if NCH > 1:
            g_a = g_a * cmask_ref[...]
            g_b = g_b * cmask_ref[...]
            xc = slice(p * (NCH - 1) * PAIR_W, (p + 1) * (NCH - 1) * PAIR_W)
            qx_p = qx[rows, xc]
            kx_p = kx[rows, xc]
            lane_x = lax.broadcasted_iota(jnp.int32, (1, qx_p.shape[1]), 1)
            head_a_x = (lane_x & (PAIR_W - 1)) < D_HEAD
            zx = jnp.zeros_like(qx_p)
            g_a = g_a + _dot_nt(jnp.where(head_a_x, qx_p, zx), kx_p)
            g_b = g_b + _dot_nt(jnp.where(head_a_x, zx, qx_p), kx_p)
        qd_p = qd[rows, pc]
        kd_p = kd[rows, pc]
        zd = jnp.zeros_like(qd_p)
        g_a = g_a + _dot_nt(jnp.where(head_a, qd_p, zd), kd_p) * dmask_ref[...]
        g_b = g_b + _dot_nt(jnp.where(head_a, zd, qd_p), kd_p) * dmask_ref[...]
        o_g = jnp.where(head_a, _dot(g_a.astype(BF16), vg), _dot(g_b.astype(BF16), vg))
        s_old = sst[0, p] if carry else pair_state(hg0_ref, sg, p)
        o_g = o_g + _dot(qin_buf[rows, pc], s_old.astype(BF16))
        ut = _dot_tn(kout_buf[rows, pc], vg)
        b_last = proj[r0 + LS - 1:r0 + LS, C_FG + p * PAIR_W:C_FG + (p + 1) * PAIR_W]
        s_decay = jnp.exp(jnp.broadcast_to(b_last, (PAIR_W, PAIR_W)).T)
        s_new = s_old * s_decay + jnp.where(bd_mask, ut, 0.0)
        if pipelined:
            s_new = jnp.where(keep, s_new, s_old)
        gc = slice(GROUP_W + p * PAIR_W, GROUP_W + (p + 1) * PAIR_W)
        obuf[rows, gc] = o_g

        if carry:
            rst[0, p] = r_new
            sst[0, p] = s_new
        for out_ref, new in ((retout_ref, r_new), (hgout_ref, s_new)):
            out_ref[st, 2 * p] = new[:D_HEAD, :D_HEAD]
            out_ref[st, 2 * p + 1] = new[D_HEAD:, D_HEAD:]

    for sg in range(NSEG):
        for p in range(N_PAIRS):
            pair_unit(sg, p)

    xprev[...] = x
    dbuf[...] = jnp.zeros_like(dbuf)

    @pl.when(risky)
    def _():
        for c in range(TB // CHUNK):
            diag_chunk(c)

    if not pipelined:
        _mixer_back(proj, obuf, dbuf, xprev, hgn_ref, ones_ref, wout_bf, gffn_ref, wr_ref, br_ref, stri_ref,
                    x1_ref, h2s_ref, gs_ref, rinfo_ref, info_ref)


def _mixer_back(proj, obuf, dbuf, xprev, hgn_ref, ones_ref, wout_bf, gffn_ref, wr_ref, br_ref, stri_ref,
                x1_ref, h2s_ref, gs_ref, rinfo_ref, info_ref):
    o_r = obuf[:, 0:GROUP_W]
    mu = _group_sum(o_r, ones_ref) * (1.0 / D_HEAD)
    dlt = o_r - mu
    var = _group_sum(dlt * dlt, ones_ref) * (1.0 / D_HEAD)
    y_r = dlt * lax.rsqrt(var + RMS_EPS) * _silu(proj[:, C_GR:C_GR + GROUP_W])
    o_g = obuf[:, GROUP_W:2 * GROUP_W] + dbuf[...]
    ms = _group_sum(o_g * o_g, ones_ref) * (1.0 / D_HEAD)
    y_g = o_g * lax.rsqrt(ms + RMS_EPS) * hgn_ref[...] * _silu(proj[:, C_GG:C_GG + GROUP_W])
    mix = jnp.concatenate([y_r, y_g], axis=1).astype(BF16)
    x1 = xprev[...] + _dot(mix, wout_bf[...])
    x1_ref[...] = x1

    h2 = x1 * lax.rsqrt(jnp.mean(x1 * x1, axis=-1, keepdims=True) + RMS_EPS) * gffn_ref[...]
    h_hi, h_lo = _split(h2, 2)
    w_hi, w_lo = _split(wr_ref[...], 2)
    hi_terms = _dot(h_hi, jnp.concatenate([w_hi, w_lo], axis=1))
    logits = hi_terms[:, :ROUTER_W] + hi_terms[:, ROUTER_W:] + _dot(h_lo, w_hi) + br_ref[...]
    lane = lax.broadcasted_iota(jnp.int32, (1, ROUTER_W), 1).astype(F32)
    neg = -jnp.inf
    no_lane = float(ROUTER_W)
    gl = jnp.where(lane < N_GROUPS, logits, neg)
    gmax = jnp.max(gl, axis=-1, keepdims=True)
    g_idx = jnp.min(jnp.where(gl == gmax, lane, no_lane), axis=-1, keepdims=True)
    prob_g = 1.0 / jnp.sum(jnp.exp(gl - gmax), axis=-1, keepdims=True)
    e_lo = N_GROUPS + EXPERTS_PER_GROUP * g_idx
    el = jnp.where((lane >= e_lo) & (lane < e_lo + EXPERTS_PER_GROUP), logits, neg)
    v1 = jnp.max(el, axis=-1, keepdims=True)
    i1 = jnp.min(jnp.where(el == v1, lane, no_lane), axis=-1, keepdims=True)
    el2 = jnp.where(lane == i1, neg, el)
    v2 = jnp.max(el2, axis=-1, keepdims=True)
    i2 = jnp.min(jnp.where(el2 == v2, lane, no_lane), axis=-1, keepdims=True)
    t = jnp.exp(v2 - v1)
    p1 = 1.0 / (1.0 + t)
    p2 = t * p1
    gate = jnp.where(lane == i1, prob_g * p1, 0.0) + jnp.where(lane == i2, prob_g * p2, 0.0)

    onehot = jnp.where(lane == g_idx, 1.0, 0.0)
    before = _dot(stri_ref[...], onehot.astype(BF16))
    count = jnp.sum(onehot, axis=0, keepdims=True)
    padded = jnp.floor((count + (SLAB - 1.0)) * (1.0 / SLAB)) * SLAB
    padded8 = jnp.broadcast_to(padded, (8, ROUTER_W))
    seg_start = sum(jnp.where(lane >= k, pltpu.roll(padded8, k, 1), 0.0) for k in range(1, N_GROUPS))[0:1]
    rank = jnp.sum(onehot * (seg_start + before), axis=-1, keepdims=True)
    col = lax.broadcasted_iota(jnp.int32, (1, SORT_W), 1).astype(F32)
    place = jnp.where(col == rank, 1.0, 0.0).astype(BF16)
    h2s_ref[...] = _dot_tn(place, h2.astype(BF16))[:SORT_ROWS].astype(BF16)
    gs_ref[...] = sum(_dot_tn(place, part) for part in _split(gate, 3))[:SORT_ROWS]
    rinfo_ref[...] = jnp.where(lane == 0, rank, 0.0)
    slab_lo = (lane * SLAB).astype(F32)
    slab_group = jnp.full((1, ROUTER_W), -1.0, F32)
    for g in range(N_GROUPS):
        s_g = jnp.sum(jnp.where(lane == g, seg_start, 0.0), axis=-1, keepdims=True)
        n_g = jnp.sum(jnp.where(lane == g, padded, 0.0), axis=-1, keepdims=True)
        slab_group = jnp.where((slab_lo >= s_g) & (slab_lo < s_g + n_g), float(g), slab_group)
    info_ref[0] = jnp.broadcast_to(slab_group.astype(jnp.int32), (8, ROUTER_W))


def _slab_gather(table_ref, first, n_slabs, srcs, bufs, sems, slot, *, wait):
    n_rows = n_slabs * SLAB
    for src, buf, sem in zip(srcs, bufs, sems):
        if wait:
            pltpu.make_async_copy(src.at[pl.ds(0, n_rows)], buf.at[slot, pl.ds(0, n_rows)], sem.at[slot]).wait()
            continue
        for j in range(n_slabs):
            row = pl.multiple_of(jnp.maximum(table_ref[first + j], 0) * SLAB, SLAB)
            pltpu.make_async_copy(src.at[pl.ds(row, SLAB)], buf.at[slot, pl.ds(j * SLAB, SLAB)],
                                  sem.at[slot]).start()


def _moe_kernel(src_ref, tg_ref, nt_ref, h2s_hbm, gs_hbm, wg_ref, wu_ref, wd_ref, ys_ref, hbuf, gbuf, sem_h, sem_g):
    i = pl.program_id(0)
    n_tiles = nt_ref[0]
    slot = lax.rem(i, 2)
    gather = functools.partial(_slab_gather, src_ref, srcs=(h2s_hbm, gs_hbm), bufs=(hbuf, gbuf),
                               sems=(sem_h, sem_g), n_slabs=TILE_SLABS)

    @pl.when(i == 0)
    def _():
        gather(first=0, slot=0, wait=False)

    @pl.when(i < n_tiles)
    def _():
        gather(first=i * TILE_SLABS, slot=slot, wait=True)

        @pl.when(i + 1 < n_tiles)
        def _():
            gather(first=(i + 1) * TILE_SLABS, slot=1 - slot, wait=False)

        hb = hbuf[slot]
        gates = gbuf[slot]
        lane = lax.broadcasted_iota(jnp.int32, (1, ROUTER_W), 1)
        first_lane = N_GROUPS + tg_ref[i] * EXPERTS_PER_GROUP
        acc = jnp.zeros((TM, D_MODEL), F32)
        for e in range(EXPERTS_PER_GROUP):
            he = (_silu(_dot(hb, wg_ref[e])) * _dot(hb, wu_ref[e])).astype(BF16)
            gcol = jnp.sum(jnp.where(lane == first_lane + e, gates, 0.0), axis=-1, keepdims=True)
            acc = acc + gcol * _dot(he, wd_ref[e])
        ys_ref[...] = acc.astype(BF16)

    @pl.when(i >= n_tiles)
    def _():
        ys_ref[...] = jnp.zeros_like(ys_ref)


def _final_kernel(steps_p, dest_ref, x1_ref, rinfo_ref, ys_hbm, gfin_ref, yp_ref, ysmp_ref, ybuf, sem):
    i = pl.program_id(0)
    slot = lax.rem(i, 2)
    n_slabs = FINAL_BLOCKS * SLABS_PER_BLOCK
    gather = functools.partial(_slab_gather, dest_ref, srcs=(ys_hbm,), bufs=(ybuf,), sems=(sem,), n_slabs=n_slabs)

    @pl.when(i == 0)
    def _():
        gather(first=0, slot=0, wait=False)

    gather(first=i * n_slabs, slot=slot, wait=True)

    @pl.when(i + 1 < pl.num_programs(0))
    def _():
        gather(first=(i + 1) * n_slabs, slot=1 - slot, wait=False)

    yb = ybuf[slot]
    col = lax.broadcasted_iota(jnp.int32, (1, FINAL_BLOCKS * SORT_ROWS), 1).astype(F32)
    moe = []
    for b in range(FINAL_BLOCKS):
        rank = rinfo_ref[b * TB:(b + 1) * TB, 0:1] + float(b * SORT_ROWS)
        moe.append(_dot(jnp.where(col == rank, 1.0, 0.0).astype(BF16), yb))
    xo = x1_ref[...] + jnp.concatenate(moe, axis=0)
    y = xo * lax.rsqrt(jnp.mean(xo * xo, axis=-1, keepdims=True) + RMS_EPS) * gfin_ref[...]

    @pl.when(i < steps_p)
    def _():
        yp_ref[...] = y

    @pl.when(i >= steps_p)
    def _():
        ysmp_ref[...] = y


def _const_spec(shape, pipeline_mode=None):
    nd = len(shape)
    return pl.BlockSpec(shape, lambda i: (0,) * nd, pipeline_mode=pipeline_mode)


def _mixer_call(x2d, ret0, hg0, params, *, seg_len, carry, pipelined, block_offset, total_blocks, shared=(),
                expert_weights=()):
    (gmix, win, lbl, hgn, wout, gffn, wr, br) = params
    T = x2d.shape[0]
    LS = seg_len
    n_seg = TB // LS
    NCH = LS // CHUNK
    n_states = ret0.shape[0]
    n_own = T // TB
    n_fill = 0 if shared else total_blocks - n_own - block_offset
    lag = 1 if pipelined else 0
    grid = n_own + lag + n_fill

    f32 = np.float32
    lg = np.log1p(-(f32(2.0) ** (f32(-5.0) - np.arange(N_HEADS, dtype=f32)))).astype(f32)
    tt = np.arange(LS, dtype=f32)
    ch = np.arange(LS) // CHUNK
    dret = np.exp(np.abs(tt[:, None] - tt[None, :])[None] * lg[:, None, None]).astype(f32)
    dret = np.where((ch[None, :] <= ch[:, None])[None], dret, f32(0.0))
    lg_lane = np.repeat(lg, D_HEAD)[None, :]
    xi = np.exp((tt[:, None] + f32(1.0)) * lg_lane).astype(f32)
    zeta = np.exp((f32(LS - 1.0) - tt)[:, None] * lg_lane).astype(f32)
    gls = np.exp(f32(LS) * lg_lane).astype(f32)
    half = D_HEAD // 2
    inv_freq = ROPE_BASE ** (-jnp.arange(half, dtype=F32) / half)
    invf = jnp.tile(inv_freq, PAIR_W // half)[None, :]
    sgn = np.where((np.arange(PAIR_W) % D_HEAD) < half, f32(-1.0), f32(1.0))[None, :]
    tri = jnp.asarray(np.tril(np.ones((LS, LS), np.float32)), BF16)
    lane_head = np.arange(256) // D_HEAD
    ones_bd = jnp.asarray((lane_head[:, None] == lane_head[None, :]).astype(np.float32), BF16)
    cmask = jnp.asarray((ch[:, None] == ch[None, :]).astype(np.float32))
    sb = np.arange(LS) // SUB
    dmask = jnp.asarray(((sb[:, None] == sb[None, :]) & (np.arange(LS)[:, None] >= np.arange(LS)[None, :]))
                        .astype(np.float32))

    if carry:
        state_spec = pl.BlockSpec((1, N_HEADS, D_HEAD, D_HEAD), lambda i: (0, 0, 0, 0))
    else:
        state_spec = pl.BlockSpec((n_seg, N_HEADS, D_HEAD, D_HEAD), lambda i: (i, 0, 0, 0))
    single = pl.Buffered(1)
    stri = jnp.asarray(np.tril(np.ones((TB, TB), np.float32), -1), BF16)
    in_row_spec = pl.BlockSpec((TB, D_MODEL), lambda i: (jnp.minimum(i, n_own - 1), 0))
    out_block = lambda i: jnp.maximum(i - lag, 0) + block_offset
    out_row_spec = lambda rows, w: pl.BlockSpec((rows, w), lambda i: (out_block(i), 0))
    nw = (N_SUB - 1) * N_PAIRS * PAIR_W
    nx = max(NCH - 1, 1) * N_PAIRS * PAIR_W

    cast_w = bool(expert_weights)
    w2d = [w.reshape(-1, w.shape[-1]) for w in expert_weights]
    w_specs = [pl.BlockSpec((w.shape[0] // n_own, w.shape[1]), lambda i: (jnp.minimum(i, n_own - 1), 0)) for w in w2d]
    w_shapes = [jax.ShapeDtypeStruct(w.shape, BF16) for w in w2d]
    n_in = N_MIXER_INPUTS + len(w2d)

    kern = functools.partial(_mixer_kernel, (LS, n_seg, carry, pipelined, n_own, n_fill, len(shared), cast_w))
    return pl.pallas_call(
        kern,
        grid=(grid,),
        in_specs=[
            in_row_spec,
            _const_spec((1, D_MODEL)), _const_spec(win.shape, single), _const_spec((1, PAIR_W)),
            _const_spec((1, PAIR_W)), _const_spec(lbl.shape), _const_spec((1, GROUP_W)),
            _const_spec(wout.shape, single), _const_spec((1, D_MODEL)),
            _const_spec(wr.shape), _const_spec((1, ROUTER_W)),
            _const_spec((LS, LS)), _const_spec((N_HEADS, LS, LS)), _const_spec((LS, GROUP_W)),
            _const_spec((LS, GROUP_W)), _const_spec((1, GROUP_W)), _const_spec((256, 256)), _const_spec((LS, LS)),
            _const_spec((TB, TB)), _const_spec((LS, LS)),
            state_spec, state_spec,
        ] + w_specs + [pl.BlockSpec(memory_space=pl.ANY)] * len(shared),
        out_specs=[out_row_spec(TB, D_MODEL), out_row_spec(SORT_ROWS, D_MODEL), out_row_spec(SORT_ROWS, ROUTER_W),
                   out_row_spec(TB, ROUTER_W),
                   pl.BlockSpec((1, 8, ROUTER_W), lambda i: (out_block(i), 0, 0)),
                   state_spec, state_spec] + w_specs,
        out_shape=[
            jax.ShapeDtypeStruct((total_blocks * TB, D_MODEL), F32),
            jax.ShapeDtypeStruct((total_blocks * SORT_ROWS, D_MODEL), BF16),
            jax.ShapeDtypeStruct((total_blocks * SORT_ROWS, ROUTER_W), F32),
            jax.ShapeDtypeStruct((total_blocks * TB, ROUTER_W), F32),
            jax.ShapeDtypeStruct((total_blocks, 8, ROUTER_W), jnp.int32),
            jax.ShapeDtypeStruct((n_states, N_HEADS, D_HEAD, D_HEAD), F32),
            jax.ShapeDtypeStruct((n_states, N_HEADS, D_HEAD, D_HEAD), F32),
        ] + w_shapes,
        input_output_aliases={n_in + k: k for k in range(len(shared))},
        scratch_shapes=[
            *[pltpu.VMEM((TB, GROUP_W), F32) for _ in range(8)],
            pltpu.VMEM((TB, GROUP_W), F32),
            pltpu.VMEM((TB, 2 * GROUP_W), F32),
            pltpu.VMEM((TB, GROUP_W), F32),
            pltpu.VMEM((TB, D_MODEL), F32),
            pltpu.VMEM((TB, nw), BF16), pltpu.VMEM((TB, nw), BF16),
            pltpu.VMEM((TB, nx), BF16), pltpu.VMEM((TB, nx), BF16),
            pltpu.VMEM((TB, GROUP_W), BF16), pltpu.VMEM((TB, GROUP_W), BF16),
            pltpu.VMEM((TB, GROUP_W), BF16), pltpu.VMEM((TB, GROUP_W), BF16),
            pltpu.VMEM((N_SUB * (SUB // 2) * (SUB + SUB // 2), GROUP_W), BF16),
            pltpu.VMEM((TB, PAIR_W), F32), pltpu.VMEM((TB, PAIR_W), F32),
            pltpu.VMEM(win.shape, BF16), pltpu.VMEM(wout.shape, BF16),
            pltpu.VMEM((1, N_PAIRS, PAIR_W, PAIR_W), F32),
            pltpu.VMEM((1, N_PAIRS, PAIR_W, PAIR_W), F32),
        ],
        compiler_params=pltpu.CompilerParams(
            dimension_semantics=("arbitrary",), vmem_limit_bytes=V7X_VMEM_LIMIT),
    )(x2d, gmix, win, invf, sgn, lbl, hgn, wout, gffn, wr, br, tri, dret, xi, zeta, gls, ones_bd, cmask, stri, dmask,
      ret0, hg0, *w2d, *shared)


def _dispatch_tables(info, n_tiles_max):
    slab_group = info[:, 0, :SLABS_PER_BLOCK].reshape(-1)
    n_slabs = slab_group.shape[0]
    valid = slab_group >= 0
    onehot = (slab_group[:, None] == jnp.arange(N_GROUPS, dtype=jnp.int32)[None, :]).astype(jnp.int32)
    within = jnp.cumsum(onehot, axis=0) - onehot
    tiles = (jnp.sum(onehot, axis=0) + TILE_SLABS - 1) // TILE_SLABS
    tile_end = jnp.cumsum(tiles)
    g = jnp.clip(slab_group, 0, N_GROUPS - 1)
    dest = (tile_end - tiles)[g] * TILE_SLABS + jnp.take_along_axis(within, g[:, None], axis=1)[:, 0]
    dest = jnp.where(valid, dest, -1).astype(jnp.int32)
    n_slots = n_tiles_max * TILE_SLABS
    src = jnp.full((n_slots,), -1, jnp.int32).at[jnp.where(valid, dest, n_slots)].set(
        jnp.arange(n_slabs, dtype=jnp.int32), mode="drop")
    tile_group = jnp.sum(jnp.arange(n_tiles_max, dtype=jnp.int32)[:, None] >= tile_end[None, :], axis=1)
    tile_group = jnp.minimum(tile_group, N_GROUPS - 1).astype(jnp.int32)
    return src, dest, tile_group, tile_end[-1:].astype(jnp.int32)


def _moe_call(src, tile_group, n_tiles, h2s, gs, w_gate, w_up, w_down):
    n_tiles_max = tile_group.shape[0]
    group_of_tile = lambda i, s, tg, nt: (tg[i], 0, 0)
    return pl.pallas_call(
        _moe_kernel,
        grid_spec=pltpu.PrefetchScalarGridSpec(
            num_scalar_prefetch=3,
            grid=(n_tiles_max,),
            in_specs=[
                pl.BlockSpec(memory_space=pl.ANY), pl.BlockSpec(memory_space=pl.ANY),
                pl.BlockSpec((EXPERTS_PER_GROUP, D_MODEL, D_EXPERT), group_of_tile),
                pl.BlockSpec((EXPERTS_PER_GROUP, D_MODEL, D_EXPERT), group_of_tile),
                pl.BlockSpec((EXPERTS_PER_GROUP, D_EXPERT, D_MODEL), group_of_tile),
            ],
            out_specs=pl.BlockSpec((TM, D_MODEL), lambda i, s, tg, nt: (i, 0)),
            scratch_shapes=[
                pltpu.VMEM((2, TM, D_MODEL), BF16), pltpu.VMEM((2, TM, ROUTER_W), F32),
                pltpu.SemaphoreType.DMA((2,)), pltpu.SemaphoreType.DMA((2,)),
            ],
        ),
        out_shape=jax.ShapeDtypeStruct((n_tiles_max * TM, D_MODEL), BF16),
        compiler_params=pltpu.CompilerParams(
            dimension_semantics=("arbitrary",), vmem_limit_bytes=V7X_VMEM_LIMIT),
    )(src, tile_group, n_tiles, h2s, gs, w_gate, w_up, w_down)


def _final_call(dest, x1, rinfo, ys, gfin, blocks_p):
    n_blocks = x1.shape[0] // TB
    assert blocks_p % FINAL_BLOCKS == 0 and n_blocks % FINAL_BLOCKS == 0
    steps_p = blocks_p // FINAL_BLOCKS
    rows = FINAL_BLOCKS * TB
    return pl.pallas_call(
        functools.partial(_final_kernel, steps_p),
        grid_spec=pltpu.PrefetchScalarGridSpec(
            num_scalar_prefetch=1,
            grid=(n_blocks // FINAL_BLOCKS,),
            in_specs=[
                pl.BlockSpec((rows, D_MODEL), lambda i, d: (i, 0)),
                pl.BlockSpec((rows, ROUTER_W), lambda i, d: (i, 0)),
                pl.BlockSpec(memory_space=pl.ANY),
                pl.BlockSpec((1, D_MODEL), lambda i, d: (0, 0)),
            ],
            out_specs=[
                pl.BlockSpec((rows, D_MODEL), lambda i, d: (jnp.minimum(i, steps_p - 1), 0)),
                pl.BlockSpec((rows, D_MODEL), lambda i, d: (jnp.maximum(i - steps_p, 0), 0)),
            ],
            scratch_shapes=[pltpu.VMEM((2, FINAL_BLOCKS * SORT_ROWS, D_MODEL), BF16), pltpu.SemaphoreType.DMA((2,))],
        ),
        out_shape=[jax.ShapeDtypeStruct((blocks_p * TB, D_MODEL), F32),
                   jax.ShapeDtypeStruct(((n_blocks - blocks_p) * TB, D_MODEL), F32)],
        compiler_params=pltpu.CompilerParams(
            dimension_semantics=("arbitrary",), vmem_limit_bytes=V7X_VMEM_LIMIT),
    )(dest, x1, rinfo, ys, gfin)


def kernel(x_prompt, x_sample, state_ret, state_hgrn, norm_mix_g, w_in, hgrn_lb_logits, hgrn_norm_g, w_out,
           norm_ffn_g, w_router_group, b_router_group, w_router_expert, b_router_expert, w_exp_gate, w_exp_up,
           w_exp_down, norm_final_g):
    depth = w_in.shape[0]
    assert depth == 1 and hgrn_lb_logits.shape[0] == 2, "single-layer configuration only"
    bp, seq, d = x_prompt.shape
    db, dec_len, _ = x_sample.shape
    assert bp == 1 and d == D_MODEL and dec_len == CHUNK and seq % 256 == 0 and db % 4 == 0

    pad = ROUTER_W - N_GROUPS - N_EXPERTS
    wr = jnp.concatenate([w_router_group[0], w_router_expert[0], jnp.zeros((D_MODEL, pad), F32)], axis=1)
    br = jnp.concatenate([b_router_group[0], b_router_expert[0], jnp.zeros((pad,), F32)])[None, :]
    params = (norm_mix_g[0][None, :], w_in[0], hgrn_lb_logits,
              jnp.tile(hgrn_norm_g[0], N_HEADS)[None, :], w_out[0], norm_ffn_g[0][None, :], wr, br)

    zeros_state = jnp.zeros((1, N_HEADS, D_HEAD, D_HEAD), F32)
    blocks_p = seq // TB
    blocks_s = db * dec_len // TB
    n_blocks = blocks_p + blocks_s
    *shared, ret_p, hg_p, wg_bf, wu_bf, wd_bf = _mixer_call(
        x_prompt.reshape(seq, d), zeros_state, zeros_state, params, seg_len=TB, carry=True, pipelined=True,
        block_offset=0, total_blocks=n_blocks, expert_weights=(w_exp_gate[0], w_exp_up[0], w_exp_down[0]))
    x1, h2s, gs, rinfo, info, ret_s, hg_s = _mixer_call(
        x_sample.reshape(db * dec_len, d), state_ret[0], state_hgrn[0], params, seg_len=CHUNK, carry=False,
        pipelined=False,
        block_offset=blocks_p, total_blocks=n_blocks, shared=tuple(shared))

    n_tiles_max = -(-n_blocks * SLABS_PER_BLOCK // TILE_SLABS) + N_GROUPS
    src, dest, tile_group, n_tiles = _dispatch_tables(info, n_tiles_max)
    ys = _moe_call(src, tile_group, n_tiles, h2s, gs, wg_bf.reshape(w_exp_gate[0].shape),
                   wu_bf.reshape(w_exp_up[0].shape), wd_bf.reshape(w_exp_down[0].shape))
    y_p, y_s = _final_call(dest, x1, rinfo, ys, norm_final_g[None, :], blocks_p)
    return (y_p.reshape(bp, seq, d), y_s.reshape(db, dec_len, d), ret_p[None], hg_p[None], ret_s[None], hg_s[None])
```

```python
import functools

import numpy as np
import jax
import jax.numpy as jnp
from jax import lax
from jax.experimental import pallas as pl
from jax.experimental.pallas import tpu as pltpu

F32 = jnp.float32
BF16 = jnp.bfloat16

D_MODEL = 1024
N_HEADS = 8
D_HEAD = 64
GROUP_W = N_HEADS * D_HEAD
N_PAIRS = N_HEADS // 2
PAIR_W = 2 * D_HEAD
CHUNK = 64
SUB = 16
N_SUB = CHUNK // SUB
PAST_LEN = 2048
ROPE_BASE = 10000.0
RMS_EPS = 1e-6
LOG2E = 1.4426950408889634
SPAN_MAX = 80.0
N_GROUPS = 4
EXPERTS_PER_GROUP = 8
N_EXPERTS = N_GROUPS * EXPERTS_PER_GROUP
D_EXPERT = 256
ROUTER_W = 128
V7X_VMEM_LIMIT = 60 * 1024 * 1024
MXU_DIM = 256
TB = 256
SLAB = 16
SLABS_PER_BLOCK = TB // SLAB + N_GROUPS
SORT_ROWS = SLABS_PER_BLOCK * SLAB
SORT_W = 384
TILE_SLABS = 32
TM = TILE_SLABS * SLAB
FINAL_BLOCKS = 2

C_QR, C_KR, C_VR, C_GR, C_QG, C_FG, C_VG, C_GG = (i * GROUP_W for i in range(8))


def _dot(a, b):
    return jnp.dot(a, b, preferred_element_type=F32)


def _dot_nt(a, b):
    return lax.dot_general(a, b, (((1,), (1,)), ((), ())), preferred_element_type=F32)


def _dot_tn(a, b):
    return lax.dot_general(a, b, (((0,), (0,)), ((), ())), preferred_element_type=F32)


def _split(x, n):
    parts = []
    for _ in range(n):
        p = x.astype(BF16)
        parts.append(p)
        x = x - p.astype(F32)
    return parts


def _group_sum(x, ones_ref):
    xb = x.astype(BF16)
    return jnp.concatenate([_dot(xb[:, c * MXU_DIM:(c + 1) * MXU_DIM], ones_ref[...])
                            for c in range(GROUP_W // MXU_DIM)], axis=1)


class _ColumnGroups:
    def __init__(self, bufs):
        self.bufs = bufs

    def _locate(self, idx):
        rows, cols = idx
        g = cols.start // GROUP_W
        assert (cols.stop - 1) // GROUP_W == g
        return self.bufs[g], (rows, slice(cols.start - g * GROUP_W, cols.stop - g * GROUP_W))

    def __getitem__(self, idx):
        buf, at = self._locate(idx)
        return buf[at]

    def __setitem__(self, idx, value):
        buf, at = self._locate(idx)
        buf[at] = value


def _silu(x):
    return x * jax.nn.sigmoid(x)


N_MIXER_INPUTS = 21


def _mixer_kernel(cfg, *refs):
    pipelined, n_own, n_fill, n_alias, cast_w = cfg[-5:]
    n_body = n_own + (1 if pipelined else 0)
    n_in = N_MIXER_INPUTS + (3 if cast_w else 0)
    i = pl.program_id(0)

    @pl.when(i < n_body)
    def _():
        _mixer_body(cfg, *refs)

    if n_fill:
        @pl.when(i >= n_body)
        def _():
            for out_ref in refs[n_in + n_alias:n_in + n_alias + 5]:
                out_ref[...] = jnp.zeros_like(out_ref)


def _mixer_body(cfg, x_ref, gmix_ref, win_ref, invf_ref, sgn_ref, lbl_ref, hgn_ref, wout_ref, gffn_ref,
                wr_ref, br_ref, tri_ref, dret_ref, xi_ref, zeta_ref, gls_ref, ones_ref, stri_ref,
                dmask_ref, ret0_ref, hg0_ref, *rest):
    LS, NSEG, carry, pipelined, n_own, _, n_alias, cast_w = cfg
    w_chunks = ()
    if cast_w:
        w_chunks = tuple(zip(rest[:3], rest[3 + n_alias + 7:3 + n_alias + 10]))
        rest = rest[3:3 + n_alias + 7] + rest[3 + n_alias + 10:]
    (x1_ref, h2s_ref, gs_ref, rinfo_ref, info_ref, retout_ref, hgout_ref,
     *proj_bufs, kbuf, obuf, dbuf, xprev, qx, kx, qin_buf, kout_buf, qd, kd, pbuf, cos_t, sin_t, win_bf, wout_bf,
     rst, sst) = rest[n_alias:]
    proj = _ColumnGroups(proj_bufs)
    NCH = LS // CHUNK
    i = pl.program_id(0)
    lane128 = lax.broadcasted_iota(jnp.int32, (1, PAIR_W), 1)
    head_a = lane128 < D_HEAD
    bd_mask = (lax.broadcasted_iota(jnp.int32, (PAIR_W, PAIR_W), 0) < D_HEAD) == head_a

    def pair_state(ref, sg, p):
        z = jnp.zeros((D_HEAD, D_HEAD), F32)
        return jnp.concatenate([jnp.concatenate([ref[sg, 2 * p], z], axis=1),
                                jnp.concatenate([z, ref[sg, 2 * p + 1]], axis=1)], axis=0)

    @pl.when(i == 0)
    def _():
        for c in range(0, 8 * GROUP_W, GROUP_W):
            win_bf[:, c:c + GROUP_W] = win_ref[:, c:c + GROUP_W].astype(BF16)
        wout_bf[...] = wout_ref[...].astype(BF16)
        if carry:
            for p in range(N_PAIRS):
                rst[0, p] = pair_state(ret0_ref, 0, p)
                sst[0, p] = pair_state(hg0_ref, 0, p)
        row = lax.broadcasted_iota(jnp.int32, (TB, 1), 0)
        ang_row = (row & (LS - 1)).astype(F32) * invf_ref[...]
        cos_t[...] = jnp.cos(ang_row)
        sin_t[...] = jnp.sin(ang_row)
        if NCH > 1:
            qx[...] = jnp.zeros_like(qx)
            kx[...] = jnp.zeros_like(kx)
        if pipelined:
            for buf in (obuf, dbuf, xprev, proj_bufs[C_GR // GROUP_W], proj_bufs[C_GG // GROUP_W]):
                buf[...] = jnp.zeros_like(buf)

    if pipelined:
        _mixer_back(proj, obuf, dbuf, xprev, hgn_ref, ones_ref, wout_bf, gffn_ref, wr_ref, br_ref, stri_ref,
                    x1_ref, h2s_ref, gs_ref, rinfo_ref, info_ref)
    keep = i < n_own

    for w_ref, w_bf_ref in w_chunks:
        w_bf_ref[...] = w_ref[...].astype(BF16)

    x = x_ref[...]
    h = x * lax.rsqrt(jnp.mean(x * x, axis=-1, keepdims=True) + RMS_EPS) * gmix_ref[...]
    hb = h.astype(BF16)

    def project(c0, c1):
        for c in range(c0, c1, GROUP_W):
            proj[:, c:c + GROUP_W] = _dot(hb, win_bf[:, c:c + GROUP_W])

    project(C_QG, C_GG)
    project(C_QR, C_VR)

    la = lbl_ref[0:1, :]
    lb_ = lbl_ref[1:2, :]
    lmax = jnp.maximum(la, lb_)
    ea = jnp.exp(la - lmax)
    lbv = ea / (ea + jnp.exp(lb_ - lmax))
    qg = proj[:, C_QG:C_QG + GROUP_W]
    proj[:, C_QG:C_QG + GROUP_W] = _silu(qg)
    f = lbv + (1.0 - lbv) * jax.nn.sigmoid(proj[:, C_FG:C_FG + GROUP_W])
    kbuf[...] = 1.0 - f
    logf = jnp.log(f)
    for sg in range(NSEG):
        rows = slice(sg * LS, (sg + 1) * LS)
        parts = _split(logf[rows], 3)
        proj[rows, C_FG:C_FG + GROUP_W] = sum(_dot(tri_ref[...], p) for p in parts)

    project(C_VR, C_QG)
    project(C_GG, C_GG + GROUP_W)

    start = jnp.full((8, PAIR_W), i * TB if carry else PAST_LEN, jnp.int32).astype(F32)
    ang0 = (start * invf_ref[...])[0:1]
    c0 = jnp.cos(ang0)
    s0 = jnp.sin(ang0)
    cos = cos_t[...] * c0 - sin_t[...] * s0
    sin = (sin_t[...] * c0 + cos_t[...] * s0) * sgn_ref[...]
    first_half = (lane128 & (D_HEAD - 1)) < (D_HEAD // 2)
    for blk in range(2 * N_PAIRS):
        cols = slice(blk * PAIR_W, (blk + 1) * PAIR_W)
        xx = proj[:, cols]
        partner = jnp.where(first_half, pltpu.roll(xx, PAIR_W - D_HEAD // 2, 1), pltpu.roll(xx, D_HEAD // 2, 1))
        r = xx * cos + partner * sin
        if blk < N_PAIRS:
            r = r * (D_HEAD ** -0.5)
        proj[:, cols] = r

    ciota = lax.broadcasted_iota(jnp.int32, (CHUNK, 1), 0)

    def exact_chunk(c, _):
        r0 = pl.multiple_of(c * CHUNK, CHUNK)
        qc = proj[pl.ds(r0, CHUNK), C_QG:C_QG + GROUP_W]
        bc = proj[pl.ds(r0, CHUNK), C_FG:C_FG + GROUP_W]
        acc = [jnp.zeros((CHUNK, MXU_DIM), F32) for _ in range(2)]
        for sb in range(N_SUB):
            rs = r0 + sb * SUB
            for s in range(SUB):
                ks = kbuf[pl.ds(rs + s, 1), :]
                bs = proj[pl.ds(rs + s, 1), C_FG:C_FG + GROUP_W]
                p = qc * ks * jnp.exp(jnp.minimum(bc - bs, 0.0))
                pbuf[s * CHUNK:(s + 1) * CHUNK, :] = jnp.where(ciota >= sb * SUB + s, p, 0.0).astype(BF16)
            for hh in range(2):
                sc = _dot(pbuf[:, hh * MXU_DIM:(hh + 1) * MXU_DIM], ones_ref[...])
                for s in range(SUB):
                    vs = proj[pl.ds(rs + s, 1), C_VG + hh * MXU_DIM:C_VG + (hh + 1) * MXU_DIM]
                    acc[hh] = acc[hh] + sc[s * CHUNK:(s + 1) * CHUNK] * vs
        dbuf[pl.ds(r0, CHUNK), :] = jnp.concatenate(acc, axis=1)
        return 0

    b_all = proj[:, C_FG:C_FG + GROUP_W]
    first = jnp.concatenate([jnp.broadcast_to(b_all[r:r + 1, :], (CHUNK, GROUP_W)) for r in range(0, TB, CHUNK)],
                            axis=0)
    span = first - b_all
    risky = jnp.max(span) > SPAN_MAX
    q_local = proj[:, C_QG:C_QG + GROUP_W] * jnp.exp(-span)
    qd[...] = jnp.where(risky, 0.0, q_local).astype(BF16)
    kd[...] = (kbuf[...] * jnp.exp(jnp.minimum(span, SPAN_MAX))).astype(BF16)

    for sg in range(NSEG):
        r0 = sg * LS
        rows = slice(r0, r0 + LS)
        st = 0 if carry else sg
        qg_s = proj[rows, C_QG:C_QG + GROUP_W]
        kg_s = kbuf[rows, :]
        b_s = proj[rows, C_FG:C_FG + GROUP_W]

        for c in range(NCH - 1):
            e = c * CHUNK + CHUNK - 1
            bref = b_s[e:e + 1, :]
            q_rows = slice(e + 1, LS)
            k_rows = slice(c * CHUNK, e + 1)
            qt = (qg_s[q_rows] * jnp.exp(b_s[q_rows] - bref)).astype(BF16)
            kt = (kg_s[k_rows] * jnp.exp(bref - b_s[k_rows])).astype(BF16)
            for p in range(N_PAIRS):
                dst = slice((p * (NCH - 1) + c) * PAIR_W, (p * (NCH - 1) + c + 1) * PAIR_W)
                src = slice(p * PAIR_W, (p + 1) * PAIR_W)
                qx[r0 + q_rows.start:r0 + q_rows.stop, dst] = qt[:, src]
                kx[r0 + k_rows.start:r0 + k_rows.stop, dst] = kt[:, src]

        b_last = b_s[LS - 1:LS, :]
        qin_buf[rows, :] = (qg_s * jnp.exp(b_s)).astype(BF16)
        kout_buf[rows, :] = (kg_s * jnp.exp(b_last - b_s)).astype(BF16)

    def pair_unit(sg, p):
        r0 = sg * LS
        rows = slice(r0, r0 + LS)
        st = 0 if carry else sg
        pc = slice(p * PAIR_W, (p + 1) * PAIR_W)

        qb = proj[rows, C_QR + p * PAIR_W:C_QR + (p + 1) * PAIR_W].astype(BF16)
        kr = proj[rows, C_KR + p * PAIR_W:C_KR + (p + 1) * PAIR_W]
        kb = kr.astype(BF16)
        vb = proj[rows, C_VR + p * PAIR_W:C_VR + (p + 1) * PAIR_W].astype(BF16)
        zero = jnp.zeros_like(qb)
        s_a = _dot_nt(jnp.where(head_a, qb, zero), kb) * dret_ref[2 * p]
        s_b = _dot_nt(jnp.where(head_a, zero, qb), kb) * dret_ref[2 * p + 1]
        o_r = jnp.where(head_a, _dot(s_a.astype(BF16), vb), _dot(s_b.astype(BF16), vb))
        r_old = rst[0, p] if carry else pair_state(ret0_ref, sg, p)
        o_r = o_r + _dot(qb, r_old.astype(BF16)) * xi_ref[:, pc]
        u = _dot_tn((kr * zeta_ref[:, pc]).astype(BF16), vb)
        r_new = r_old * gls_ref[:, pc] + jnp.where(bd_mask, u, 0.0)
        if pipelined:
            r_new = jnp.where(keep, r_new, r_old)
        obuf[rows, pc] = o_r

        vg = proj[rows, C_VG + p * PAIR_W:C_VG + (p + 1) * PAIR_W].astype(BF16)
        qd_p = qd[rows, pc]
        kd_p = kd[rows, pc]
        zd = jnp.zeros_like(qd_p)
        g_a = _dot_nt(jnp.where(head_a, qd_p, zd), kd_p) * dmask_ref[...]
        g_b = _dot_nt(jnp.where(head_a, zd, qd_p), kd_p) * dmask_ref[...]
        if NCH > 1:
            xc = slice(p * (NCH - 1) * PAIR_W, (p + 1) * (NCH - 1) * PAIR_W)
            qx_p = qx[rows, xc]
            kx_p = kx[rows, xc]
            lane_x = lax.broadcasted_iota(jnp.int32, (1, qx_p.shape[1]), 1)
            head_a_x = (lane_x & (PAIR_W - 1)) < D_HEAD
            zx = jnp.zeros_like(qx_p)
            g_a = g_a + _dot_nt(jnp.where(head_a_x, qx_p, zx), kx_p)
            g_b = g_b + _dot_nt(jnp.where(head_a_x, zx, qx_p), kx_p)
        o_g = jnp.where(head_a, _dot(g_a.astype(BF16), vg), _dot(g_b.astype(BF16), vg))
        s_old = sst[0, p] if carry else pair_state(hg0_ref, sg, p)
        o_g = o_g + _dot(qin_buf[rows, pc], s_old.astype(BF16))
        ut = _dot_tn(kout_buf[rows, pc], vg)
        b_last = proj[r0 + LS - 1:r0 + LS, C_FG + p * PAIR_W:C_FG + (p + 1) * PAIR_W]
        s_decay = jnp.exp(jnp.broadcast_to(b_last, (PAIR_W, PAIR_W)).T)
        s_new = s_old * s_decay + jnp.where(bd_mask, ut, 0.0)
        if pipelined:
            s_new = jnp.where(keep, s_new, s_old)
        gc = slice(GROUP_W + p * PAIR_W, GROUP_W + (p + 1) * PAIR_W)
        obuf[rows, gc] = o_g

        if carry:
            rst[0, p] = r_new
            sst[0, p] = s_new
        for out_ref, new in ((retout_ref, r_new), (hgout_ref, s_new)):
            out_ref[st, 2 * p] = new[:D_HEAD, :D_HEAD]
            out_ref[st, 2 * p + 1] = new[D_HEAD:, D_HEAD:]

    for sg in range(NSEG):
        for p in range(N_PAIRS):
            pair_unit(sg, p)

    xprev[...] = x
    dbuf[...] = jnp.zeros_like(dbuf)

    @pl.when(risky)
    def _():
        lax.fori_loop(0, TB // CHUNK, exact_chunk, 0)

    if not pipelined:
        _mixer_back(proj, obuf, dbuf, xprev, hgn_ref, ones_ref, wout_bf, gffn_ref, wr_ref, br_ref, stri_ref,
                    x1_ref, h2s_ref, gs_ref, rinfo_ref, info_ref)


def _mixer_back(proj, obuf, dbuf, xprev, hgn_ref, ones_ref, wout_bf, gffn_ref, wr_ref, br_ref, stri_ref,
                x1_ref, h2s_ref, gs_ref, rinfo_ref, info_ref):
    o_r = obuf[:, 0:GROUP_W]
    mu = _group_sum(o_r, ones_ref) * (1.0 / D_HEAD)
    dlt = o_r - mu
    var = _group_sum(dlt * dlt, ones_ref) * (1.0 / D_HEAD)
    y_r = dlt * lax.rsqrt(var + RMS_EPS) * _silu(proj[:, C_GR:C_GR + GROUP_W])
    o_g = obuf[:, GROUP_W:2 * GROUP_W] + dbuf[...]
    ms = _group_sum(o_g * o_g, ones_ref) * (1.0 / D_HEAD)
    y_g = o_g * lax.rsqrt(ms + RMS_EPS) * hgn_ref[...] * _silu(proj[:, C_GG:C_GG + GROUP_W])
    mix = jnp.concatenate([y_r, y_g], axis=1).astype(BF16)
    x1 = xprev[...] + _dot(mix, wout_bf[...])
    x1_ref[...] = x1

    h2 = x1 * lax.rsqrt(jnp.mean(x1 * x1, axis=-1, keepdims=True) + RMS_EPS) * gffn_ref[...]
    h_hi, h_lo = _split(h2, 2)
    w_hi, w_lo = _split(wr_ref[...], 2)
    hi_terms = _dot(h_hi, jnp.concatenate([w_hi, w_lo], axis=1))
    logits = hi_terms[:, :ROUTER_W] + hi_terms[:, ROUTER_W:] + _dot(h_lo, w_hi) + br_ref[...]
    lane = lax.broadcasted_iota(jnp.int32, (1, ROUTER_W), 1).astype(F32)
    neg = -jnp.inf
    no_lane = float(ROUTER_W)
    gl = jnp.where(lane < N_GROUPS, logits, neg)
    gmax = jnp.max(gl, axis=-1, keepdims=True)
    g_idx = jnp.min(jnp.where(gl == gmax, lane, no_lane), axis=-1, keepdims=True)
    prob_g = 1.0 / jnp.sum(jnp.exp(gl - gmax), axis=-1, keepdims=True)
    e_lo = N_GROUPS + EXPERTS_PER_GROUP * g_idx
    el = jnp.where((lane >= e_lo) & (lane < e_lo + EXPERTS_PER_GROUP), logits, neg)
    v1 = jnp.max(el, axis=-1, keepdims=True)
    i1 = jnp.min(jnp.where(el == v1, lane, no_lane), axis=-1, keepdims=True)
    el2 = jnp.where(lane == i1, neg, el)
    v2 = jnp.max(el2, axis=-1, keepdims=True)
    i2 = jnp.min(jnp.where(el2 == v2, lane, no_lane), axis=-1, keepdims=True)
    t = jnp.exp(v2 - v1)
    p1 = 1.0 / (1.0 + t)
    p2 = t * p1
    gate = jnp.where(lane == i1, prob_g * p1, 0.0) + jnp.where(lane == i2, prob_g * p2, 0.0)

    onehot = jnp.where(lane == g_idx, 1.0, 0.0)
    before = _dot(stri_ref[...], onehot.astype(BF16))
    count = jnp.sum(onehot, axis=0, keepdims=True)
    padded = jnp.floor((count + (SLAB - 1.0)) * (1.0 / SLAB)) * SLAB
    padded8 = jnp.broadcast_to(padded, (8, ROUTER_W))
    seg_start = sum(jnp.where(lane >= k, pltpu.roll(padded8, k, 1), 0.0) for k in range(1, N_GROUPS))[0:1]
    rank = jnp.sum(onehot * (seg_start + before), axis=-1, keepdims=True)
    col = lax.broadcasted_iota(jnp.int32, (1, SORT_W), 1).astype(F32)
    place = jnp.where(col == rank, 1.0, 0.0).astype(BF16)
    h2s_ref[...] = _dot_tn(place, h2.astype(BF16))[:SORT_ROWS].astype(BF16)
    gs_ref[...] = sum(_dot_tn(place, part) for part in _split(gate, 3))[:SORT_ROWS]
    rinfo_ref[...] = jnp.where(lane == 0, rank, 0.0)
    slab_lo = (lane * SLAB).astype(F32)
    slab_group = jnp.full((1, ROUTER_W), -1.0, F32)
    for g in range(N_GROUPS):
        s_g = jnp.sum(jnp.where(lane == g, seg_start, 0.0), axis=-1, keepdims=True)
        n_g = jnp.sum(jnp.where(lane == g, padded, 0.0), axis=-1, keepdims=True)
        slab_group = jnp.where((slab_lo >= s_g) & (slab_lo < s_g + n_g), float(g), slab_group)
    info_ref[0] = jnp.broadcast_to(slab_group.astype(jnp.int32), (8, ROUTER_W))


def _slab_gather(table_ref, first, n_slabs, srcs, bufs, sems, slot, *, wait):
    n_rows = n_slabs * SLAB
    for src, buf, sem in zip(srcs, bufs, sems):
        if wait:
            pltpu.make_async_copy(src.at[pl.ds(0, n_rows)], buf.at[slot, pl.ds(0, n_rows)], sem.at[slot]).wait()
            continue
        for j in range(n_slabs):
            row = pl.multiple_of(jnp.maximum(table_ref[first + j], 0) * SLAB, SLAB)
            pltpu.make_async_copy(src.at[pl.ds(row, SLAB)], buf.at[slot, pl.ds(j * SLAB, SLAB)],
                                  sem.at[slot]).start()


def _moe_kernel(src_ref, tg_ref, nt_ref, h2s_hbm, gs_hbm, wg_ref, wu_ref, wd_ref, ys_ref, hbuf, gbuf, sem_h, sem_g):
    i = pl.program_id(0)
    n_tiles = nt_ref[0]
    slot = lax.rem(i, 2)
    gather = functools.partial(_slab_gather, src_ref, srcs=(h2s_hbm, gs_hbm), bufs=(hbuf, gbuf),
                               sems=(sem_h, sem_g), n_slabs=TILE_SLABS)

    @pl.when(i == 0)
    def _():
        gather(first=0, slot=0, wait=False)

    @pl.when(i < n_tiles)
    def _():
        gather(first=i * TILE_SLABS, slot=slot, wait=True)

        @pl.when(i + 1 < n_tiles)
        def _():
            gather(first=(i + 1) * TILE_SLABS, slot=1 - slot, wait=False)

        hb = hbuf[slot]
        gates = gbuf[slot]
        lane = lax.broadcasted_iota(jnp.int32, (1, ROUTER_W), 1)
        first_lane = N_GROUPS + tg_ref[i] * EXPERTS_PER_GROUP
        acc = jnp.zeros((TM, D_MODEL), F32)
        for e in range(EXPERTS_PER_GROUP):
            he = (_silu(_dot(hb, wg_ref[e])) * _dot(hb, wu_ref[e])).astype(BF16)
            gcol = jnp.sum(jnp.where(lane == first_lane + e, gates, 0.0), axis=-1, keepdims=True)
            acc = acc + gcol * _dot(he, wd_ref[e])
        ys_ref[...] = acc.astype(BF16)

    @pl.when(i >= n_tiles)
    def _():
        ys_ref[...] = jnp.zeros_like(ys_ref)


def _final_kernel(steps_p, dest_ref, x1_ref, rinfo_ref, ys_hbm, gfin_ref, yp_ref, ysmp_ref, ybuf, sem):
    i = pl.program_id(0)
    slot = lax.rem(i, 2)
    n_slabs = FINAL_BLOCKS * SLABS_PER_BLOCK
    gather = functools.partial(_slab_gather, dest_ref, srcs=(ys_hbm,), bufs=(ybuf,), sems=(sem,), n_slabs=n_slabs)

    @pl.when(i == 0)
    def _():
        gather(first=0, slot=0, wait=False)

    gather(first=i * n_slabs, slot=slot, wait=True)

    @pl.when(i + 1 < pl.num_programs(0))
    def _():
        gather(first=(i + 1) * n_slabs, slot=1 - slot, wait=False)

    yb = ybuf[slot]
    col = lax.broadcasted_iota(jnp.int32, (1, FINAL_BLOCKS * SORT_ROWS), 1).astype(F32)
    moe = []
    for b in range(FINAL_BLOCKS):
        rank = rinfo_ref[b * TB:(b + 1) * TB, 0:1] + float(b * SORT_ROWS)
        moe.append(_dot(jnp.where(col == rank, 1.0, 0.0).astype(BF16), yb))
    xo = x1_ref[...] + jnp.concatenate(moe, axis=0)
    y = xo * lax.rsqrt(jnp.mean(xo * xo, axis=-1, keepdims=True) + RMS_EPS) * gfin_ref[...]

    @pl.when(i < steps_p)
    def _():
        yp_ref[...] = y

    @pl.when(i >= steps_p)
    def _():
        ysmp_ref[...] = y


def _const_spec(shape, pipeline_mode=None):
    nd = len(shape)
    return pl.BlockSpec(shape, lambda i: (0,) * nd, pipeline_mode=pipeline_mode)


def _mixer_call(x2d, ret0, hg0, params, *, seg_len, carry, pipelined, block_offset, total_blocks, shared=(),
                expert_weights=()):
    (gmix, win, lbl, hgn, wout, gffn, wr, br) = params
    T = x2d.shape[0]
    LS = seg_len
    n_seg = TB // LS
    NCH = LS // CHUNK
    n_states = ret0.shape[0]
    n_own = T // TB
    n_fill = 0 if shared else total_blocks - n_own - block_offset
    lag = 1 if pipelined else 0
    grid = n_own + lag + n_fill

    f32 = np.float32
    lg = np.log1p(-(f32(2.0) ** (f32(-5.0) - np.arange(N_HEADS, dtype=f32)))).astype(f32)
    tt = np.arange(LS, dtype=f32)
    ch = np.arange(LS) // CHUNK
    dret = np.exp(np.abs(tt[:, None] - tt[None, :])[None] * lg[:, None, None]).astype(f32)
    dret = np.where((ch[None, :] <= ch[:, None])[None], dret, f32(0.0))
    lg_lane = np.repeat(lg, D_HEAD)[None, :]
    xi = np.exp((tt[:, None] + f32(1.0)) * lg_lane).astype(f32)
    zeta = np.exp((f32(LS - 1.0) - tt)[:, None] * lg_lane).astype(f32)
    gls = np.exp(f32(LS) * lg_lane).astype(f32)
    half = D_HEAD // 2
    inv_freq = ROPE_BASE ** (-jnp.arange(half, dtype=F32) / half)
    invf = jnp.tile(inv_freq, PAIR_W // half)[None, :]
    sgn = np.where((np.arange(PAIR_W) % D_HEAD) < half, f32(-1.0), f32(1.0))[None, :]
    tri = jnp.asarray(np.tril(np.ones((LS, LS), np.float32)), BF16)
    lane_head = np.arange(MXU_DIM) // D_HEAD
    ones_bd = jnp.asarray((lane_head[:, None] == lane_head[None, :]).astype(np.float32), BF16)
    dmask = jnp.asarray(((ch[:, None] == ch[None, :]) & (np.arange(LS)[:, None] >= np.arange(LS)[None, :]))
                        .astype(np.float32))

    if carry:
        state_spec = pl.BlockSpec((1, N_HEADS, D_HEAD, D_HEAD), lambda i: (0, 0, 0, 0))
    else:
        state_spec = pl.BlockSpec((n_seg, N_HEADS, D_HEAD, D_HEAD), lambda i: (i, 0, 0, 0))
    single = pl.Buffered(1)
    stri = jnp.asarray(np.tril(np.ones((TB, TB), np.float32), -1), BF16)
    in_row_spec = pl.BlockSpec((TB, D_MODEL), lambda i: (jnp.minimum(i, n_own - 1), 0))
    out_block = lambda i: jnp.maximum(i - lag, 0) + block_offset
    out_row_spec = lambda rows, w: pl.BlockSpec((rows, w), lambda i: (out_block(i), 0))
    nx = max(NCH - 1, 1) * N_PAIRS * PAIR_W

    cast_w = bool(expert_weights)
    w2d = [w.reshape(-1, w.shape[-1]) for w in expert_weights]
    w_specs = [pl.BlockSpec((w.shape[0] // n_own, w.shape[1]), lambda i: (jnp.minimum(i, n_own - 1), 0)) for w in w2d]
    w_shapes = [jax.ShapeDtypeStruct(w.shape, BF16) for w in w2d]
    n_in = N_MIXER_INPUTS + len(w2d)

    kern = functools.partial(_mixer_kernel, (LS, n_seg, carry, pipelined, n_own, n_fill, len(shared), cast_w))
    return pl.pallas_call(
        kern,
        grid=(grid,),
        in_specs=[
            in_row_spec,
            _const_spec((1, D_MODEL)), _const_spec(win.shape, single), _const_spec((1, PAIR_W)),
            _const_spec((1, PAIR_W)), _const_spec(lbl.shape), _const_spec((1, GROUP_W)),
            _const_spec(wout.shape, single), _const_spec((1, D_MODEL)),
            _const_spec(wr.shape), _const_spec((1, ROUTER_W)),
            _const_spec((LS, LS)), _const_spec((N_HEADS, LS, LS)), _const_spec((LS, GROUP_W)),
            _const_spec((LS, GROUP_W)), _const_spec((1, GROUP_W)), _const_spec((MXU_DIM, MXU_DIM)),
            _const_spec((TB, TB)), _const_spec((LS, LS)),
            state_spec, state_spec,
        ] + w_specs + [pl.BlockSpec(memory_space=pl.ANY)] * len(shared),
        out_specs=[out_row_spec(TB, D_MODEL), out_row_spec(SORT_ROWS, D_MODEL), out_row_spec(SORT_ROWS, ROUTER_W),
                   out_row_spec(TB, ROUTER_W),
                   pl.BlockSpec((1, 8, ROUTER_W), lambda i: (out_block(i), 0, 0)),
                   state_spec, state_spec] + w_specs,
        out_shape=[
            jax.ShapeDtypeStruct((total_blocks * TB, D_MODEL), F32),
            jax.ShapeDtypeStruct((total_blocks * SORT_ROWS, D_MODEL), BF16),
            jax.ShapeDtypeStruct((total_blocks * SORT_ROWS, ROUTER_W), F32),
            jax.ShapeDtypeStruct((total_blocks * TB, ROUTER_W), F32),
            jax.ShapeDtypeStruct((total_blocks, 8, ROUTER_W), jnp.int32),
            jax.ShapeDtypeStruct((n_states, N_HEADS, D_HEAD, D_HEAD), F32),
            jax.ShapeDtypeStruct((n_states, N_HEADS, D_HEAD, D_HEAD), F32),
        ] + w_shapes,
        input_output_aliases={n_in + k: k for k in range(len(shared))},
        scratch_shapes=[
            *[pltpu.VMEM((TB, GROUP_W), F32) for _ in range(8)],
            pltpu.VMEM((TB, GROUP_W), F32),
            pltpu.VMEM((TB, 2 * GROUP_W), F32),
            pltpu.VMEM((TB, GROUP_W), F32),
            pltpu.VMEM((TB, D_MODEL), F32),
            pltpu.VMEM((TB, nx), BF16), pltpu.VMEM((TB, nx), BF16),
            pltpu.VMEM((TB, GROUP_W), BF16), pltpu.VMEM((TB, GROUP_W), BF16),
            pltpu.VMEM((TB, GROUP_W), BF16), pltpu.VMEM((TB, GROUP_W), BF16),
            pltpu.VMEM((SUB * CHUNK, GROUP_W), BF16),
            pltpu.VMEM((TB, PAIR_W), F32), pltpu.VMEM((TB, PAIR_W), F32),
            pltpu.VMEM(win.shape, BF16), pltpu.VMEM(wout.shape, BF16),
            pltpu.VMEM((1, N_PAIRS, PAIR_W, PAIR_W), F32),
            pltpu.VMEM((1, N_PAIRS, PAIR_W, PAIR_W), F32),
        ],
        compiler_params=pltpu.CompilerParams(
            dimension_semantics=("arbitrary",), vmem_limit_bytes=V7X_VMEM_LIMIT),
    )(x2d, gmix, win, invf, sgn, lbl, hgn, wout, gffn, wr, br, tri, dret, xi, zeta, gls, ones_bd, stri, dmask,
      ret0, hg0, *w2d, *shared)


def _dispatch_tables(info, n_tiles_max):
    slab_group = info[:, 0, :SLABS_PER_BLOCK].reshape(-1)
    n_slabs = slab_group.shape[0]
    valid = slab_group >= 0
    onehot = (slab_group[:, None] == jnp.arange(N_GROUPS, dtype=jnp.int32)[None, :]).astype(jnp.int32)
    within = jnp.cumsum(onehot, axis=0) - onehot
    tiles = (jnp.sum(onehot, axis=0) + TILE_SLABS - 1) // TILE_SLABS
    tile_end = jnp.cumsum(tiles)
    g = jnp.clip(slab_group, 0, N_GROUPS - 1)
    dest = (tile_end - tiles)[g] * TILE_SLABS + jnp.take_along_axis(within, g[:, None], axis=1)[:, 0]
    dest = jnp.where(valid, dest, -1).astype(jnp.int32)
    n_slots = n_tiles_max * TILE_SLABS
    src = jnp.full((n_slots,), -1, jnp.int32).at[jnp.where(valid, dest, n_slots)].set(
        jnp.arange(n_slabs, dtype=jnp.int32), mode="drop")
    tile_group = jnp.sum(jnp.arange(n_tiles_max, dtype=jnp.int32)[:, None] >= tile_end[None, :], axis=1)
    tile_group = jnp.minimum(tile_group, N_GROUPS - 1).astype(jnp.int32)
    return src, dest, tile_group, tile_end[-1:].astype(jnp.int32)


def _moe_call(src, tile_group, n_tiles, h2s, gs, w_gate, w_up, w_down):
    n_tiles_max = tile_group.shape[0]
    group_of_tile = lambda i, s, tg, nt: (tg[i], 0, 0)
    return pl.pallas_call(
        _moe_kernel,
        grid_spec=pltpu.PrefetchScalarGridSpec(
            num_scalar_prefetch=3,
            grid=(n_tiles_max,),
            in_specs=[
                pl.BlockSpec(memory_space=pl.ANY), pl.BlockSpec(memory_space=pl.ANY),
                pl.BlockSpec((EXPERTS_PER_GROUP, D_MODEL, D_EXPERT), group_of_tile),
                pl.BlockSpec((EXPERTS_PER_GROUP, D_MODEL, D_EXPERT), group_of_tile),
                pl.BlockSpec((EXPERTS_PER_GROUP, D_EXPERT, D_MODEL), group_of_tile),
            ],
            out_specs=pl.BlockSpec((TM, D_MODEL), lambda i, s, tg, nt: (i, 0)),
            scratch_shapes=[
                pltpu.VMEM((2, TM, D_MODEL), BF16), pltpu.VMEM((2, TM, ROUTER_W), F32),
                pltpu.SemaphoreType.DMA((2,)), pltpu.SemaphoreType.DMA((2,)),
            ],
        ),
        out_shape=jax.ShapeDtypeStruct((n_tiles_max * TM, D_MODEL), BF16),
        compiler_params=pltpu.CompilerParams(
            dimension_semantics=("arbitrary",), vmem_limit_bytes=V7X_VMEM_LIMIT),
    )(src, tile_group, n_tiles, h2s, gs, w_gate, w_up, w_down)


def _final_call(dest, x1, rinfo, ys, gfin, blocks_p):
    n_blocks = x1.shape[0] // TB
    assert blocks_p % FINAL_BLOCKS == 0 and n_blocks % FINAL_BLOCKS == 0
    steps_p = blocks_p // FINAL_BLOCKS
    rows = FINAL_BLOCKS * TB
    return pl.pallas_call(
        functools.partial(_final_kernel, steps_p),
        grid_spec=pltpu.PrefetchScalarGridSpec(
            num_scalar_prefetch=1,
            grid=(n_blocks // FINAL_BLOCKS,),
            in_specs=[
                pl.BlockSpec((rows, D_MODEL), lambda i, d: (i, 0)),
                pl.BlockSpec((rows, ROUTER_W), lambda i, d: (i, 0)),
                pl.BlockSpec(memory_space=pl.ANY),
                pl.BlockSpec((1, D_MODEL), lambda i, d: (0, 0)),
            ],
            out_specs=[
                pl.BlockSpec((rows, D_MODEL), lambda i, d: (jnp.minimum(i, steps_p - 1), 0)),
                pl.BlockSpec((rows, D_MODEL), lambda i, d: (jnp.maximum(i - steps_p, 0), 0)),
            ],
            scratch_shapes=[pltpu.VMEM((2, FINAL_BLOCKS * SORT_ROWS, D_MODEL), BF16), pltpu.SemaphoreType.DMA((2,))],
        ),
        out_shape=[jax.ShapeDtypeStruct((blocks_p * TB, D_MODEL), F32),
                   jax.ShapeDtypeStruct(((n_blocks - blocks_p) * TB, D_MODEL), F32)],
        compiler_params=pltpu.CompilerParams(
            dimension_semantics=("arbitrary",), vmem_limit_bytes=V7X_VMEM_LIMIT),
    )(dest, x1, rinfo, ys, gfin)


def kernel(x_prompt, x_sample, state_ret, state_hgrn, norm_mix_g, w_in, hgrn_lb_logits, hgrn_norm_g, w_out,
           norm_ffn_g, w_router_group, b_router_group, w_router_expert, b_router_expert, w_exp_gate, w_exp_up,
           w_exp_down, norm_final_g):
    depth = w_in.shape[0]
    assert depth == 1 and hgrn_lb_logits.shape[0] == 2, "single-layer configuration only"
    bp, seq, d = x_prompt.shape
    db, dec_len, _ = x_sample.shape
    assert bp == 1 and d == D_MODEL and dec_len == CHUNK and seq % 256 == 0 and db % 4 == 0

    pad = ROUTER_W - N_GROUPS - N_EXPERTS
    wr = jnp.concatenate([w_router_group[0], w_router_expert[0], jnp.zeros((D_MODEL, pad), F32)], axis=1)
    br = jnp.concatenate([b_router_group[0], b_router_expert[0], jnp.zeros((pad,), F32)])[None, :]
    params = (norm_mix_g[0][None, :], w_in[0], hgrn_lb_logits,
              jnp.tile(hgrn_norm_g[0], N_HEADS)[None, :], w_out[0], norm_ffn_g[0][None, :], wr, br)

    zeros_state = jnp.zeros((1, N_HEADS, D_HEAD, D_HEAD), F32)
    blocks_p = seq // TB
    blocks_s = db * dec_len // TB
    n_blocks = blocks_p + blocks_s
    *shared, ret_p, hg_p, wg_bf, wu_bf, wd_bf = _mixer_call(
        x_prompt.reshape(seq, d), zeros_state, zeros_state, params, seg_len=TB, carry=True, pipelined=True,
        block_offset=0, total_blocks=n_blocks, expert_weights=(w_exp_gate[0], w_exp_up[0], w_exp_down[0]))
    x1, h2s, gs, rinfo, info, ret_s, hg_s = _mixer_call(
        x_sample.reshape(db * dec_len, d), state_ret[0], state_hgrn[0], params, seg_len=CHUNK, carry=False,
        pipelined=False,
        block_offset=blocks_p, total_blocks=n_blocks, shared=tuple(shared))

    n_tiles_max = -(-n_blocks * SLABS_PER_BLOCK // TILE_SLABS) + N_GROUPS
    src, dest, tile_group, n_tiles = _dispatch_tables(info, n_tiles_max)
    ys = _moe_call(src, tile_group, n_tiles, h2s, gs, wg_bf.reshape(w_exp_gate[0].shape),
                   wu_bf.reshape(w_exp_up[0].shape), wd_bf.reshape(w_exp_down[0].shape))
    y_p, y_s = _final_call(dest, x1, rinfo, ys, norm_final_g[None, :], blocks_p)
    return (y_p.reshape(bp, seq, d), y_s.reshape(db, dec_len, d), ret_p[None], hg_p[None], ret_s[None], hg_s[None])
```

```python
import functools

import numpy as np
import jax
import jax.numpy as jnp
from jax import lax
from jax.experimental import pallas as pl
from jax.experimental.pallas import tpu as pltpu

F32 = jnp.float32
BF16 = jnp.bfloat16

D_MODEL = 1024
N_HEADS = 8
D_HEAD = 64
GROUP_W = N_HEADS * D_HEAD
N_PAIRS = N_HEADS // 2
PAIR_W = 2 * D_HEAD
CHUNK = 64
SUB = 16
N_SUB = CHUNK // SUB
PAST_LEN = 2048
ROPE_BASE = 10000.0
RMS_EPS = 1e-6
LOG2E = 1.4426950408889634
SPAN_MAX = 80.0
N_GROUPS = 4
EXPERTS_PER_GROUP = 8
N_EXPERTS = N_GROUPS * EXPERTS_PER_GROUP
D_EXPERT = 256
ROUTER_W = 128
V7X_VMEM_LIMIT = 60 * 1024 * 1024
MXU_DIM = 256
TB = 256
SLAB = 16
SLABS_PER_BLOCK = TB // SLAB + N_GROUPS
SORT_ROWS = SLABS_PER_BLOCK * SLAB
SORT_W = 384
TILE_SLABS = 32
TM = TILE_SLABS * SLAB
FINAL_BLOCKS = 2

C_QR, C_KR, C_VR, C_GR, C_QG, C_FG, C_VG, C_GG = (i * GROUP_W for i in range(8))


def _dot(a, b):
    return jnp.dot(a, b, preferred_element_type=F32)


def _dot_nt(a, b):
    return lax.dot_general(a, b, (((1,), (1,)), ((), ())), preferred_element_type=F32)


def _dot_tn(a, b):
    return lax.dot_general(a, b, (((0,), (0,)), ((), ())), preferred_element_type=F32)


def _split(x, n):
    parts = []
    for _ in range(n):
        p = x.astype(BF16)
        parts.append(p)
        x = x - p.astype(F32)
    return parts


def _group_sum(x, ones_ref):
    xb = x.astype(BF16)
    return jnp.concatenate([_dot(xb[:, c * MXU_DIM:(c + 1) * MXU_DIM], ones_ref[...])
                            for c in range(GROUP_W // MXU_DIM)], axis=1)


class _ColumnGroups:
    def __init__(self, bufs):
        self.bufs = bufs

    def _locate(self, idx):
        rows, cols = idx
        g = cols.start // GROUP_W
        assert (cols.stop - 1) // GROUP_W == g
        return self.bufs[g], (rows, slice(cols.start - g * GROUP_W, cols.stop - g * GROUP_W))

    def __getitem__(self, idx):
        buf, at = self._locate(idx)
        return buf[at]

    def __setitem__(self, idx, value):
        buf, at = self._locate(idx)
        buf[at] = value


def _silu(x):
    return x * jax.nn.sigmoid(x)


N_MIXER_INPUTS = 21


def _mixer_kernel(cfg, *refs):
    pipelined, n_own, n_fill, n_alias, cast_w = cfg[-5:]
    n_body = n_own + (1 if pipelined else 0)
    n_in = N_MIXER_INPUTS + (3 if cast_w else 0)
    i = pl.program_id(0)

    @pl.when(i < n_body)
    def _():
        _mixer_body(cfg, *refs)

    if n_fill:
        @pl.when(i >= n_body)
        def _():
            for out_ref in refs[n_in + n_alias:n_in + n_alias + 5]:
                out_ref[...] = jnp.zeros_like(out_ref)


def _mixer_body(cfg, x_ref, gmix_ref, win_ref, invf_ref, sgn_ref, lbl_ref, hgn_ref, wout_ref, gffn_ref,
                wr_ref, br_ref, tri_ref, dret_ref, xi_ref, zeta_ref, gls_ref, ones_ref, stri_ref,
                dmask_ref, ret0_ref, hg0_ref, *rest):
    LS, NSEG, carry, pipelined, n_own, _, n_alias, cast_w = cfg
    w_chunks = ()
    if cast_w:
        w_chunks = tuple(zip(rest[:3], rest[3 + n_alias + 7:3 + n_alias + 10]))
        rest = rest[3:3 + n_alias + 7] + rest[3 + n_alias + 10:]
    (x1_ref, h2s_ref, gs_ref, rinfo_ref, info_ref, retout_ref, hgout_ref,
     *proj_bufs, kbuf, obuf, dbuf, xprev, qin_buf, kout_buf, qd, kd, pbuf, cos_t, sin_t, win_bf, wout_bf,
     rst, sst) = rest[n_alias:]
    proj = _ColumnGroups(proj_bufs)
    NCH = LS // CHUNK
    i = pl.program_id(0)
    lane128 = lax.broadcasted_iota(jnp.int32, (1, PAIR_W), 1)
    head_a = lane128 < D_HEAD
    bd_mask = (lax.broadcasted_iota(jnp.int32, (PAIR_W, PAIR_W), 0) < D_HEAD) == head_a

    def pair_state(ref, sg, p):
        z = jnp.zeros((D_HEAD, D_HEAD), F32)
        return jnp.concatenate([jnp.concatenate([ref[sg, 2 * p], z], axis=1),
                                jnp.concatenate([z, ref[sg, 2 * p + 1]], axis=1)], axis=0)

    @pl.when(i == 0)
    def _():
        for c in range(0, 8 * GROUP_W, GROUP_W):
            win_bf[:, c:c + GROUP_W] = win_ref[:, c:c + GROUP_W].astype(BF16)
        wout_bf[...] = wout_ref[...].astype(BF16)
        if carry:
            for p in range(N_PAIRS):
                rst[0, p] = pair_state(ret0_ref, 0, p)
                sst[0, p] = pair_state(hg0_ref, 0, p)
        row = lax.broadcasted_iota(jnp.int32, (TB, 1), 0)
        ang_row = (row & (LS - 1)).astype(F32) * invf_ref[...]
        cos_t[...] = jnp.cos(ang_row)
        sin_t[...] = jnp.sin(ang_row)
        if pipelined:
            for buf in (obuf, dbuf, xprev, proj_bufs[C_GR // GROUP_W], proj_bufs[C_GG // GROUP_W]):
                buf[...] = jnp.zeros_like(buf)

    if pipelined:
        _mixer_back(proj, obuf, dbuf, xprev, hgn_ref, ones_ref, wout_bf, gffn_ref, wr_ref, br_ref, stri_ref,
                    x1_ref, h2s_ref, gs_ref, rinfo_ref, info_ref)
    keep = i < n_own

    for w_ref, w_bf_ref in w_chunks:
        w_bf_ref[...] = w_ref[...].astype(BF16)

    x = x_ref[...]
    h = x * lax.rsqrt(jnp.mean(x * x, axis=-1, keepdims=True) + RMS_EPS) * gmix_ref[...]
    hb = h.astype(BF16)

    def project(c0, c1):
        for c in range(c0, c1, GROUP_W):
            proj[:, c:c + GROUP_W] = _dot(hb, win_bf[:, c:c + GROUP_W])

    project(C_QG, C_GG)
    project(C_QR, C_VR)

    la = lbl_ref[0:1, :]
    lb_ = lbl_ref[1:2, :]
    lmax = jnp.maximum(la, lb_)
    ea = jnp.exp(la - lmax)
    lbv = ea / (ea + jnp.exp(lb_ - lmax))
    qg = proj[:, C_QG:C_QG + GROUP_W]
    proj[:, C_QG:C_QG + GROUP_W] = _silu(qg)
    f = lbv + (1.0 - lbv) * jax.nn.sigmoid(proj[:, C_FG:C_FG + GROUP_W])
    kbuf[...] = 1.0 - f
    logf = jnp.log(f)
    for sg in range(NSEG):
        rows = slice(sg * LS, (sg + 1) * LS)
        parts = _split(logf[rows], 3)
        proj[rows, C_FG:C_FG + GROUP_W] = sum(_dot(tri_ref[...], p) for p in parts)

    project(C_VR, C_QG)
    project(C_GG, C_GG + GROUP_W)

    start = jnp.full((8, PAIR_W), i * TB if carry else PAST_LEN, jnp.int32).astype(F32)
    ang0 = (start * invf_ref[...])[0:1]
    c0 = jnp.cos(ang0)
    s0 = jnp.sin(ang0)
    cos = cos_t[...] * c0 - sin_t[...] * s0
    sin = (sin_t[...] * c0 + cos_t[...] * s0) * sgn_ref[...]
    first_half = (lane128 & (D_HEAD - 1)) < (D_HEAD // 2)
    for blk in range(2 * N_PAIRS):
        cols = slice(blk * PAIR_W, (blk + 1) * PAIR_W)
        xx = proj[:, cols]
        partner = jnp.where(first_half, pltpu.roll(xx, PAIR_W - D_HEAD // 2, 1), pltpu.roll(xx, D_HEAD // 2, 1))
        r = xx * cos + partner * sin
        if blk < N_PAIRS:
            r = r * (D_HEAD ** -0.5)
        proj[:, cols] = r

    ciota = lax.broadcasted_iota(jnp.int32, (CHUNK, 1), 0)

    def exact_chunk(c, _):
        r0 = pl.multiple_of(c * CHUNK, CHUNK)
        qc = proj[pl.ds(r0, CHUNK), C_QG:C_QG + GROUP_W]
        bc = proj[pl.ds(r0, CHUNK), C_FG:C_FG + GROUP_W]
        acc = [jnp.zeros((CHUNK, MXU_DIM), F32) for _ in range(2)]
        for sb in range(N_SUB):
            rs = r0 + sb * SUB
            for s in range(SUB):
                ks = kbuf[pl.ds(rs + s, 1), :]
                bs = proj[pl.ds(rs + s, 1), C_FG:C_FG + GROUP_W]
                p = qc * ks * jnp.exp(jnp.minimum(bc - bs, 0.0))
                pbuf[s * CHUNK:(s + 1) * CHUNK, :] = jnp.where(ciota >= sb * SUB + s, p, 0.0).astype(BF16)
            for hh in range(2):
                sc = _dot(pbuf[:, hh * MXU_DIM:(hh + 1) * MXU_DIM], ones_ref[...])
                for s in range(SUB):
                    vs = proj[pl.ds(rs + s, 1), C_VG + hh * MXU_DIM:C_VG + (hh + 1) * MXU_DIM]
                    acc[hh] = acc[hh] + sc[s * CHUNK:(s + 1) * CHUNK] * vs
        dbuf[pl.ds(r0, CHUNK), :] = jnp.concatenate(acc, axis=1)
        return 0

    b_all = proj[:, C_FG:C_FG + GROUP_W]
    first = jnp.concatenate([jnp.broadcast_to(b_all[r:r + 1, :], (CHUNK, GROUP_W)) for r in range(0, TB, CHUNK)],
                            axis=0)
    span = first - b_all
    risky = jnp.max(span) > SPAN_MAX
    q_local = proj[:, C_QG:C_QG + GROUP_W] * jnp.exp(-span)
    qd[...] = jnp.where(risky, 0.0, q_local).astype(BF16)
    kd[...] = (kbuf[...] * jnp.exp(jnp.minimum(span, SPAN_MAX))).astype(BF16)

    for sg in range(NSEG):
        r0 = sg * LS
        rows = slice(r0, r0 + LS)
        st = 0 if carry else sg
        qg_s = proj[rows, C_QG:C_QG + GROUP_W]
        kg_s = kbuf[rows, :]
        b_s = proj[rows, C_FG:C_FG + GROUP_W]

        for c in range(NCH):
            cr = slice(c * CHUNK, (c + 1) * CHUNK)
            b_c = b_s[cr]
            b_in = b_s[c * CHUNK - 1:c * CHUNK, :] if c else jnp.zeros((1, GROUP_W), F32)
            b_out = b_s[(c + 1) * CHUNK - 1:(c + 1) * CHUNK, :]
            qin_buf[r0 + cr.start:r0 + cr.stop, :] = (qg_s[cr] * jnp.exp(b_c - b_in)).astype(BF16)
            kout_buf[r0 + cr.start:r0 + cr.stop, :] = (kg_s[cr] * jnp.exp(b_out - b_c)).astype(BF16)

    def pair_unit(sg, p):
        r0 = sg * LS
        rows = slice(r0, r0 + LS)
        st = 0 if carry else sg
        pc = slice(p * PAIR_W, (p + 1) * PAIR_W)

        qb = proj[rows, C_QR + p * PAIR_W:C_QR + (p + 1) * PAIR_W].astype(BF16)
        kr = proj[rows, C_KR + p * PAIR_W:C_KR + (p + 1) * PAIR_W]
        kb = kr.astype(BF16)
        vb = proj[rows, C_VR + p * PAIR_W:C_VR + (p + 1) * PAIR_W].astype(BF16)
        zero = jnp.zeros_like(qb)
        s_a = _dot_nt(jnp.where(head_a, qb, zero), kb) * dret_ref[2 * p]
        s_b = _dot_nt(jnp.where(head_a, zero, qb), kb) * dret_ref[2 * p + 1]
        o_r = jnp.where(head_a, _dot(s_a.astype(BF16), vb), _dot(s_b.astype(BF16), vb))
        r_old = rst[0, p] if carry else pair_state(ret0_ref, sg, p)
        o_r = o_r + _dot(qb, r_old.astype(BF16)) * xi_ref[:, pc]
        u = _dot_tn((kr * zeta_ref[:, pc]).astype(BF16), vb)
        r_new = r_old * gls_ref[:, pc] + jnp.where(bd_mask, u, 0.0)
        if pipelined:
            r_new = jnp.where(keep, r_new, r_old)
        obuf[rows, pc] = o_r

        vg = proj[rows, C_VG + p * PAIR_W:C_VG + (p + 1) * PAIR_W].astype(BF16)
        qd_p = qd[rows, pc]
        kd_p = kd[rows, pc]
        zd = jnp.zeros_like(qd_p)
        g_a = _dot_nt(jnp.where(head_a, qd_p, zd), kd_p) * dmask_ref[...]
        g_b = _dot_nt(jnp.where(head_a, zd, qd_p), kd_p) * dmask_ref[...]
        o_g = jnp.where(head_a, _dot(g_a.astype(BF16), vg), _dot(g_b.astype(BF16), vg))
        s_old = sst[0, p] if carry else pair_state(hg0_ref, sg, p)
        s_new = s_old
        from_state = []
        bcol = C_FG + p * PAIR_W
        for c in range(NCH):
            cr = slice(r0 + c * CHUNK, r0 + (c + 1) * CHUNK)
            from_state.append(_dot(qin_buf[cr, pc], s_new.astype(BF16)))
            ut = _dot_tn(kout_buf[cr, pc], vg[c * CHUNK:(c + 1) * CHUNK])
            b_out = proj[cr.stop - 1:cr.stop, bcol:bcol + PAIR_W]
            if c:
                b_out = b_out - proj[cr.start - 1:cr.start, bcol:bcol + PAIR_W]
            s_decay = jnp.exp(jnp.broadcast_to(b_out, (PAIR_W, PAIR_W)).T)
            s_new = s_new * s_decay + jnp.where(bd_mask, ut, 0.0)
        o_g = o_g + (jnp.concatenate(from_state, axis=0) if NCH > 1 else from_state[0])
        if pipelined:
            s_new = jnp.where(keep, s_new, s_old)
        gc = slice(GROUP_W + p * PAIR_W, GROUP_W + (p + 1) * PAIR_W)
        obuf[rows, gc] = o_g

        if carry:
            rst[0, p] = r_new
            sst[0, p] = s_new
        for out_ref, new in ((retout_ref, r_new), (hgout_ref, s_new)):
            out_ref[st, 2 * p] = new[:D_HEAD, :D_HEAD]
            out_ref[st, 2 * p + 1] = new[D_HEAD:, D_HEAD:]

    for sg in range(NSEG):
        for p in range(N_PAIRS):
            pair_unit(sg, p)

    xprev[...] = x
    dbuf[...] = jnp.zeros_like(dbuf)

    @pl.when(risky)
    def _():
        lax.fori_loop(0, TB // CHUNK, exact_chunk, 0)

    if not pipelined:
        _mixer_back(proj, obuf, dbuf, xprev, hgn_ref, ones_ref, wout_bf, gffn_ref, wr_ref, br_ref, stri_ref,
                    x1_ref, h2s_ref, gs_ref, rinfo_ref, info_ref)


def _mixer_back(proj, obuf, dbuf, xprev, hgn_ref, ones_ref, wout_bf, gffn_ref, wr_ref, br_ref, stri_ref,
                x1_ref, h2s_ref, gs_ref, rinfo_ref, info_ref):
    o_r = obuf[:, 0:GROUP_W]
    mu = _group_sum(o_r, ones_ref) * (1.0 / D_HEAD)
    dlt = o_r - mu
    var = _group_sum(dlt * dlt, ones_ref) * (1.0 / D_HEAD)
    y_r = dlt * lax.rsqrt(var + RMS_EPS) * _silu(proj[:, C_GR:C_GR + GROUP_W])
    o_g = obuf[:, GROUP_W:2 * GROUP_W] + dbuf[...]
    ms = _group_sum(o_g * o_g, ones_ref) * (1.0 / D_HEAD)
    y_g = o_g * lax.rsqrt(ms + RMS_EPS) * hgn_ref[...] * _silu(proj[:, C_GG:C_GG + GROUP_W])
    mix = jnp.concatenate([y_r, y_g], axis=1).astype(BF16)
    x1 = xprev[...] + _dot(mix, wout_bf[...])
    x1_ref[...] = x1

    h2 = x1 * lax.rsqrt(jnp.mean(x1 * x1, axis=-1, keepdims=True) + RMS_EPS) * gffn_ref[...]
    h_hi, h_lo = _split(h2, 2)
    w_hi, w_lo = _split(wr_ref[...], 2)
    hi_terms = _dot(h_hi, jnp.concatenate([w_hi, w_lo], axis=1))
    logits = hi_terms[:, :ROUTER_W] + hi_terms[:, ROUTER_W:] + _dot(h_lo, w_hi) + br_ref[...]
    lane = lax.broadcasted_iota(jnp.int32, (1, ROUTER_W), 1).astype(F32)
    neg = -jnp.inf
    no_lane = float(ROUTER_W)
    gl = jnp.where(lane < N_GROUPS, logits, neg)
    gmax = jnp.max(gl, axis=-1, keepdims=True)
    g_idx = jnp.min(jnp.where(gl == gmax, lane, no_lane), axis=-1, keepdims=True)
    prob_g = 1.0 / jnp.sum(jnp.exp(gl - gmax), axis=-1, keepdims=True)
    e_lo = N_GROUPS + EXPERTS_PER_GROUP * g_idx
    el = jnp.where((lane >= e_lo) & (lane < e_lo + EXPERTS_PER_GROUP), logits, neg)
    v1 = jnp.max(el, axis=-1, keepdims=True)
    i1 = jnp.min(jnp.where(el == v1, lane, no_lane), axis=-1, keepdims=True)
    el2 = jnp.where(lane == i1, neg, el)
    v2 = jnp.max(el2, axis=-1, keepdims=True)
    i2 = jnp.min(jnp.where(el2 == v2, lane, no_lane), axis=-1, keepdims=True)
    t = jnp.exp(v2 - v1)
    p1 = 1.0 / (1.0 + t)
    p2 = t * p1
    gate = jnp.where(lane == i1, prob_g * p1, 0.0) + jnp.where(lane == i2, prob_g * p2, 0.0)

    onehot = jnp.where(lane == g_idx, 1.0, 0.0)
    before = _dot(stri_ref[...], onehot.astype(BF16))
    count = jnp.sum(onehot, axis=0, keepdims=True)
    padded = jnp.floor((count + (SLAB - 1.0)) * (1.0 / SLAB)) * SLAB
    padded8 = jnp.broadcast_to(padded, (8, ROUTER_W))
    seg_start = sum(jnp.where(lane >= k, pltpu.roll(padded8, k, 1), 0.0) for k in range(1, N_GROUPS))[0:1]
    rank = jnp.sum(onehot * (seg_start + before), axis=-1, keepdims=True)
    col = lax.broadcasted_iota(jnp.int32, (1, SORT_W), 1).astype(F32)
    place = jnp.where(col == rank, 1.0, 0.0).astype(BF16)
    h2s_ref[...] = _dot_tn(place, h2.astype(BF16))[:SORT_ROWS].astype(BF16)
    gs_ref[...] = sum(_dot_tn(place, part) for part in _split(gate, 3))[:SORT_ROWS]
    rinfo_ref[...] = jnp.where(lane == 0, rank, 0.0)
    slab_lo = (lane * SLAB).astype(F32)
    slab_group = jnp.full((1, ROUTER_W), -1.0, F32)
    for g in range(N_GROUPS):
        s_g = jnp.sum(jnp.where(lane == g, seg_start, 0.0), axis=-1, keepdims=True)
        n_g = jnp.sum(jnp.where(lane == g, padded, 0.0), axis=-1, keepdims=True)
        slab_group = jnp.where((slab_lo >= s_g) & (slab_lo < s_g + n_g), float(g), slab_group)
    info_ref[0] = jnp.broadcast_to(slab_group.astype(jnp.int32), (8, ROUTER_W))


def _slab_gather(table_ref, first, n_slabs, srcs, bufs, sems, slot, *, wait):
    n_rows = n_slabs * SLAB
    for src, buf, sem in zip(srcs, bufs, sems):
        if wait:
            pltpu.make_async_copy(src.at[pl.ds(0, n_rows)], buf.at[slot, pl.ds(0, n_rows)], sem.at[slot]).wait()
            continue
        for j in range(n_slabs):
            row = pl.multiple_of(jnp.maximum(table_ref[first + j], 0) * SLAB, SLAB)
            pltpu.make_async_copy(src.at[pl.ds(row, SLAB)], buf.at[slot, pl.ds(j * SLAB, SLAB)],
                                  sem.at[slot]).start()


def _moe_kernel(src_ref, tg_ref, nt_ref, h2s_hbm, gs_hbm, wg_ref, wu_ref, wd_ref, ys_ref, hbuf, gbuf, sem_h, sem_g):
    i = pl.program_id(0)
    n_tiles = nt_ref[0]
    slot = lax.rem(i, 2)
    gather = functools.partial(_slab_gather, src_ref, srcs=(h2s_hbm, gs_hbm), bufs=(hbuf, gbuf),
                               sems=(sem_h, sem_g), n_slabs=TILE_SLABS)

    @pl.when(i == 0)
    def _():
        gather(first=0, slot=0, wait=False)

    @pl.when(i < n_tiles)
    def _():
        gather(first=i * TILE_SLABS, slot=slot, wait=True)

        @pl.when(i + 1 < n_tiles)
        def _():
            gather(first=(i + 1) * TILE_SLABS, slot=1 - slot, wait=False)

        hb = hbuf[slot]
        gates = gbuf[slot]
        lane = lax.broadcasted_iota(jnp.int32, (1, ROUTER_W), 1)
        first_lane = N_GROUPS + tg_ref[i] * EXPERTS_PER_GROUP
        acc = jnp.zeros((TM, D_MODEL), F32)
        for e in range(EXPERTS_PER_GROUP):
            he = (_silu(_dot(hb, wg_ref[e])) * _dot(hb, wu_ref[e])).astype(BF16)
            gcol = jnp.sum(jnp.where(lane == first_lane + e, gates, 0.0), axis=-1, keepdims=True)
            acc = acc + gcol * _dot(he, wd_ref[e])
        ys_ref[...] = acc.astype(BF16)

    @pl.when(i >= n_tiles)
    def _():
        ys_ref[...] = jnp.zeros_like(ys_ref)


def _final_kernel(steps_p, dest_ref, x1_ref, rinfo_ref, ys_hbm, gfin_ref, yp_ref, ysmp_ref, ybuf, sem):
    i = pl.program_id(0)
    slot = lax.rem(i, 2)
    n_slabs = FINAL_BLOCKS * SLABS_PER_BLOCK
    gather = functools.partial(_slab_gather, dest_ref, srcs=(ys_hbm,), bufs=(ybuf,), sems=(sem,), n_slabs=n_slabs)

    @pl.when(i == 0)
    def _():
        gather(first=0, slot=0, wait=False)

    gather(first=i * n_slabs, slot=slot, wait=True)

    @pl.when(i + 1 < pl.num_programs(0))
    def _():
        gather(first=(i + 1) * n_slabs, slot=1 - slot, wait=False)

    yb = ybuf[slot]
    col = lax.broadcasted_iota(jnp.int32, (1, FINAL_BLOCKS * SORT_ROWS), 1).astype(F32)
    moe = []
    for b in range(FINAL_BLOCKS):
        rank = rinfo_ref[b * TB:(b + 1) * TB, 0:1] + float(b * SORT_ROWS)
        moe.append(_dot(jnp.where(col == rank, 1.0, 0.0).astype(BF16), yb))
    xo = x1_ref[...] + jnp.concatenate(moe, axis=0)
    y = xo * lax.rsqrt(jnp.mean(xo * xo, axis=-1, keepdims=True) + RMS_EPS) * gfin_ref[...]

    @pl.when(i < steps_p)
    def _():
        yp_ref[...] = y

    @pl.when(i >= steps_p)
    def _():
        ysmp_ref[...] = y


def _const_spec(shape, pipeline_mode=None):
    nd = len(shape)
    return pl.BlockSpec(shape, lambda i: (0,) * nd, pipeline_mode=pipeline_mode)


def _mixer_call(x2d, ret0, hg0, params, *, seg_len, carry, pipelined, block_offset, total_blocks, shared=(),
                expert_weights=()):
    (gmix, win, lbl, hgn, wout, gffn, wr, br) = params
    T = x2d.shape[0]
    LS = seg_len
    n_seg = TB // LS
    NCH = LS // CHUNK
    n_states = ret0.shape[0]
    n_own = T // TB
    n_fill = 0 if shared else total_blocks - n_own - block_offset
    lag = 1 if pipelined else 0
    grid = n_own + lag + n_fill

    f32 = np.float32
    lg = np.log1p(-(f32(2.0) ** (f32(-5.0) - np.arange(N_HEADS, dtype=f32)))).astype(f32)
    tt = np.arange(LS, dtype=f32)
    ch = np.arange(LS) // CHUNK
    dret = np.exp(np.abs(tt[:, None] - tt[None, :])[None] * lg[:, None, None]).astype(f32)
    dret = np.where((ch[None, :] <= ch[:, None])[None], dret, f32(0.0))
    lg_lane = np.repeat(lg, D_HEAD)[None, :]
    xi = np.exp((tt[:, None] + f32(1.0)) * lg_lane).astype(f32)
    zeta = np.exp((f32(LS - 1.0) - tt)[:, None] * lg_lane).astype(f32)
    gls = np.exp(f32(LS) * lg_lane).astype(f32)
    half = D_HEAD // 2
    inv_freq = ROPE_BASE ** (-jnp.arange(half, dtype=F32) / half)
    invf = jnp.tile(inv_freq, PAIR_W // half)[None, :]
    sgn = np.where((np.arange(PAIR_W) % D_HEAD) < half, f32(-1.0), f32(1.0))[None, :]
    tri = jnp.asarray(np.tril(np.ones((LS, LS), np.float32)), BF16)
    lane_head = np.arange(MXU_DIM) // D_HEAD
    ones_bd = jnp.asarray((lane_head[:, None] == lane_head[None, :]).astype(np.float32), BF16)
    dmask = jnp.asarray(((ch[:, None] == ch[None, :]) & (np.arange(LS)[:, None] >= np.arange(LS)[None, :]))
                        .astype(np.float32))

    if carry:
        state_spec = pl.BlockSpec((1, N_HEADS, D_HEAD, D_HEAD), lambda i: (0, 0, 0, 0))
    else:
        state_spec = pl.BlockSpec((n_seg, N_HEADS, D_HEAD, D_HEAD), lambda i: (i, 0, 0, 0))
    single = pl.Buffered(1)
    stri = jnp.asarray(np.tril(np.ones((TB, TB), np.float32), -1), BF16)
    in_row_spec = pl.BlockSpec((TB, D_MODEL), lambda i: (jnp.minimum(i, n_own - 1), 0))
    out_block = lambda i: jnp.maximum(i - lag, 0) + block_offset
    out_row_spec = lambda rows, w: pl.BlockSpec((rows, w), lambda i: (out_block(i), 0))

    cast_w = bool(expert_weights)
    w2d = [w.reshape(-1, w.shape[-1]) for w in expert_weights]
    w_specs = [pl.BlockSpec((w.shape[0] // n_own, w.shape[1]), lambda i: (jnp.minimum(i, n_own - 1), 0)) for w in w2d]
    w_shapes = [jax.ShapeDtypeStruct(w.shape, BF16) for w in w2d]
    n_in = N_MIXER_INPUTS + len(w2d)

    kern = functools.partial(_mixer_kernel, (LS, n_seg, carry, pipelined, n_own, n_fill, len(shared), cast_w))
    return pl.pallas_call(
        kern,
        grid=(grid,),
        in_specs=[
            in_row_spec,
            _const_spec((1, D_MODEL)), _const_spec(win.shape, single), _const_spec((1, PAIR_W)),
            _const_spec((1, PAIR_W)), _const_spec(lbl.shape), _const_spec((1, GROUP_W)),
            _const_spec(wout.shape, single), _const_spec((1, D_MODEL)),
            _const_spec(wr.shape), _const_spec((1, ROUTER_W)),
            _const_spec((LS, LS)), _const_spec((N_HEADS, LS, LS)), _const_spec((LS, GROUP_W)),
            _const_spec((LS, GROUP_W)), _const_spec((1, GROUP_W)), _const_spec((MXU_DIM, MXU_DIM)),
            _const_spec((TB, TB)), _const_spec((LS, LS)),
            state_spec, state_spec,
        ] + w_specs + [pl.BlockSpec(memory_space=pl.ANY)] * len(shared),
        out_specs=[out_row_spec(TB, D_MODEL), out_row_spec(SORT_ROWS, D_MODEL), out_row_spec(SORT_ROWS, ROUTER_W),
                   out_row_spec(TB, ROUTER_W),
                   pl.BlockSpec((1, 8, ROUTER_W), lambda i: (out_block(i), 0, 0)),
                   state_spec, state_spec] + w_specs,
        out_shape=[
            jax.ShapeDtypeStruct((total_blocks * TB, D_MODEL), F32),
            jax.ShapeDtypeStruct((total_blocks * SORT_ROWS, D_MODEL), BF16),
            jax.ShapeDtypeStruct((total_blocks * SORT_ROWS, ROUTER_W), F32),
            jax.ShapeDtypeStruct((total_blocks * TB, ROUTER_W), F32),
            jax.ShapeDtypeStruct((total_blocks, 8, ROUTER_W), jnp.int32),
            jax.ShapeDtypeStruct((n_states, N_HEADS, D_HEAD, D_HEAD), F32),
            jax.ShapeDtypeStruct((n_states, N_HEADS, D_HEAD, D_HEAD), F32),
        ] + w_shapes,
        input_output_aliases={n_in + k: k for k in range(len(shared))},
        scratch_shapes=[
            *[pltpu.VMEM((TB, GROUP_W), F32) for _ in range(8)],
            pltpu.VMEM((TB, GROUP_W), F32),
            pltpu.VMEM((TB, 2 * GROUP_W), F32),
            pltpu.VMEM((TB, GROUP_W), F32),
            pltpu.VMEM((TB, D_MODEL), F32),
            pltpu.VMEM((TB, GROUP_W), BF16), pltpu.VMEM((TB, GROUP_W), BF16),
            pltpu.VMEM((TB, GROUP_W), BF16), pltpu.VMEM((TB, GROUP_W), BF16),
            pltpu.VMEM((SUB * CHUNK, GROUP_W), BF16),
            pltpu.VMEM((TB, PAIR_W), F32), pltpu.VMEM((TB, PAIR_W), F32),
            pltpu.VMEM(win.shape, BF16), pltpu.VMEM(wout.shape, BF16),
            pltpu.VMEM((1, N_PAIRS, PAIR_W, PAIR_W), F32),
            pltpu.VMEM((1, N_PAIRS, PAIR_W, PAIR_W), F32),
        ],
        compiler_params=pltpu.CompilerParams(
            dimension_semantics=("arbitrary",), vmem_limit_bytes=V7X_VMEM_LIMIT),
    )(x2d, gmix, win, invf, sgn, lbl, hgn, wout, gffn, wr, br, tri, dret, xi, zeta, gls, ones_bd, stri, dmask,
      ret0, hg0, *w2d, *shared)


def _dispatch_tables(info, n_tiles_max):
    slab_group = info[:, 0, :SLABS_PER_BLOCK].reshape(-1)
    n_slabs = slab_group.shape[0]
    valid = slab_group >= 0
    onehot = (slab_group[:, None] == jnp.arange(N_GROUPS, dtype=jnp.int32)[None, :]).astype(jnp.int32)
    within = jnp.cumsum(onehot, axis=0) - onehot
    tiles = (jnp.sum(onehot, axis=0) + TILE_SLABS - 1) // TILE_SLABS
    tile_end = jnp.cumsum(tiles)
    g = jnp.clip(slab_group, 0, N_GROUPS - 1)
    dest = (tile_end - tiles)[g] * TILE_SLABS + jnp.take_along_axis(within, g[:, None], axis=1)[:, 0]
    dest = jnp.where(valid, dest, -1).astype(jnp.int32)
    n_slots = n_tiles_max * TILE_SLABS
    src = jnp.full((n_slots,), -1, jnp.int32).at[jnp.where(valid, dest, n_slots)].set(
        jnp.arange(n_slabs, dtype=jnp.int32), mode="drop")
    tile_group = jnp.sum(jnp.arange(n_tiles_max, dtype=jnp.int32)[:, None] >= tile_end[None, :], axis=1)
    tile_group = jnp.minimum(tile_group, N_GROUPS - 1).astype(jnp.int32)
    return src, dest, tile_group, tile_end[-1:].astype(jnp.int32)


def _moe_call(src, tile_group, n_tiles, h2s, gs, w_gate, w_up, w_down):
    n_tiles_max = tile_group.shape[0]
    group_of_tile = lambda i, s, tg, nt: (tg[i], 0, 0)
    return pl.pallas_call(
        _moe_kernel,
        grid_spec=pltpu.PrefetchScalarGridSpec(
            num_scalar_prefetch=3,
            grid=(n_tiles_max,),
            in_specs=[
                pl.BlockSpec(memory_space=pl.ANY), pl.BlockSpec(memory_space=pl.ANY),
                pl.BlockSpec((EXPERTS_PER_GROUP, D_MODEL, D_EXPERT), group_of_tile),
                pl.BlockSpec((EXPERTS_PER_GROUP, D_MODEL, D_EXPERT), group_of_tile),
                pl.BlockSpec((EXPERTS_PER_GROUP, D_EXPERT, D_MODEL), group_of_tile),
            ],
            out_specs=pl.BlockSpec((TM, D_MODEL), lambda i, s, tg, nt: (i, 0)),
            scratch_shapes=[
                pltpu.VMEM((2, TM, D_MODEL), BF16), pltpu.VMEM((2, TM, ROUTER_W), F32),
                pltpu.SemaphoreType.DMA((2,)), pltpu.SemaphoreType.DMA((2,)),
            ],
        ),
        out_shape=jax.ShapeDtypeStruct((n_tiles_max * TM, D_MODEL), BF16),
        compiler_params=pltpu.CompilerParams(
            dimension_semantics=("arbitrary",), vmem_limit_bytes=V7X_VMEM_LIMIT),
    )(src, tile_group, n_tiles, h2s, gs, w_gate, w_up, w_down)


def _final_call(dest, x1, rinfo, ys, gfin, blocks_p):
    n_blocks = x1.shape[0] // TB
    assert blocks_p % FINAL_BLOCKS == 0 and n_blocks % FINAL_BLOCKS == 0
    steps_p = blocks_p // FINAL_BLOCKS
    rows = FINAL_BLOCKS * TB
    return pl.pallas_call(
        functools.partial(_final_kernel, steps_p),
        grid_spec=pltpu.PrefetchScalarGridSpec(
            num_scalar_prefetch=1,
            grid=(n_blocks // FINAL_BLOCKS,),
            in_specs=[
                pl.BlockSpec((rows, D_MODEL), lambda i, d: (i, 0)),
                pl.BlockSpec((rows, ROUTER_W), lambda i, d: (i, 0)),
                pl.BlockSpec(memory_space=pl.ANY),
                pl.BlockSpec((1, D_MODEL), lambda i, d: (0, 0)),
            ],
            out_specs=[
                pl.BlockSpec((rows, D_MODEL), lambda i, d: (jnp.minimum(i, steps_p - 1), 0)),
                pl.BlockSpec((rows, D_MODEL), lambda i, d: (jnp.maximum(i - steps_p, 0), 0)),
            ],
            scratch_shapes=[pltpu.VMEM((2, FINAL_BLOCKS * SORT_ROWS, D_MODEL), BF16), pltpu.SemaphoreType.DMA((2,))],
        ),
        out_shape=[jax.ShapeDtypeStruct((blocks_p * TB, D_MODEL), F32),
                   jax.ShapeDtypeStruct(((n_blocks - blocks_p) * TB, D_MODEL), F32)],
        compiler_params=pltpu.CompilerParams(
            dimension_semantics=("arbitrary",), vmem_limit_bytes=V7X_VMEM_LIMIT),
    )(dest, x1, rinfo, ys, gfin)


def kernel(x_prompt, x_sample, state_ret, state_hgrn, norm_mix_g, w_in, hgrn_lb_logits, hgrn_norm_g, w_out,
           norm_ffn_g, w_router_group, b_router_group, w_router_expert, b_router_expert, w_exp_gate, w_exp_up,
           w_exp_down, norm_final_g):
    depth = w_in.shape[0]
    assert depth == 1 and hgrn_lb_logits.shape[0] == 2, "single-layer configuration only"
    bp, seq, d = x_prompt.shape
    db, dec_len, _ = x_sample.shape
    assert bp == 1 and d == D_MODEL and dec_len == CHUNK and seq % 256 == 0 and db % 4 == 0

    pad = ROUTER_W - N_GROUPS - N_EXPERTS
    wr = jnp.concatenate([w_router_group[0], w_router_expert[0], jnp.zeros((D_MODEL, pad), F32)], axis=1)
    br = jnp.concatenate([b_router_group[0], b_router_expert[0], jnp.zeros((pad,), F32)])[None, :]
    params = (norm_mix_g[0][None, :], w_in[0], hgrn_lb_logits,
              jnp.tile(hgrn_norm_g[0], N_HEADS)[None, :], w_out[0], norm_ffn_g[0][None, :], wr, br)

    zeros_state = jnp.zeros((1, N_HEADS, D_HEAD, D_HEAD), F32)
    blocks_p = seq // TB
    blocks_s = db * dec_len // TB
    n_blocks = blocks_p + blocks_s
    *shared, ret_p, hg_p, wg_bf, wu_bf, wd_bf = _mixer_call(
        x_prompt.reshape(seq, d), zeros_state, zeros_state, params, seg_len=TB, carry=True, pipelined=True,
        block_offset=0, total_blocks=n_blocks, expert_weights=(w_exp_gate[0], w_exp_up[0], w_exp_down[0]))
    x1, h2s, gs, rinfo, info, ret_s, hg_s = _mixer_call(
        x_sample.reshape(db * dec_len, d), state_ret[0], state_hgrn[0], params, seg_len=CHUNK, carry=False,
        pipelined=False,
        block_offset=blocks_p, total_blocks=n_blocks, shared=tuple(shared))

    n_tiles_max = -(-n_blocks * SLABS_PER_BLOCK // TILE_SLABS) + N_GROUPS
    src, dest, tile_group, n_tiles = _dispatch_tables(info, n_tiles_max)
    ys = _moe_call(src, tile_group, n_tiles, h2s, gs, wg_bf.reshape(w_exp_gate[0].shape),
                   wu_bf.reshape(w_exp_up[0].shape), wd_bf.reshape(w_exp_down[0].shape))
    y_p, y_s = _final_call(dest, x1, rinfo, ys, norm_final_g[None, :], blocks_p)
    return (y_p.reshape(bp, seq, d), y_s.reshape(db, dec_len, d), ret_p[None], hg_p[None], ret_s[None], hg_s[None])
```

```python
import functools

import numpy as np
import jax
import jax.numpy as jnp
from jax import lax
from jax.experimental import pallas as pl
from jax.experimental.pallas import tpu as pltpu

F32 = jnp.float32
BF16 = jnp.bfloat16

D_MODEL = 1024
N_HEADS = 8
D_HEAD = 64
GROUP_W = N_HEADS * D_HEAD
N_PAIRS = N_HEADS // 2
PAIR_W = 2 * D_HEAD
CHUNK = 64
SUB = 16
N_SUB = CHUNK // SUB
PAST_LEN = 2048
ROPE_BASE = 10000.0
RMS_EPS = 1e-6
SPAN_MAX = 80.0
N_GROUPS = 4
EXPERTS_PER_GROUP = 8
N_EXPERTS = N_GROUPS * EXPERTS_PER_GROUP
D_EXPERT = 256
ROUTER_W = 128
V7X_VMEM_LIMIT = 60 * 1024 * 1024
MXU_DIM = 256
TB = 256
SLAB = 16
SLABS_PER_BLOCK = TB // SLAB + N_GROUPS
SORT_ROWS = SLABS_PER_BLOCK * SLAB
SORT_W = 384
TILE_SLABS = 32
TM = TILE_SLABS * SLAB
FINAL_BLOCKS = 2

C_QR, C_KR, C_VR, C_GR, C_QG, C_FG, C_VG, C_GG = (i * GROUP_W for i in range(8))


def _dot(a, b):
    return jnp.dot(a, b, preferred_element_type=F32)


def _dot_nt(a, b):
    return lax.dot_general(a, b, (((1,), (1,)), ((), ())), preferred_element_type=F32)


def _dot_tn(a, b):
    return lax.dot_general(a, b, (((0,), (0,)), ((), ())), preferred_element_type=F32)


def _split(x, n):
    parts = []
    for _ in range(n):
        p = x.astype(BF16)
        parts.append(p)
        x = x - p.astype(F32)
    return parts


def _group_sum(x, ones_ref):
    xb = x.astype(BF16)
    return jnp.concatenate([_dot(xb[:, c * MXU_DIM:(c + 1) * MXU_DIM], ones_ref[...])
                            for c in range(GROUP_W // MXU_DIM)], axis=1)


class _ColumnGroups:
    def __init__(self, bufs):
        self.bufs = bufs

    def _locate(self, idx):
        rows, cols = idx
        g = cols.start // GROUP_W
        assert (cols.stop - 1) // GROUP_W == g
        return self.bufs[g], (rows, slice(cols.start - g * GROUP_W, cols.stop - g * GROUP_W))

    def __getitem__(self, idx):
        buf, at = self._locate(idx)
        return buf[at]

    def __setitem__(self, idx, value):
        buf, at = self._locate(idx)
        buf[at] = value


def _silu(x):
    return x * jax.nn.sigmoid(x)


N_MIXER_INPUTS = 21


def _mixer_kernel(cfg, *refs):
    pipelined, n_own, n_fill, n_alias, cast_w = cfg[-5:]
    n_body = n_own + (1 if pipelined else 0)
    n_in = N_MIXER_INPUTS + (3 if cast_w else 0)
    i = pl.program_id(0)

    @pl.when(i < n_body)
    def _():
        _mixer_body(cfg, *refs)

    if n_fill:
        @pl.when(i >= n_body)
        def _():
            for out_ref in refs[n_in + n_alias:n_in + n_alias + 5]:
                out_ref[...] = jnp.zeros_like(out_ref)


def _mixer_body(cfg, x_ref, gmix_ref, win_ref, invf_ref, sgn_ref, lbl_ref, hgn_ref, wout_ref, gffn_ref,
                wr_ref, br_ref, tri_ref, dret_ref, xi_ref, zeta_ref, gls_ref, ones_ref, stri_ref,
                dmask_ref, ret0_ref, hg0_ref, *rest):
    LS, NSEG, carry, pipelined, n_own, _, n_alias, cast_w = cfg
    w_chunks = ()
    if cast_w:
        w_chunks = tuple(zip(rest[:3], rest[3 + n_alias + 7:3 + n_alias + 10]))
        rest = rest[3:3 + n_alias + 7] + rest[3 + n_alias + 10:]
    (x1_ref, h2s_ref, gs_ref, rinfo_ref, info_ref, retout_ref, hgout_ref,
     *proj_bufs, kbuf, obuf, xprev, qin_buf, kout_buf, qd, kd, pbuf, cos_t, sin_t, win_bf, wout_bf,
     rst, sst) = rest[n_alias:]
    proj = _ColumnGroups(proj_bufs)
    NCH = LS // CHUNK
    i = pl.program_id(0)
    lane128 = lax.broadcasted_iota(jnp.int32, (1, PAIR_W), 1)
    head_a = lane128 < D_HEAD
    bd_mask = (lax.broadcasted_iota(jnp.int32, (PAIR_W, PAIR_W), 0) < D_HEAD) == head_a

    def pair_state(ref, sg, p):
        z = jnp.zeros((D_HEAD, D_HEAD), F32)
        return jnp.concatenate([jnp.concatenate([ref[sg, 2 * p], z], axis=1),
                                jnp.concatenate([z, ref[sg, 2 * p + 1]], axis=1)], axis=0)

    @pl.when(i == 0)
    def _():
        for c in range(0, 8 * GROUP_W, GROUP_W):
            win_bf[:, c:c + GROUP_W] = win_ref[:, c:c + GROUP_W].astype(BF16)
        wout_bf[...] = wout_ref[...].astype(BF16)
        if carry:
            for p in range(N_PAIRS):
                rst[0, p] = pair_state(ret0_ref, 0, p)
                sst[0, p] = pair_state(hg0_ref, 0, p)
        row = lax.broadcasted_iota(jnp.int32, (TB, 1), 0)
        ang_row = (row & (LS - 1)).astype(F32) * invf_ref[...]
        cos_t[...] = jnp.cos(ang_row)
        sin_t[...] = jnp.sin(ang_row)
        if pipelined:
            for buf in (obuf, xprev, proj_bufs[C_GR // GROUP_W], proj_bufs[C_GG // GROUP_W]):
                buf[...] = jnp.zeros_like(buf)

    if pipelined:
        _mixer_back(proj, obuf, xprev, hgn_ref, ones_ref, wout_bf, gffn_ref, wr_ref, br_ref, stri_ref,
                    x1_ref, h2s_ref, gs_ref, rinfo_ref, info_ref)
    keep = i < n_own

    for w_ref, w_bf_ref in w_chunks:
        w_bf_ref[...] = w_ref[...].astype(BF16)

    x = x_ref[...]
    h = x * lax.rsqrt(jnp.mean(x * x, axis=-1, keepdims=True) + RMS_EPS) * gmix_ref[...]
    hb = h.astype(BF16)

    def project(c0, c1):
        for c in range(c0, c1, GROUP_W):
            proj[:, c:c + GROUP_W] = _dot(hb, win_bf[:, c:c + GROUP_W])

    project(C_QG, C_GG)
    project(C_QR, C_VR)

    la = lbl_ref[0:1, :]
    lb_ = lbl_ref[1:2, :]
    lmax = jnp.maximum(la, lb_)
    ea = jnp.exp(la - lmax)
    lbv = ea / (ea + jnp.exp(lb_ - lmax))
    qg = proj[:, C_QG:C_QG + GROUP_W]
    proj[:, C_QG:C_QG + GROUP_W] = _silu(qg)
    f = lbv + (1.0 - lbv) * jax.nn.sigmoid(proj[:, C_FG:C_FG + GROUP_W])
    kbuf[...] = 1.0 - f
    logf = jnp.log(f)
    for sg in range(NSEG):
        rows = slice(sg * LS, (sg + 1) * LS)
        parts = _split(logf[rows], 3)
        proj[rows, C_FG:C_FG + GROUP_W] = sum(_dot(tri_ref[...], p) for p in parts)

    project(C_VR, C_QG)
    project(C_GG, C_GG + GROUP_W)

    start = jnp.full((8, PAIR_W), i * TB if carry else PAST_LEN, jnp.int32).astype(F32)
    ang0 = (start * invf_ref[...])[0:1]
    c0 = jnp.cos(ang0)
    s0 = jnp.sin(ang0)
    cos = cos_t[...] * c0 - sin_t[...] * s0
    sin = (sin_t[...] * c0 + cos_t[...] * s0) * sgn_ref[...]
    first_half = (lane128 & (D_HEAD - 1)) < (D_HEAD // 2)
    for blk in range(2 * N_PAIRS):
        cols = slice(blk * PAIR_W, (blk + 1) * PAIR_W)
        xx = proj[:, cols]
        partner = jnp.where(first_half, pltpu.roll(xx, PAIR_W - D_HEAD // 2, 1), pltpu.roll(xx, D_HEAD // 2, 1))
        r = xx * cos + partner * sin
        if blk < N_PAIRS:
            r = r * (D_HEAD ** -0.5)
        proj[:, cols] = r

    ciota = lax.broadcasted_iota(jnp.int32, (CHUNK, 1), 0)

    def exact_chunk(c, _):
        r0 = pl.multiple_of(c * CHUNK, CHUNK)
        qc = proj[pl.ds(r0, CHUNK), C_QG:C_QG + GROUP_W]
        bc = proj[pl.ds(r0, CHUNK), C_FG:C_FG + GROUP_W]
        acc = [jnp.zeros((CHUNK, MXU_DIM), F32) for _ in range(2)]
        for sb in range(N_SUB):
            rs = r0 + sb * SUB
            for s in range(SUB):
                ks = kbuf[pl.ds(rs + s, 1), :]
                bs = proj[pl.ds(rs + s, 1), C_FG:C_FG + GROUP_W]
                p = qc * ks * jnp.exp(jnp.minimum(bc - bs, 0.0))
                pbuf[s * CHUNK:(s + 1) * CHUNK, :] = jnp.where(ciota >= sb * SUB + s, p, 0.0).astype(BF16)
            for hh in range(2):
                sc = _dot(pbuf[:, hh * MXU_DIM:(hh + 1) * MXU_DIM], ones_ref[...])
                for s in range(SUB):
                    vs = proj[pl.ds(rs + s, 1), C_VG + hh * MXU_DIM:C_VG + (hh + 1) * MXU_DIM]
                    acc[hh] = acc[hh] + sc[s * CHUNK:(s + 1) * CHUNK] * vs
        hg_cols = slice(GROUP_W, 2 * GROUP_W)
        obuf[pl.ds(r0, CHUNK), hg_cols] = obuf[pl.ds(r0, CHUNK), hg_cols] + jnp.concatenate(acc, axis=1)
        return 0

    b_all = proj[:, C_FG:C_FG + GROUP_W]
    first = jnp.concatenate([jnp.broadcast_to(b_all[r:r + 1, :], (CHUNK, GROUP_W)) for r in range(0, TB, CHUNK)],
                            axis=0)
    span = first - b_all
    risky = jnp.max(span) > SPAN_MAX
    q_local = proj[:, C_QG:C_QG + GROUP_W] * jnp.exp(-span)
    qd[...] = jnp.where(risky, 0.0, q_local).astype(BF16)
    kd[...] = (kbuf[...] * jnp.exp(jnp.minimum(span, SPAN_MAX))).astype(BF16)

    for sg in range(NSEG):
        r0 = sg * LS
        rows = slice(r0, r0 + LS)
        qg_s = proj[rows, C_QG:C_QG + GROUP_W]
        kg_s = kbuf[rows, :]
        b_s = proj[rows, C_FG:C_FG + GROUP_W]

        for c in range(NCH):
            cr = slice(c * CHUNK, (c + 1) * CHUNK)
            b_c = b_s[cr]
            b_in = b_s[c * CHUNK - 1:c * CHUNK, :] if c else jnp.zeros((1, GROUP_W), F32)
            b_out = b_s[(c + 1) * CHUNK - 1:(c + 1) * CHUNK, :]
            qin_buf[r0 + cr.start:r0 + cr.stop, :] = (qg_s[cr] * jnp.exp(b_c - b_in)).astype(BF16)
            kout_buf[r0 + cr.start:r0 + cr.stop, :] = (kg_s[cr] * jnp.exp(b_out - b_c)).astype(BF16)

    def pair_unit(sg, p):
        r0 = sg * LS
        rows = slice(r0, r0 + LS)
        st = 0 if carry else sg
        pc = slice(p * PAIR_W, (p + 1) * PAIR_W)

        qb = proj[rows, C_QR + p * PAIR_W:C_QR + (p + 1) * PAIR_W].astype(BF16)
        kr = proj[rows, C_KR + p * PAIR_W:C_KR + (p + 1) * PAIR_W]
        kb = kr.astype(BF16)
        vb = proj[rows, C_VR + p * PAIR_W:C_VR + (p + 1) * PAIR_W].astype(BF16)
        zero = jnp.zeros_like(qb)
        s_a = _dot_nt(jnp.where(head_a, qb, zero), kb) * dret_ref[2 * p]
        s_b = _dot_nt(jnp.where(head_a, zero, qb), kb) * dret_ref[2 * p + 1]
        o_r = jnp.where(head_a, _dot(s_a.astype(BF16), vb), _dot(s_b.astype(BF16), vb))
        r_old = rst[0, p] if carry else pair_state(ret0_ref, sg, p)
        o_r = o_r + _dot(qb, r_old.astype(BF16)) * xi_ref[:, pc]
        u = _dot_tn((kr * zeta_ref[:, pc]).astype(BF16), vb)
        r_new = r_old * gls_ref[:, pc] + jnp.where(bd_mask, u, 0.0)
        if pipelined:
            r_new = jnp.where(keep, r_new, r_old)
        obuf[rows, pc] = o_r

        vg = proj[rows, C_VG + p * PAIR_W:C_VG + (p + 1) * PAIR_W].astype(BF16)
        qd_p = qd[rows, pc]
        kd_p = kd[rows, pc]
        zd = jnp.zeros_like(qd_p)
        g_a = _dot_nt(jnp.where(head_a, qd_p, zd), kd_p) * dmask_ref[...]
        g_b = _dot_nt(jnp.where(head_a, zd, qd_p), kd_p) * dmask_ref[...]
        o_g = jnp.where(head_a, _dot(g_a.astype(BF16), vg), _dot(g_b.astype(BF16), vg))
        s_old = sst[0, p] if carry else pair_state(hg0_ref, sg, p)
        s_new = s_old
        from_state = []
        bcol = C_FG + p * PAIR_W
        for c in range(NCH):
            cr = slice(r0 + c * CHUNK, r0 + (c + 1) * CHUNK)
            from_state.append(_dot(qin_buf[cr, pc], s_new.astype(BF16)))
            ut = _dot_tn(kout_buf[cr, pc], vg[c * CHUNK:(c + 1) * CHUNK])
            b_out = proj[cr.stop - 1:cr.stop, bcol:bcol + PAIR_W]
            if c:
                b_out = b_out - proj[cr.start - 1:cr.start, bcol:bcol + PAIR_W]
            s_decay = jnp.exp(jnp.broadcast_to(b_out, (PAIR_W, PAIR_W)).T)
            s_new = s_new * s_decay + jnp.where(bd_mask, ut, 0.0)
        o_g = o_g + (jnp.concatenate(from_state, axis=0) if NCH > 1 else from_state[0])
        if pipelined:
            s_new = jnp.where(keep, s_new, s_old)
        gc = slice(GROUP_W + p * PAIR_W, GROUP_W + (p + 1) * PAIR_W)
        obuf[rows, gc] = o_g

        if carry:
            rst[0, p] = r_new
            sst[0, p] = s_new
        for out_ref, new in ((retout_ref, r_new), (hgout_ref, s_new)):
            out_ref[st, 2 * p] = new[:D_HEAD, :D_HEAD]
            out_ref[st, 2 * p + 1] = new[D_HEAD:, D_HEAD:]

    for sg in range(NSEG):
        for p in range(N_PAIRS):
            pair_unit(sg, p)

    xprev[...] = x

    @pl.when(risky)
    def _():
        lax.fori_loop(0, TB // CHUNK, exact_chunk, 0)

    if not pipelined:
        _mixer_back(proj, obuf, xprev, hgn_ref, ones_ref, wout_bf, gffn_ref, wr_ref, br_ref, stri_ref,
                    x1_ref, h2s_ref, gs_ref, rinfo_ref, info_ref)


def _mixer_back(proj, obuf, xprev, hgn_ref, ones_ref, wout_bf, gffn_ref, wr_ref, br_ref, stri_ref,
                x1_ref, h2s_ref, gs_ref, rinfo_ref, info_ref):
    o_r = obuf[:, 0:GROUP_W]
    mu = _group_sum(o_r, ones_ref) * (1.0 / D_HEAD)
    dlt = o_r - mu
    var = _group_sum(dlt * dlt, ones_ref) * (1.0 / D_HEAD)
    y_r = dlt * lax.rsqrt(var + RMS_EPS) * _silu(proj[:, C_GR:C_GR + GROUP_W])
    o_g = obuf[:, GROUP_W:2 * GROUP_W]
    ms = _group_sum(o_g * o_g, ones_ref) * (1.0 / D_HEAD)
    y_g = o_g * lax.rsqrt(ms + RMS_EPS) * hgn_ref[...] * _silu(proj[:, C_GG:C_GG + GROUP_W])
    mix = jnp.concatenate([y_r, y_g], axis=1).astype(BF16)
    x1 = xprev[...] + _dot(mix, wout_bf[...])
    x1_ref[...] = x1

    h2 = x1 * lax.rsqrt(jnp.mean(x1 * x1, axis=-1, keepdims=True) + RMS_EPS) * gffn_ref[...]
    h_hi, h_lo = _split(h2, 2)
    w_hi, w_lo = _split(wr_ref[...], 2)
    hi_terms = _dot(h_hi, jnp.concatenate([w_hi, w_lo], axis=1))
    logits = hi_terms[:, :ROUTER_W] + hi_terms[:, ROUTER_W:] + _dot(h_lo, w_hi) + br_ref[...]
    lane = lax.broadcasted_iota(jnp.int32, (1, ROUTER_W), 1).astype(F32)
    neg = -jnp.inf
    no_lane = float(ROUTER_W)
    gl = jnp.where(lane < N_GROUPS, logits, neg)
    gmax = jnp.max(gl, axis=-1, keepdims=True)
    g_idx = jnp.min(jnp.where(gl == gmax, lane, no_lane), axis=-1, keepdims=True)
    prob_g = 1.0 / jnp.sum(jnp.exp(gl - gmax), axis=-1, keepdims=True)
    e_lo = N_GROUPS + EXPERTS_PER_GROUP * g_idx
    el = jnp.where((lane >= e_lo) & (lane < e_lo + EXPERTS_PER_GROUP), logits, neg)
    v1 = jnp.max(el, axis=-1, keepdims=True)
    i1 = jnp.min(jnp.where(el == v1, lane, no_lane), axis=-1, keepdims=True)
    el2 = jnp.where(lane == i1, neg, el)
    v2 = jnp.max(el2, axis=-1, keepdims=True)
    i2 = jnp.min(jnp.where(el2 == v2, lane, no_lane), axis=-1, keepdims=True)
    t = jnp.exp(v2 - v1)
    p1 = 1.0 / (1.0 + t)
    p2 = t * p1
    gate = jnp.where(lane == i1, prob_g * p1, 0.0) + jnp.where(lane == i2, prob_g * p2, 0.0)

    onehot = jnp.where(lane == g_idx, 1.0, 0.0)
    before = _dot(stri_ref[...], onehot.astype(BF16))
    count = jnp.sum(onehot, axis=0, keepdims=True)
    padded = jnp.floor((count + (SLAB - 1.0)) * (1.0 / SLAB)) * SLAB
    padded8 = jnp.broadcast_to(padded, (8, ROUTER_W))
    seg_start = sum(jnp.where(lane >= k, pltpu.roll(padded8, k, 1), 0.0) for k in range(1, N_GROUPS))[0:1]
    rank = jnp.sum(onehot * (seg_start + before), axis=-1, keepdims=True)
    col = lax.broadcasted_iota(jnp.int32, (1, SORT_W), 1).astype(F32)
    place = jnp.where(col == rank, 1.0, 0.0).astype(BF16)
    h2s_ref[...] = _dot_tn(place, h2.astype(BF16))[:SORT_ROWS].astype(BF16)
    gs_ref[...] = sum(_dot_tn(place, part) for part in _split(gate, 3))[:SORT_ROWS]
    rinfo_ref[...] = jnp.where(lane == 0, rank, 0.0)
    slab_lo = (lane * SLAB).astype(F32)
    slab_group = jnp.full((1, ROUTER_W), -1.0, F32)
    for g in range(N_GROUPS):
        s_g = jnp.sum(jnp.where(lane == g, seg_start, 0.0), axis=-1, keepdims=True)
        n_g = jnp.sum(jnp.where(lane == g, padded, 0.0), axis=-1, keepdims=True)
        slab_group = jnp.where((slab_lo >= s_g) & (slab_lo < s_g + n_g), float(g), slab_group)
    info_ref[0] = jnp.broadcast_to(slab_group.astype(jnp.int32), (8, ROUTER_W))


def _slab_gather(table_ref, first, n_slabs, srcs, bufs, sems, slot, *, wait):
    n_rows = n_slabs * SLAB
    for src, buf, sem in zip(srcs, bufs, sems):
        if wait:
            pltpu.make_async_copy(src.at[pl.ds(0, n_rows)], buf.at[slot, pl.ds(0, n_rows)], sem.at[slot]).wait()
            continue
        for j in range(n_slabs):
            row = pl.multiple_of(jnp.maximum(table_ref[first + j], 0) * SLAB, SLAB)
            pltpu.make_async_copy(src.at[pl.ds(row, SLAB)], buf.at[slot, pl.ds(j * SLAB, SLAB)],
                                  sem.at[slot]).start()


def _moe_kernel(src_ref, tg_ref, nt_ref, h2s_hbm, gs_hbm, wg_ref, wu_ref, wd_ref, ys_ref, hbuf, gbuf, sem_h, sem_g):
    i = pl.program_id(0)
    n_tiles = nt_ref[0]
    slot = lax.rem(i, 2)
    gather = functools.partial(_slab_gather, src_ref, srcs=(h2s_hbm, gs_hbm), bufs=(hbuf, gbuf),
                               sems=(sem_h, sem_g), n_slabs=TILE_SLABS)

    @pl.when(i == 0)
    def _():
        gather(first=0, slot=0, wait=False)

    @pl.when(i < n_tiles)
    def _():
        gather(first=i * TILE_SLABS, slot=slot, wait=True)

        @pl.when(i + 1 < n_tiles)
        def _():
            gather(first=(i + 1) * TILE_SLABS, slot=1 - slot, wait=False)

        hb = hbuf[slot]
        gates = gbuf[slot]
        lane = lax.broadcasted_iota(jnp.int32, (1, ROUTER_W), 1)
        first_lane = N_GROUPS + tg_ref[i] * EXPERTS_PER_GROUP
        acc = jnp.zeros((TM, D_MODEL), F32)
        for e in range(EXPERTS_PER_GROUP):
            he = (_silu(_dot(hb, wg_ref[e])) * _dot(hb, wu_ref[e])).astype(BF16)
            gcol = jnp.sum(jnp.where(lane == first_lane + e, gates, 0.0), axis=-1, keepdims=True)
            acc = acc + gcol * _dot(he, wd_ref[e])
        ys_ref[...] = acc.astype(BF16)

    @pl.when(i >= n_tiles)
    def _():
        ys_ref[...] = jnp.zeros_like(ys_ref)


def _final_kernel(steps_p, dest_ref, x1_ref, rinfo_ref, ys_hbm, gfin_ref, yp_ref, ysmp_ref, ybuf, sem):
    i = pl.program_id(0)
    slot = lax.rem(i, 2)
    n_slabs = FINAL_BLOCKS * SLABS_PER_BLOCK
    gather = functools.partial(_slab_gather, dest_ref, srcs=(ys_hbm,), bufs=(ybuf,), sems=(sem,), n_slabs=n_slabs)

    @pl.when(i == 0)
    def _():
        gather(first=0, slot=0, wait=False)

    gather(first=i * n_slabs, slot=slot, wait=True)

    @pl.when(i + 1 < pl.num_programs(0))
    def _():
        gather(first=(i + 1) * n_slabs, slot=1 - slot, wait=False)

    yb = ybuf[slot]
    col = lax.broadcasted_iota(jnp.int32, (1, FINAL_BLOCKS * SORT_ROWS), 1).astype(F32)
    moe = []
    for b in range(FINAL_BLOCKS):
        rank = rinfo_ref[b * TB:(b + 1) * TB, 0:1] + float(b * SORT_ROWS)
        moe.append(_dot(jnp.where(col == rank, 1.0, 0.0).astype(BF16), yb))
    xo = x1_ref[...] + jnp.concatenate(moe, axis=0)
    y = xo * lax.rsqrt(jnp.mean(xo * xo, axis=-1, keepdims=True) + RMS_EPS) * gfin_ref[...]

    @pl.when(i < steps_p)
    def _():
        yp_ref[...] = y

    @pl.when(i >= steps_p)
    def _():
        ysmp_ref[...] = y


def _const_spec(shape, pipeline_mode=None):
    nd = len(shape)
    return pl.BlockSpec(shape, lambda i: (0,) * nd, pipeline_mode=pipeline_mode)


def _mixer_call(x2d, ret0, hg0, params, *, seg_len, carry, pipelined, block_offset, total_blocks, shared=(),
                expert_weights=()):
    (gmix, win, lbl, hgn, wout, gffn, wr, br) = params
    T = x2d.shape[0]
    LS = seg_len
    n_seg = TB // LS
    NCH = LS // CHUNK
    n_states = ret0.shape[0]
    n_own = T // TB
    n_fill = 0 if shared else total_blocks - n_own - block_offset
    lag = 1 if pipelined else 0
    grid = n_own + lag + n_fill

    f32 = np.float32
    lg = np.log1p(-(f32(2.0) ** (f32(-5.0) - np.arange(N_HEADS, dtype=f32)))).astype(f32)
    tt = np.arange(LS, dtype=f32)
    ch = np.arange(LS) // CHUNK
    dret = np.exp(np.abs(tt[:, None] - tt[None, :])[None] * lg[:, None, None]).astype(f32)
    dret = np.where((ch[None, :] <= ch[:, None])[None], dret, f32(0.0))
    lg_lane = np.repeat(lg, D_HEAD)[None, :]
    xi = np.exp((tt[:, None] + f32(1.0)) * lg_lane).astype(f32)
    zeta = np.exp((f32(LS - 1.0) - tt)[:, None] * lg_lane).astype(f32)
    gls = np.exp(f32(LS) * lg_lane).astype(f32)
    half = D_HEAD // 2
    inv_freq = ROPE_BASE ** (-jnp.arange(half, dtype=F32) / half)
    invf = jnp.tile(inv_freq, PAIR_W // half)[None, :]
    sgn = np.where((np.arange(PAIR_W) % D_HEAD) < half, f32(-1.0), f32(1.0))[None, :]
    tri = jnp.asarray(np.tril(np.ones((LS, LS), np.float32)), BF16)
    lane_head = np.arange(MXU_DIM) // D_HEAD
    ones_bd = jnp.asarray((lane_head[:, None] == lane_head[None, :]).astype(np.float32), BF16)
    dmask = jnp.asarray(((ch[:, None] == ch[None, :]) & (np.arange(LS)[:, None] >= np.arange(LS)[None, :]))
                        .astype(np.float32))

    if carry:
        state_spec = pl.BlockSpec((1, N_HEADS, D_HEAD, D_HEAD), lambda i: (0, 0, 0, 0))
    else:
        state_spec = pl.BlockSpec((n_seg, N_HEADS, D_HEAD, D_HEAD), lambda i: (i, 0, 0, 0))
    single = pl.Buffered(1)
    stri = jnp.asarray(np.tril(np.ones((TB, TB), np.float32), -1), BF16)
    in_row_spec = pl.BlockSpec((TB, D_MODEL), lambda i: (jnp.minimum(i, n_own - 1), 0))
    out_block = lambda i: jnp.maximum(i - lag, 0) + block_offset
    out_row_spec = lambda rows, w: pl.BlockSpec((rows, w), lambda i: (out_block(i), 0))

    cast_w = bool(expert_weights)
    w2d = [w.reshape(-1, w.shape[-1]) for w in expert_weights]
    w_specs = [pl.BlockSpec((w.shape[0] // n_own, w.shape[1]), lambda i: (jnp.minimum(i, n_own - 1), 0)) for w in w2d]
    w_shapes = [jax.ShapeDtypeStruct(w.shape, BF16) for w in w2d]
    n_in = N_MIXER_INPUTS + len(w2d)

    kern = functools.partial(_mixer_kernel, (LS, n_seg, carry, pipelined, n_own, n_fill, len(shared), cast_w))
    return pl.pallas_call(
        kern,
        grid=(grid,),
        in_specs=[
            in_row_spec,
            _const_spec((1, D_MODEL)), _const_spec(win.shape, single), _const_spec((1, PAIR_W)),
            _const_spec((1, PAIR_W)), _const_spec(lbl.shape), _const_spec((1, GROUP_W)),
            _const_spec(wout.shape, single), _const_spec((1, D_MODEL)),
            _const_spec(wr.shape), _const_spec((1, ROUTER_W)),
            _const_spec((LS, LS)), _const_spec((N_HEADS, LS, LS)), _const_spec((LS, GROUP_W)),
            _const_spec((LS, GROUP_W)), _const_spec((1, GROUP_W)), _const_spec((MXU_DIM, MXU_DIM)),
            _const_spec((TB, TB)), _const_spec((LS, LS)),
            state_spec, state_spec,
        ] + w_specs + [pl.BlockSpec(memory_space=pl.ANY)] * len(shared),
        out_specs=[out_row_spec(TB, D_MODEL), out_row_spec(SORT_ROWS, D_MODEL), out_row_spec(SORT_ROWS, ROUTER_W),
                   out_row_spec(TB, ROUTER_W),
                   pl.BlockSpec((1, 8, ROUTER_W), lambda i: (out_block(i), 0, 0)),
                   state_spec, state_spec] + w_specs,
        out_shape=[
            jax.ShapeDtypeStruct((total_blocks * TB, D_MODEL), F32),
            jax.ShapeDtypeStruct((total_blocks * SORT_ROWS, D_MODEL), BF16),
            jax.ShapeDtypeStruct((total_blocks * SORT_ROWS, ROUTER_W), F32),
            jax.ShapeDtypeStruct((total_blocks * TB, ROUTER_W), F32),
            jax.ShapeDtypeStruct((total_blocks, 8, ROUTER_W), jnp.int32),
            jax.ShapeDtypeStruct((n_states, N_HEADS, D_HEAD, D_HEAD), F32),
            jax.ShapeDtypeStruct((n_states, N_HEADS, D_HEAD, D_HEAD), F32),
        ] + w_shapes,
        input_output_aliases={n_in + k: k for k in range(len(shared))},
        scratch_shapes=[
            *[pltpu.VMEM((TB, GROUP_W), F32) for _ in range(8)],
            pltpu.VMEM((TB, GROUP_W), F32),
            pltpu.VMEM((TB, 2 * GROUP_W), F32),
            pltpu.VMEM((TB, D_MODEL), F32),
            pltpu.VMEM((TB, GROUP_W), BF16), pltpu.VMEM((TB, GROUP_W), BF16),
            pltpu.VMEM((TB, GROUP_W), BF16), pltpu.VMEM((TB, GROUP_W), BF16),
            pltpu.VMEM((SUB * CHUNK, GROUP_W), BF16),
            pltpu.VMEM((TB, PAIR_W), F32), pltpu.VMEM((TB, PAIR_W), F32),
            pltpu.VMEM(win.shape, BF16), pltpu.VMEM(wout.shape, BF16),
            pltpu.VMEM((1, N_PAIRS, PAIR_W, PAIR_W), F32),
            pltpu.VMEM((1, N_PAIRS, PAIR_W, PAIR_W), F32),
        ],
        compiler_params=pltpu.CompilerParams(
            dimension_semantics=("arbitrary",), vmem_limit_bytes=V7X_VMEM_LIMIT),
    )(x2d, gmix, win, invf, sgn, lbl, hgn, wout, gffn, wr, br, tri, dret, xi, zeta, gls, ones_bd, stri, dmask,
      ret0, hg0, *w2d, *shared)


def _dispatch_tables(info, n_tiles_max):
    slab_group = info[:, 0, :SLABS_PER_BLOCK].reshape(-1)
    n_slabs = slab_group.shape[0]
    valid = slab_group >= 0
    onehot = (slab_group[:, None] == jnp.arange(N_GROUPS, dtype=jnp.int32)[None, :]).astype(jnp.int32)
    within = jnp.cumsum(onehot, axis=0) - onehot
    tiles = (jnp.sum(onehot, axis=0) + TILE_SLABS - 1) // TILE_SLABS
    tile_end = jnp.cumsum(tiles)
    g = jnp.clip(slab_group, 0, N_GROUPS - 1)
    dest = (tile_end - tiles)[g] * TILE_SLABS + jnp.take_along_axis(within, g[:, None], axis=1)[:, 0]
    dest = jnp.where(valid, dest, -1).astype(jnp.int32)
    n_slots = n_tiles_max * TILE_SLABS
    src = jnp.full((n_slots,), -1, jnp.int32).at[jnp.where(valid, dest, n_slots)].set(
        jnp.arange(n_slabs, dtype=jnp.int32), mode="drop")
    tile_group = jnp.sum(jnp.arange(n_tiles_max, dtype=jnp.int32)[:, None] >= tile_end[None, :], axis=1)
    tile_group = jnp.minimum(tile_group, N_GROUPS - 1).astype(jnp.int32)
    return src, dest, tile_group, tile_end[-1:].astype(jnp.int32)


def _moe_call(src, tile_group, n_tiles, h2s, gs, w_gate, w_up, w_down):
    n_tiles_max = tile_group.shape[0]
    group_of_tile = lambda i, s, tg, nt: (tg[i], 0, 0)
    return pl.pallas_call(
        _moe_kernel,
        grid_spec=pltpu.PrefetchScalarGridSpec(
            num_scalar_prefetch=3,
            grid=(n_tiles_max,),
            in_specs=[
                pl.BlockSpec(memory_space=pl.ANY), pl.BlockSpec(memory_space=pl.ANY),
                pl.BlockSpec((EXPERTS_PER_GROUP, D_MODEL, D_EXPERT), group_of_tile),
                pl.BlockSpec((EXPERTS_PER_GROUP, D_MODEL, D_EXPERT), group_of_tile),
                pl.BlockSpec((EXPERTS_PER_GROUP, D_EXPERT, D_MODEL), group_of_tile),
            ],
            out_specs=pl.BlockSpec((TM, D_MODEL), lambda i, s, tg, nt: (i, 0)),
            scratch_shapes=[
                pltpu.VMEM((2, TM, D_MODEL), BF16), pltpu.VMEM((2, TM, ROUTER_W), F32),
                pltpu.SemaphoreType.DMA((2,)), pltpu.SemaphoreType.DMA((2,)),
            ],
        ),
        out_shape=jax.ShapeDtypeStruct((n_tiles_max * TM, D_MODEL), BF16),
        compiler_params=pltpu.CompilerParams(
            dimension_semantics=("arbitrary",), vmem_limit_bytes=V7X_VMEM_LIMIT),
    )(src, tile_group, n_tiles, h2s, gs, w_gate, w_up, w_down)


def _final_call(dest, x1, rinfo, ys, gfin, blocks_p):
    n_blocks = x1.shape[0] // TB
    assert blocks_p % FINAL_BLOCKS == 0 and n_blocks % FINAL_BLOCKS == 0
    steps_p = blocks_p // FINAL_BLOCKS
    rows = FINAL_BLOCKS * TB
    return pl.pallas_call(
        functools.partial(_final_kernel, steps_p),
        grid_spec=pltpu.PrefetchScalarGridSpec(
            num_scalar_prefetch=1,
            grid=(n_blocks // FINAL_BLOCKS,),
            in_specs=[
                pl.BlockSpec((rows, D_MODEL), lambda i, d: (i, 0)),
                pl.BlockSpec((rows, ROUTER_W), lambda i, d: (i, 0)),
                pl.BlockSpec(memory_space=pl.ANY),
                pl.BlockSpec((1, D_MODEL), lambda i, d: (0, 0)),
            ],
            out_specs=[
                pl.BlockSpec((rows, D_MODEL), lambda i, d: (jnp.minimum(i, steps_p - 1), 0)),
                pl.BlockSpec((rows, D_MODEL), lambda i, d: (jnp.maximum(i - steps_p, 0), 0)),
            ],
            scratch_shapes=[pltpu.VMEM((2, FINAL_BLOCKS * SORT_ROWS, D_MODEL), BF16), pltpu.SemaphoreType.DMA((2,))],
        ),
        out_shape=[jax.ShapeDtypeStruct((blocks_p * TB, D_MODEL), F32),
                   jax.ShapeDtypeStruct(((n_blocks - blocks_p) * TB, D_MODEL), F32)],
        compiler_params=pltpu.CompilerParams(
            dimension_semantics=("arbitrary",), vmem_limit_bytes=V7X_VMEM_LIMIT),
    )(dest, x1, rinfo, ys, gfin)


def kernel(x_prompt, x_sample, state_ret, state_hgrn, norm_mix_g, w_in, hgrn_lb_logits, hgrn_norm_g, w_out,
           norm_ffn_g, w_router_group, b_router_group, w_router_expert, b_router_expert, w_exp_gate, w_exp_up,
           w_exp_down, norm_final_g):
    depth = w_in.shape[0]
    assert depth == 1 and hgrn_lb_logits.shape[0] == 2, "single-layer configuration only"
    bp, seq, d = x_prompt.shape
    db, dec_len, _ = x_sample.shape
    assert bp == 1 and d == D_MODEL and dec_len == CHUNK and seq % 256 == 0 and db % 4 == 0

    pad = ROUTER_W - N_GROUPS - N_EXPERTS
    wr = jnp.concatenate([w_router_group[0], w_router_expert[0], jnp.zeros((D_MODEL, pad), F32)], axis=1)
    br = jnp.concatenate([b_router_group[0], b_router_expert[0], jnp.zeros((pad,), F32)])[None, :]
    params = (norm_mix_g[0][None, :], w_in[0], hgrn_lb_logits,
              jnp.tile(hgrn_norm_g[0], N_HEADS)[None, :], w_out[0], norm_ffn_g[0][None, :], wr, br)

    zeros_state = jnp.zeros((1, N_HEADS, D_HEAD, D_HEAD), F32)
    blocks_p = seq // TB
    blocks_s = db * dec_len // TB
    n_blocks = blocks_p + blocks_s
    *shared, ret_p, hg_p, wg_bf, wu_bf, wd_bf = _mixer_call(
        x_prompt.reshape(seq, d), zeros_state, zeros_state, params, seg_len=TB, carry=True, pipelined=True,
        block_offset=0, total_blocks=n_blocks, expert_weights=(w_exp_gate[0], w_exp_up[0], w_exp_down[0]))
    x1, h2s, gs, rinfo, info, ret_s, hg_s = _mixer_call(
        x_sample.reshape(db * dec_len, d), state_ret[0], state_hgrn[0], params, seg_len=CHUNK, carry=False,
        pipelined=False,
        block_offset=blocks_p, total_blocks=n_blocks, shared=tuple(shared))

    n_tiles_max = -(-n_blocks * SLABS_PER_BLOCK // TILE_SLABS) + N_GROUPS
    src, dest, tile_group, n_tiles = _dispatch_tables(info, n_tiles_max)
    ys = _moe_call(src, tile_group, n_tiles, h2s, gs, wg_bf.reshape(w_exp_gate[0].shape),
                   wu_bf.reshape(w_exp_up[0].shape), wd_bf.reshape(w_exp_down[0].shape))
    y_p, y_s = _final_call(dest, x1, rinfo, ys, norm_final_g[None, :], blocks_p)
    return (y_p.reshape(bp, seq, d), y_s.reshape(db, dec_len, d), ret_p[None], hg_p[None], ret_s[None], hg_s[None])
```

```python
import functools

import numpy as np
import jax
import jax.numpy as jnp
from jax import lax
from jax.experimental import pallas as pl
from jax.experimental.pallas import tpu as pltpu

F32 = jnp.float32
BF16 = jnp.bfloat16

D_MODEL = 1024
N_HEADS = 8
D_HEAD = 64
GROUP_W = N_HEADS * D_HEAD
N_PAIRS = N_HEADS // 2
PAIR_W = 2 * D_HEAD
CHUNK = 64
SUB = 16
N_SUB = CHUNK // SUB
PAST_LEN = 2048
ROPE_BASE = 10000.0
RMS_EPS = 1e-6
SPAN_MAX = 80.0
N_GROUPS = 4
EXPERTS_PER_GROUP = 8
N_EXPERTS = N_GROUPS * EXPERTS_PER_GROUP
D_EXPERT = 256
ROUTER_W = 128
V7X_VMEM_LIMIT = 60 * 1024 * 1024
MXU_DIM = 256
TB = 256
SLAB = 16
SLABS_PER_BLOCK = TB // SLAB + N_GROUPS
SORT_ROWS = SLABS_PER_BLOCK * SLAB
SORT_W = 384
TILE_SLABS = 32
TM = TILE_SLABS * SLAB
FINAL_BLOCKS = 2
GATE_PARTS = 2
H2S_W = D_MODEL + GATE_PARTS * ROUTER_W
N_SHARED = 4

C_QR, C_KR, C_VR, C_GR, C_QG, C_FG, C_VG, C_GG = (i * GROUP_W for i in range(8))


def _dot(a, b):
    return jnp.dot(a, b, preferred_element_type=F32)


def _dot_nt(a, b):
    return lax.dot_general(a, b, (((1,), (1,)), ((), ())), preferred_element_type=F32)


def _dot_tn(a, b):
    return lax.dot_general(a, b, (((0,), (0,)), ((), ())), preferred_element_type=F32)


def _split(x, n):
    parts = []
    for _ in range(n):
        p = x.astype(BF16)
        parts.append(p)
        x = x - p.astype(F32)
    return parts


def _group_sum(x, ones_ref):
    xb = x.astype(BF16)
    return jnp.concatenate([_dot(xb[:, c * MXU_DIM:(c + 1) * MXU_DIM], ones_ref[...])
                            for c in range(GROUP_W // MXU_DIM)], axis=1)


class _ColumnGroups:
    def __init__(self, bufs):
        self.bufs = bufs

    def _locate(self, idx):
        rows, cols = idx
        g = cols.start // GROUP_W
        assert (cols.stop - 1) // GROUP_W == g
        return self.bufs[g], (rows, slice(cols.start - g * GROUP_W, cols.stop - g * GROUP_W))

    def __getitem__(self, idx):
        buf, at = self._locate(idx)
        return buf[at]

    def __setitem__(self, idx, value):
        buf, at = self._locate(idx)
        buf[at] = value


def _silu(x):
    return x * jax.nn.sigmoid(x)


N_MIXER_INPUTS = 21


def _mixer_kernel(cfg, *refs):
    pipelined, n_own, n_fill, n_alias, cast_w = cfg[-5:]
    n_body = n_own + (1 if pipelined else 0)
    n_in = N_MIXER_INPUTS + (3 if cast_w else 0)
    i = pl.program_id(0)

    @pl.when(i < n_body)
    def _():
        _mixer_body(cfg, *refs)

    if n_fill:
        @pl.when(i >= n_body)
        def _():
            for out_ref in refs[n_in + n_alias:n_in + n_alias + N_SHARED]:
                out_ref[...] = jnp.zeros_like(out_ref)


def _mixer_body(cfg, x_ref, gmix_ref, win_ref, invf_ref, sgn_ref, lbl_ref, hgn_ref, wout_ref, gffn_ref,
                wr_ref, br_ref, tri_ref, dret_ref, xi_ref, zeta_ref, gls_ref, ones_ref, stri_ref,
                dmask_ref, ret0_ref, hg0_ref, *rest):
    LS, NSEG, carry, pipelined, n_own, _, n_alias, cast_w = cfg
    w_chunks = ()
    if cast_w:
        n_out = N_SHARED + 2
        w_chunks = tuple(zip(rest[:3], rest[3 + n_alias + n_out:3 + n_alias + n_out + 3]))
        rest = rest[3:3 + n_alias + n_out] + rest[3 + n_alias + n_out + 3:]
    (x1_ref, h2s_ref, rinfo_ref, info_ref, retout_ref, hgout_ref,
     *proj_bufs, kbuf, obuf, xprev, qin_buf, kout_buf, qd, kd, pbuf, cos_t, sin_t, win_bf, wout_bf,
     rst, sst) = rest[n_alias:]
    proj = _ColumnGroups(proj_bufs)
    NCH = LS // CHUNK
    i = pl.program_id(0)
    lane128 = lax.broadcasted_iota(jnp.int32, (1, PAIR_W), 1)
    head_a = lane128 < D_HEAD
    bd_mask = (lax.broadcasted_iota(jnp.int32, (PAIR_W, PAIR_W), 0) < D_HEAD) == head_a

    def pair_state(ref, sg, p):
        z = jnp.zeros((D_HEAD, D_HEAD), F32)
        return jnp.concatenate([jnp.concatenate([ref[sg, 2 * p], z], axis=1),
                                jnp.concatenate([z, ref[sg, 2 * p + 1]], axis=1)], axis=0)

    @pl.when(i == 0)
    def _():
        for c in range(0, 8 * GROUP_W, GROUP_W):
            win_bf[:, c:c + GROUP_W] = win_ref[:, c:c + GROUP_W].astype(BF16)
        wout_bf[...] = wout_ref[...].astype(BF16)
        if carry:
            for p in range(N_PAIRS):
                rst[0, p] = pair_state(ret0_ref, 0, p)
                sst[0, p] = pair_state(hg0_ref, 0, p)
        row = lax.broadcasted_iota(jnp.int32, (TB, 1), 0)
        ang_row = (row & (LS - 1)).astype(F32) * invf_ref[...]
        cos_t[...] = jnp.cos(ang_row)
        sin_t[...] = jnp.sin(ang_row)
        if pipelined:
            for buf in (obuf, xprev, proj_bufs[C_GR // GROUP_W], proj_bufs[C_GG // GROUP_W]):
                buf[...] = jnp.zeros_like(buf)

    if pipelined:
        _mixer_back(proj, obuf, xprev, hgn_ref, ones_ref, wout_bf, gffn_ref, wr_ref, br_ref, stri_ref,
                    x1_ref, h2s_ref, rinfo_ref, info_ref)
    keep = i < n_own

    for w_ref, w_bf_ref in w_chunks:
        w_bf_ref[...] = w_ref[...].astype(BF16)

    x = x_ref[...]
    h = x * lax.rsqrt(jnp.mean(x * x, axis=-1, keepdims=True) + RMS_EPS) * gmix_ref[...]
    hb = h.astype(BF16)

    def project(c0, c1):
        for c in range(c0, c1, GROUP_W):
            proj[:, c:c + GROUP_W] = _dot(hb, win_bf[:, c:c + GROUP_W])

    project(C_QG, C_GG)
    project(C_QR, C_VR)

    la = lbl_ref[0:1, :]
    lb_ = lbl_ref[1:2, :]
    lmax = jnp.maximum(la, lb_)
    ea = jnp.exp(la - lmax)
    lbv = ea / (ea + jnp.exp(lb_ - lmax))
    qg = proj[:, C_QG:C_QG + GROUP_W]
    proj[:, C_QG:C_QG + GROUP_W] = _silu(qg)
    f = lbv + (1.0 - lbv) * jax.nn.sigmoid(proj[:, C_FG:C_FG + GROUP_W])
    kbuf[...] = 1.0 - f
    logf = jnp.log(f)
    for sg in range(NSEG):
        rows = slice(sg * LS, (sg + 1) * LS)
        parts = _split(logf[rows], 3)
        proj[rows, C_FG:C_FG + GROUP_W] = sum(_dot(tri_ref[...], p) for p in parts)

    project(C_VR, C_QG)
    project(C_GG, C_GG + GROUP_W)

    start = jnp.full((8, PAIR_W), i * TB if carry else PAST_LEN, jnp.int32).astype(F32)
    ang0 = (start * invf_ref[...])[0:1]
    c0 = jnp.cos(ang0)
    s0 = jnp.sin(ang0)
    cos = cos_t[...] * c0 - sin_t[...] * s0
    sin = (sin_t[...] * c0 + cos_t[...] * s0) * sgn_ref[...]
    first_half = (lane128 & (D_HEAD - 1)) < (D_HEAD // 2)
    for blk in range(2 * N_PAIRS):
        cols = slice(blk * PAIR_W, (blk + 1) * PAIR_W)
        xx = proj[:, cols]
        partner = jnp.where(first_half, pltpu.roll(xx, PAIR_W - D_HEAD // 2, 1), pltpu.roll(xx, D_HEAD // 2, 1))
        r = xx * cos + partner * sin
        if blk < N_PAIRS:
            r = r * (D_HEAD ** -0.5)
        proj[:, cols] = r

    ciota = lax.broadcasted_iota(jnp.int32, (CHUNK, 1), 0)

    def exact_chunk(c, _):
        r0 = pl.multiple_of(c * CHUNK, CHUNK)
        qc = proj[pl.ds(r0, CHUNK), C_QG:C_QG + GROUP_W]
        bc = proj[pl.ds(r0, CHUNK), C_FG:C_FG + GROUP_W]
        acc = [jnp.zeros((CHUNK, MXU_DIM), F32) for _ in range(2)]
        for sb in range(N_SUB):
            rs = r0 + sb * SUB
            for s in range(SUB):
                ks = kbuf[pl.ds(rs + s, 1), :]
                bs = proj[pl.ds(rs + s, 1), C_FG:C_FG + GROUP_W]
                p = qc * ks * jnp.exp(jnp.minimum(bc - bs, 0.0))
                pbuf[s * CHUNK:(s + 1) * CHUNK, :] = jnp.where(ciota >= sb * SUB + s, p, 0.0).astype(BF16)
            for hh in range(2):
                sc = _dot(pbuf[:, hh * MXU_DIM:(hh + 1) * MXU_DIM], ones_ref[...])
                for s in range(SUB):
                    vs = proj[pl.ds(rs + s, 1), C_VG + hh * MXU_DIM:C_VG + (hh + 1) * MXU_DIM]
                    acc[hh] = acc[hh] + sc[s * CHUNK:(s + 1) * CHUNK] * vs
        hg_cols = slice(GROUP_W, 2 * GROUP_W)
        obuf[pl.ds(r0, CHUNK), hg_cols] = obuf[pl.ds(r0, CHUNK), hg_cols] + jnp.concatenate(acc, axis=1)
        return 0

    b_all = proj[:, C_FG:C_FG + GROUP_W]
    first = jnp.concatenate([jnp.broadcast_to(b_all[r:r + 1, :], (CHUNK, GROUP_W)) for r in range(0, TB, CHUNK)],
                            axis=0)
    span = first - b_all
    risky = jnp.max(span) > SPAN_MAX
    q_local = proj[:, C_QG:C_QG + GROUP_W] * jnp.exp(-span)
    qd[...] = jnp.where(risky, 0.0, q_local).astype(BF16)
    kd[...] = (kbuf[...] * jnp.exp(jnp.minimum(span, SPAN_MAX))).astype(BF16)

    for sg in range(NSEG):
        r0 = sg * LS
        rows = slice(r0, r0 + LS)
        qg_s = proj[rows, C_QG:C_QG + GROUP_W]
        kg_s = kbuf[rows, :]
        b_s = proj[rows, C_FG:C_FG + GROUP_W]

        for c in range(NCH):
            cr = slice(c * CHUNK, (c + 1) * CHUNK)
            b_c = b_s[cr]
            b_in = b_s[c * CHUNK - 1:c * CHUNK, :] if c else jnp.zeros((1, GROUP_W), F32)
            b_out = b_s[(c + 1) * CHUNK - 1:(c + 1) * CHUNK, :]
            qin_buf[r0 + cr.start:r0 + cr.stop, :] = (qg_s[cr] * jnp.exp(b_c - b_in)).astype(BF16)
            kout_buf[r0 + cr.start:r0 + cr.stop, :] = (kg_s[cr] * jnp.exp(b_out - b_c)).astype(BF16)

    def pair_unit(sg, p):
        r0 = sg * LS
        rows = slice(r0, r0 + LS)
        st = 0 if carry else sg
        pc = slice(p * PAIR_W, (p + 1) * PAIR_W)

        qb = proj[rows, C_QR + p * PAIR_W:C_QR + (p + 1) * PAIR_W].astype(BF16)
        kr = proj[rows, C_KR + p * PAIR_W:C_KR + (p + 1) * PAIR_W]
        kb = kr.astype(BF16)
        vb = proj[rows, C_VR + p * PAIR_W:C_VR + (p + 1) * PAIR_W].astype(BF16)
        zero = jnp.zeros_like(qb)
        s_a = _dot_nt(jnp.where(head_a, qb, zero), kb) * dret_ref[2 * p]
        s_b = _dot_nt(jnp.where(head_a, zero, qb), kb) * dret_ref[2 * p + 1]
        o_r = jnp.where(head_a, _dot(s_a.astype(BF16), vb), _dot(s_b.astype(BF16), vb))
        r_old = rst[0, p] if carry else pair_state(ret0_ref, sg, p)
        o_r = o_r + _dot(qb, r_old.astype(BF16)) * xi_ref[:, pc]
        u = _dot_tn((kr * zeta_ref[:, pc]).astype(BF16), vb)
        r_new = r_old * gls_ref[:, pc] + jnp.where(bd_mask, u, 0.0)
        if pipelined:
            r_new = jnp.where(keep, r_new, r_old)
        obuf[rows, pc] = o_r

        vg = proj[rows, C_VG + p * PAIR_W:C_VG + (p + 1) * PAIR_W].astype(BF16)
        qd_p = qd[rows, pc]
        kd_p = kd[rows, pc]
        zd = jnp.zeros_like(qd_p)
        g_a = _dot_nt(jnp.where(head_a, qd_p, zd), kd_p) * dmask_ref[...]
        g_b = _dot_nt(jnp.where(head_a, zd, qd_p), kd_p) * dmask_ref[...]
        o_g = jnp.where(head_a, _dot(g_a.astype(BF16), vg), _dot(g_b.astype(BF16), vg))
        s_old = sst[0, p] if carry else pair_state(hg0_ref, sg, p)
        s_new = s_old
        from_state = []
        bcol = C_FG + p * PAIR_W
        for c in range(NCH):
            cr = slice(r0 + c * CHUNK, r0 + (c + 1) * CHUNK)
            from_state.append(_dot(qin_buf[cr, pc], s_new.astype(BF16)))
            ut = _dot_tn(kout_buf[cr, pc], vg[c * CHUNK:(c + 1) * CHUNK])
            b_out = proj[cr.stop - 1:cr.stop, bcol:bcol + PAIR_W]
            if c:
                b_out = b_out - proj[cr.start - 1:cr.start, bcol:bcol + PAIR_W]
            s_decay = jnp.exp(jnp.broadcast_to(b_out, (PAIR_W, PAIR_W)).T)
            s_new = s_new * s_decay + jnp.where(bd_mask, ut, 0.0)
        o_g = o_g + (jnp.concatenate(from_state, axis=0) if NCH > 1 else from_state[0])
        if pipelined:
            s_new = jnp.where(keep, s_new, s_old)
        gc = slice(GROUP_W + p * PAIR_W, GROUP_W + (p + 1) * PAIR_W)
        obuf[rows, gc] = o_g

        if carry:
            rst[0, p] = r_new
            sst[0, p] = s_new
        for out_ref, new in ((retout_ref, r_new), (hgout_ref, s_new)):
            out_ref[st, 2 * p] = new[:D_HEAD, :D_HEAD]
            out_ref[st, 2 * p + 1] = new[D_HEAD:, D_HEAD:]

    for sg in range(NSEG):
        for p in range(N_PAIRS):
            pair_unit(sg, p)

    xprev[...] = x

    @pl.when(risky)
    def _():
        lax.fori_loop(0, TB // CHUNK, exact_chunk, 0)

    if not pipelined:
        _mixer_back(proj, obuf, xprev, hgn_ref, ones_ref, wout_bf, gffn_ref, wr_ref, br_ref, stri_ref,
                    x1_ref, h2s_ref, rinfo_ref, info_ref)


def _mixer_back(proj, obuf, xprev, hgn_ref, ones_ref, wout_bf, gffn_ref, wr_ref, br_ref, stri_ref,
                x1_ref, h2s_ref, rinfo_ref, info_ref):
    o_r = obuf[:, 0:GROUP_W]
    mu = _group_sum(o_r, ones_ref) * (1.0 / D_HEAD)
    dlt = o_r - mu
    var = _group_sum(dlt * dlt, ones_ref) * (1.0 / D_HEAD)
    y_r = dlt * lax.rsqrt(var + RMS_EPS) * _silu(proj[:, C_GR:C_GR + GROUP_W])
    o_g = obuf[:, GROUP_W:2 * GROUP_W]
    ms = _group_sum(o_g * o_g, ones_ref) * (1.0 / D_HEAD)
    y_g = o_g * lax.rsqrt(ms + RMS_EPS) * hgn_ref[...] * _silu(proj[:, C_GG:C_GG + GROUP_W])
    mix = jnp.concatenate([y_r, y_g], axis=1).astype(BF16)
    x1 = xprev[...] + _dot(mix, wout_bf[...])
    x1_ref[...] = x1

    h2 = x1 * lax.rsqrt(jnp.mean(x1 * x1, axis=-1, keepdims=True) + RMS_EPS) * gffn_ref[...]
    h_hi, h_lo = _split(h2, 2)
    w_hi, w_lo = _split(wr_ref[...], 2)
    hi_terms = _dot(h_hi, jnp.concatenate([w_hi, w_lo], axis=1))
    logits = hi_terms[:, :ROUTER_W] + hi_terms[:, ROUTER_W:] + _dot(h_lo, w_hi) + br_ref[...]
    lane = lax.broadcasted_iota(jnp.int32, (1, ROUTER_W), 1).astype(F32)
    neg = -jnp.inf
    no_lane = float(ROUTER_W)
    gl = jnp.where(lane < N_GROUPS, logits, neg)
    gmax = jnp.max(gl, axis=-1, keepdims=True)
    g_idx = jnp.min(jnp.where(gl == gmax, lane, no_lane), axis=-1, keepdims=True)
    prob_g = 1.0 / jnp.sum(jnp.exp(gl - gmax), axis=-1, keepdims=True)
    e_lo = N_GROUPS + EXPERTS_PER_GROUP * g_idx
    el = jnp.where((lane >= e_lo) & (lane < e_lo + EXPERTS_PER_GROUP), logits, neg)
    v1 = jnp.max(el, axis=-1, keepdims=True)
    i1 = jnp.min(jnp.where(el == v1, lane, no_lane), axis=-1, keepdims=True)
    el2 = jnp.where(lane == i1, neg, el)
    v2 = jnp.max(el2, axis=-1, keepdims=True)
    i2 = jnp.min(jnp.where(el2 == v2, lane, no_lane), axis=-1, keepdims=True)
    t = jnp.exp(v2 - v1)
    p1 = 1.0 / (1.0 + t)
    p2 = t * p1
    gate = jnp.where(lane == i1, prob_g * p1, 0.0) + jnp.where(lane == i2, prob_g * p2, 0.0)

    onehot = jnp.where(lane == g_idx, 1.0, 0.0)
    before = _dot(stri_ref[...], onehot.astype(BF16))
    count = jnp.sum(onehot, axis=0, keepdims=True)
    padded = jnp.floor((count + (SLAB - 1.0)) * (1.0 / SLAB)) * SLAB
    padded8 = jnp.broadcast_to(padded, (8, ROUTER_W))
    seg_start = sum(jnp.where(lane >= k, pltpu.roll(padded8, k, 1), 0.0) for k in range(1, N_GROUPS))[0:1]
    rank = jnp.sum(onehot * (seg_start + before), axis=-1, keepdims=True)
    col = lax.broadcasted_iota(jnp.int32, (1, SORT_W), 1).astype(F32)
    place = jnp.where(col == rank, 1.0, 0.0).astype(BF16)
    row = jnp.concatenate([h2.astype(BF16)] + _split(gate, GATE_PARTS), axis=1)
    h2s_ref[...] = _dot_tn(place, row)[:SORT_ROWS].astype(BF16)
    rinfo_ref[...] = jnp.where(lane == 0, rank, 0.0)
    slab_lo = (lane * SLAB).astype(F32)
    slab_group = jnp.full((1, ROUTER_W), -1.0, F32)
    for g in range(N_GROUPS):
        s_g = jnp.sum(jnp.where(lane == g, seg_start, 0.0), axis=-1, keepdims=True)
        n_g = jnp.sum(jnp.where(lane == g, padded, 0.0), axis=-1, keepdims=True)
        slab_group = jnp.where((slab_lo >= s_g) & (slab_lo < s_g + n_g), float(g), slab_group)
    info_ref[0] = jnp.broadcast_to(slab_group.astype(jnp.int32), (8, ROUTER_W))


def _slab_gather(table_ref, first, n_slabs, srcs, bufs, sems, slot, *, wait):
    n_rows = n_slabs * SLAB
    for src, buf, sem in zip(srcs, bufs, sems):
        if wait:
            pltpu.make_async_copy(src.at[pl.ds(0, n_rows)], buf.at[slot, pl.ds(0, n_rows)], sem.at[slot]).wait()
            continue
        for j in range(n_slabs):
            row = pl.multiple_of(jnp.maximum(table_ref[first + j], 0) * SLAB, SLAB)
            pltpu.make_async_copy(src.at[pl.ds(row, SLAB)], buf.at[slot, pl.ds(j * SLAB, SLAB)],
                                  sem.at[slot]).start()


def _moe_kernel(src_ref, tg_ref, nt_ref, h2s_hbm, wg_ref, wu_ref, wd_ref, ys_ref, hbuf, sem_h):
    i = pl.program_id(0)
    n_tiles = nt_ref[0]
    slot = lax.rem(i, 2)
    gather = functools.partial(_slab_gather, src_ref, srcs=(h2s_hbm,), bufs=(hbuf,), sems=(sem_h,),
                               n_slabs=TILE_SLABS)

    @pl.when(i == 0)
    def _():
        gather(first=0, slot=0, wait=False)

    @pl.when(i < n_tiles)
    def _():
        gather(first=i * TILE_SLABS, slot=slot, wait=True)

        @pl.when(i + 1 < n_tiles)
        def _():
            gather(first=(i + 1) * TILE_SLABS, slot=1 - slot, wait=False)

        hb = hbuf[slot, :, :D_MODEL]
        gates = sum(hbuf[slot, :, D_MODEL + k * ROUTER_W:D_MODEL + (k + 1) * ROUTER_W].astype(F32)
                    for k in range(GATE_PARTS))
        lane = lax.broadcasted_iota(jnp.int32, (1, ROUTER_W), 1)
        first_lane = N_GROUPS + tg_ref[i] * EXPERTS_PER_GROUP
        acc = jnp.zeros((TM, D_MODEL), F32)
        for e in range(EXPERTS_PER_GROUP):
            he = (_silu(_dot(hb, wg_ref[e])) * _dot(hb, wu_ref[e])).astype(BF16)
            gcol = jnp.sum(jnp.where(lane == first_lane + e, gates, 0.0), axis=-1, keepdims=True)
            acc = acc + gcol * _dot(he, wd_ref[e])
        ys_ref[...] = acc.astype(BF16)

    @pl.when(i >= n_tiles)
    def _():
        ys_ref[...] = jnp.zeros_like(ys_ref)


def _final_kernel(steps_p, dest_ref, x1_ref, rinfo_ref, ys_hbm, gfin_ref, yp_ref, ysmp_ref, ybuf, sem):
    i = pl.program_id(0)
    slot = lax.rem(i, 2)
    n_slabs = FINAL_BLOCKS * SLABS_PER_BLOCK
    gather = functools.partial(_slab_gather, dest_ref, srcs=(ys_hbm,), bufs=(ybuf,), sems=(sem,), n_slabs=n_slabs)

    @pl.when(i == 0)
    def _():
        gather(first=0, slot=0, wait=False)

    gather(first=i * n_slabs, slot=slot, wait=True)

    @pl.when(i + 1 < pl.num_programs(0))
    def _():
        gather(first=(i + 1) * n_slabs, slot=1 - slot, wait=False)

    yb = ybuf[slot]
    col = lax.broadcasted_iota(jnp.int32, (1, FINAL_BLOCKS * SORT_ROWS), 1).astype(F32)
    moe = []
    for b in range(FINAL_BLOCKS):
        rank = rinfo_ref[b * TB:(b + 1) * TB, 0:1] + float(b * SORT_ROWS)
        moe.append(_dot(jnp.where(col == rank, 1.0, 0.0).astype(BF16), yb))
    xo = x1_ref[...] + jnp.concatenate(moe, axis=0)
    y = xo * lax.rsqrt(jnp.mean(xo * xo, axis=-1, keepdims=True) + RMS_EPS) * gfin_ref[...]

    @pl.when(i < steps_p)
    def _():
        yp_ref[...] = y

    @pl.when(i >= steps_p)
    def _():
        ysmp_ref[...] = y


def _const_spec(shape, pipeline_mode=None):
    nd = len(shape)
    return pl.BlockSpec(shape, lambda i: (0,) * nd, pipeline_mode=pipeline_mode)


def _mixer_call(x2d, ret0, hg0, params, *, seg_len, carry, pipelined, block_offset, total_blocks, shared=(),
                expert_weights=()):
    (gmix, win, lbl, hgn, wout, gffn, wr, br) = params
    T = x2d.shape[0]
    LS = seg_len
    n_seg = TB // LS
    NCH = LS // CHUNK
    n_states = ret0.shape[0]
    n_own = T // TB
    n_fill = 0 if shared else total_blocks - n_own - block_offset
    lag = 1 if pipelined else 0
    grid = n_own + lag + n_fill

    f32 = np.float32
    lg = np.log1p(-(f32(2.0) ** (f32(-5.0) - np.arange(N_HEADS, dtype=f32)))).astype(f32)
    tt = np.arange(LS, dtype=f32)
    ch = np.arange(LS) // CHUNK
    dret = np.exp(np.abs(tt[:, None] - tt[None, :])[None] * lg[:, None, None]).astype(f32)
    dret = np.where((ch[None, :] <= ch[:, None])[None], dret, f32(0.0))
    lg_lane = np.repeat(lg, D_HEAD)[None, :]
    xi = np.exp((tt[:, None] + f32(1.0)) * lg_lane).astype(f32)
    zeta = np.exp((f32(LS - 1.0) - tt)[:, None] * lg_lane).astype(f32)
    gls = np.exp(f32(LS) * lg_lane).astype(f32)
    half = D_HEAD // 2
    inv_freq = ROPE_BASE ** (-jnp.arange(half, dtype=F32) / half)
    invf = jnp.tile(inv_freq, PAIR_W // half)[None, :]
    sgn = np.where((np.arange(PAIR_W) % D_HEAD) < half, f32(-1.0), f32(1.0))[None, :]
    tri = jnp.asarray(np.tril(np.ones((LS, LS), np.float32)), BF16)
    lane_head = np.arange(MXU_DIM) // D_HEAD
    ones_bd = jnp.asarray((lane_head[:, None] == lane_head[None, :]).astype(np.float32), BF16)
    dmask = jnp.asarray(((ch[:, None] == ch[None, :]) & (np.arange(LS)[:, None] >= np.arange(LS)[None, :]))
                        .astype(np.float32))

    if carry:
        state_spec = pl.BlockSpec((1, N_HEADS, D_HEAD, D_HEAD), lambda i: (0, 0, 0, 0))
    else:
        state_spec = pl.BlockSpec((n_seg, N_HEADS, D_HEAD, D_HEAD), lambda i: (i, 0, 0, 0))
    single = pl.Buffered(1)
    stri = jnp.asarray(np.tril(np.ones((TB, TB), np.float32), -1), BF16)
    in_row_spec = pl.BlockSpec((TB, D_MODEL), lambda i: (jnp.minimum(i, n_own - 1), 0))
    out_block = lambda i: jnp.maximum(i - lag, 0) + block_offset
    out_row_spec = lambda rows, w: pl.BlockSpec((rows, w), lambda i: (out_block(i), 0))

    cast_w = bool(expert_weights)
    w2d = [w.reshape(-1, w.shape[-1]) for w in expert_weights]
    w_specs = [pl.BlockSpec((w.shape[0] // n_own, w.shape[1]), lambda i: (jnp.minimum(i, n_own - 1), 0)) for w in w2d]
    w_shapes = [jax.ShapeDtypeStruct(w.shape, BF16) for w in w2d]
    n_in = N_MIXER_INPUTS + len(w2d)

    kern = functools.partial(_mixer_kernel, (LS, n_seg, carry, pipelined, n_own, n_fill, len(shared), cast_w))
    return pl.pallas_call(
        kern,
        grid=(grid,),
        in_specs=[
            in_row_spec,
            _const_spec((1, D_MODEL)), _const_spec(win.shape, single), _const_spec((1, PAIR_W)),
            _const_spec((1, PAIR_W)), _const_spec(lbl.shape), _const_spec((1, GROUP_W)),
            _const_spec(wout.shape, single), _const_spec((1, D_MODEL)),
            _const_spec(wr.shape), _const_spec((1, ROUTER_W)),
            _const_spec((LS, LS)), _const_spec((N_HEADS, LS, LS)), _const_spec((LS, GROUP_W)),
            _const_spec((LS, GROUP_W)), _const_spec((1, GROUP_W)), _const_spec((MXU_DIM, MXU_DIM)),
            _const_spec((TB, TB)), _const_spec((LS, LS)),
            state_spec, state_spec,
        ] + w_specs + [pl.BlockSpec(memory_space=pl.ANY)] * len(shared),
        out_specs=[out_row_spec(TB, D_MODEL), out_row_spec(SORT_ROWS, H2S_W), out_row_spec(TB, ROUTER_W),
                   pl.BlockSpec((1, 8, ROUTER_W), lambda i: (out_block(i), 0, 0)),
                   state_spec, state_spec] + w_specs,
        out_shape=[
            jax.ShapeDtypeStruct((total_blocks * TB, D_MODEL), F32),
            jax.ShapeDtypeStruct((total_blocks * SORT_ROWS, H2S_W), BF16),
            jax.ShapeDtypeStruct((total_blocks * TB, ROUTER_W), F32),
            jax.ShapeDtypeStruct((total_blocks, 8, ROUTER_W), jnp.int32),
            jax.ShapeDtypeStruct((n_states, N_HEADS, D_HEAD, D_HEAD), F32),
            jax.ShapeDtypeStruct((n_states, N_HEADS, D_HEAD, D_HEAD), F32),
        ] + w_shapes,
        input_output_aliases={n_in + k: k for k in range(len(shared))},
        scratch_shapes=[
            *[pltpu.VMEM((TB, GROUP_W), F32) for _ in range(8)],
            pltpu.VMEM((TB, GROUP_W), F32),
            pltpu.VMEM((TB, 2 * GROUP_W), F32),
            pltpu.VMEM((TB, D_MODEL), F32),
            pltpu.VMEM((TB, GROUP_W), BF16), pltpu.VMEM((TB, GROUP_W), BF16),
            pltpu.VMEM((TB, GROUP_W), BF16), pltpu.VMEM((TB, GROUP_W), BF16),
            pltpu.VMEM((SUB * CHUNK, GROUP_W), BF16),
            pltpu.VMEM((TB, PAIR_W), F32), pltpu.VMEM((TB, PAIR_W), F32),
            pltpu.VMEM(win.shape, BF16), pltpu.VMEM(wout.shape, BF16),
            pltpu.VMEM((1, N_PAIRS, PAIR_W, PAIR_W), F32),
            pltpu.VMEM((1, N_PAIRS, PAIR_W, PAIR_W), F32),
        ],
        compiler_params=pltpu.CompilerParams(
            dimension_semantics=("arbitrary",), vmem_limit_bytes=V7X_VMEM_LIMIT),
    )(x2d, gmix, win, invf, sgn, lbl, hgn, wout, gffn, wr, br, tri, dret, xi, zeta, gls, ones_bd, stri, dmask,
      ret0, hg0, *w2d, *shared)


def _dispatch_tables(info, n_tiles_max):
    slab_group = info[:, 0, :SLABS_PER_BLOCK].reshape(-1)
    n_slabs = slab_group.shape[0]
    valid = slab_group >= 0
    onehot = (slab_group[:, None] == jnp.arange(N_GROUPS, dtype=jnp.int32)[None, :]).astype(jnp.int32)
    within = jnp.cumsum(onehot, axis=0) - onehot
    tiles = (jnp.sum(onehot, axis=0) + TILE_SLABS - 1) // TILE_SLABS
    tile_end = jnp.cumsum(tiles)
    g = jnp.clip(slab_group, 0, N_GROUPS - 1)
    dest = (tile_end - tiles)[g] * TILE_SLABS + jnp.take_along_axis(within, g[:, None], axis=1)[:, 0]
    dest = jnp.where(valid, dest, -1).astype(jnp.int32)
    n_slots = n_tiles_max * TILE_SLABS
    src = jnp.full((n_slots,), -1, jnp.int32).at[jnp.where(valid, dest, n_slots)].set(
        jnp.arange(n_slabs, dtype=jnp.int32), mode="drop")
    tile_group = jnp.sum(jnp.arange(n_tiles_max, dtype=jnp.int32)[:, None] >= tile_end[None, :], axis=1)
    tile_group = jnp.minimum(tile_group, N_GROUPS - 1).astype(jnp.int32)
    return src, dest, tile_group, tile_end[-1:].astype(jnp.int32)


def _moe_call(src, tile_group, n_tiles, h2s, w_gate, w_up, w_down):
    n_tiles_max = tile_group.shape[0]
    group_of_tile = lambda i, s, tg, nt: (tg[i], 0, 0)
    return pl.pallas_call(
        _moe_kernel,
        grid_spec=pltpu.PrefetchScalarGridSpec(
            num_scalar_prefetch=3,
            grid=(n_tiles_max,),
            in_specs=[
                pl.BlockSpec(memory_space=pl.ANY),
                pl.BlockSpec((EXPERTS_PER_GROUP, D_MODEL, D_EXPERT), group_of_tile),
                pl.BlockSpec((EXPERTS_PER_GROUP, D_MODEL, D_EXPERT), group_of_tile),
                pl.BlockSpec((EXPERTS_PER_GROUP, D_EXPERT, D_MODEL), group_of_tile),
            ],
            out_specs=pl.BlockSpec((TM, D_MODEL), lambda i, s, tg, nt: (i, 0)),
            scratch_shapes=[
                pltpu.VMEM((2, TM, H2S_W), BF16), pltpu.SemaphoreType.DMA((2,)),
            ],
        ),
        out_shape=jax.ShapeDtypeStruct((n_tiles_max * TM, D_MODEL), BF16),
        compiler_params=pltpu.CompilerParams(
            dimension_semantics=("arbitrary",), vmem_limit_bytes=V7X_VMEM_LIMIT),
    )(src, tile_group, n_tiles, h2s, w_gate, w_up, w_down)


def _final_call(dest, x1, rinfo, ys, gfin, blocks_p):
    n_blocks = x1.shape[0] // TB
    assert blocks_p % FINAL_BLOCKS == 0 and n_blocks % FINAL_BLOCKS == 0
    steps_p = blocks_p // FINAL_BLOCKS
    rows = FINAL_BLOCKS * TB
    return pl.pallas_call(
        functools.partial(_final_kernel, steps_p),
        grid_spec=pltpu.PrefetchScalarGridSpec(
            num_scalar_prefetch=1,
            grid=(n_blocks // FINAL_BLOCKS,),
            in_specs=[
                pl.BlockSpec((rows, D_MODEL), lambda i, d: (i, 0)),
                pl.BlockSpec((rows, ROUTER_W), lambda i, d: (i, 0)),
                pl.BlockSpec(memory_space=pl.ANY),
                pl.BlockSpec((1, D_MODEL), lambda i, d: (0, 0)),
            ],
            out_specs=[
                pl.BlockSpec((rows, D_MODEL), lambda i, d: (jnp.minimum(i, steps_p - 1), 0)),
                pl.BlockSpec((rows, D_MODEL), lambda i, d: (jnp.maximum(i - steps_p, 0), 0)),
            ],
            scratch_shapes=[pltpu.VMEM((2, FINAL_BLOCKS * SORT_ROWS, D_MODEL), BF16), pltpu.SemaphoreType.DMA((2,))],
        ),
        out_shape=[jax.ShapeDtypeStruct((blocks_p * TB, D_MODEL), F32),
                   jax.ShapeDtypeStruct(((n_blocks - blocks_p) * TB, D_MODEL), F32)],
        compiler_params=pltpu.CompilerParams(
            dimension_semantics=("arbitrary",), vmem_limit_bytes=V7X_VMEM_LIMIT),
    )(dest, x1, rinfo, ys, gfin)


def kernel(x_prompt, x_sample, state_ret, state_hgrn, norm_mix_g, w_in, hgrn_lb_logits, hgrn_norm_g, w_out,
           norm_ffn_g, w_router_group, b_router_group, w_router_expert, b_router_expert, w_exp_gate, w_exp_up,
           w_exp_down, norm_final_g):
    depth = w_in.shape[0]
    assert depth == 1 and hgrn_lb_logits.shape[0] == 2, "single-layer configuration only"
    bp, seq, d = x_prompt.shape
    db, dec_len, _ = x_sample.shape
    assert bp == 1 and d == D_MODEL and dec_len == CHUNK and seq % 256 == 0 and db % 4 == 0

    pad = ROUTER_W - N_GROUPS - N_EXPERTS
    wr = jnp.concatenate([w_router_group[0], w_router_expert[0], jnp.zeros((D_MODEL, pad), F32)], axis=1)
    br = jnp.concatenate([b_router_group[0], b_router_expert[0], jnp.zeros((pad,), F32)])[None, :]
    params = (norm_mix_g[0][None, :], w_in[0], hgrn_lb_logits,
              jnp.tile(hgrn_norm_g[0], N_HEADS)[None, :], w_out[0], norm_ffn_g[0][None, :], wr, br)

    zeros_state = jnp.zeros((1, N_HEADS, D_HEAD, D_HEAD), F32)
    blocks_p = seq // TB
    blocks_s = db * dec_len // TB
    n_blocks = blocks_p + blocks_s
    *shared, ret_p, hg_p, wg_bf, wu_bf, wd_bf = _mixer_call(
        x_prompt.reshape(seq, d), zeros_state, zeros_state, params, seg_len=TB, carry=True, pipelined=True,
        block_offset=0, total_blocks=n_blocks, expert_weights=(w_exp_gate[0], w_exp_up[0], w_exp_down[0]))
    x1, h2s, rinfo, info, ret_s, hg_s = _mixer_call(
        x_sample.reshape(db * dec_len, d), state_ret[0], state_hgrn[0], params, seg_len=CHUNK, carry=False,
        pipelined=False,
        block_offset=blocks_p, total_blocks=n_blocks, shared=tuple(shared))

    n_tiles_max = -(-n_blocks * SLABS_PER_BLOCK // TILE_SLABS) + N_GROUPS
    src, dest, tile_group, n_tiles = _dispatch_tables(info, n_tiles_max)
    ys = _moe_call(src, tile_group, n_tiles, h2s, wg_bf.reshape(w_exp_gate[0].shape),
                   wu_bf.reshape(w_exp_up[0].shape), wd_bf.reshape(w_exp_down[0].shape))
    y_p, y_s = _final_call(dest, x1, rinfo, ys, norm_final_g[None, :], blocks_p)
    return (y_p.reshape(bp, seq, d), y_s.reshape(db, dec_len, d), ret_p[None], hg_p[None], ret_s[None], hg_s[None])
```
